```python
import math
import jax, jax.numpy as jnp
from jax import lax
import numpy as np

D_MODEL = 4096
BATCH = 4
SEQ = 2048
DEPTH = 2

MEM_LEN = 256
EPS = 1e-6

SWA_WIDTH = D_MODEL // 4
GLA_WIDTH = D_MODEL // 4
SSD_WIDTH = D_MODEL // 2
MIX_WIDTH = SWA_WIDTH + GLA_WIDTH + SSD_WIDTH

SWA_HEAD_DIM = 64
SWA_HEADS = SWA_WIDTH // SWA_HEAD_DIM
SWA_KV_HEADS = SWA_HEADS // 8
SWA_KV_WIDTH = SWA_KV_HEADS * SWA_HEAD_DIM
SWA_WINDOW = 128
SWA_BLOCK = 128

REL_BUCKETS = 32
REL_MAX_DIST = 128

GLA_HEADS = 4
GLA_VAL_DIM = GLA_WIDTH // GLA_HEADS
GLA_KEY_DIM = GLA_VAL_DIM // 2
GLA_KEY_WIDTH = GLA_HEADS * GLA_KEY_DIM
GLA_GATE_RANK = 16
GLA_GATE_NORMALIZER = 16.0
GLA_CHUNK = 64

SSD_HEAD_DIM = 64
SSD_HEADS = SSD_WIDTH // SSD_HEAD_DIM
SSD_GROUPS = 8
SSD_STATE = 128
SSD_CONV = 4
SSD_CHUNK = 128
SSD_CONV_CH = SSD_WIDTH + 2 * SSD_GROUPS * SSD_STATE

X_HEADS = 4
X_HEAD_DIM = 128
X_WIDTH = X_HEADS * X_HEAD_DIM

FFN_HIDDEN = 256 * math.ceil(8 * D_MODEL / 3 / 256)

IN_SPLITS = (SWA_WIDTH, SWA_KV_WIDTH, SWA_KV_WIDTH,
             GLA_KEY_WIDTH, GLA_KEY_WIDTH, GLA_WIDTH, GLA_WIDTH, GLA_GATE_RANK,
             SSD_WIDTH, SSD_CONV_CH, SSD_HEADS)
IN_DIM = sum(IN_SPLITS)

kernel_name = "hymba_swa_gla_ssd_xmem_trunk"


def rms_norm(x, g):
    xf = x.astype(jnp.float32)
    y = xf * lax.rsqrt(jnp.mean(xf * xf, axis=-1, keepdims=True) + EPS)
    return (y * g.astype(jnp.float32)).astype(x.dtype)


def t5_bucket(dist):
    n = jnp.maximum(dist, 0)
    max_exact = REL_BUCKETS // 2
    nf = jnp.maximum(n, 1).astype(jnp.float32)
    large = max_exact + (jnp.log(nf / max_exact) / math.log(REL_MAX_DIST / max_exact)
                         * (REL_BUCKETS - max_exact)).astype(jnp.int32)
    large = jnp.minimum(large, REL_BUCKETS - 1)
    return jnp.where(n < max_exact, n, large)


def swa_band_bias(rel_bias, n_blocks):
    i = jnp.arange(SWA_BLOCK, dtype=jnp.int32)[:, None]
    j = jnp.arange(2 * SWA_BLOCK, dtype=jnp.int32)[None, :]
    dist = i + SWA_BLOCK - j
    bias = jnp.transpose(rel_bias[t5_bucket(dist)], (2, 0, 1))
    n = jnp.arange(n_blocks, dtype=jnp.int32)[:, None, None]
    k_abs = n * SWA_BLOCK - SWA_BLOCK + j[None]
    valid = (dist[None] >= 0) & (dist[None] < SWA_WINDOW) & (k_abs >= 0)
    return bias, valid


def swa_mixer(q, k, v, q_gain, k_gain, sinks, out_gain, band_bias, band_valid):
    B_, T, H, dh = q.shape
    nb = T // SWA_BLOCK
    G = H // SWA_KV_HEADS
    q = rms_norm(q, q_gain)
    k = rms_norm(k, k_gain)
    qb = q.reshape(B_, nb, SWA_BLOCK, SWA_KV_HEADS, G, dh)
    pad = ((0, 0), (SWA_BLOCK, 0), (0, 0), (0, 0))
    kb = jnp.pad(k, pad).reshape(B_, nb + 1, SWA_BLOCK, SWA_KV_HEADS, dh)
    vb = jnp.pad(v, pad).reshape(B_, nb + 1, SWA_BLOCK, SWA_KV_HEADS, dh)
    kband = jnp.concatenate([kb[:, :-1], kb[:, 1:]], axis=2)
    vband = jnp.concatenate([vb[:, :-1], vb[:, 1:]], axis=2)
    s = jnp.einsum('bnihgd,bnjhd->bhgnij', qb, kband,
                   preferred_element_type=jnp.float32) * (dh ** -0.5)
    bias = band_bias.astype(jnp.float32).reshape(SWA_KV_HEADS, G, SWA_BLOCK, 2 * SWA_BLOCK)
    s = jnp.where(band_valid[None, None, None], s + bias[None, :, :, None], -jnp.inf)
    sink = sinks.astype(jnp.float32).reshape(SWA_KV_HEADS, G)[None, :, :, None, None, None]
    m = jnp.maximum(jnp.max(s, axis=-1, keepdims=True), sink)
    p = jnp.exp(s - m)
    p = p / (jnp.sum(p, axis=-1, keepdims=True) + jnp.exp(sink - m))
    o = jnp.einsum('bhgnij,bnjhd->bnihgd', p, vband.astype(jnp.float32))
    o = o.reshape(B_, T, H * dh)
    return rms_norm(o, out_gain)


def gla_mixer(q, k, v, r, g_low, w_gk_up, b_gk_up, norm_gain):
    B_, T, _ = q.shape
    H, dk, dv, C = GLA_HEADS, GLA_KEY_DIM, GLA_VAL_DIM, GLA_CHUNK
    N = T // C
    g = jax.nn.log_sigmoid((g_low @ w_gk_up + b_gk_up).astype(jnp.float32)) / GLA_GATE_NORMALIZER
    qf = q.astype(jnp.float32).reshape(B_, N, C, H, dk) * (dk ** -0.5)
    kf = k.astype(jnp.float32).reshape(B_, N, C, H, dk)
    vf = v.astype(jnp.float32).reshape(B_, N, C, H, dv)
    bcum = jnp.cumsum(g.reshape(B_, N, C, H, dk), axis=2)
    q_dec = qf * jnp.exp(bcum)
    att = jnp.einsum('bnihd,bnjhd->bnhij', q_dec, kf * jnp.exp(-bcum))
    causal = jnp.tril(jnp.ones((C, C), dtype=bool))
    att = jnp.where(causal, att, 0.0)
    o_intra = jnp.einsum('bnhij,bnjhv->bnihv', att, vf)
    blast = bcum[:, :, -1]
    dS = jnp.einsum('bnjhd,bnjhv->bnhdv', kf * jnp.exp(blast[:, :, None] - bcum), vf)

    def step(S, inp):
        dec, ds = inp
        return dec[..., None] * S + ds, S

    S0 = jnp.zeros((B_, H, dk, dv), jnp.float32)
    _, S_prev = lax.scan(step, S0, (jnp.moveaxis(jnp.exp(blast), 1, 0), jnp.moveaxis(dS, 1, 0)))
    S_prev = jnp.moveaxis(S_prev, 0, 1)
    o_inter = jnp.einsum('bnihd,bnhdv->bnihv', q_dec, S_prev)
    o = (o_intra + o_inter).reshape(B_, T, H, dv)
    o = rms_norm(o, norm_gain) * jax.nn.silu(r.astype(jnp.float32).reshape(B_, T, H, dv))
    return o.reshape(B_, T, H * dv)


def ssd_mixer(z, xbc, dt_raw, conv_w, conv_b, dt_bias, a_log, d_skip, norm_gain):
    B_, T, _ = xbc.shape
    G, Hg, P, Nst, L = SSD_GROUPS, SSD_HEADS // SSD_GROUPS, SSD_HEAD_DIM, SSD_STATE, SSD_CHUNK
    nc = T // L
    xbc = lax.conv_general_dilated(xbc, conv_w.astype(xbc.dtype), window_strides=(1,),
                                   padding=[(SSD_CONV - 1, 0)],
                                   dimension_numbers=('NWC', 'WIO', 'NWC'),
                                   feature_group_count=SSD_CONV_CH) + conv_b
    xbc = jax.nn.silu(xbc)
    xs, Bm, Cm = jnp.split(xbc, [SSD_WIDTH, SSD_WIDTH + G * Nst], axis=-1)
    x = xs.astype(jnp.float32).reshape(B_, nc, L, G, Hg, P)
    Bm = Bm.astype(jnp.float32).reshape(B_, nc, L, G, Nst)
    Cm = Cm.astype(jnp.float32).reshape(B_, nc, L, G, Nst)
    dt = jax.nn.softplus(dt_raw.astype(jnp.float32) + dt_bias.astype(jnp.float32))
    dt = dt.reshape(B_, nc, L, G, Hg)
    A = -jnp.exp(a_log.astype(jnp.float32)).reshape(G, Hg)
    a_cum = jnp.cumsum(dt * A, axis=2)
    xd = x * dt[..., None]
    diff = a_cum[:, :, :, None] - a_cum[:, :, None, :]
    causal = jnp.tril(jnp.ones((L, L), dtype=bool))[:, :, None, None]
    decay = jnp.exp(jnp.where(causal, diff, -jnp.inf))
    cb = jnp.einsum('bclgn,bcsgn->bclsg', Cm, Bm)
    y_diag = jnp.einsum('bclsgk,bcsgkp->bclgkp', cb[..., None] * decay, xd)
    decay_s = jnp.exp(a_cum[:, :, -1:] - a_cum)
    states = jnp.einsum('bclgn,bclgkp->bcgkpn', Bm, decay_s[..., None] * xd)
    chunk_decay = jnp.exp(a_cum[:, :, -1])

    def step(S, inp):
        dec, st = inp
        return dec[..., None, None] * S + st, S

    S0 = jnp.zeros((B_, G, Hg, P, Nst), jnp.float32)
    _, S_prev = lax.scan(step, S0, (jnp.moveaxis(chunk_decay, 1, 0), jnp.moveaxis(states, 1, 0)))
    S_prev = jnp.moveaxis(S_prev, 0, 1)
    y_off = jnp.einsum('bclgn,bcgkpn->bclgkp', Cm, S_prev) * jnp.exp(a_cum)[..., None]
    y = y_diag + y_off + x * d_skip.astype(jnp.float32).reshape(G, Hg)[:, :, None]
    y = y.reshape(B_, T, SSD_WIDTH) * jax.nn.silu(z.astype(jnp.float32))
    y = rms_norm(y.reshape(B_, T, G, SSD_WIDTH // G), norm_gain.reshape(G, SSD_WIDTH // G))
    return y.reshape(B_, T, SSD_WIDTH)


def cross_attention(hn, memn, w_q, w_k, w_v, w_o, q_gain, k_gain):
    B_, T, _ = hn.shape
    M = memn.shape[1]
    q = rms_norm((hn @ w_q).reshape(B_, T, X_HEADS, X_HEAD_DIM), q_gain)
    k = rms_norm((memn @ w_k).reshape(B_, M, X_HEADS, X_HEAD_DIM), k_gain)
    v = (memn @ w_v).reshape(B_, M, X_HEADS, X_HEAD_DIM)
    s = jnp.einsum('bthd,bmhd->bhtm', q, k, preferred_element_type=jnp.float32) * (X_HEAD_DIM ** -0.5)
    p = jax.nn.softmax(s, axis=-1)
    o = jnp.einsum('bhtm,bmhd->bthd', p, v.astype(jnp.float32)).astype(hn.dtype)
    return o.reshape(B_, T, X_WIDTH) @ w_o


def setup_inputs(seed: int = 0) -> dict:
    key = jax.random.key(seed)
    ks = iter(jax.random.split(key, 48))
    f32 = jnp.float32
    L = DEPTH

    def nrm(shape, scale):
        return scale * jax.random.normal(next(ks), shape, f32)

    def gain(shape):
        return 1.0 + 0.05 * jax.random.normal(next(ks), shape, f32)

    x = nrm((BATCH, SEQ, D_MODEL), 1.0)
    mem = nrm((BATCH, MEM_LEN, D_MODEL), 1.0)
    rel_bias = nrm((REL_BUCKETS, SWA_HEADS), 0.5)
    ln_mix = gain((L, D_MODEL))
    w_in = nrm((L, D_MODEL, IN_DIM), D_MODEL ** -0.5)
    swa_q_gain = gain((L, SWA_HEAD_DIM))
    swa_k_gain = gain((L, SWA_HEAD_DIM))
    swa_sinks = nrm((L, SWA_HEADS), 0.5)
    swa_out_gain = gain((L, SWA_WIDTH))
    gla_w_gk_up = nrm((L, GLA_GATE_RANK, GLA_KEY_WIDTH), GLA_GATE_RANK ** -0.5)
    gla_b_gk_up = nrm((L, GLA_KEY_WIDTH), 0.1)
    gla_norm_gain = gain((L, GLA_VAL_DIM))
    ssd_conv_w = nrm((L, SSD_CONV, 1, SSD_CONV_CH), SSD_CONV ** -0.5)
    ssd_conv_b = nrm((L, SSD_CONV_CH), 0.02)
    dt0 = jnp.exp(jax.random.uniform(next(ks), (L, SSD_HEADS), f32, math.log(1e-3), math.log(1e-1)))
    ssd_dt_bias = dt0 + jnp.log(-jnp.expm1(-dt0))
    ssd_a_log = jnp.log(jax.random.uniform(next(ks), (L, SSD_HEADS), f32, 1.0, 16.0))
    ssd_d = gain((L, SSD_HEADS))
    ssd_norm_gain = gain((L, SSD_WIDTH))
    w_mix_out = nrm((L, MIX_WIDTH, D_MODEL), MIX_WIDTH ** -0.5)
    ln_x = gain((L, D_MODEL))
    ln_mem = gain((L, D_MODEL))
    x_w_q = nrm((L, D_MODEL, X_WIDTH), D_MODEL ** -0.5)
    x_w_k = nrm((L, D_MODEL, X_WIDTH), D_MODEL ** -0.5)
    x_w_v = nrm((L, D_MODEL, X_WIDTH), D_MODEL ** -0.5)
    x_w_o = nrm((L, X_WIDTH, D_MODEL), X_WIDTH ** -0.5)
    x_q_gain = gain((L, X_HEAD_DIM))
    x_k_gain = gain((L, X_HEAD_DIM))
    ln_ffn = gain((L, D_MODEL))
    ffn_w_gate = nrm((L, D_MODEL, FFN_HIDDEN), D_MODEL ** -0.5)
    ffn_w_up = nrm((L, D_MODEL, FFN_HIDDEN), D_MODEL ** -0.5)
    ffn_w_down = nrm((L, FFN_HIDDEN, D_MODEL), FFN_HIDDEN ** -0.5)
    return {"x": x, "mem": mem, "rel_bias": rel_bias, "ln_mix": ln_mix, "w_in": w_in,
            "swa_q_gain": swa_q_gain, "swa_k_gain": swa_k_gain, "swa_sinks": swa_sinks,
            "swa_out_gain": swa_out_gain, "gla_w_gk_up": gla_w_gk_up, "gla_b_gk_up": gla_b_gk_up,
            "gla_norm_gain": gla_norm_gain, "ssd_conv_w": ssd_conv_w, "ssd_conv_b": ssd_conv_b,
            "ssd_dt_bias": ssd_dt_bias, "ssd_a_log": ssd_a_log, "ssd_d": ssd_d,
            "ssd_norm_gain": ssd_norm_gain, "w_mix_out": w_mix_out, "ln_x": ln_x, "ln_mem": ln_mem,
            "x_w_q": x_w_q, "x_w_k": x_w_k, "x_w_v": x_w_v, "x_w_o": x_w_o,
            "x_q_gain": x_q_gain, "x_k_gain": x_k_gain, "ln_ffn": ln_ffn,
            "ffn_w_gate": ffn_w_gate, "ffn_w_up": ffn_w_up, "ffn_w_down": ffn_w_down}


def reference(x, mem, rel_bias, ln_mix, w_in, swa_q_gain, swa_k_gain, swa_sinks, swa_out_gain,
              gla_w_gk_up, gla_b_gk_up, gla_norm_gain, ssd_conv_w, ssd_conv_b, ssd_dt_bias,
              ssd_a_log, ssd_d, ssd_norm_gain, w_mix_out, ln_x, ln_mem, x_w_q, x_w_k, x_w_v,
              x_w_o, x_q_gain, x_k_gain, ln_ffn, ffn_w_gate, ffn_w_up, ffn_w_down):
    B_, T, _ = x.shape
    offsets = np.cumsum(IN_SPLITS)[:-1].tolist()
    band_bias, band_valid = swa_band_bias(rel_bias, T // SWA_BLOCK)
    h = x
    for l in range(DEPTH):
        hn = rms_norm(h, ln_mix[l])
        proj = hn @ w_in[l]
        (a_q, a_k, a_v, b_q, b_k, b_v, b_r, b_glow,
         c_z, c_xbc, c_dt) = jnp.split(proj, offsets, axis=-1)
        y_a = swa_mixer(a_q.reshape(B_, T, SWA_HEADS, SWA_HEAD_DIM),
                        a_k.reshape(B_, T, SWA_KV_HEADS, SWA_HEAD_DIM),
                        a_v.reshape(B_, T, SWA_KV_HEADS, SWA_HEAD_DIM),
                        swa_q_gain[l], swa_k_gain[l], swa_sinks[l], swa_out_gain[l],
                        band_bias, band_valid)
        y_b = gla_mixer(b_q, b_k, b_v, b_r, b_glow, gla_w_gk_up[l], gla_b_gk_up[l], gla_norm_gain[l])
        y_c = ssd_mixer(c_z, c_xbc, c_dt, ssd_conv_w[l], ssd_conv_b[l], ssd_dt_bias[l],
                        ssd_a_log[l], ssd_d[l], ssd_norm_gain[l])
        y = jnp.concatenate([y_a.astype(h.dtype), y_b.astype(h.dtype), y_c.astype(h.dtype)], axis=-1)
        h = h + y @ w_mix_out[l]
        h = h + cross_attention(rms_norm(h, ln_x[l]), rms_norm(mem, ln_mem[l]),
                                x_w_q[l], x_w_k[l], x_w_v[l], x_w_o[l], x_q_gain[l], x_k_gain[l])
        hn = rms_norm(h, ln_ffn[l])
        h = h + (jax.nn.silu(hn @ ffn_w_gate[l]) * (hn @ ffn_w_up[l])) @ ffn_w_down[l]
    return h
```

```python
import functools
import math

import numpy as np
import jax
import jax.numpy as jnp
from jax import lax
from jax.experimental import pallas as pl
from jax.experimental.pallas import tpu as pltpu

F32 = jnp.float32
BF16 = jnp.bfloat16

D_MODEL = 4096
DEPTH = 2
EPS = 1e-6
SWA_WIDTH = 1024
SWA_HEAD_DIM = 64
SWA_HEADS = 16
SWA_KV_HEADS = 2
SWA_GROUP = SWA_HEADS // SWA_KV_HEADS
SWA_KV_WIDTH = SWA_KV_HEADS * SWA_HEAD_DIM
SWA_WINDOW = 128
SWA_BLOCK = 128
REL_BUCKETS = 32
REL_MAX_DIST = 128
GLA_WIDTH = 1024
GLA_HEADS = 4
GLA_VAL_DIM = 256
GLA_KEY_DIM = 128
GLA_KEY_WIDTH = GLA_HEADS * GLA_KEY_DIM
GLA_GATE_RANK = 16
GLA_GATE_NORMALIZER = 16.0
GLA_CHUNK = 64
SSD_WIDTH = 2048
SSD_HEAD_DIM = 64
SSD_HEADS = 32
SSD_GROUPS = 8
SSD_HEADS_PER_GROUP = SSD_HEADS // SSD_GROUPS
SSD_STATE = 128
SSD_CONV = 4
SSD_CHUNK = 128
SSD_BC_WIDTH = SSD_GROUPS * SSD_STATE
SSD_CONV_CH = SSD_WIDTH + 2 * SSD_BC_WIDTH
SSD_GROUP_WIDTH = SSD_WIDTH // SSD_GROUPS
X_HEADS = 4
X_HEAD_DIM = 128
X_WIDTH = X_HEADS * X_HEAD_DIM
FFN_HIDDEN = 11008

V7X_LANES = 128
V7X_VMEM_BYTES = 64 * 1024 * 1024
VMEM_REQUEST_CAP = (V7X_VMEM_BYTES * 7) // 8
FFN_TILE = 1024
FFN_HIDDEN_PAD = FFN_TILE * math.ceil(FFN_HIDDEN / FFN_TILE)
GLA_TIME_BLOCK = 256


def _params(semantics, block_bytes):
    limit = min(VMEM_REQUEST_CAP, 2 * block_bytes + 16 * 1024 * 1024)
    return pltpu.CompilerParams(dimension_semantics=semantics, vmem_limit_bytes=int(limit))


def _nbytes(shape, dtype):
    return int(np.prod(shape)) * jnp.dtype(dtype).itemsize


def _split_bf16(x, terms):
    parts = []
    r = x
    for t in range(terms):
        p = r.astype(BF16)
        parts.append(p)
        if t + 1 < terms:
            r = r - p.astype(F32)
    return parts


def _dot(a, b):
    return jnp.dot(a, b, preferred_element_type=F32)


def _dot_nt(a, b):
    return lax.dot_general(a, b, (((1,), (1,)), ((), ())), preferred_element_type=F32)


def _dot_tn(a, b):
    return lax.dot_general(a, b, (((0,), (0,)), ((), ())), preferred_element_type=F32)


def _dot_exact_lhs(a_bf16, x, terms):
    acc = None
    for p in _split_bf16(x, terms):
        d = _dot(a_bf16, p)
        acc = d if acc is None else acc + d
    return acc


def _dot_exact_rhs(x, b_bf16, terms):
    acc = None
    for p in _split_bf16(x, terms):
        d = _dot(p, b_bf16)
        acc = d if acc is None else acc + d
    return acc


def _silu(x):
    return x / (1.0 + jnp.exp(-x))


def _softplus(x):
    return jnp.maximum(x, 0.0) + jnp.log1p(jnp.exp(-jnp.abs(x)))


def _rms(x, gain):
    return x * lax.rsqrt(jnp.mean(x * x, axis=-1, keepdims=True) + EPS) * gain


def _rmsnorm_body(x_ref, g_ref, o_ref):
    o_ref[...] = _rms(x_ref[...], g_ref[...]).astype(o_ref.dtype)


def rmsnorm(x, gain, tm=256):
    m, d = x.shape
    tm = min(tm, m)
    return pl.pallas_call(
        _rmsnorm_body,
        grid=(m // tm,),
        in_specs=[pl.BlockSpec((tm, d), lambda i: (i, 0)),
                  pl.BlockSpec((1, d), lambda i: (0, 0))],
        out_specs=pl.BlockSpec((tm, d), lambda i: (i, 0)),
        out_shape=jax.ShapeDtypeStruct((m, d), BF16),
        compiler_params=_params(("parallel",), _nbytes((tm, d), F32) + _nbytes((tm, d), BF16)),
        name="rmsnorm",
    )(x, gain.reshape(1, d))


def _mm_body(*refs, nk, has_res):
    a_ref, w_ref = refs[0], refs[1]
    r_ref = refs[2] if has_res else None
    o_ref = refs[2 + has_res]
    part = _dot(a_ref[...], w_ref[...])
    if nk == 1:
        if has_res:
            part = part + r_ref[...]
        o_ref[...] = part.astype(o_ref.dtype)
        return
    k = pl.program_id(2)

    @pl.when(k == 0)
    def _():
        o_ref[...] = (part + r_ref[...]) if has_res else part

    @pl.when(k > 0)
    def _():
        o_ref[...] += part


def matmul(a, w, *, out_dtype, res=None, tm, tn, tk=None, name="matmul"):
    m, kdim = a.shape
    n = w.shape[1]
    tm, tn = min(tm, m), min(tn, n)
    tk = kdim if tk is None else tk
    nk = kdim // tk
    assert m % tm == 0 and n % tn == 0 and kdim % tk == 0
    assert nk == 1 or out_dtype == F32
    has_res = res is not None
    in_specs = [pl.BlockSpec((tm, tk), lambda i, j, k: (i, k)),
                pl.BlockSpec((tk, tn), lambda i, j, k: (k, j))]
    args = [a, w]
    nbytes = _nbytes((tm, tk), BF16) + _nbytes((tk, tn), BF16) + _nbytes((tm, tn), out_dtype) + _nbytes((tm, tn), F32)
    if has_res:
        in_specs.append(pl.BlockSpec((tm, tn), lambda i, j, k: (i, j)))
        args.append(res)
        nbytes += _nbytes((tm, tn), F32)
    return pl.pallas_call(
        functools.partial(_mm_body, nk=nk, has_res=has_res),
        grid=(m // tm, n // tn, nk),
        in_specs=in_specs,
        out_specs=pl.BlockSpec((tm, tn), lambda i, j, k: (i, j)),
        out_shape=jax.ShapeDtypeStruct((m, n), out_dtype),
        compiler_params=_params(("parallel", "parallel", "arbitrary"), nbytes),
        name=name,
    )(*args)


def _mixout_body(ya_ref, yb_ref, yc_ref, w_ref, r_ref, o_ref):
    acc = r_ref[...]
    acc = acc + _dot(ya_ref[...], w_ref[0:SWA_WIDTH, :])
    acc = acc + _dot(yb_ref[...], w_ref[SWA_WIDTH:SWA_WIDTH + GLA_WIDTH, :])
    acc = acc + _dot(yc_ref[...], w_ref[SWA_WIDTH + GLA_WIDTH:, :])
    o_ref[...] = acc


def mix_out(ya, yb, yc, w, res, tm=1024, tn=512):
    m = ya.shape[0]
    kdim, n = w.shape
    tm = min(tm, m)
    nbytes = (_nbytes((tm, kdim), BF16) + _nbytes((kdim, tn), BF16) + 3 * _nbytes((tm, tn), F32))
    return pl.pallas_call(
        _mixout_body,
        grid=(m // tm, n // tn),
        in_specs=[pl.BlockSpec((tm, SWA_WIDTH), lambda i, j: (i, 0)),
                  pl.BlockSpec((tm, GLA_WIDTH), lambda i, j: (i, 0)),
                  pl.BlockSpec((tm, SSD_WIDTH), lambda i, j: (i, 0)),
                  pl.BlockSpec((kdim, tn), lambda i, j: (0, j)),
                  pl.BlockSpec((tm, tn), lambda i, j: (i, j))],
        out_specs=pl.BlockSpec((tm, tn), lambda i, j: (i, j)),
        out_shape=jax.ShapeDtypeStruct((m, n), F32),
        compiler_params=_params(("parallel", "parallel"), nbytes),
        name="mix_out",
    )(ya, yb, yc, w, res)


def _gateup_body(a_ref, wg_ref, wu_ref, o_ref):
    a = a_ref[...]
    g = _dot(a, wg_ref[...])
    u = _dot(a, wu_ref[...])
    o_ref[...] = (_silu(g) * u).astype(o_ref.dtype)


def ffn_gate_up(a, wg, wu, tm=1024, tn=512):
    m, kdim = a.shape
    n = wg.shape[1]
    tm = min(tm, m)
    nbytes = (_nbytes((tm, kdim), BF16) + 2 * _nbytes((kdim, tn), BF16) + _nbytes((tm, tn), BF16)
              + 3 * _nbytes((tm, tn), F32))
    return pl.pallas_call(
        _gateup_body,
        grid=(m // tm, n // tn),
        in_specs=[pl.BlockSpec((tm, kdim), lambda i, j: (i, 0)),
                  pl.BlockSpec((kdim, tn), lambda i, j: (0, j)),
                  pl.BlockSpec((kdim, tn), lambda i, j: (0, j))],
        out_specs=pl.BlockSpec((tm, tn), lambda i, j: (i, j)),
        out_shape=jax.ShapeDtypeStruct((m, n), BF16),
        compiler_params=_params(("parallel", "parallel"), nbytes),
        name="ffn_gate_up",
    )(a, wg, wu)


def _swa_body(sink_ref, q_ref, kvc_ref, kvp_ref, bias_ref, qg_ref, kg_ref, og_ref, o_ref, acc_ref):
    n = pl.program_id(1)
    hd = SWA_HEAD_DIM
    kband = jnp.concatenate([kvp_ref[:, :SWA_KV_WIDTH], kvc_ref[:, :SWA_KV_WIDTH]], axis=0).astype(F32)
    vband = jnp.concatenate([kvp_ref[:, SWA_KV_WIDTH:], kvc_ref[:, SWA_KV_WIDTH:]], axis=0)
    col = lax.broadcasted_iota(jnp.int32, (SWA_BLOCK, 2 * SWA_BLOCK), 1)
    key_exists = (col >= SWA_BLOCK) | (n > 0)
    kn = [_rms(kband[:, hd * j:hd * (j + 1)], kg_ref[...]).astype(BF16) for j in range(SWA_KV_HEADS)]
    for h in range(SWA_HEADS):
        j = h // SWA_GROUP
        qh = _rms(q_ref[:, hd * h:hd * (h + 1)].astype(F32), qg_ref[...])
        s = _dot_nt(qh.astype(BF16), kn[j]) * (hd ** -0.5) + bias_ref[h]
        s = jnp.where(key_exists, s, -jnp.inf)
        sink = sink_ref[h]
        mx = jnp.maximum(jnp.max(s, axis=-1, keepdims=True), sink)
        p = jnp.exp(s - mx)
        denom = jnp.sum(p, axis=-1, keepdims=True) + jnp.exp(sink - mx)
        oh = _dot(p.astype(BF16), vband[:, hd * j:hd * (j + 1)]) / denom
        acc_ref[:, hd * h:hd * (h + 1)] = oh
    o_ref[...] = _rms(acc_ref[...], og_ref[...]).astype(o_ref.dtype)


def swa_mixer(qkv, bias, sinks, q_gain, k_gain, out_gain, batch, seq):
    nb = seq // SWA_BLOCK
    kv_blk = SWA_WIDTH // (2 * SWA_KV_WIDTH)
    nbytes = (_nbytes((SWA_BLOCK, SWA_WIDTH), BF16) * 2 + 2 * _nbytes((SWA_BLOCK, 2 * SWA_KV_WIDTH), BF16)
              + _nbytes(bias.shape, F32) + _nbytes((SWA_BLOCK, SWA_WIDTH), F32))
    return pl.pallas_call(
        _swa_body,
        grid=(batch, nb),
        in_specs=[pl.BlockSpec(memory_space=pltpu.SMEM),
                  pl.BlockSpec((SWA_BLOCK, SWA_WIDTH), lambda b, n: (b * nb + n, 0)),
                  pl.BlockSpec((SWA_BLOCK, 2 * SWA_KV_WIDTH), lambda b, n: (b * nb + n, kv_blk)),
                  pl.BlockSpec((SWA_BLOCK, 2 * SWA_KV_WIDTH),
                               lambda b, n: (b * nb + jnp.maximum(n - 1, 0), kv_blk)),
                  pl.BlockSpec(bias.shape, lambda b, n: (0, 0, 0)),
                  pl.BlockSpec((1, SWA_HEAD_DIM), lambda b, n: (0, 0)),
                  pl.BlockSpec((1, SWA_HEAD_DIM), lambda b, n: (0, 0)),
                  pl.BlockSpec((1, SWA_WIDTH), lambda b, n: (0, 0))],
        out_specs=pl.BlockSpec((SWA_BLOCK, SWA_WIDTH), lambda b, n: (b * nb + n, 0)),
        out_shape=jax.ShapeDtypeStruct((batch * seq, SWA_WIDTH), BF16),
        scratch_shapes=[pltpu.VMEM((SWA_BLOCK, SWA_WIDTH), F32)],
        compiler_params=_params(("parallel", "parallel"), nbytes),
        name="swa_mixer",
    )(sinks, qkv, qkv, qkv, bias, q_gain.reshape(1, -1), k_gain.reshape(1, -1), out_gain.reshape(1, -1))


def _t5_bucket(dist):
    n = jnp.maximum(dist, 0)
    max_exact = REL_BUCKETS // 2
    nf = jnp.maximum(n, 1).astype(F32)
    large = max_exact + (jnp.log(nf / max_exact) / math.log(REL_MAX_DIST / max_exact)
                         * (REL_BUCKETS - max_exact)).astype(jnp.int32)
    large = jnp.minimum(large, REL_BUCKETS - 1)
    return jnp.where(n < max_exact, n, large)


def swa_band_bias(rel_bias):
    i = jnp.arange(SWA_BLOCK, dtype=jnp.int32)[:, None]
    j = jnp.arange(2 * SWA_BLOCK, dtype=jnp.int32)[None, :]
    dist = i + SWA_BLOCK - j
    bias = jnp.transpose(rel_bias[_t5_bucket(dist)], (2, 0, 1)).astype(F32)
    in_window = (dist >= 0) & (dist < SWA_WINDOW)
    return jnp.where(in_window[None], bias, -jnp.inf)


def _gla_body(x_ref, gl_ref, wup_ref, bup_ref, gain_ref, o_ref, state_ref):
    tb = x_ref.shape[0]
    dk, dv, c = GLA_KEY_DIM, GLA_VAL_DIM, GLA_CHUNK

    @pl.when(pl.program_id(1) == 0)
    def _():
        state_ref[...] = jnp.zeros_like(state_ref)

    gl_hi, gl_lo = _split_bf16(gl_ref[...], 2)
    w_hi, w_lo = _split_bf16(wup_ref[...], 2)
    pre = _dot(gl_hi, w_hi) + _dot(gl_hi, w_lo) + _dot(gl_lo, w_hi) + bup_ref[...]
    g = (jnp.minimum(pre, 0.0) - jnp.log1p(jnp.exp(-jnp.abs(pre)))) * (1.0 / GLA_GATE_NORMALIZER)

    row = lax.broadcasted_iota(jnp.int32, (tb, tb), 0)
    colm = lax.broadcasted_iota(jnp.int32, (tb, tb), 1)
    same_chunk_lower = ((row // c) == (colm // c)) & (colm <= row)
    bcum_all = _dot_exact_lhs(same_chunk_lower.astype(BF16), g, 3)

    ri = lax.broadcasted_iota(jnp.int32, (c, c), 0)
    ci = lax.broadcasted_iota(jnp.int32, (c, c), 1)
    causal = ci <= ri
    for ch in range(tb // c):
        r0 = ch * c
        for h in range(GLA_HEADS):
            bcum = bcum_all[r0:r0 + c, dk * h:dk * (h + 1)]
            blast = bcum[c - 1:c, :]
            q = x_ref[r0:r0 + c, dk * h:dk * (h + 1)].astype(F32)
            k = x_ref[r0:r0 + c, GLA_KEY_WIDTH + dk * h:GLA_KEY_WIDTH + dk * (h + 1)].astype(F32)
            v = x_ref[r0:r0 + c, 2 * GLA_KEY_WIDTH + dv * h:2 * GLA_KEY_WIDTH + dv * (h + 1)]
            r = x_ref[r0:r0 + c, 2 * GLA_KEY_WIDTH + GLA_WIDTH + dv * h:
                      2 * GLA_KEY_WIDTH + GLA_WIDTH + dv * (h + 1)].astype(F32)
            qd = (q * (dk ** -0.5) * jnp.exp(bcum)).astype(BF16)
            kd = (k * jnp.exp(-bcum)).astype(BF16)
            kl = (k * jnp.exp(blast - bcum)).astype(BF16)
            att = jnp.where(causal, _dot_nt(qd, kd), 0.0)
            st = state_ref[h]
            o = _dot(att.astype(BF16), v) + _dot_nt(qd, st.astype(BF16))
            state_ref[h] = st * jnp.exp(blast) + _dot_tn(v, kl)
            o = _rms(o, gain_ref[...]) * _silu(r)
            o_ref[r0:r0 + c, dv * h:dv * (h + 1)] = o.astype(o_ref.dtype)


def gla_mixer(x, small, w_up_pad, b_up, norm_gain, batch, seq):
    tb = min(GLA_TIME_BLOCK, seq)
    nt = seq // tb
    width = x.shape[1]
    nbytes = (_nbytes((tb, width), BF16) + _nbytes((tb, V7X_LANES), F32) + _nbytes(w_up_pad.shape, F32)
              + _nbytes((tb, GLA_WIDTH), BF16) + _nbytes((GLA_HEADS, GLA_VAL_DIM, GLA_KEY_DIM), F32)
              + 8 * _nbytes((tb, GLA_KEY_WIDTH), F32))
    return pl.pallas_call(
        _gla_body,
        grid=(batch, nt),
        in_specs=[pl.BlockSpec((tb, width), lambda b, t: (b * nt + t, 0)),
                  pl.BlockSpec((tb, V7X_LANES), lambda b, t: (b * nt + t, 0)),
                  pl.BlockSpec(w_up_pad.shape, lambda b, t: (0, 0)),
                  pl.BlockSpec((1, GLA_KEY_WIDTH), lambda b, t: (0, 0)),
                  pl.BlockSpec((1, GLA_VAL_DIM), lambda b, t: (0, 0))],
        out_specs=pl.BlockSpec((tb, GLA_WIDTH), lambda b, t: (b * nt + t, 0)),
        out_shape=jax.ShapeDtypeStruct((batch * seq, GLA_WIDTH), BF16),
        scratch_shapes=[pltpu.VMEM((GLA_HEADS, GLA_VAL_DIM, GLA_KEY_DIM), F32)],
        compiler_params=_params(("parallel", "arbitrary"), nbytes),
        name="gla_mixer",
    )(x, small, w_up_pad, b_up.reshape(1, -1), norm_gain.reshape(1, -1))


def _ssd_body(x_ref, dt_ref, cw_ref, cb_ref, dtb_ref, alog_ref, dexp_ref, gain_ref, expand_ref,
              o_ref, state_ref, tail_ref, y_ref):
    L, P, N = SSD_CHUNK, SSD_HEAD_DIM, SSD_STATE
    gw = SSD_GROUP_WIDTH

    @pl.when(pl.program_id(1) == 0)
    def _():
        state_ref[...] = jnp.zeros_like(state_ref)
        tail_ref[...] = jnp.zeros_like(tail_ref)

    xin = x_ref[:, SSD_WIDTH:].astype(F32)
    xext = jnp.concatenate([tail_ref[...], xin], axis=0)
    tail_ref[...] = xin[L - 8:, :]
    conv = cb_ref[...] + cw_ref[SSD_CONV - 1:SSD_CONV, :] * xin
    for j in range(1, SSD_CONV):
        conv = conv + cw_ref[SSD_CONV - 1 - j:SSD_CONV - j, :] * xext[8 - j:8 - j + L, :]
    xbc = _silu(conv)
    xs = xbc[:, :SSD_WIDTH]
    bm = xbc[:, SSD_WIDTH:SSD_WIDTH + SSD_BC_WIDTH].astype(BF16)
    cm = xbc[:, SSD_WIDTH + SSD_BC_WIDTH:].astype(BF16)

    dt = _softplus(dt_ref[...] + dtb_ref[...])
    dta = dt * (-jnp.exp(alog_ref[...]))
    ri = lax.broadcasted_iota(jnp.int32, (L, L), 0)
    ci = lax.broadcasted_iota(jnp.int32, (L, L), 1)
    causal = ci <= ri
    a_cum = _dot_exact_lhs(causal.astype(BF16), dta, 3)
    a_cum_t = a_cum.T
    expand = expand_ref[...]
    a_exp = _dot_exact_rhs(a_cum, expand, 3)
    dt_exp = _dot_exact_rhs(dt, expand, 2)
    a_last = a_exp[L - 1:L, :]
    xd = xs * dt_exp
    xdec = (xd * jnp.exp(a_last - a_exp)).astype(BF16)
    xd16 = xd.astype(BF16)
    out_scale = jnp.exp(a_exp)

    for g in range(SSD_GROUPS):
        bg = bm[:, N * g:N * (g + 1)]
        cg = cm[:, N * g:N * (g + 1)]
        cb = _dot_nt(cg, bg)
        for kk in range(SSD_HEADS_PER_GROUP):
            h = g * SSD_HEADS_PER_GROUP + kk
            diff = a_cum[:, h:h + 1] - a_cum_t[h:h + 1, :]
            m = cb * jnp.exp(jnp.where(causal, diff, -jnp.inf))
            y_ref[:, P * h:P * (h + 1)] = _dot(m.astype(BF16), xd16[:, P * h:P * (h + 1)])
        sg = state_ref[:, gw * g:gw * (g + 1)]
        y_off = _dot(cg, sg.astype(BF16)) * out_scale[:, gw * g:gw * (g + 1)]
        state_ref[:, gw * g:gw * (g + 1)] = (sg * jnp.exp(a_last[:, gw * g:gw * (g + 1)])
                                             + _dot_tn(bg, xdec[:, gw * g:gw * (g + 1)]))
        yg = y_ref[:, gw * g:gw * (g + 1)] + y_off + xs[:, gw * g:gw * (g + 1)] * dexp_ref[:, gw * g:gw * (g + 1)]
        yg = yg * _silu(x_ref[:, gw * g:gw * (g + 1)].astype(F32))
        o_ref[:, gw * g:gw * (g + 1)] = _rms(yg, gain_ref[:, gw * g:gw * (g + 1)]).astype(o_ref.dtype)


def ssd_mixer(x, small, conv_w, conv_b, dt_bias, a_log, d_skip, norm_gain, batch, seq):
    L = SSD_CHUNK
    nc = seq // L
    width = x.shape[1]
    pad = V7X_LANES - SSD_HEADS
    dtb = jnp.pad(dt_bias, (0, pad)).reshape(1, V7X_LANES)
    alog = jnp.pad(a_log, (0, pad)).reshape(1, V7X_LANES)
    dexp = jnp.repeat(d_skip, SSD_HEAD_DIM).reshape(1, SSD_WIDTH)
    expand = (jnp.arange(V7X_LANES)[:, None] == (jnp.arange(SSD_WIDTH)[None, :] // SSD_HEAD_DIM)).astype(BF16)
    nbytes = (_nbytes((L, width), BF16) + _nbytes((L, V7X_LANES), F32) + _nbytes((L, SSD_WIDTH), BF16)
              + _nbytes(expand.shape, BF16) + 2 * _nbytes((SSD_STATE, SSD_WIDTH), F32)
              + 12 * _nbytes((L, SSD_CONV_CH), F32))
    return pl.pallas_call(
        _ssd_body,
        grid=(batch, nc),
        in_specs=[pl.BlockSpec((L, width), lambda b, c: (b * nc + c, 0)),
                  pl.BlockSpec((L, V7X_LANES), lambda b, c: (b * nc + c, 1)),
                  pl.BlockSpec((SSD_CONV, SSD_CONV_CH), lambda b, c: (0, 0)),
                  pl.BlockSpec((1, SSD_CONV_CH), lambda b, c: (0, 0)),
                  pl.BlockSpec((1, V7X_LANES), lambda b, c: (0, 0)),
                  pl.BlockSpec((1, V7X_LANES), lambda b, c: (0, 0)),
                  pl.BlockSpec((1, SSD_WIDTH), lambda b, c: (0, 0)),
                  pl.BlockSpec((1, SSD_WIDTH), lambda b, c: (0, 0)),
                  pl.BlockSpec(expand.shape, lambda b, c: (0, 0))],
        out_specs=pl.BlockSpec((L, SSD_WIDTH), lambda b, c: (b * nc + c, 0)),
        out_shape=jax.ShapeDtypeStruct((batch * seq, SSD_WIDTH), BF16),
        scratch_shapes=[pltpu.VMEM((SSD_STATE, SSD_WIDTH), F32),
                        pltpu.VMEM((8, SSD_CONV_CH), F32),
                        pltpu.VMEM((L, SSD_WIDTH), F32)],
        compiler_params=_params(("parallel", "arbitrary"), nbytes),
        name="ssd_mixer",
    )(x, small, conv_w.reshape(SSD_CONV, SSD_CONV_CH), conv_b.reshape(1, -1), dtb, alog, dexp,
      norm_gain.reshape(1, -1), expand)


def _xattn_body(q_ref, kv_ref, qg_ref, kg_ref, o_ref):
    hd = X_HEAD_DIM
    for h in range(X_HEADS):
        qh = _rms(q_ref[:, hd * h:hd * (h + 1)], qg_ref[...]).astype(BF16)
        kh = _rms(kv_ref[:, hd * h:hd * (h + 1)], kg_ref[...]).astype(BF16)
        vh = kv_ref[:, X_WIDTH + hd * h:X_WIDTH + hd * (h + 1)].astype(BF16)
        s = _dot_nt(qh, kh) * (hd ** -0.5)
        p = jnp.exp(s - jnp.max(s, axis=-1, keepdims=True))
        oh = _dot(p.astype(BF16), vh) / jnp.sum(p, axis=-1, keepdims=True)
        o_ref[:, hd * h:hd * (h + 1)] = oh.astype(o_ref.dtype)


def cross_attention_core(q, kv, q_gain, k_gain, batch, seq, mem_len, tq=512):
    tq = min(tq, seq)
    nq = seq // tq
    nbytes = _nbytes((tq, X_WIDTH), F32) + _nbytes((mem_len, 2 * X_WIDTH), F32) + _nbytes((tq, X_WIDTH), BF16) \
        + 4 * _nbytes((tq, mem_len), F32)
    return pl.pallas_call(
        _xattn_body,
        grid=(batch, nq),
        in_specs=[pl.BlockSpec((tq, X_WIDTH), lambda b, t: (b * nq + t, 0)),
                  pl.BlockSpec((mem_len, 2 * X_WIDTH), lambda b, t: (b, 0)),
                  pl.BlockSpec((1, X_HEAD_DIM), lambda b, t: (0, 0)),
                  pl.BlockSpec((1, X_HEAD_DIM), lambda b, t: (0, 0))],
        out_specs=pl.BlockSpec((tq, X_WIDTH), lambda b, t: (b * nq + t, 0)),
        out_shape=jax.ShapeDtypeStruct((batch * seq, X_WIDTH), BF16),
        compiler_params=_params(("parallel", "parallel"), nbytes),
        name="xattn_core",
    )(q, kv, q_gain.reshape(1, -1), k_gain.reshape(1, -1))


def _layer_weights(w_in, gla_w_gk_up, ffn_w_gate, ffn_w_up, ffn_w_down, x_w_k, x_w_v):
    o = np.cumsum([0, SWA_WIDTH, SWA_KV_WIDTH, SWA_KV_WIDTH, GLA_KEY_WIDTH, GLA_KEY_WIDTH, GLA_WIDTH,
                   GLA_WIDTH, GLA_GATE_RANK, SSD_WIDTH, SSD_CONV_CH, SSD_HEADS]).tolist()
    w16 = w_in.astype(BF16)
    w_swa = w16[:, o[0]:o[3]]
    w_gla = w16[:, o[3]:o[7]]
    w_ssd = w16[:, o[8]:o[10]]
    lane_pad = lambda w: jnp.pad(w, ((0, 0), (0, V7X_LANES - w.shape[1])))
    w_small = jnp.concatenate([lane_pad(w16[:, o[7]:o[8]]), lane_pad(w16[:, o[10]:o[11]])], axis=1)
    w_up_pad = jnp.pad(gla_w_gk_up, ((0, V7X_LANES - GLA_GATE_RANK), (0, 0)))
    fpad = FFN_HIDDEN_PAD - FFN_HIDDEN
    wg = jnp.pad(ffn_w_gate.astype(BF16), ((0, 0), (0, fpad)))
    wu = jnp.pad(ffn_w_up.astype(BF16), ((0, 0), (0, fpad)))
    wd = jnp.pad(ffn_w_down.astype(BF16), ((0, fpad), (0, 0)))
    w_kv = jnp.concatenate([x_w_k, x_w_v], axis=1).astype(BF16)
    return w_swa, w_gla, w_ssd, w_small, w_up_pad, wg, wu, wd, w_kv


def kernel(x, mem, rel_bias, ln_mix, w_in, swa_q_gain, swa_k_gain, swa_sinks, swa_out_gain, gla_w_gk_up, gla_b_gk_up, gla_norm_gain, ssd_conv_w, ssd_conv_b, ssd_dt_bias, ssd_a_log, ssd_d, ssd_norm_gain, w_mix_out, ln_x, ln_mem, x_w_q, x_w_k, x_w_v, x_w_o, x_q_gain, x_k_gain, ln_ffn, ffn_w_gate, ffn_w_up, ffn_w_down):
    batch, seq, d = x.shape
    mem_len = mem.shape[1]
    m = batch * seq
    band_bias = swa_band_bias(rel_bias)
    h = x.reshape(m, d)
    mem2 = mem.reshape(batch * mem_len, d)
    for l in range(DEPTH):
        (w_swa, w_gla, w_ssd, w_small, w_up_pad, wg, wu, wd, w_kv) = _layer_weights(
            w_in[l], gla_w_gk_up[l], ffn_w_gate[l], ffn_w_up[l], ffn_w_down[l], x_w_k[l], x_w_v[l])
        hn = rmsnorm(h, ln_mix[l])
        p_swa = matmul(hn, w_swa, out_dtype=BF16, tm=1024, tn=640, name="proj_swa")
        p_gla = matmul(hn, w_gla, out_dtype=BF16, tm=1024, tn=512, name="proj_gla")
        p_ssd = matmul(hn, w_ssd, out_dtype=BF16, tm=1024, tn=512, name="proj_ssd")
        p_small = matmul(hn, w_small, out_dtype=F32, tm=1024, tn=256, name="proj_small")
        y_a = swa_mixer(p_swa, band_bias, swa_sinks[l], swa_q_gain[l], swa_k_gain[l], swa_out_gain[l], batch, seq)
        y_b = gla_mixer(p_gla, p_small, w_up_pad, gla_b_gk_up[l], gla_norm_gain[l], batch, seq)
        y_c = ssd_mixer(p_ssd, p_small, ssd_conv_w[l], ssd_conv_b[l], ssd_dt_bias[l], ssd_a_log[l], ssd_d[l],
                        ssd_norm_gain[l], batch, seq)
        h = mix_out(y_a, y_b, y_c, w_mix_out[l].astype(BF16), h)
        hx = rmsnorm(h, ln_x[l])
        memn = rmsnorm(mem2, ln_mem[l])
        q = matmul(hx, x_w_q[l].astype(BF16), out_dtype=F32, tm=1024, tn=512, name="xattn_q")
        kv = matmul(memn, w_kv, out_dtype=F32, tm=1024, tn=512, name="xattn_kv")
        o = cross_attention_core(q, kv, x_q_gain[l], x_k_gain[l], batch, seq, mem_len)
        h = matmul(o, x_w_o[l].astype(BF16), out_dtype=F32, res=h, tm=1024, tn=1024, name="xattn_out")
        hf = rmsnorm(h, ln_ffn[l])
        hidden = ffn_gate_up(hf, wg, wu)
        h = matmul(hidden, wd, out_dtype=F32, res=h, tm=1024, tn=2048, tk=FFN_TILE, name="ffn_down")
    return h.reshape(batch, seq, d)
```

```python
import functools
import math

import numpy as np
import jax
import jax.numpy as jnp
from jax import lax
from jax.experimental import pallas as pl
from jax.experimental.pallas import tpu as pltpu

F32 = jnp.float32
BF16 = jnp.bfloat16

D_MODEL = 4096
DEPTH = 2
EPS = 1e-6
SWA_WIDTH = 1024
SWA_HEAD_DIM = 64
SWA_HEADS = 16
SWA_KV_HEADS = 2
SWA_GROUP = SWA_HEADS // SWA_KV_HEADS
SWA_KV_WIDTH = SWA_KV_HEADS * SWA_HEAD_DIM
SWA_WINDOW = 128
SWA_BLOCK = 128
REL_BUCKETS = 32
REL_MAX_DIST = 128
GLA_WIDTH = 1024
GLA_HEADS = 4
GLA_VAL_DIM = 256
GLA_KEY_DIM = 128
GLA_KEY_WIDTH = GLA_HEADS * GLA_KEY_DIM
GLA_GATE_RANK = 16
GLA_GATE_NORMALIZER = 16.0
GLA_CHUNK = 64
SSD_WIDTH = 2048
SSD_HEAD_DIM = 64
SSD_HEADS = 32
SSD_GROUPS = 8
SSD_HEADS_PER_GROUP = SSD_HEADS // SSD_GROUPS
SSD_STATE = 128
SSD_CONV = 4
SSD_CHUNK = 128
SSD_BC_WIDTH = SSD_GROUPS * SSD_STATE
SSD_CONV_CH = SSD_WIDTH + 2 * SSD_BC_WIDTH
SSD_GROUP_WIDTH = SSD_WIDTH // SSD_GROUPS
X_HEADS = 4
X_HEAD_DIM = 128
X_WIDTH = X_HEADS * X_HEAD_DIM
FFN_HIDDEN = 11008

V7X_LANES = 128
V7X_VMEM_BYTES = 64 * 1024 * 1024
VMEM_REQUEST_CAP = (V7X_VMEM_BYTES * 7) // 8
V7X_MXU_WIDTH = 256
WCAST_TILE = V7X_MXU_WIDTH
GLA_TIME_BLOCK = 256


def _params(semantics, block_bytes):
    limit = min(VMEM_REQUEST_CAP, 2 * block_bytes + 16 * 1024 * 1024)
    return pltpu.CompilerParams(dimension_semantics=semantics, vmem_limit_bytes=int(limit))


def _nbytes(shape, dtype):
    return int(np.prod(shape)) * jnp.dtype(dtype).itemsize


def _split_bf16(x, terms):
    parts = []
    r = x
    for t in range(terms):
        p = r.astype(BF16)
        parts.append(p)
        if t + 1 < terms:
            r = r - p.astype(F32)
    return parts


def _dot(a, b):
    return jnp.dot(a, b, preferred_element_type=F32)


def _dot_nt(a, b):
    return lax.dot_general(a, b, (((1,), (1,)), ((), ())), preferred_element_type=F32)


def _dot_tn(a, b):
    return lax.dot_general(a, b, (((0,), (0,)), ((), ())), preferred_element_type=F32)


def _dot_exact_lhs(a_bf16, x, terms):
    acc = None
    for p in _split_bf16(x, terms):
        d = _dot(a_bf16, p)
        acc = d if acc is None else acc + d
    return acc


def _dot_exact_rhs(x, b_bf16, terms):
    acc = None
    for p in _split_bf16(x, terms):
        d = _dot(p, b_bf16)
        acc = d if acc is None else acc + d
    return acc


def _silu(x):
    return x / (1.0 + jnp.exp(-x))


def _softplus(x):
    return jnp.maximum(x, 0.0) + jnp.log1p(jnp.exp(-jnp.abs(x)))


def _rms(x, gain):
    return x * lax.rsqrt(jnp.mean(x * x, axis=-1, keepdims=True) + EPS) * gain


def _rmsnorm_body(x_ref, g_ref, o_ref):
    o_ref[...] = _rms(x_ref[...], g_ref[...]).astype(o_ref.dtype)


def rmsnorm(x, gain, tm=256):
    m, d = x.shape
    tm = min(tm, m)
    return pl.pallas_call(
        _rmsnorm_body,
        grid=(m // tm,),
        in_specs=[pl.BlockSpec((tm, d), lambda i: (i, 0)),
                  pl.BlockSpec((1, d), lambda i: (0, 0))],
        out_specs=pl.BlockSpec((tm, d), lambda i: (i, 0)),
        out_shape=jax.ShapeDtypeStruct((m, d), BF16),
        compiler_params=_params(("parallel",), _nbytes((tm, d), F32) + _nbytes((tm, d), BF16)),
        name="rmsnorm",
    )(x, gain.reshape(1, d))


def _mm_body(*refs, nk, has_res):
    a_ref, w_ref = refs[0], refs[1]
    r_ref = refs[2] if has_res else None
    o_ref = refs[2 + has_res]
    part = _dot(a_ref[...], w_ref[...])
    if nk == 1:
        if has_res:
            part = part + r_ref[...]
        o_ref[...] = part.astype(o_ref.dtype)
        return
    k = pl.program_id(2)

    @pl.when(k == 0)
    def _():
        o_ref[...] = (part + r_ref[...]) if has_res else part

    @pl.when(k > 0)
    def _():
        o_ref[...] += part


def matmul(a, w, *, out_dtype, res=None, tm, tn, tk=None, name="matmul"):
    m, kdim = a.shape
    n = w.shape[1]
    tm, tn = min(tm, m), min(tn, n)
    tk = kdim if tk is None else tk
    nk = kdim // tk
    assert m % tm == 0 and n % tn == 0 and kdim % tk == 0
    assert nk == 1 or out_dtype == F32
    has_res = res is not None
    in_specs = [pl.BlockSpec((tm, tk), lambda i, j, k: (i, k)),
                pl.BlockSpec((tk, tn), lambda i, j, k: (k, j))]
    args = [a, w]
    nbytes = _nbytes((tm, tk), BF16) + _nbytes((tk, tn), BF16) + _nbytes((tm, tn), out_dtype) + _nbytes((tm, tn), F32)
    if has_res:
        in_specs.append(pl.BlockSpec((tm, tn), lambda i, j, k: (i, j)))
        args.append(res)
        nbytes += _nbytes((tm, tn), F32)
    return pl.pallas_call(
        functools.partial(_mm_body, nk=nk, has_res=has_res),
        grid=(m // tm, n // tn, nk),
        in_specs=in_specs,
        out_specs=pl.BlockSpec((tm, tn), lambda i, j, k: (i, j)),
        out_shape=jax.ShapeDtypeStruct((m, n), out_dtype),
        compiler_params=_params(("parallel", "parallel", "arbitrary"), nbytes),
        name=name,
    )(*args)


def _mixout_body(ya_ref, yb_ref, yc_ref, w_ref, r_ref, o_ref, wb_ref):
    @pl.when(pl.program_id(1) == 0)
    def _():
        wb_ref[...] = w_ref[...].astype(BF16)

    acc = r_ref[...]
    acc = acc + _dot(ya_ref[...], wb_ref[0:SWA_WIDTH, :])
    acc = acc + _dot(yb_ref[...], wb_ref[SWA_WIDTH:SWA_WIDTH + GLA_WIDTH, :])
    acc = acc + _dot(yc_ref[...], wb_ref[SWA_WIDTH + GLA_WIDTH:, :])
    o_ref[...] = acc


def mix_out(ya, yb, yc, w_all, layer, res, tm=1024, tn=WCAST_TILE):
    m = ya.shape[0]
    _, kdim, n = w_all.shape
    tm = min(tm, m)
    nbytes = (_nbytes((tm, kdim), BF16) + _nbytes((kdim, tn), F32) + _nbytes((kdim, tn), BF16)
              + 3 * _nbytes((tm, tn), F32))
    return pl.pallas_call(
        _mixout_body,
        grid=(n // tn, m // tm),
        in_specs=[pl.BlockSpec((tm, SWA_WIDTH), lambda j, i: (i, 0)),
                  pl.BlockSpec((tm, GLA_WIDTH), lambda j, i: (i, 0)),
                  pl.BlockSpec((tm, SSD_WIDTH), lambda j, i: (i, 0)),
                  pl.BlockSpec((None, kdim, tn), lambda j, i: (layer, 0, j)),
                  pl.BlockSpec((tm, tn), lambda j, i: (i, j))],
        out_specs=pl.BlockSpec((tm, tn), lambda j, i: (i, j)),
        out_shape=jax.ShapeDtypeStruct((m, n), F32),
        scratch_shapes=[pltpu.VMEM((kdim, tn), BF16)],
        compiler_params=_params(("parallel", "arbitrary"), nbytes),
        name="mix_out",
    )(ya, yb, yc, w_all, res)


def _proj_body(a_ref, w_ref, o_ref, wb_ref):
    @pl.when(pl.program_id(1) == 0)
    def _():
        wb_ref[...] = w_ref[...].astype(BF16)

    o_ref[...] = _dot(a_ref[...], wb_ref[...]).astype(o_ref.dtype)


def proj_wcast(a, w_all, layer, col0, ncols, *, out_dtype, tm=1024, tn=WCAST_TILE, name="proj"):
    m, kdim = a.shape
    tm = min(tm, m)
    assert col0 % tn == 0 and ncols % tn == 0
    cb0 = col0 // tn
    nbytes = (_nbytes((tm, kdim), BF16) + _nbytes((kdim, tn), F32) + _nbytes((kdim, tn), BF16)
              + _nbytes((tm, tn), out_dtype) + _nbytes((tm, tn), F32))
    return pl.pallas_call(
        _proj_body,
        grid=(ncols // tn, m // tm),
        in_specs=[pl.BlockSpec((tm, kdim), lambda j, i: (i, 0)),
                  pl.BlockSpec((None, kdim, tn), lambda j, i: (layer, 0, cb0 + j))],
        out_specs=pl.BlockSpec((tm, tn), lambda j, i: (i, j)),
        out_shape=jax.ShapeDtypeStruct((m, ncols), out_dtype),
        scratch_shapes=[pltpu.VMEM((kdim, tn), BF16)],
        compiler_params=_params(("parallel", "arbitrary"), nbytes),
        name=name,
    )(a, w_all)


def _gateup_body(a_ref, wg_ref, wu_ref, o_ref, wgb_ref, wub_ref):
    @pl.when(pl.program_id(1) == 0)
    def _():
        wgb_ref[...] = wg_ref[...].astype(BF16)
        wub_ref[...] = wu_ref[...].astype(BF16)

    a = a_ref[...]
    g = _dot(a, wgb_ref[...])
    u = _dot(a, wub_ref[...])
    o_ref[...] = (_silu(g) * u).astype(o_ref.dtype)


def ffn_gate_up(a, wg_all, wu_all, layer, tm=1024, tn=WCAST_TILE):
    m, kdim = a.shape
    n = wg_all.shape[2]
    tm = min(tm, m)
    nbytes = (_nbytes((tm, kdim), BF16) + 2 * _nbytes((kdim, tn), F32) + 2 * _nbytes((kdim, tn), BF16)
              + _nbytes((tm, tn), BF16) + 3 * _nbytes((tm, tn), F32))
    return pl.pallas_call(
        _gateup_body,
        grid=(n // tn, m // tm),
        in_specs=[pl.BlockSpec((tm, kdim), lambda j, i: (i, 0)),
                  pl.BlockSpec((None, kdim, tn), lambda j, i: (layer, 0, j)),
                  pl.BlockSpec((None, kdim, tn), lambda j, i: (layer, 0, j))],
        out_specs=pl.BlockSpec((tm, tn), lambda j, i: (i, j)),
        out_shape=jax.ShapeDtypeStruct((m, n), BF16),
        scratch_shapes=[pltpu.VMEM((kdim, tn), BF16), pltpu.VMEM((kdim, tn), BF16)],
        compiler_params=_params(("parallel", "arbitrary"), nbytes),
        name="ffn_gate_up",
    )(a, wg_all, wu_all)


def _swa_body(sink_ref, q_ref, kvc_ref, kvp_ref, bias_ref, qg_ref, kg_ref, og_ref, o_ref, acc_ref):
    n = pl.program_id(1)
    hd = SWA_HEAD_DIM
    kband = jnp.concatenate([kvp_ref[:, :SWA_KV_WIDTH], kvc_ref[:, :SWA_KV_WIDTH]], axis=0).astype(F32)
    vband = jnp.concatenate([kvp_ref[:, SWA_KV_WIDTH:], kvc_ref[:, SWA_KV_WIDTH:]], axis=0)
    col = lax.broadcasted_iota(jnp.int32, (SWA_BLOCK, 2 * SWA_BLOCK), 1)
    key_exists = (col >= SWA_BLOCK) | (n > 0)
    kn = [_rms(kband[:, hd * j:hd * (j + 1)], kg_ref[...]).astype(BF16) for j in range(SWA_KV_HEADS)]
    for h in range(SWA_HEADS):
        j = h // SWA_GROUP
        qh = _rms(q_ref[:, hd * h:hd * (h + 1)].astype(F32), qg_ref[...])
        s = _dot_nt(qh.astype(BF16), kn[j]) * (hd ** -0.5) + bias_ref[h]
        s = jnp.where(key_exists, s, -jnp.inf)
        sink = sink_ref[h]
        mx = jnp.maximum(jnp.max(s, axis=-1, keepdims=True), sink)
        p = jnp.exp(s - mx)
        denom = jnp.sum(p, axis=-1, keepdims=True) + jnp.exp(sink - mx)
        oh = _dot(p.astype(BF16), vband[:, hd * j:hd * (j + 1)]) / denom
        acc_ref[:, hd * h:hd * (h + 1)] = oh
    o_ref[...] = _rms(acc_ref[...], og_ref[...]).astype(o_ref.dtype)


def swa_mixer(qkv, bias, sinks, q_gain, k_gain, out_gain, batch, seq):
    nb = seq // SWA_BLOCK
    kv_blk = SWA_WIDTH // (2 * SWA_KV_WIDTH)
    nbytes = (_nbytes((SWA_BLOCK, SWA_WIDTH), BF16) * 2 + 2 * _nbytes((SWA_BLOCK, 2 * SWA_KV_WIDTH), BF16)
              + _nbytes(bias.shape, F32) + _nbytes((SWA_BLOCK, SWA_WIDTH), F32))
    return pl.pallas_call(
        _swa_body,
        grid=(batch, nb),
        in_specs=[pl.BlockSpec(memory_space=pltpu.SMEM),
                  pl.BlockSpec((SWA_BLOCK, SWA_WIDTH), lambda b, n: (b * nb + n, 0)),
                  pl.BlockSpec((SWA_BLOCK, 2 * SWA_KV_WIDTH), lambda b, n: (b * nb + n, kv_blk)),
                  pl.BlockSpec((SWA_BLOCK, 2 * SWA_KV_WIDTH),
                               lambda b, n: (b * nb + jnp.maximum(n - 1, 0), kv_blk)),
                  pl.BlockSpec(bias.shape, lambda b, n: (0, 0, 0)),
                  pl.BlockSpec((1, SWA_HEAD_DIM), lambda b, n: (0, 0)),
                  pl.BlockSpec((1, SWA_HEAD_DIM), lambda b, n: (0, 0)),
                  pl.BlockSpec((1, SWA_WIDTH), lambda b, n: (0, 0))],
        out_specs=pl.BlockSpec((SWA_BLOCK, SWA_WIDTH), lambda b, n: (b * nb + n, 0)),
        out_shape=jax.ShapeDtypeStruct((batch * seq, SWA_WIDTH), BF16),
        scratch_shapes=[pltpu.VMEM((SWA_BLOCK, SWA_WIDTH), F32)],
        compiler_params=_params(("parallel", "parallel"), nbytes),
        name="swa_mixer",
    )(sinks, qkv, qkv, qkv, bias, q_gain.reshape(1, -1), k_gain.reshape(1, -1), out_gain.reshape(1, -1))


def _t5_bucket(dist):
    n = jnp.maximum(dist, 0)
    max_exact = REL_BUCKETS // 2
    nf = jnp.maximum(n, 1).astype(F32)
    large = max_exact + (jnp.log(nf / max_exact) / math.log(REL_MAX_DIST / max_exact)
                         * (REL_BUCKETS - max_exact)).astype(jnp.int32)
    large = jnp.minimum(large, REL_BUCKETS - 1)
    return jnp.where(n < max_exact, n, large)


def swa_band_bias(rel_bias):
    i = jnp.arange(SWA_BLOCK, dtype=jnp.int32)[:, None]
    j = jnp.arange(2 * SWA_BLOCK, dtype=jnp.int32)[None, :]
    dist = i + SWA_BLOCK - j
    onehot = _t5_bucket(dist)[None] == jnp.arange(REL_BUCKETS, dtype=jnp.int32)[:, None, None]
    bias = jnp.sum(jnp.where(onehot[:, None], rel_bias.astype(F32)[:, :, None, None], 0.0), axis=0)
    in_window = (dist >= 0) & (dist < SWA_WINDOW)
    return jnp.where(in_window[None], bias, -jnp.inf)


def _gla_body(x_ref, gl_ref, wup_ref, bup_ref, gain_ref, o_ref, state_ref):
    tb = x_ref.shape[0]
    dk, dv, c = GLA_KEY_DIM, GLA_VAL_DIM, GLA_CHUNK

    @pl.when(pl.program_id(1) == 0)
    def _():
        state_ref[...] = jnp.zeros_like(state_ref)

    gl_hi, gl_lo = _split_bf16(gl_ref[...], 2)
    w_hi, w_lo = _split_bf16(wup_ref[...], 2)
    pre = _dot(gl_hi, w_hi) + _dot(gl_hi, w_lo) + _dot(gl_lo, w_hi) + bup_ref[...]
    g = (jnp.minimum(pre, 0.0) - jnp.log1p(jnp.exp(-jnp.abs(pre)))) * (1.0 / GLA_GATE_NORMALIZER)

    row = lax.broadcasted_iota(jnp.int32, (tb, tb), 0)
    colm = lax.broadcasted_iota(jnp.int32, (tb, tb), 1)
    same_chunk_lower = ((row // c) == (colm // c)) & (colm <= row)
    bcum_all = _dot_exact_lhs(same_chunk_lower.astype(BF16), g, 3)

    ri = lax.broadcasted_iota(jnp.int32, (c, c), 0)
    ci = lax.broadcasted_iota(jnp.int32, (c, c), 1)
    causal = ci <= ri
    for ch in range(tb // c):
        r0 = ch * c
        for h in range(GLA_HEADS):
            bcum = bcum_all[r0:r0 + c, dk * h:dk * (h + 1)]
            blast = bcum[c - 1:c, :]
            q = x_ref[r0:r0 + c, dk * h:dk * (h + 1)].astype(F32)
            k = x_ref[r0:r0 + c, GLA_KEY_WIDTH + dk * h:GLA_KEY_WIDTH + dk * (h + 1)].astype(F32)
            v = x_ref[r0:r0 + c, 2 * GLA_KEY_WIDTH + dv * h:2 * GLA_KEY_WIDTH + dv * (h + 1)]
            r = x_ref[r0:r0 + c, 2 * GLA_KEY_WIDTH + GLA_WIDTH + dv * h:
                      2 * GLA_KEY_WIDTH + GLA_WIDTH + dv * (h + 1)].astype(F32)
            qd = (q * (dk ** -0.5) * jnp.exp(bcum)).astype(BF16)
            kd = (k * jnp.exp(-bcum)).astype(BF16)
            kl = (k * jnp.exp(blast - bcum)).astype(BF16)
            att = jnp.where(causal, _dot_nt(qd, kd), 0.0)
            st = state_ref[h]
            o = _dot(att.astype(BF16), v) + _dot_nt(qd, st.astype(BF16))
            state_ref[h] = st * jnp.exp(blast) + _dot_tn(v, kl)
            o = _rms(o, gain_ref[...]) * _silu(r)
            o_ref[r0:r0 + c, dv * h:dv * (h + 1)] = o.astype(o_ref.dtype)


def gla_mixer(x, small, w_up_pad, b_up, norm_gain, batch, seq):
    tb = min(GLA_TIME_BLOCK, seq)
    nt = seq // tb
    width = x.shape[1]
    nbytes = (_nbytes((tb, width), BF16) + _nbytes((tb, V7X_LANES), F32) + _nbytes(w_up_pad.shape, F32)
              + _nbytes((tb, GLA_WIDTH), BF16) + _nbytes((GLA_HEADS, GLA_VAL_DIM, GLA_KEY_DIM), F32)
              + 8 * _nbytes((tb, GLA_KEY_WIDTH), F32))
    return pl.pallas_call(
        _gla_body,
        grid=(batch, nt),
        in_specs=[pl.BlockSpec((tb, width), lambda b, t: (b * nt + t, 0)),
                  pl.BlockSpec((tb, V7X_LANES), lambda b, t: (b * nt + t, 0)),
                  pl.BlockSpec(w_up_pad.shape, lambda b, t: (0, 0)),
                  pl.BlockSpec((1, GLA_KEY_WIDTH), lambda b, t: (0, 0)),
                  pl.BlockSpec((1, GLA_VAL_DIM), lambda b, t: (0, 0))],
        out_specs=pl.BlockSpec((tb, GLA_WIDTH), lambda b, t: (b * nt + t, 0)),
        out_shape=jax.ShapeDtypeStruct((batch * seq, GLA_WIDTH), BF16),
        scratch_shapes=[pltpu.VMEM((GLA_HEADS, GLA_VAL_DIM, GLA_KEY_DIM), F32)],
        compiler_params=_params(("parallel", "arbitrary"), nbytes),
        name="gla_mixer",
    )(x, small, w_up_pad, b_up.reshape(1, -1), norm_gain.reshape(1, -1))


def _ssd_body(x_ref, dt_ref, cw_ref, cb_ref, dtb_ref, alog_ref, dexp_ref, gain_ref, expand_ref,
              o_ref, state_ref, tail_ref, y_ref):
    L, P, N = SSD_CHUNK, SSD_HEAD_DIM, SSD_STATE
    gw = SSD_GROUP_WIDTH

    @pl.when(pl.program_id(1) == 0)
    def _():
        state_ref[...] = jnp.zeros_like(state_ref)
        tail_ref[...] = jnp.zeros_like(tail_ref)

    xin = x_ref[:, SSD_WIDTH:].astype(F32)
    xext = jnp.concatenate([tail_ref[...], xin], axis=0)
    tail_ref[...] = xin[L - 8:, :]
    conv = cb_ref[...] + cw_ref[SSD_CONV - 1:SSD_CONV, :] * xin
    for j in range(1, SSD_CONV):
        conv = conv + cw_ref[SSD_CONV - 1 - j:SSD_CONV - j, :] * xext[8 - j:8 - j + L, :]
    xbc = _silu(conv)
    xs = xbc[:, :SSD_WIDTH]
    bm = xbc[:, SSD_WIDTH:SSD_WIDTH + SSD_BC_WIDTH].astype(BF16)
    cm = xbc[:, SSD_WIDTH + SSD_BC_WIDTH:].astype(BF16)

    dt = _softplus(dt_ref[...] + dtb_ref[...])
    dta = dt * (-jnp.exp(alog_ref[...]))
    ri = lax.broadcasted_iota(jnp.int32, (L, L), 0)
    ci = lax.broadcasted_iota(jnp.int32, (L, L), 1)
    causal = ci <= ri
    a_cum = _dot_exact_lhs(causal.astype(BF16), dta, 3)
    a_cum_t = a_cum.T
    expand = expand_ref[...]
    a_exp = _dot_exact_rhs(a_cum, expand, 3)
    dt_exp = _dot_exact_rhs(dt, expand, 2)
    a_last = a_exp[L - 1:L, :]
    xd = xs * dt_exp
    xdec = (xd * jnp.exp(a_last - a_exp)).astype(BF16)
    xd16 = xd.astype(BF16)
    out_scale = jnp.exp(a_exp)

    for g in range(SSD_GROUPS):
        bg = bm[:, N * g:N * (g + 1)]
        cg = cm[:, N * g:N * (g + 1)]
        cb = _dot_nt(cg, bg)
        for kk in range(SSD_HEADS_PER_GROUP):
            h = g * SSD_HEADS_PER_GROUP + kk
            diff = a_cum[:, h:h + 1] - a_cum_t[h:h + 1, :]
            m = cb * jnp.exp(jnp.where(causal, diff, -jnp.inf))
            y_ref[:, P * h:P * (h + 1)] = _dot(m.astype(BF16), xd16[:, P * h:P * (h + 1)])
        sg = state_ref[:, gw * g:gw * (g + 1)]
        y_off = _dot(cg, sg.astype(BF16)) * out_scale[:, gw * g:gw * (g + 1)]
        state_ref[:, gw * g:gw * (g + 1)] = (sg * jnp.exp(a_last[:, gw * g:gw * (g + 1)])
                                             + _dot_tn(bg, xdec[:, gw * g:gw * (g + 1)]))
        yg = y_ref[:, gw * g:gw * (g + 1)] + y_off + xs[:, gw * g:gw * (g + 1)] * dexp_ref[:, gw * g:gw * (g + 1)]
        yg = yg * _silu(x_ref[:, gw * g:gw * (g + 1)].astype(F32))
        o_ref[:, gw * g:gw * (g + 1)] = _rms(yg, gain_ref[:, gw * g:gw * (g + 1)]).astype(o_ref.dtype)


def ssd_mixer(x, small, conv_w, conv_b, dt_bias, a_log, d_skip, norm_gain, batch, seq):
    L = SSD_CHUNK
    nc = seq // L
    width = x.shape[1]
    pad = V7X_LANES - SSD_HEADS
    dtb = jnp.pad(dt_bias, (0, pad)).reshape(1, V7X_LANES)
    alog = jnp.pad(a_log, (0, pad)).reshape(1, V7X_LANES)
    dexp = jnp.repeat(d_skip, SSD_HEAD_DIM).reshape(1, SSD_WIDTH)
    expand = (jnp.arange(V7X_LANES)[:, None] == (jnp.arange(SSD_WIDTH)[None, :] // SSD_HEAD_DIM)).astype(BF16)
    nbytes = (_nbytes((L, width), BF16) + _nbytes((L, V7X_LANES), F32) + _nbytes((L, SSD_WIDTH), BF16)
              + _nbytes(expand.shape, BF16) + 2 * _nbytes((SSD_STATE, SSD_WIDTH), F32)
              + 12 * _nbytes((L, SSD_CONV_CH), F32))
    return pl.pallas_call(
        _ssd_body,
        grid=(batch, nc),
        in_specs=[pl.BlockSpec((L, width), lambda b, c: (b * nc + c, 0)),
                  pl.BlockSpec((L, V7X_LANES), lambda b, c: (b * nc + c, 1)),
                  pl.BlockSpec((SSD_CONV, SSD_CONV_CH), lambda b, c: (0, 0)),
                  pl.BlockSpec((1, SSD_CONV_CH), lambda b, c: (0, 0)),
                  pl.BlockSpec((1, V7X_LANES), lambda b, c: (0, 0)),
                  pl.BlockSpec((1, V7X_LANES), lambda b, c: (0, 0)),
                  pl.BlockSpec((1, SSD_WIDTH), lambda b, c: (0, 0)),
                  pl.BlockSpec((1, SSD_WIDTH), lambda b, c: (0, 0)),
                  pl.BlockSpec(expand.shape, lambda b, c: (0, 0))],
        out_specs=pl.BlockSpec((L, SSD_WIDTH), lambda b, c: (b * nc + c, 0)),
        out_shape=jax.ShapeDtypeStruct((batch * seq, SSD_WIDTH), BF16),
        scratch_shapes=[pltpu.VMEM((SSD_STATE, SSD_WIDTH), F32),
                        pltpu.VMEM((8, SSD_CONV_CH), F32),
                        pltpu.VMEM((L, SSD_WIDTH), F32)],
        compiler_params=_params(("parallel", "arbitrary"), nbytes),
        name="ssd_mixer",
    )(x, small, conv_w.reshape(SSD_CONV, SSD_CONV_CH), conv_b.reshape(1, -1), dtb, alog, dexp,
      norm_gain.reshape(1, -1), expand)


def _xattn_body(q_ref, kv_ref, qg_ref, kg_ref, o_ref):
    hd = X_HEAD_DIM
    for h in range(X_HEADS):
        qh = _rms(q_ref[:, hd * h:hd * (h + 1)], qg_ref[...]).astype(BF16)
        kh = _rms(kv_ref[:, hd * h:hd * (h + 1)], kg_ref[...]).astype(BF16)
        vh = kv_ref[:, X_WIDTH + hd * h:X_WIDTH + hd * (h + 1)].astype(BF16)
        s = _dot_nt(qh, kh) * (hd ** -0.5)
        p = jnp.exp(s - jnp.max(s, axis=-1, keepdims=True))
        oh = _dot(p.astype(BF16), vh) / jnp.sum(p, axis=-1, keepdims=True)
        o_ref[:, hd * h:hd * (h + 1)] = oh.astype(o_ref.dtype)


def cross_attention_core(q, kv, q_gain, k_gain, batch, seq, mem_len, tq=512):
    tq = min(tq, seq)
    nq = seq // tq
    nbytes = _nbytes((tq, X_WIDTH), F32) + _nbytes((mem_len, 2 * X_WIDTH), F32) + _nbytes((tq, X_WIDTH), BF16) \
        + 4 * _nbytes((tq, mem_len), F32)
    return pl.pallas_call(
        _xattn_body,
        grid=(batch, nq),
        in_specs=[pl.BlockSpec((tq, X_WIDTH), lambda b, t: (b * nq + t, 0)),
                  pl.BlockSpec((mem_len, 2 * X_WIDTH), lambda b, t: (b, 0)),
                  pl.BlockSpec((1, X_HEAD_DIM), lambda b, t: (0, 0)),
                  pl.BlockSpec((1, X_HEAD_DIM), lambda b, t: (0, 0))],
        out_specs=pl.BlockSpec((tq, X_WIDTH), lambda b, t: (b * nq + t, 0)),
        out_shape=jax.ShapeDtypeStruct((batch * seq, X_WIDTH), BF16),
        compiler_params=_params(("parallel", "parallel"), nbytes),
        name="xattn_core",
    )(q, kv, q_gain.reshape(1, -1), k_gain.reshape(1, -1))


_IN_OFF = np.cumsum([0, SWA_WIDTH, SWA_KV_WIDTH, SWA_KV_WIDTH, GLA_KEY_WIDTH, GLA_KEY_WIDTH, GLA_WIDTH,
                     GLA_WIDTH, GLA_GATE_RANK, SSD_WIDTH, SSD_CONV_CH, SSD_HEADS]).tolist()
SWA_COL0, GLA_COL0, GLOW_COL0, SSD_COL0, DT_COL0 = _IN_OFF[0], _IN_OFF[3], _IN_OFF[7], _IN_OFF[8], _IN_OFF[10]


def _unaligned_weights(w_in, gla_w_gk_up, x_w_k, x_w_v):
    w_ssd = w_in[:, SSD_COL0:DT_COL0].astype(BF16)
    lane_pad = lambda w: jnp.pad(w, ((0, 0), (0, V7X_LANES - w.shape[1])))
    w_small = jnp.concatenate([lane_pad(w_in[:, GLOW_COL0:SSD_COL0]), lane_pad(w_in[:, DT_COL0:])],
                              axis=1).astype(BF16)
    w_up_pad = jnp.pad(gla_w_gk_up, ((0, V7X_LANES - GLA_GATE_RANK), (0, 0)))
    w_kv = jnp.concatenate([x_w_k, x_w_v], axis=1).astype(BF16)
    return w_ssd, w_small, w_up_pad, w_kv


def kernel(x, mem, rel_bias, ln_mix, w_in, swa_q_gain, swa_k_gain, swa_sinks, swa_out_gain, gla_w_gk_up, gla_b_gk_up, gla_norm_gain, ssd_conv_w, ssd_conv_b, ssd_dt_bias, ssd_a_log, ssd_d, ssd_norm_gain, w_mix_out, ln_x, ln_mem, x_w_q, x_w_k, x_w_v, x_w_o, x_q_gain, x_k_gain, ln_ffn, ffn_w_gate, ffn_w_up, ffn_w_down):
    batch, seq, d = x.shape
    mem_len = mem.shape[1]
    m = batch * seq
    band_bias = swa_band_bias(rel_bias)
    h = x.reshape(m, d)
    mem2 = mem.reshape(batch * mem_len, d)
    for l in range(DEPTH):
        w_ssd, w_small, w_up_pad, w_kv = _unaligned_weights(w_in[l], gla_w_gk_up[l], x_w_k[l], x_w_v[l])
        hn = rmsnorm(h, ln_mix[l])
        p_swa = proj_wcast(hn, w_in, l, SWA_COL0, GLA_COL0 - SWA_COL0, out_dtype=BF16, name="proj_swa")
        p_gla = proj_wcast(hn, w_in, l, GLA_COL0, GLOW_COL0 - GLA_COL0, out_dtype=BF16, name="proj_gla")
        p_ssd = matmul(hn, w_ssd, out_dtype=BF16, tm=1024, tn=512, name="proj_ssd")
        p_small = matmul(hn, w_small, out_dtype=F32, tm=1024, tn=256, name="proj_small")
        y_a = swa_mixer(p_swa, band_bias, swa_sinks[l], swa_q_gain[l], swa_k_gain[l], swa_out_gain[l], batch, seq)
        y_b = gla_mixer(p_gla, p_small, w_up_pad, gla_b_gk_up[l], gla_norm_gain[l], batch, seq)
        y_c = ssd_mixer(p_ssd, p_small, ssd_conv_w[l], ssd_conv_b[l], ssd_dt_bias[l], ssd_a_log[l], ssd_d[l],
                        ssd_norm_gain[l], batch, seq)
        h = mix_out(y_a, y_b, y_c, w_mix_out, l, h)
        hx = rmsnorm(h, ln_x[l])
        memn = rmsnorm(mem2, ln_mem[l])
        q = matmul(hx, x_w_q[l].astype(BF16), out_dtype=F32, tm=1024, tn=512, name="xattn_q")
        kv = matmul(memn, w_kv, out_dtype=F32, tm=1024, tn=512, name="xattn_kv")
        o = cross_attention_core(q, kv, x_q_gain[l], x_k_gain[l], batch, seq, mem_len)
        h = matmul(o, x_w_o[l].astype(BF16), out_dtype=F32, res=h, tm=1024, tn=1024, name="xattn_out")
        hf = rmsnorm(h, ln_ffn[l])
        hidden = ffn_gate_up(hf, ffn_w_gate, ffn_w_up, l)
        h = matmul(hidden, ffn_w_down[l].astype(BF16), out_dtype=F32, res=h, tm=512, tn=256, name="ffn_down")
    return h.reshape(batch, seq, d)
```

```python
import functools
import math

import numpy as np
import jax
import jax.numpy as jnp
from jax import lax
from jax.experimental import pallas as pl
from jax.experimental.pallas import tpu as pltpu

F32 = jnp.float32
BF16 = jnp.bfloat16

D_MODEL = 4096
DEPTH = 2
EPS = 1e-6
SWA_WIDTH = 1024
SWA_HEAD_DIM = 64
SWA_HEADS = 16
SWA_KV_HEADS = 2
SWA_GROUP = SWA_HEADS // SWA_KV_HEADS
SWA_KV_WIDTH = SWA_KV_HEADS * SWA_HEAD_DIM
SWA_WINDOW = 128
SWA_BLOCK = 128
REL_BUCKETS = 32
REL_MAX_DIST = 128
GLA_WIDTH = 1024
GLA_HEADS = 4
GLA_VAL_DIM = 256
GLA_KEY_DIM = 128
GLA_KEY_WIDTH = GLA_HEADS * GLA_KEY_DIM
GLA_GATE_RANK = 16
GLA_GATE_NORMALIZER = 16.0
GLA_CHUNK = 64
SSD_WIDTH = 2048
SSD_HEAD_DIM = 64
SSD_HEADS = 32
SSD_GROUPS = 8
SSD_HEADS_PER_GROUP = SSD_HEADS // SSD_GROUPS
SSD_STATE = 128
SSD_CONV = 4
SSD_CHUNK = 128
SSD_BC_WIDTH = SSD_GROUPS * SSD_STATE
SSD_CONV_CH = SSD_WIDTH + 2 * SSD_BC_WIDTH
SSD_GROUP_WIDTH = SSD_WIDTH // SSD_GROUPS
X_HEADS = 4
X_HEAD_DIM = 128
X_WIDTH = X_HEADS * X_HEAD_DIM
FFN_HIDDEN = 11008

V7X_LANES = 128
V7X_VMEM_BYTES = 64 * 1024 * 1024
VMEM_REQUEST_CAP = (V7X_VMEM_BYTES * 7) // 8
V7X_MXU_WIDTH = 256
FFN_TILE = V7X_MXU_WIDTH
PROJ_TILE = 2 * V7X_MXU_WIDTH
GLA_TIME_BLOCK = 256
SSD_DT_LANE0 = V7X_LANES - SSD_HEADS


def _params(semantics, block_bytes):
    limit = min(VMEM_REQUEST_CAP, 2 * block_bytes + 16 * 1024 * 1024)
    return pltpu.CompilerParams(dimension_semantics=semantics, vmem_limit_bytes=int(limit))


def _nbytes(shape, dtype):
    return int(np.prod(shape)) * jnp.dtype(dtype).itemsize


def _split_bf16(x, terms):
    parts = []
    r = x
    for t in range(terms):
        p = r.astype(BF16)
        parts.append(p)
        if t + 1 < terms:
            r = r - p.astype(F32)
    return parts


def _dot(a, b):
    return jnp.dot(a, b, preferred_element_type=F32)


def _dot_nt(a, b):
    return lax.dot_general(a, b, (((1,), (1,)), ((), ())), preferred_element_type=F32)


def _dot_tn(a, b):
    return lax.dot_general(a, b, (((0,), (0,)), ((), ())), preferred_element_type=F32)


def _dot_exact_lhs(a_bf16, x, terms):
    acc = None
    for p in _split_bf16(x, terms):
        d = _dot(a_bf16, p)
        acc = d if acc is None else acc + d
    return acc


def _dot_exact_rhs(x, b_bf16, terms):
    acc = None
    for p in _split_bf16(x, terms):
        d = _dot(p, b_bf16)
        acc = d if acc is None else acc + d
    return acc


def _silu(x):
    return x / (1.0 + jnp.exp(-x))


def _softplus(x):
    return jnp.maximum(x, 0.0) + jnp.log1p(jnp.exp(-jnp.abs(x)))


def _rms(x, gain):
    return x * lax.rsqrt(jnp.mean(x * x, axis=-1, keepdims=True) + EPS) * gain


def _rmsnorm_body(x_ref, g_ref, o_ref):
    o_ref[...] = _rms(x_ref[...], g_ref[...]).astype(o_ref.dtype)


def rmsnorm(x, gain, tm=256):
    m, d = x.shape
    tm = min(tm, m)
    return pl.pallas_call(
        _rmsnorm_body,
        grid=(m // tm,),
        in_specs=[pl.BlockSpec((tm, d), lambda i: (i, 0)),
                  pl.BlockSpec((1, d), lambda i: (0, 0))],
        out_specs=pl.BlockSpec((tm, d), lambda i: (i, 0)),
        out_shape=jax.ShapeDtypeStruct((m, d), BF16),
        compiler_params=_params(("parallel",), _nbytes((tm, d), F32) + _nbytes((tm, d), BF16)),
        name="rmsnorm",
    )(x, gain.reshape(1, d))


def _mm_body(*refs, nk, has_res):
    a_ref, w_ref = refs[0], refs[1]
    r_ref = refs[2] if has_res else None
    o_ref = refs[2 + has_res]
    part = _dot(a_ref[...], w_ref[...])
    if nk == 1:
        if has_res:
            part = part + r_ref[...]
        o_ref[...] = part.astype(o_ref.dtype)
        return
    k = pl.program_id(2)

    @pl.when(k == 0)
    def _():
        o_ref[...] = (part + r_ref[...]) if has_res else part

    @pl.when(k > 0)
    def _():
        o_ref[...] += part


def matmul(a, w, *, out_dtype, res=None, tm, tn, tk=None, layer=None, name="matmul"):
    m, kdim = a.shape
    n = w.shape[-1]
    tm, tn = min(tm, m), min(tn, n)
    tk = kdim if tk is None else tk
    nk = kdim // tk
    assert m % tm == 0 and n % tn == 0 and kdim % tk == 0
    assert nk == 1 or out_dtype == F32
    has_res = res is not None
    if layer is None:
        w_spec = pl.BlockSpec((tk, tn), lambda i, j, k: (k, j))
    else:
        w_spec = pl.BlockSpec((None, tk, tn), lambda i, j, k: (layer, k, j))
    in_specs = [pl.BlockSpec((tm, tk), lambda i, j, k: (i, k)), w_spec]
    args = [a, w]
    nbytes = _nbytes((tm, tk), BF16) + _nbytes((tk, tn), BF16) + _nbytes((tm, tn), out_dtype) + _nbytes((tm, tn), F32)
    if has_res:
        in_specs.append(pl.BlockSpec((tm, tn), lambda i, j, k: (i, j)))
        args.append(res)
        nbytes += _nbytes((tm, tn), F32)
    return pl.pallas_call(
        functools.partial(_mm_body, nk=nk, has_res=has_res),
        grid=(m // tm, n // tn, nk),
        in_specs=in_specs,
        out_specs=pl.BlockSpec((tm, tn), lambda i, j, k: (i, j)),
        out_shape=jax.ShapeDtypeStruct((m, n), out_dtype),
        compiler_params=_params(("parallel", "parallel", "arbitrary"), nbytes),
        name=name,
    )(*args)


def _mixout_body(ya_ref, yb_ref, yc_ref, w_ref, r_ref, o_ref, wb_ref):
    @pl.when(pl.program_id(1) == 0)
    def _():
        wb_ref[...] = w_ref[...].astype(BF16)

    acc = r_ref[...]
    acc = acc + _dot(ya_ref[...], wb_ref[0:SWA_WIDTH, :])
    acc = acc + _dot(yb_ref[...], wb_ref[SWA_WIDTH:SWA_WIDTH + GLA_WIDTH, :])
    acc = acc + _dot(yc_ref[...], wb_ref[SWA_WIDTH + GLA_WIDTH:, :])
    o_ref[...] = acc


def mix_out(ya, yb, yc, w_all, layer, res, tm=1024, tn=PROJ_TILE):
    m = ya.shape[0]
    _, kdim, n = w_all.shape
    tm = min(tm, m)
    nbytes = (_nbytes((tm, kdim), BF16) + _nbytes((kdim, tn), F32) + _nbytes((kdim, tn), BF16)
              + 3 * _nbytes((tm, tn), F32))
    return pl.pallas_call(
        _mixout_body,
        grid=(n // tn, m // tm),
        in_specs=[pl.BlockSpec((tm, SWA_WIDTH), lambda j, i: (i, 0)),
                  pl.BlockSpec((tm, GLA_WIDTH), lambda j, i: (i, 0)),
                  pl.BlockSpec((tm, SSD_WIDTH), lambda j, i: (i, 0)),
                  pl.BlockSpec((None, kdim, tn), lambda j, i: (layer, 0, j)),
                  pl.BlockSpec((tm, tn), lambda j, i: (i, j))],
        out_specs=pl.BlockSpec((tm, tn), lambda j, i: (i, j)),
        out_shape=jax.ShapeDtypeStruct((m, n), F32),
        scratch_shapes=[pltpu.VMEM((kdim, tn), BF16)],
        compiler_params=_params(("parallel", "arbitrary"), nbytes),
        name="mix_out",
    )(ya, yb, yc, w_all, res)


_XPOSE_ROWS, _XPOSE_COLS = 256, 512


def _proj_body(a_ref, w_ref, o_ref, wb_ref):
    _, tn, kdim = w_ref.shape

    @pl.when(pl.program_id(1) == 0)
    def _():
        for r0 in range(0, tn, _XPOSE_ROWS):
            r1 = min(r0 + _XPOSE_ROWS, tn)
            for k0 in range(0, kdim, _XPOSE_COLS):
                wb_ref[k0:k0 + _XPOSE_COLS, r0:r1] = w_ref[0, r0:r1, k0:k0 + _XPOSE_COLS].T.astype(BF16)

    o_ref[...] = _dot(a_ref[...], wb_ref[...]).astype(o_ref.dtype)


def proj_wcast(a, wt_all, layer, row_start, n_tiles, tn, *, out_dtype, tm=1024, name="proj"):
    m, kdim = a.shape
    tm = min(tm, m)
    nbytes = (_nbytes((tm, kdim), BF16) + _nbytes((kdim, tn), F32) + _nbytes((kdim, tn), BF16)
              + _nbytes((tm, tn), out_dtype) + _nbytes((tm, tn), F32))
    return pl.pallas_call(
        _proj_body,
        grid=(n_tiles, m // tm),
        in_specs=[pl.BlockSpec((tm, kdim), lambda j, i: (i, 0)),
                  pl.BlockSpec((pl.Element(1), pl.Element(tn), pl.Element(kdim)),
                               lambda j, i: (layer, pl.multiple_of(row_start(j), 8), 0))],
        out_specs=pl.BlockSpec((tm, tn), lambda j, i: (i, j)),
        out_shape=jax.ShapeDtypeStruct((m, n_tiles * tn), out_dtype),
        scratch_shapes=[pltpu.VMEM((kdim, tn), BF16)],
        compiler_params=_params(("parallel", "arbitrary"), nbytes),
        name=name,
    )(a, wt_all)


def _gateup_body(a_ref, wg_ref, wu_ref, o_ref, wgb_ref, wub_ref):
    @pl.when(pl.program_id(1) == 0)
    def _():
        wgb_ref[...] = wg_ref[...].astype(BF16)
        wub_ref[...] = wu_ref[...].astype(BF16)

    a = a_ref[...]
    g = _dot(a, wgb_ref[...])
    u = _dot(a, wub_ref[...])
    o_ref[...] = (_silu(g) * u).astype(o_ref.dtype)


def ffn_gate_up(a, wg_all, wu_all, layer, tm=1024, tn=FFN_TILE):
    m, kdim = a.shape
    n = wg_all.shape[2]
    tm = min(tm, m)
    nbytes = (_nbytes((tm, kdim), BF16) + 2 * _nbytes((kdim, tn), F32) + 2 * _nbytes((kdim, tn), BF16)
              + _nbytes((tm, tn), BF16) + 3 * _nbytes((tm, tn), F32))
    return pl.pallas_call(
        _gateup_body,
        grid=(n // tn, m // tm),
        in_specs=[pl.BlockSpec((tm, kdim), lambda j, i: (i, 0)),
                  pl.BlockSpec((None, kdim, tn), lambda j, i: (layer, 0, j)),
                  pl.BlockSpec((None, kdim, tn), lambda j, i: (layer, 0, j))],
        out_specs=pl.BlockSpec((tm, tn), lambda j, i: (i, j)),
        out_shape=jax.ShapeDtypeStruct((m, n), BF16),
        scratch_shapes=[pltpu.VMEM((kdim, tn), BF16), pltpu.VMEM((kdim, tn), BF16)],
        compiler_params=_params(("parallel", "arbitrary"), nbytes),
        name="ffn_gate_up",
    )(a, wg_all, wu_all)


def _swa_body(sink_ref, q_ref, kvc_ref, kvp_ref, bias_ref, qg_ref, kg_ref, og_ref, o_ref, acc_ref):
    n = pl.program_id(1)
    hd = SWA_HEAD_DIM
    kband = jnp.concatenate([kvp_ref[:, :SWA_KV_WIDTH], kvc_ref[:, :SWA_KV_WIDTH]], axis=0).astype(F32)
    vband = jnp.concatenate([kvp_ref[:, SWA_KV_WIDTH:], kvc_ref[:, SWA_KV_WIDTH:]], axis=0)
    col = lax.broadcasted_iota(jnp.int32, (SWA_BLOCK, 2 * SWA_BLOCK), 1)
    key_exists = (col >= SWA_BLOCK) | (n > 0)
    kn = [_rms(kband[:, hd * j:hd * (j + 1)], kg_ref[...]).astype(BF16) for j in range(SWA_KV_HEADS)]
    for h in range(SWA_HEADS):
        j = h // SWA_GROUP
        qh = _rms(q_ref[:, hd * h:hd * (h + 1)].astype(F32), qg_ref[...])
        s = _dot_nt(qh.astype(BF16), kn[j]) * (hd ** -0.5) + bias_ref[h]
        s = jnp.where(key_exists, s, -jnp.inf)
        sink = sink_ref[h]
        mx = jnp.maximum(jnp.max(s, axis=-1, keepdims=True), sink)
        p = jnp.exp(s - mx)
        denom = jnp.sum(p, axis=-1, keepdims=True) + jnp.exp(sink - mx)
        oh = _dot(p.astype(BF16), vband[:, hd * j:hd * (j + 1)]) / denom
        acc_ref[:, hd * h:hd * (h + 1)] = oh
    o_ref[...] = _rms(acc_ref[...], og_ref[...]).astype(o_ref.dtype)


def swa_mixer(qkv, bias, sinks, q_gain, k_gain, out_gain, batch, seq):
    nb = seq // SWA_BLOCK
    kv_blk = SWA_WIDTH // (2 * SWA_KV_WIDTH)
    nbytes = (_nbytes((SWA_BLOCK, SWA_WIDTH), BF16) * 2 + 2 * _nbytes((SWA_BLOCK, 2 * SWA_KV_WIDTH), BF16)
              + _nbytes(bias.shape, F32) + _nbytes((SWA_BLOCK, SWA_WIDTH), F32))
    return pl.pallas_call(
        _swa_body,
        grid=(batch, nb),
        in_specs=[pl.BlockSpec(memory_space=pltpu.SMEM),
                  pl.BlockSpec((SWA_BLOCK, SWA_WIDTH), lambda b, n: (b * nb + n, 0)),
                  pl.BlockSpec((SWA_BLOCK, 2 * SWA_KV_WIDTH), lambda b, n: (b * nb + n, kv_blk)),
                  pl.BlockSpec((SWA_BLOCK, 2 * SWA_KV_WIDTH),
                               lambda b, n: (b * nb + jnp.maximum(n - 1, 0), kv_blk)),
                  pl.BlockSpec(bias.shape, lambda b, n: (0, 0, 0)),
                  pl.BlockSpec((1, SWA_HEAD_DIM), lambda b, n: (0, 0)),
                  pl.BlockSpec((1, SWA_HEAD_DIM), lambda b, n: (0, 0)),
                  pl.BlockSpec((1, SWA_WIDTH), lambda b, n: (0, 0))],
        out_specs=pl.BlockSpec((SWA_BLOCK, SWA_WIDTH), lambda b, n: (b * nb + n, 0)),
        out_shape=jax.ShapeDtypeStruct((batch * seq, SWA_WIDTH), BF16),
        scratch_shapes=[pltpu.VMEM((SWA_BLOCK, SWA_WIDTH), F32)],
        compiler_params=_params(("parallel", "parallel"), nbytes),
        name="swa_mixer",
    )(sinks, qkv, qkv, qkv, bias, q_gain.reshape(1, -1), k_gain.reshape(1, -1), out_gain.reshape(1, -1))


def _t5_bucket(dist):
    n = jnp.maximum(dist, 0)
    max_exact = REL_BUCKETS // 2
    nf = jnp.maximum(n, 1).astype(F32)
    large = max_exact + (jnp.log(nf / max_exact) / math.log(REL_MAX_DIST / max_exact)
                         * (REL_BUCKETS - max_exact)).astype(jnp.int32)
    large = jnp.minimum(large, REL_BUCKETS - 1)
    return jnp.where(n < max_exact, n, large)


def swa_band_bias(rel_bias):
    i = jnp.arange(SWA_BLOCK, dtype=jnp.int32)[:, None]
    j = jnp.arange(2 * SWA_BLOCK, dtype=jnp.int32)[None, :]
    dist = i + SWA_BLOCK - j
    onehot = _t5_bucket(dist)[None] == jnp.arange(REL_BUCKETS, dtype=jnp.int32)[:, None, None]
    bias = jnp.sum(jnp.where(onehot[:, None], rel_bias.astype(F32)[:, :, None, None], 0.0), axis=0)
    in_window = (dist >= 0) & (dist < SWA_WINDOW)
    return jnp.where(in_window[None], bias, -jnp.inf)


def _gla_body(x_ref, gl_ref, wup_ref, bup_ref, gain_ref, o_ref, state_ref):
    tb = x_ref.shape[0]
    dk, dv, c = GLA_KEY_DIM, GLA_VAL_DIM, GLA_CHUNK

    @pl.when(pl.program_id(1) == 0)
    def _():
        state_ref[...] = jnp.zeros_like(state_ref)

    gl_hi, gl_lo = _split_bf16(gl_ref[...], 2)
    w_hi, w_lo = _split_bf16(wup_ref[...], 2)
    pre = _dot(gl_hi, w_hi) + _dot(gl_hi, w_lo) + _dot(gl_lo, w_hi) + bup_ref[...]
    g = (jnp.minimum(pre, 0.0) - jnp.log1p(jnp.exp(-jnp.abs(pre)))) * (1.0 / GLA_GATE_NORMALIZER)

    row = lax.broadcasted_iota(jnp.int32, (tb, tb), 0)
    colm = lax.broadcasted_iota(jnp.int32, (tb, tb), 1)
    same_chunk_lower = ((row // c) == (colm // c)) & (colm <= row)
    bcum_all = _dot_exact_lhs(same_chunk_lower.astype(BF16), g, 3)

    ri = lax.broadcasted_iota(jnp.int32, (c, c), 0)
    ci = lax.broadcasted_iota(jnp.int32, (c, c), 1)
    causal = ci <= ri
    for ch in range(tb // c):
        r0 = ch * c
        for h in range(GLA_HEADS):
            bcum = bcum_all[r0:r0 + c, dk * h:dk * (h + 1)]
            blast = bcum[c - 1:c, :]
            q = x_ref[r0:r0 + c, dk * h:dk * (h + 1)].astype(F32)
            k = x_ref[r0:r0 + c, GLA_KEY_WIDTH + dk * h:GLA_KEY_WIDTH + dk * (h + 1)].astype(F32)
            v = x_ref[r0:r0 + c, 2 * GLA_KEY_WIDTH + dv * h:2 * GLA_KEY_WIDTH + dv * (h + 1)]
            r = x_ref[r0:r0 + c, 2 * GLA_KEY_WIDTH + GLA_WIDTH + dv * h:
                      2 * GLA_KEY_WIDTH + GLA_WIDTH + dv * (h + 1)].astype(F32)
            qd = (q * (dk ** -0.5) * jnp.exp(bcum)).astype(BF16)
            kd = (k * jnp.exp(-bcum)).astype(BF16)
            kl = (k * jnp.exp(blast - bcum)).astype(BF16)
            att = jnp.where(causal, _dot_nt(qd, kd), 0.0)
            st = state_ref[h]
            o = _dot(att.astype(BF16), v) + _dot_nt(qd, st.astype(BF16))
            state_ref[h] = st * jnp.exp(blast) + _dot_tn(v, kl)
            o = _rms(o, gain_ref[...]) * _silu(r)
            o_ref[r0:r0 + c, dv * h:dv * (h + 1)] = o.astype(o_ref.dtype)


def gla_mixer(x, small, w_up_pad, b_up, norm_gain, batch, seq):
    tb = min(GLA_TIME_BLOCK, seq)
    nt = seq // tb
    width = x.shape[1]
    nbytes = (_nbytes((tb, width), BF16) + _nbytes((tb, V7X_LANES), F32) + _nbytes(w_up_pad.shape, F32)
              + _nbytes((tb, GLA_WIDTH), BF16) + _nbytes((GLA_HEADS, GLA_VAL_DIM, GLA_KEY_DIM), F32)
              + 8 * _nbytes((tb, GLA_KEY_WIDTH), F32))
    return pl.pallas_call(
        _gla_body,
        grid=(batch, nt),
        in_specs=[pl.BlockSpec((tb, width), lambda b, t: (b * nt + t, 0)),
                  pl.BlockSpec((tb, V7X_LANES), lambda b, t: (b * nt + t, 0)),
                  pl.BlockSpec(w_up_pad.shape, lambda b, t: (0, 0)),
                  pl.BlockSpec((1, GLA_KEY_WIDTH), lambda b, t: (0, 0)),
                  pl.BlockSpec((1, GLA_VAL_DIM), lambda b, t: (0, 0))],
        out_specs=pl.BlockSpec((tb, GLA_WIDTH), lambda b, t: (b * nt + t, 0)),
        out_shape=jax.ShapeDtypeStruct((batch * seq, GLA_WIDTH), BF16),
        scratch_shapes=[pltpu.VMEM((GLA_HEADS, GLA_VAL_DIM, GLA_KEY_DIM), F32)],
        compiler_params=_params(("parallel", "arbitrary"), nbytes),
        name="gla_mixer",
    )(x, small, w_up_pad, b_up.reshape(1, -1), norm_gain.reshape(1, -1))


def _ssd_body(x_ref, dt_ref, cw_ref, cb_ref, dtb_ref, alog_ref, dexp_ref, gain_ref, expand_ref,
              o_ref, state_ref, tail_ref, y_ref):
    L, P, N = SSD_CHUNK, SSD_HEAD_DIM, SSD_STATE
    gw = SSD_GROUP_WIDTH

    @pl.when(pl.program_id(1) == 0)
    def _():
        state_ref[...] = jnp.zeros_like(state_ref)
        tail_ref[...] = jnp.zeros_like(tail_ref)

    xin = x_ref[:, SSD_WIDTH:].astype(F32)
    xext = jnp.concatenate([tail_ref[...], xin], axis=0)
    tail_ref[...] = xin[L - 8:, :]
    conv = cb_ref[...] + cw_ref[SSD_CONV - 1:SSD_CONV, :] * xin
    for j in range(1, SSD_CONV):
        conv = conv + cw_ref[SSD_CONV - 1 - j:SSD_CONV - j, :] * xext[8 - j:8 - j + L, :]
    xbc = _silu(conv)
    xs = xbc[:, :SSD_WIDTH]
    bm = xbc[:, SSD_WIDTH:SSD_WIDTH + SSD_BC_WIDTH].astype(BF16)
    cm = xbc[:, SSD_WIDTH + SSD_BC_WIDTH:].astype(BF16)

    dt = _softplus(dt_ref[...] + dtb_ref[...])
    dta = dt * (-jnp.exp(alog_ref[...]))
    ri = lax.broadcasted_iota(jnp.int32, (L, L), 0)
    ci = lax.broadcasted_iota(jnp.int32, (L, L), 1)
    causal = ci <= ri
    a_cum = _dot_exact_lhs(causal.astype(BF16), dta, 3)
    a_cum_t = a_cum.T
    expand = expand_ref[...]
    a_exp = _dot_exact_rhs(a_cum, expand, 3)
    dt_exp = _dot_exact_rhs(dt, expand, 2)
    a_last = a_exp[L - 1:L, :]
    xd = xs * dt_exp
    xdec = (xd * jnp.exp(a_last - a_exp)).astype(BF16)
    xd16 = xd.astype(BF16)
    out_scale = jnp.exp(a_exp)

    for g in range(SSD_GROUPS):
        bg = bm[:, N * g:N * (g + 1)]
        cg = cm[:, N * g:N * (g + 1)]
        cb = _dot_nt(cg, bg)
        for kk in range(SSD_HEADS_PER_GROUP):
            h = g * SSD_HEADS_PER_GROUP + kk
            hl = SSD_DT_LANE0 + h
            diff = a_cum[:, hl:hl + 1] - a_cum_t[hl:hl + 1, :]
            m = cb * jnp.exp(jnp.where(causal, diff, -jnp.inf))
            y_ref[:, P * h:P * (h + 1)] = _dot(m.astype(BF16), xd16[:, P * h:P * (h + 1)])
        sg = state_ref[:, gw * g:gw * (g + 1)]
        y_off = _dot(cg, sg.astype(BF16)) * out_scale[:, gw * g:gw * (g + 1)]
        state_ref[:, gw * g:gw * (g + 1)] = (sg * jnp.exp(a_last[:, gw * g:gw * (g + 1)])
                                             + _dot_tn(bg, xdec[:, gw * g:gw * (g + 1)]))
        yg = y_ref[:, gw * g:gw * (g + 1)] + y_off + xs[:, gw * g:gw * (g + 1)] * dexp_ref[:, gw * g:gw * (g + 1)]
        yg = yg * _silu(x_ref[:, gw * g:gw * (g + 1)].astype(F32))
        o_ref[:, gw * g:gw * (g + 1)] = _rms(yg, gain_ref[:, gw * g:gw * (g + 1)]).astype(o_ref.dtype)


def ssd_mixer(x, small, conv_w, conv_b, dt_bias, a_log, d_skip, norm_gain, batch, seq):
    L = SSD_CHUNK
    nc = seq // L
    width = x.shape[1]
    lane_pad = (SSD_DT_LANE0, V7X_LANES - SSD_DT_LANE0 - SSD_HEADS)
    dtb = jnp.pad(dt_bias, lane_pad).reshape(1, V7X_LANES)
    alog = jnp.pad(a_log, lane_pad).reshape(1, V7X_LANES)
    dexp = jnp.repeat(d_skip, SSD_HEAD_DIM).reshape(1, SSD_WIDTH)
    expand = (jnp.arange(V7X_LANES)[:, None] - SSD_DT_LANE0
              == (jnp.arange(SSD_WIDTH)[None, :] // SSD_HEAD_DIM)).astype(BF16)
    nbytes = (_nbytes((L, width), BF16) + _nbytes((L, V7X_LANES), F32) + _nbytes((L, SSD_WIDTH), BF16)
              + _nbytes(expand.shape, BF16) + 2 * _nbytes((SSD_STATE, SSD_WIDTH), F32)
              + 12 * _nbytes((L, SSD_CONV_CH), F32))
    return pl.pallas_call(
        _ssd_body,
        grid=(batch, nc),
        in_specs=[pl.BlockSpec((L, width), lambda b, c: (b * nc + c, 0)),
                  pl.BlockSpec((L, V7X_LANES), lambda b, c: (b * nc + c, 1)),
                  pl.BlockSpec((SSD_CONV, SSD_CONV_CH), lambda b, c: (0, 0)),
                  pl.BlockSpec((1, SSD_CONV_CH), lambda b, c: (0, 0)),
                  pl.BlockSpec((1, V7X_LANES), lambda b, c: (0, 0)),
                  pl.BlockSpec((1, V7X_LANES), lambda b, c: (0, 0)),
                  pl.BlockSpec((1, SSD_WIDTH), lambda b, c: (0, 0)),
                  pl.BlockSpec((1, SSD_WIDTH), lambda b, c: (0, 0)),
                  pl.BlockSpec(expand.shape, lambda b, c: (0, 0))],
        out_specs=pl.BlockSpec((L, SSD_WIDTH), lambda b, c: (b * nc + c, 0)),
        out_shape=jax.ShapeDtypeStruct((batch * seq, SSD_WIDTH), BF16),
        scratch_shapes=[pltpu.VMEM((SSD_STATE, SSD_WIDTH), F32),
                        pltpu.VMEM((8, SSD_CONV_CH), F32),
                        pltpu.VMEM((L, SSD_WIDTH), F32)],
        compiler_params=_params(("parallel", "arbitrary"), nbytes),
        name="ssd_mixer",
    )(x, small, conv_w.reshape(SSD_CONV, SSD_CONV_CH), conv_b.reshape(1, -1), dtb, alog, dexp,
      norm_gain.reshape(1, -1), expand)


def _xattn_body(q_ref, kv_ref, qg_ref, kg_ref, o_ref):
    hd = X_HEAD_DIM
    for h in range(X_HEADS):
        qh = _rms(q_ref[:, hd * h:hd * (h + 1)], qg_ref[...]).astype(BF16)
        kh = _rms(kv_ref[:, hd * h:hd * (h + 1)], kg_ref[...]).astype(BF16)
        vh = kv_ref[:, X_WIDTH + hd * h:X_WIDTH + hd * (h + 1)].astype(BF16)
        s = _dot_nt(qh, kh) * (hd ** -0.5)
        p = jnp.exp(s - jnp.max(s, axis=-1, keepdims=True))
        oh = _dot(p.astype(BF16), vh) / jnp.sum(p, axis=-1, keepdims=True)
        o_ref[:, hd * h:hd * (h + 1)] = oh.astype(o_ref.dtype)


def cross_attention_core(q, kv, q_gain, k_gain, batch, seq, mem_len, tq=512):
    tq = min(tq, seq)
    nq = seq // tq
    nbytes = _nbytes((tq, X_WIDTH), F32) + _nbytes((mem_len, 2 * X_WIDTH), F32) + _nbytes((tq, X_WIDTH), BF16) \
        + 4 * _nbytes((tq, mem_len), F32)
    return pl.pallas_call(
        _xattn_body,
        grid=(batch, nq),
        in_specs=[pl.BlockSpec((tq, X_WIDTH), lambda b, t: (b * nq + t, 0)),
                  pl.BlockSpec((mem_len, 2 * X_WIDTH), lambda b, t: (b, 0)),
                  pl.BlockSpec((1, X_HEAD_DIM), lambda b, t: (0, 0)),
                  pl.BlockSpec((1, X_HEAD_DIM), lambda b, t: (0, 0))],
        out_specs=pl.BlockSpec((tq, X_WIDTH), lambda b, t: (b * nq + t, 0)),
        out_shape=jax.ShapeDtypeStruct((batch * seq, X_WIDTH), BF16),
        compiler_params=_params(("parallel", "parallel"), nbytes),
        name="xattn_core",
    )(q, kv, q_gain.reshape(1, -1), k_gain.reshape(1, -1))


_IN_OFF = np.cumsum([0, SWA_WIDTH, SWA_KV_WIDTH, SWA_KV_WIDTH, GLA_KEY_WIDTH, GLA_KEY_WIDTH, GLA_WIDTH,
                     GLA_WIDTH, GLA_GATE_RANK, SSD_WIDTH, SSD_CONV_CH, SSD_HEADS]).tolist()
SWA_COL0, GLA_COL0, GLOW_COL0, SSD_COL0, DT_COL0 = _IN_OFF[0], _IN_OFF[3], _IN_OFF[7], _IN_OFF[8], _IN_OFF[10]


IN_DIM = _IN_OFF[-1]
SMALL_ROW_STARTS = (GLOW_COL0, IN_DIM - V7X_LANES)
assert SMALL_ROW_STARTS[1] + SSD_DT_LANE0 == DT_COL0


def kernel(x, mem, rel_bias, ln_mix, w_in, swa_q_gain, swa_k_gain, swa_sinks, swa_out_gain, gla_w_gk_up, gla_b_gk_up, gla_norm_gain, ssd_conv_w, ssd_conv_b, ssd_dt_bias, ssd_a_log, ssd_d, ssd_norm_gain, w_mix_out, ln_x, ln_mem, x_w_q, x_w_k, x_w_v, x_w_o, x_q_gain, x_k_gain, ln_ffn, ffn_w_gate, ffn_w_up, ffn_w_down):
    batch, seq, d = x.shape
    mem_len = mem.shape[1]
    m = batch * seq
    band_bias = swa_band_bias(rel_bias)
    h = x.reshape(m, d)
    mem2 = mem.reshape(batch * mem_len, d)
    w_in_t = jnp.swapaxes(w_in, 1, 2)
    w_down16 = ffn_w_down.astype(BF16)
    for l in range(DEPTH):
        w_up_pad = jnp.pad(gla_w_gk_up[l], ((0, V7X_LANES - GLA_GATE_RANK), (0, 0)))
        w_kv = jnp.concatenate([x_w_k[l], x_w_v[l]], axis=1).astype(BF16)
        hn = rmsnorm(h, ln_mix[l])
        swa_tile = (GLA_COL0 - SWA_COL0) // 2
        p_swa = proj_wcast(hn, w_in_t, l, lambda j: SWA_COL0 + swa_tile * j, 2, swa_tile,
                           out_dtype=BF16, name="proj_swa")
        p_gla = proj_wcast(hn, w_in_t, l, lambda j: GLA_COL0 + PROJ_TILE * j, (GLOW_COL0 - GLA_COL0) // PROJ_TILE,
                           PROJ_TILE, out_dtype=BF16, name="proj_gla")
        p_ssd = proj_wcast(hn, w_in_t, l, lambda j: SSD_COL0 + PROJ_TILE * j, (DT_COL0 - SSD_COL0) // PROJ_TILE,
                           PROJ_TILE, out_dtype=BF16, name="proj_ssd")
        p_small = proj_wcast(hn, w_in_t, l, lambda j: jnp.where(j == 0, *SMALL_ROW_STARTS), 2, V7X_LANES,
                             out_dtype=F32, name="proj_small")
        y_a = swa_mixer(p_swa, band_bias, swa_sinks[l], swa_q_gain[l], swa_k_gain[l], swa_out_gain[l], batch, seq)
        y_b = gla_mixer(p_gla, p_small, w_up_pad, gla_b_gk_up[l], gla_norm_gain[l], batch, seq)
        y_c = ssd_mixer(p_ssd, p_small, ssd_conv_w[l], ssd_conv_b[l], ssd_dt_bias[l], ssd_a_log[l], ssd_d[l],
                        ssd_norm_gain[l], batch, seq)
        h = mix_out(y_a, y_b, y_c, w_mix_out, l, h)
        hx = rmsnorm(h, ln_x[l])
        memn = rmsnorm(mem2, ln_mem[l])
        q = matmul(hx, x_w_q[l].astype(BF16), out_dtype=F32, tm=1024, tn=512, name="xattn_q")
        kv = matmul(memn, w_kv, out_dtype=F32, tm=1024, tn=512, name="xattn_kv")
        o = cross_attention_core(q, kv, x_q_gain[l], x_k_gain[l], batch, seq, mem_len)
        h = matmul(o, x_w_o[l].astype(BF16), out_dtype=F32, res=h, tm=1024, tn=1024, name="xattn_out")
        hf = rmsnorm(h, ln_ffn[l])
        hidden = ffn_gate_up(hf, ffn_w_gate, ffn_w_up, l)
        h = matmul(hidden, w_down16, layer=l, out_dtype=F32, res=h, tm=512, tn=256, name="ffn_down")
    return h.reshape(batch, seq, d)
```

```python
import functools
import math

import numpy as np
import jax
import jax.numpy as jnp
from jax import lax
from jax.experimental import pallas as pl
from jax.experimental.pallas import tpu as pltpu

F32 = jnp.float32
BF16 = jnp.bfloat16

D_MODEL = 4096
DEPTH = 2
EPS = 1e-6
SWA_WIDTH = 1024
SWA_HEAD_DIM = 64
SWA_HEADS = 16
SWA_KV_HEADS = 2
SWA_GROUP = SWA_HEADS // SWA_KV_HEADS
SWA_KV_WIDTH = SWA_KV_HEADS * SWA_HEAD_DIM
SWA_WINDOW = 128
SWA_BLOCK = 128
REL_BUCKETS = 32
REL_MAX_DIST = 128
GLA_WIDTH = 1024
GLA_HEADS = 4
GLA_VAL_DIM = 256
GLA_KEY_DIM = 128
GLA_KEY_WIDTH = GLA_HEADS * GLA_KEY_DIM
GLA_GATE_RANK = 16
GLA_GATE_NORMALIZER = 16.0
GLA_CHUNK = 64
SSD_WIDTH = 2048
SSD_HEAD_DIM = 64
SSD_HEADS = 32
SSD_GROUPS = 8
SSD_HEADS_PER_GROUP = SSD_HEADS // SSD_GROUPS
SSD_STATE = 128
SSD_CONV = 4
SSD_CHUNK = 128
SSD_BC_WIDTH = SSD_GROUPS * SSD_STATE
SSD_CONV_CH = SSD_WIDTH + 2 * SSD_BC_WIDTH
SSD_GROUP_WIDTH = SSD_WIDTH // SSD_GROUPS
X_HEADS = 4
X_HEAD_DIM = 128
X_WIDTH = X_HEADS * X_HEAD_DIM
FFN_HIDDEN = 11008

V7X_LANES = 128
V7X_VMEM_BYTES = 64 * 1024 * 1024
VMEM_REQUEST_CAP = (V7X_VMEM_BYTES * 7) // 8
V7X_MXU_WIDTH = 256
FFN_TILE = V7X_MXU_WIDTH
PROJ_TILE = 2 * V7X_MXU_WIDTH
GLA_TIME_BLOCK = 256
SSD_DT_LANE0 = V7X_LANES - SSD_HEADS


def _params(semantics, block_bytes):
    limit = min(VMEM_REQUEST_CAP, 2 * block_bytes + 16 * 1024 * 1024)
    return pltpu.CompilerParams(dimension_semantics=semantics, vmem_limit_bytes=int(limit))


def _nbytes(shape, dtype):
    return int(np.prod(shape)) * jnp.dtype(dtype).itemsize


def _split_bf16(x, terms):
    parts = []
    r = x
    for t in range(terms):
        p = r.astype(BF16)
        parts.append(p)
        if t + 1 < terms:
            r = r - p.astype(F32)
    return parts


def _dot(a, b):
    return jnp.dot(a, b, preferred_element_type=F32)


def _dot_nt(a, b):
    return lax.dot_general(a, b, (((1,), (1,)), ((), ())), preferred_element_type=F32)


def _dot_tn(a, b):
    return lax.dot_general(a, b, (((0,), (0,)), ((), ())), preferred_element_type=F32)


def _dot_exact_lhs(a_bf16, x, terms):
    acc = None
    for p in _split_bf16(x, terms):
        d = _dot(a_bf16, p)
        acc = d if acc is None else acc + d
    return acc


def _dot_exact_rhs(x, b_bf16, terms):
    acc = None
    for p in _split_bf16(x, terms):
        d = _dot(p, b_bf16)
        acc = d if acc is None else acc + d
    return acc


def _silu(x):
    return x / (1.0 + jnp.exp(-x))


def _softplus(x):
    return jnp.maximum(x, 0.0) + jnp.log1p(jnp.exp(-jnp.abs(x)))


def _rms(x, gain):
    return x * lax.rsqrt(jnp.mean(x * x, axis=-1, keepdims=True) + EPS) * gain


def _rmsnorm_body(x_ref, g_ref, o_ref):
    o_ref[...] = _rms(x_ref[...], g_ref[...]).astype(o_ref.dtype)


def rmsnorm(x, gain, tm=256):
    m, d = x.shape
    tm = min(tm, m)
    return pl.pallas_call(
        _rmsnorm_body,
        grid=(m // tm,),
        in_specs=[pl.BlockSpec((tm, d), lambda i: (i, 0)),
                  pl.BlockSpec((1, d), lambda i: (0, 0))],
        out_specs=pl.BlockSpec((tm, d), lambda i: (i, 0)),
        out_shape=jax.ShapeDtypeStruct((m, d), BF16),
        compiler_params=_params(("parallel",), _nbytes((tm, d), F32) + _nbytes((tm, d), BF16)),
        name="rmsnorm",
    )(x, gain.reshape(1, d))


def _mm_body(*refs, nk, has_res):
    a_ref, w_ref = refs[0], refs[1]
    r_ref = refs[2] if has_res else None
    o_ref = refs[2 + has_res]
    part = _dot(a_ref[...], w_ref[...])
    if nk == 1:
        if has_res:
            part = part + r_ref[...]
        o_ref[...] = part.astype(o_ref.dtype)
        return
    k = pl.program_id(2)

    @pl.when(k == 0)
    def _():
        o_ref[...] = (part + r_ref[...]) if has_res else part

    @pl.when(k > 0)
    def _():
        o_ref[...] += part


def matmul(a, w, *, out_dtype, res=None, tm, tn, tk=None, layer=None, name="matmul"):
    m, kdim = a.shape
    n = w.shape[-1]
    tm, tn = min(tm, m), min(tn, n)
    tk = kdim if tk is None else tk
    nk = kdim // tk
    assert m % tm == 0 and n % tn == 0 and kdim % tk == 0
    assert nk == 1 or out_dtype == F32
    has_res = res is not None
    if layer is None:
        w_spec = pl.BlockSpec((tk, tn), lambda i, j, k: (k, j))
    else:
        w_spec = pl.BlockSpec((None, tk, tn), lambda i, j, k: (layer, k, j))
    in_specs = [pl.BlockSpec((tm, tk), lambda i, j, k: (i, k)), w_spec]
    args = [a, w]
    nbytes = _nbytes((tm, tk), BF16) + _nbytes((tk, tn), BF16) + _nbytes((tm, tn), out_dtype) + _nbytes((tm, tn), F32)
    if has_res:
        in_specs.append(pl.BlockSpec((tm, tn), lambda i, j, k: (i, j)))
        args.append(res)
        nbytes += _nbytes((tm, tn), F32)
    return pl.pallas_call(
        functools.partial(_mm_body, nk=nk, has_res=has_res),
        grid=(m // tm, n // tn, nk),
        in_specs=in_specs,
        out_specs=pl.BlockSpec((tm, tn), lambda i, j, k: (i, j)),
        out_shape=jax.ShapeDtypeStruct((m, n), out_dtype),
        compiler_params=_params(("parallel", "parallel", "arbitrary"), nbytes),
        name=name,
    )(*args)


def _mixout_body(ya_ref, yb_ref, yc_ref, w_ref, r_ref, o_ref, wb_ref):
    @pl.when(pl.program_id(1) == 0)
    def _():
        wb_ref[...] = w_ref[...].astype(BF16)

    acc = r_ref[...]
    acc = acc + _dot(ya_ref[...], wb_ref[0:SWA_WIDTH, :])
    acc = acc + _dot(yb_ref[...], wb_ref[SWA_WIDTH:SWA_WIDTH + GLA_WIDTH, :])
    acc = acc + _dot(yc_ref[...], wb_ref[SWA_WIDTH + GLA_WIDTH:, :])
    o_ref[...] = acc


def mix_out(ya, yb, yc, w_all, layer, res, tm=1024, tn=PROJ_TILE):
    m = ya.shape[0]
    _, kdim, n = w_all.shape
    tm = min(tm, m)
    nbytes = (_nbytes((tm, kdim), BF16) + _nbytes((kdim, tn), F32) + _nbytes((kdim, tn), BF16)
              + 3 * _nbytes((tm, tn), F32))
    return pl.pallas_call(
        _mixout_body,
        grid=(n // tn, m // tm),
        in_specs=[pl.BlockSpec((tm, SWA_WIDTH), lambda j, i: (i, 0)),
                  pl.BlockSpec((tm, GLA_WIDTH), lambda j, i: (i, 0)),
                  pl.BlockSpec((tm, SSD_WIDTH), lambda j, i: (i, 0)),
                  pl.BlockSpec((None, kdim, tn), lambda j, i: (layer, 0, j)),
                  pl.BlockSpec((tm, tn), lambda j, i: (i, j))],
        out_specs=pl.BlockSpec((tm, tn), lambda j, i: (i, j)),
        out_shape=jax.ShapeDtypeStruct((m, n), F32),
        scratch_shapes=[pltpu.VMEM((kdim, tn), BF16)],
        compiler_params=_params(("parallel", "arbitrary"), nbytes),
        name="mix_out",
    )(ya, yb, yc, w_all, res)


_XPOSE_ROWS, _XPOSE_COLS = 256, 512


def _proj_body(a_ref, w_ref, o_ref, wb_ref):
    _, tn, kdim = w_ref.shape

    @pl.when(pl.program_id(1) == 0)
    def _():
        for r0 in range(0, tn, _XPOSE_ROWS):
            r1 = min(r0 + _XPOSE_ROWS, tn)
            for k0 in range(0, kdim, _XPOSE_COLS):
                wb_ref[k0:k0 + _XPOSE_COLS, r0:r1] = w_ref[0, r0:r1, k0:k0 + _XPOSE_COLS].T.astype(BF16)

    o_ref[...] = _dot(a_ref[...], wb_ref[...]).astype(o_ref.dtype)


def proj_wcast(a, wt_all, layer, row_start, n_tiles, tn, *, out_dtype, tm=1024, name="proj"):
    m, kdim = a.shape
    tm = min(tm, m)
    nbytes = (_nbytes((tm, kdim), BF16) + _nbytes((kdim, tn), F32) + _nbytes((kdim, tn), BF16)
              + _nbytes((tm, tn), out_dtype) + _nbytes((tm, tn), F32))
    return pl.pallas_call(
        _proj_body,
        grid=(n_tiles, m // tm),
        in_specs=[pl.BlockSpec((tm, kdim), lambda j, i: (i, 0)),
                  pl.BlockSpec((pl.Element(1), pl.Element(tn), pl.Element(kdim)),
                               lambda j, i: (layer, pl.multiple_of(row_start(j), 8), 0))],
        out_specs=pl.BlockSpec((tm, tn), lambda j, i: (i, j)),
        out_shape=jax.ShapeDtypeStruct((m, n_tiles * tn), out_dtype),
        scratch_shapes=[pltpu.VMEM((kdim, tn), BF16)],
        compiler_params=_params(("parallel", "arbitrary"), nbytes),
        name=name,
    )(a, wt_all)


def _gateup_body(a_ref, wg_ref, wu_ref, o_ref, wgb_ref, wub_ref):
    @pl.when(pl.program_id(1) == 0)
    def _():
        wgb_ref[...] = wg_ref[...].astype(BF16)
        wub_ref[...] = wu_ref[...].astype(BF16)

    a = a_ref[...]
    g = _dot(a, wgb_ref[...])
    u = _dot(a, wub_ref[...])
    o_ref[...] = (_silu(g) * u).astype(o_ref.dtype)


def ffn_gate_up(a, wg_all, wu_all, layer, tm=1024, tn=FFN_TILE):
    m, kdim = a.shape
    n = wg_all.shape[2]
    tm = min(tm, m)
    nbytes = (_nbytes((tm, kdim), BF16) + 2 * _nbytes((kdim, tn), F32) + 2 * _nbytes((kdim, tn), BF16)
              + _nbytes((tm, tn), BF16) + 3 * _nbytes((tm, tn), F32))
    return pl.pallas_call(
        _gateup_body,
        grid=(n // tn, m // tm),
        in_specs=[pl.BlockSpec((tm, kdim), lambda j, i: (i, 0)),
                  pl.BlockSpec((None, kdim, tn), lambda j, i: (layer, 0, j)),
                  pl.BlockSpec((None, kdim, tn), lambda j, i: (layer, 0, j))],
        out_specs=pl.BlockSpec((tm, tn), lambda j, i: (i, j)),
        out_shape=jax.ShapeDtypeStruct((m, n), BF16),
        scratch_shapes=[pltpu.VMEM((kdim, tn), BF16), pltpu.VMEM((kdim, tn), BF16)],
        compiler_params=_params(("parallel", "arbitrary"), nbytes),
        name="ffn_gate_up",
    )(a, wg_all, wu_all)


def _swa_body(q_ref, kvc_ref, kvp_ref, bias_ref, sink_ref, qg_ref, kg_ref, og_ref, e_ref, et_ref, o_ref, acc_ref):
    hd, blk, grp = SWA_HEAD_DIM, SWA_BLOCK, SWA_GROUP
    kband = jnp.concatenate([kvp_ref[:, :SWA_KV_WIDTH], kvc_ref[:, :SWA_KV_WIDTH]], axis=0).astype(F32)
    vband = jnp.concatenate([kvp_ref[:, SWA_KV_WIDTH:], kvc_ref[:, SWA_KV_WIDTH:]], axis=0)
    ones = jnp.ones((2 * blk, V7X_LANES), BF16)

    q = q_ref[...].astype(F32)
    ssq = _dot_exact_rhs(q * q, e_ref[...], 2)
    inv = lax.rsqrt(ssq * (1.0 / hd) + EPS)
    qn = (q * _dot_exact_rhs(inv, et_ref[...], 3) * qg_ref[...]).astype(BF16)

    for j in range(SWA_KV_HEADS):
        kn = _rms(kband[:, hd * j:hd * (j + 1)], kg_ref[...]).astype(BF16)
        qs = jnp.concatenate([qn[:, hd * (grp * j + g):hd * (grp * j + g + 1)] for g in range(grp)], axis=0)
        s = _dot_nt(qs, kn) * (hd ** -0.5) + bias_ref[j]
        sink = sink_ref[j]
        s_prev, s_cur = s[:, :blk], s[:, blk:]
        row_max = jnp.max(jnp.maximum(s_prev, s_cur), axis=-1, keepdims=True)
        mx = jnp.maximum(jnp.broadcast_to(row_max, sink.shape), sink)
        p = jnp.concatenate([jnp.exp(s_prev - mx), jnp.exp(s_cur - mx)], axis=1).astype(BF16)
        total = _dot(p, ones) + jnp.exp(sink - mx)
        o = _dot(p, vband[:, hd * j:hd * (j + 1)]) / total[:, :hd]
        for g in range(grp):
            h = grp * j + g
            acc_ref[:, hd * h:hd * (h + 1)] = o[blk * g:blk * (g + 1), :]
    o_ref[...] = _rms(acc_ref[...], og_ref[...]).astype(o_ref.dtype)


def swa_mixer(qkv, bias, sinks, q_gain, k_gain, out_gain, batch, seq):
    nb = seq // SWA_BLOCK
    kv_blk = SWA_WIDTH // (2 * SWA_KV_WIDTH)
    rows = SWA_GROUP * SWA_BLOCK
    no_prev = jnp.arange(2 * SWA_BLOCK) < SWA_BLOCK
    bias_g = jnp.stack([jnp.where(no_prev, -jnp.inf, bias), bias]).reshape(2, SWA_KV_HEADS, rows, 2 * SWA_BLOCK)
    sink_col = jnp.broadcast_to(jnp.repeat(sinks.astype(F32), SWA_BLOCK)[:, None],
                                (SWA_HEADS * SWA_BLOCK, V7X_LANES)).reshape(SWA_KV_HEADS, rows, V7X_LANES)
    head_of_col = jnp.arange(SWA_WIDTH) // SWA_HEAD_DIM
    e = (head_of_col[:, None] == jnp.arange(V7X_LANES)[None, :]).astype(BF16)
    const = lambda shape: pl.BlockSpec(shape, lambda b, n: (0,) * len(shape))
    nbytes = (_nbytes((SWA_BLOCK, SWA_WIDTH), BF16) * 2 + 2 * _nbytes((SWA_BLOCK, 2 * SWA_KV_WIDTH), BF16)
              + _nbytes(bias.shape, F32) + _nbytes((SWA_KV_HEADS, rows, V7X_LANES), F32)
              + 2 * _nbytes(e.shape, BF16) + _nbytes((SWA_BLOCK, SWA_WIDTH), F32)
              + 4 * _nbytes((rows, 2 * SWA_BLOCK), F32))
    return pl.pallas_call(
        _swa_body,
        grid=(batch, nb),
        in_specs=[pl.BlockSpec((SWA_BLOCK, SWA_WIDTH), lambda b, n: (b * nb + n, 0)),
                  pl.BlockSpec((SWA_BLOCK, 2 * SWA_KV_WIDTH), lambda b, n: (b * nb + n, kv_blk)),
                  pl.BlockSpec((SWA_BLOCK, 2 * SWA_KV_WIDTH),
                               lambda b, n: (b * nb + jnp.maximum(n - 1, 0), kv_blk)),
                  pl.BlockSpec((None,) + bias_g.shape[1:], lambda b, n: (jnp.minimum(n, 1), 0, 0, 0)),
                  const(sink_col.shape),
                  const((1, SWA_WIDTH)), const((1, SWA_HEAD_DIM)), const((1, SWA_WIDTH)),
                  const(e.shape), const(e.T.shape)],
        out_specs=pl.BlockSpec((SWA_BLOCK, SWA_WIDTH), lambda b, n: (b * nb + n, 0)),
        out_shape=jax.ShapeDtypeStruct((batch * seq, SWA_WIDTH), BF16),
        scratch_shapes=[pltpu.VMEM((SWA_BLOCK, SWA_WIDTH), F32)],
        compiler_params=_params(("parallel", "parallel"), nbytes),
        name="swa_mixer",
    )(qkv, qkv, qkv, bias_g, sink_col, jnp.tile(q_gain, SWA_HEADS).reshape(1, -1), k_gain.reshape(1, -1),
      out_gain.reshape(1, -1), e, e.T)


def _t5_bucket(dist):
    n = jnp.maximum(dist, 0)
    max_exact = REL_BUCKETS // 2
    nf = jnp.maximum(n, 1).astype(F32)
    large = max_exact + (jnp.log(nf / max_exact) / math.log(REL_MAX_DIST / max_exact)
                         * (REL_BUCKETS - max_exact)).astype(jnp.int32)
    large = jnp.minimum(large, REL_BUCKETS - 1)
    return jnp.where(n < max_exact, n, large)


def swa_band_bias(rel_bias):
    i = jnp.arange(SWA_BLOCK, dtype=jnp.int32)[:, None]
    j = jnp.arange(2 * SWA_BLOCK, dtype=jnp.int32)[None, :]
    dist = i + SWA_BLOCK - j
    onehot = _t5_bucket(dist)[None] == jnp.arange(REL_BUCKETS, dtype=jnp.int32)[:, None, None]
    bias = jnp.sum(jnp.where(onehot[:, None], rel_bias.astype(F32)[:, :, None, None], 0.0), axis=0)
    in_window = (dist >= 0) & (dist < SWA_WINDOW)
    return jnp.where(in_window[None], bias, -jnp.inf)


def _gla_body(x_ref, gl_ref, wup_ref, bup_ref, gain_ref, o_ref, state_ref):
    tb = x_ref.shape[0]
    dk, dv, c = GLA_KEY_DIM, GLA_VAL_DIM, GLA_CHUNK

    @pl.when(pl.program_id(1) == 0)
    def _():
        state_ref[...] = jnp.zeros_like(state_ref)

    gl_hi, gl_lo = _split_bf16(gl_ref[...], 2)
    w_hi, w_lo = _split_bf16(wup_ref[...], 2)
    pre = _dot(gl_hi, w_hi) + _dot(gl_hi, w_lo) + _dot(gl_lo, w_hi) + bup_ref[...]
    g = (jnp.minimum(pre, 0.0) - jnp.log1p(jnp.exp(-jnp.abs(pre)))) * (1.0 / GLA_GATE_NORMALIZER)

    row = lax.broadcasted_iota(jnp.int32, (tb, tb), 0)
    colm = lax.broadcasted_iota(jnp.int32, (tb, tb), 1)
    same_chunk_lower = ((row // c) == (colm // c)) & (colm <= row)
    bcum_all = _dot_exact_lhs(same_chunk_lower.astype(BF16), g, 3)

    ri = lax.broadcasted_iota(jnp.int32, (c, c), 0)
    ci = lax.broadcasted_iota(jnp.int32, (c, c), 1)
    causal = ci <= ri
    for ch in range(tb // c):
        r0 = ch * c
        for h in range(GLA_HEADS):
            bcum = bcum_all[r0:r0 + c, dk * h:dk * (h + 1)]
            blast = bcum[c - 1:c, :]
            q = x_ref[r0:r0 + c, dk * h:dk * (h + 1)].astype(F32)
            k = x_ref[r0:r0 + c, GLA_KEY_WIDTH + dk * h:GLA_KEY_WIDTH + dk * (h + 1)].astype(F32)
            v = x_ref[r0:r0 + c, 2 * GLA_KEY_WIDTH + dv * h:2 * GLA_KEY_WIDTH + dv * (h + 1)]
            r = x_ref[r0:r0 + c, 2 * GLA_KEY_WIDTH + GLA_WIDTH + dv * h:
                      2 * GLA_KEY_WIDTH + GLA_WIDTH + dv * (h + 1)].astype(F32)
            qd = (q * (dk ** -0.5) * jnp.exp(bcum)).astype(BF16)
            kd = (k * jnp.exp(-bcum)).astype(BF16)
            kl = (k * jnp.exp(blast - bcum)).astype(BF16)
            att = jnp.where(causal, _dot_nt(qd, kd), 0.0)
            st = state_ref[h]
            o = _dot(att.astype(BF16), v) + _dot_nt(qd, st.astype(BF16))
            state_ref[h] = st * jnp.exp(blast) + _dot_tn(v, kl)
            o = _rms(o, gain_ref[...]) * _silu(r)
            o_ref[r0:r0 + c, dv * h:dv * (h + 1)] = o.astype(o_ref.dtype)


def gla_mixer(x, small, w_up_pad, b_up, norm_gain, batch, seq):
    tb = min(GLA_TIME_BLOCK, seq)
    nt = seq // tb
    width = x.shape[1]
    nbytes = (_nbytes((tb, width), BF16) + _nbytes((tb, V7X_LANES), F32) + _nbytes(w_up_pad.shape, F32)
              + _nbytes((tb, GLA_WIDTH), BF16) + _nbytes((GLA_HEADS, GLA_VAL_DIM, GLA_KEY_DIM), F32)
              + 8 * _nbytes((tb, GLA_KEY_WIDTH), F32))
    return pl.pallas_call(
        _gla_body,
        grid=(batch, nt),
        in_specs=[pl.BlockSpec((tb, width), lambda b, t: (b * nt + t, 0)),
                  pl.BlockSpec((tb, V7X_LANES), lambda b, t: (b * nt + t, 0)),
                  pl.BlockSpec(w_up_pad.shape, lambda b, t: (0, 0)),
                  pl.BlockSpec((1, GLA_KEY_WIDTH), lambda b, t: (0, 0)),
                  pl.BlockSpec((1, GLA_VAL_DIM), lambda b, t: (0, 0))],
        out_specs=pl.BlockSpec((tb, GLA_WIDTH), lambda b, t: (b * nt + t, 0)),
        out_shape=jax.ShapeDtypeStruct((batch * seq, GLA_WIDTH), BF16),
        scratch_shapes=[pltpu.VMEM((GLA_HEADS, GLA_VAL_DIM, GLA_KEY_DIM), F32)],
        compiler_params=_params(("parallel", "arbitrary"), nbytes),
        name="gla_mixer",
    )(x, small, w_up_pad, b_up.reshape(1, -1), norm_gain.reshape(1, -1))


def _ssd_body(x_ref, dt_ref, cw_ref, cb_ref, dtb_ref, alog_ref, dexp_ref, gain_ref, expand_ref,
              o_ref, state_ref, xe_ref, y_ref):
    L, P, N = SSD_CHUNK, SSD_HEAD_DIM, SSD_STATE
    gw = SSD_GROUP_WIDTH

    @pl.when(pl.program_id(1) == 0)
    def _():
        state_ref[...] = jnp.zeros_like(state_ref)
        xe_ref[0:8, :] = jnp.zeros((8, SSD_CONV_CH), F32)

    xe_ref[8:, :] = x_ref[:, SSD_WIDTH:].astype(F32)
    conv = cb_ref[...]
    for j in range(SSD_CONV):
        conv = conv + cw_ref[SSD_CONV - 1 - j:SSD_CONV - j, :] * xe_ref[8 - j:8 - j + L, :]
    xe_ref[0:8, :] = xe_ref[L:L + 8, :]
    xbc = _silu(conv)
    xs = xbc[:, :SSD_WIDTH]
    bm = xbc[:, SSD_WIDTH:SSD_WIDTH + SSD_BC_WIDTH].astype(BF16)
    cm = xbc[:, SSD_WIDTH + SSD_BC_WIDTH:].astype(BF16)

    dt = _softplus(dt_ref[...] + dtb_ref[...])
    dta = dt * (-jnp.exp(alog_ref[...]))
    ri = lax.broadcasted_iota(jnp.int32, (L, L), 0)
    ci = lax.broadcasted_iota(jnp.int32, (L, L), 1)
    causal = ci <= ri
    a_cum = _dot_exact_lhs(causal.astype(BF16), dta, 3)
    a_cum_t = a_cum.T
    expand = expand_ref[...]
    a_exp = _dot_exact_rhs(a_cum, expand, 3)
    dt_exp = _dot_exact_rhs(dt, expand, 2)
    a_last = a_exp[L - 1:L, :]
    xd = xs * dt_exp
    xdec = (xd * jnp.exp(a_last - a_exp)).astype(BF16)
    xd16 = xd.astype(BF16)
    out_scale = jnp.exp(a_exp)

    for g in range(SSD_GROUPS):
        bg = bm[:, N * g:N * (g + 1)]
        cg = cm[:, N * g:N * (g + 1)]
        cb = _dot_nt(cg, bg)
        for kk in range(SSD_HEADS_PER_GROUP):
            h = g * SSD_HEADS_PER_GROUP + kk
            hl = SSD_DT_LANE0 + h
            diff = a_cum[:, hl:hl + 1] - a_cum_t[hl:hl + 1, :]
            m = cb * jnp.exp(jnp.where(causal, diff, -jnp.inf))
            y_ref[:, P * h:P * (h + 1)] = _dot(m.astype(BF16), xd16[:, P * h:P * (h + 1)])
        sg = state_ref[:, gw * g:gw * (g + 1)]
        y_off = _dot(cg, sg.astype(BF16)) * out_scale[:, gw * g:gw * (g + 1)]
        state_ref[:, gw * g:gw * (g + 1)] = (sg * jnp.exp(a_last[:, gw * g:gw * (g + 1)])
                                             + _dot_tn(bg, xdec[:, gw * g:gw * (g + 1)]))
        yg = y_ref[:, gw * g:gw * (g + 1)] + y_off + xs[:, gw * g:gw * (g + 1)] * dexp_ref[:, gw * g:gw * (g + 1)]
        yg = yg * _silu(x_ref[:, gw * g:gw * (g + 1)].astype(F32))
        o_ref[:, gw * g:gw * (g + 1)] = _rms(yg, gain_ref[:, gw * g:gw * (g + 1)]).astype(o_ref.dtype)


def ssd_mixer(x, small, conv_w, conv_b, dt_bias, a_log, d_skip, norm_gain, batch, seq):
    L = SSD_CHUNK
    nc = seq // L
    width = x.shape[1]
    lane_pad = (SSD_DT_LANE0, V7X_LANES - SSD_DT_LANE0 - SSD_HEADS)
    dtb = jnp.pad(dt_bias, lane_pad).reshape(1, V7X_LANES)
    alog = jnp.pad(a_log, lane_pad).reshape(1, V7X_LANES)
    dexp = jnp.repeat(d_skip, SSD_HEAD_DIM).reshape(1, SSD_WIDTH)
    expand = (jnp.arange(V7X_LANES)[:, None] - SSD_DT_LANE0
              == (jnp.arange(SSD_WIDTH)[None, :] // SSD_HEAD_DIM)).astype(BF16)
    nbytes = (_nbytes((L, width), BF16) + _nbytes((L, V7X_LANES), F32) + _nbytes((L, SSD_WIDTH), BF16)
              + _nbytes(expand.shape, BF16) + 2 * _nbytes((SSD_STATE, SSD_WIDTH), F32)
              + 12 * _nbytes((L, SSD_CONV_CH), F32))
    return pl.pallas_call(
        _ssd_body,
        grid=(batch, nc),
        in_specs=[pl.BlockSpec((L, width), lambda b, c: (b * nc + c, 0)),
                  pl.BlockSpec((L, V7X_LANES), lambda b, c: (b * nc + c, 1)),
                  pl.BlockSpec((SSD_CONV, SSD_CONV_CH), lambda b, c: (0, 0)),
                  pl.BlockSpec((1, SSD_CONV_CH), lambda b, c: (0, 0)),
                  pl.BlockSpec((1, V7X_LANES), lambda b, c: (0, 0)),
                  pl.BlockSpec((1, V7X_LANES), lambda b, c: (0, 0)),
                  pl.BlockSpec((1, SSD_WIDTH), lambda b, c: (0, 0)),
                  pl.BlockSpec((1, SSD_WIDTH), lambda b, c: (0, 0)),
                  pl.BlockSpec(expand.shape, lambda b, c: (0, 0))],
        out_specs=pl.BlockSpec((L, SSD_WIDTH), lambda b, c: (b * nc + c, 0)),
        out_shape=jax.ShapeDtypeStruct((batch * seq, SSD_WIDTH), BF16),
        scratch_shapes=[pltpu.VMEM((SSD_STATE, SSD_WIDTH), F32),
                        pltpu.VMEM((8 + L, SSD_CONV_CH), F32),
                        pltpu.VMEM((L, SSD_WIDTH), F32)],
        compiler_params=_params(("parallel", "arbitrary"), nbytes),
        name="ssd_mixer",
    )(x, small, conv_w.reshape(SSD_CONV, SSD_CONV_CH), conv_b.reshape(1, -1), dtb, alog, dexp,
      norm_gain.reshape(1, -1), expand)


def _xattn_body(q_ref, kv_ref, qg_ref, kg_ref, o_ref):
    hd = X_HEAD_DIM
    for h in range(X_HEADS):
        qh = _rms(q_ref[:, hd * h:hd * (h + 1)], qg_ref[...]).astype(BF16)
        kh = _rms(kv_ref[:, hd * h:hd * (h + 1)], kg_ref[...]).astype(BF16)
        vh = kv_ref[:, X_WIDTH + hd * h:X_WIDTH + hd * (h + 1)].astype(BF16)
        s = _dot_nt(qh, kh) * (hd ** -0.5)
        p = jnp.exp(s - jnp.max(s, axis=-1, keepdims=True))
        oh = _dot(p.astype(BF16), vh) / jnp.sum(p, axis=-1, keepdims=True)
        o_ref[:, hd * h:hd * (h + 1)] = oh.astype(o_ref.dtype)


def cross_attention_core(q, kv, q_gain, k_gain, batch, seq, mem_len, tq=512):
    tq = min(tq, seq)
    nq = seq // tq
    nbytes = _nbytes((tq, X_WIDTH), F32) + _nbytes((mem_len, 2 * X_WIDTH), F32) + _nbytes((tq, X_WIDTH), BF16) \
        + 4 * _nbytes((tq, mem_len), F32)
    return pl.pallas_call(
        _xattn_body,
        grid=(batch, nq),
        in_specs=[pl.BlockSpec((tq, X_WIDTH), lambda b, t: (b * nq + t, 0)),
                  pl.BlockSpec((mem_len, 2 * X_WIDTH), lambda b, t: (b, 0)),
                  pl.BlockSpec((1, X_HEAD_DIM), lambda b, t: (0, 0)),
                  pl.BlockSpec((1, X_HEAD_DIM), lambda b, t: (0, 0))],
        out_specs=pl.BlockSpec((tq, X_WIDTH), lambda b, t: (b * nq + t, 0)),
        out_shape=jax.ShapeDtypeStruct((batch * seq, X_WIDTH), BF16),
        compiler_params=_params(("parallel", "parallel"), nbytes),
        name="xattn_core",
    )(q, kv, q_gain.reshape(1, -1), k_gain.reshape(1, -1))


_IN_OFF = np.cumsum([0, SWA_WIDTH, SWA_KV_WIDTH, SWA_KV_WIDTH, GLA_KEY_WIDTH, GLA_KEY_WIDTH, GLA_WIDTH,
                     GLA_WIDTH, GLA_GATE_RANK, SSD_WIDTH, SSD_CONV_CH, SSD_HEADS]).tolist()
SWA_COL0, GLA_COL0, GLOW_COL0, SSD_COL0, DT_COL0 = _IN_OFF[0], _IN_OFF[3], _IN_OFF[7], _IN_OFF[8], _IN_OFF[10]


IN_DIM = _IN_OFF[-1]
SMALL_ROW_STARTS = (GLOW_COL0, IN_DIM - V7X_LANES)
assert SMALL_ROW_STARTS[1] + SSD_DT_LANE0 == DT_COL0


def kernel(x, mem, rel_bias, ln_mix, w_in, swa_q_gain, swa_k_gain, swa_sinks, swa_out_gain, gla_w_gk_up, gla_b_gk_up, gla_norm_gain, ssd_conv_w, ssd_conv_b, ssd_dt_bias, ssd_a_log, ssd_d, ssd_norm_gain, w_mix_out, ln_x, ln_mem, x_w_q, x_w_k, x_w_v, x_w_o, x_q_gain, x_k_gain, ln_ffn, ffn_w_gate, ffn_w_up, ffn_w_down):
    batch, seq, d = x.shape
    mem_len = mem.shape[1]
    m = batch * seq
    band_bias = swa_band_bias(rel_bias)
    h = x.reshape(m, d)
    mem2 = mem.reshape(batch * mem_len, d)
    w_in_t = jnp.swapaxes(w_in, 1, 2)
    w_down16 = ffn_w_down.astype(BF16)
    for l in range(DEPTH):
        w_up_pad = jnp.pad(gla_w_gk_up[l], ((0, V7X_LANES - GLA_GATE_RANK), (0, 0)))
        w_kv = jnp.concatenate([x_w_k[l], x_w_v[l]], axis=1).astype(BF16)
        hn = rmsnorm(h, ln_mix[l])
        swa_tile = (GLA_COL0 - SWA_COL0) // 2
        p_swa = proj_wcast(hn, w_in_t, l, lambda j: SWA_COL0 + swa_tile * j, 2, swa_tile,
                           out_dtype=BF16, name="proj_swa")
        p_gla = proj_wcast(hn, w_in_t, l, lambda j: GLA_COL0 + PROJ_TILE * j, (GLOW_COL0 - GLA_COL0) // PROJ_TILE,
                           PROJ_TILE, out_dtype=BF16, name="proj_gla")
        p_ssd = proj_wcast(hn, w_in_t, l, lambda j: SSD_COL0 + PROJ_TILE * j, (DT_COL0 - SSD_COL0) // PROJ_TILE,
                           PROJ_TILE, out_dtype=BF16, name="proj_ssd")
        p_small = proj_wcast(hn, w_in_t, l, lambda j: jnp.where(j == 0, *SMALL_ROW_STARTS), 2, V7X_LANES,
                             out_dtype=F32, name="proj_small")
        y_a = swa_mixer(p_swa, band_bias, swa_sinks[l], swa_q_gain[l], swa_k_gain[l], swa_out_gain[l], batch, seq)
        y_b = gla_mixer(p_gla, p_small, w_up_pad, gla_b_gk_up[l], gla_norm_gain[l], batch, seq)
        y_c = ssd_mixer(p_ssd, p_small, ssd_conv_w[l], ssd_conv_b[l], ssd_dt_bias[l], ssd_a_log[l], ssd_d[l],
                        ssd_norm_gain[l], batch, seq)
        h = mix_out(y_a, y_b, y_c, w_mix_out, l, h)
        hx = rmsnorm(h, ln_x[l])
        memn = rmsnorm(mem2, ln_mem[l])
        q = matmul(hx, x_w_q[l].astype(BF16), out_dtype=F32, tm=1024, tn=512, name="xattn_q")
        kv = matmul(memn, w_kv, out_dtype=F32, tm=1024, tn=512, name="xattn_kv")
        o = cross_attention_core(q, kv, x_q_gain[l], x_k_gain[l], batch, seq, mem_len)
        h = matmul(o, x_w_o[l].astype(BF16), out_dtype=F32, res=h, tm=1024, tn=1024, name="xattn_out")
        hf = rmsnorm(h, ln_ffn[l])
        hidden = ffn_gate_up(hf, ffn_w_gate, ffn_w_up, l)
        h = matmul(hidden, w_down16, layer=l, out_dtype=F32, res=h, tm=512, tn=256, name="ffn_down")
    return h.reshape(batch, seq, d)
```

```python
import functools
import math

import numpy as np
import jax
import jax.numpy as jnp
from jax import lax
from jax.experimental import pallas as pl
from jax.experimental.pallas import tpu as pltpu

F32 = jnp.float32
BF16 = jnp.bfloat16

D_MODEL = 4096
DEPTH = 2
EPS = 1e-6
SWA_WIDTH = 1024
SWA_HEAD_DIM = 64
SWA_HEADS = 16
SWA_KV_HEADS = 2
SWA_GROUP = SWA_HEADS // SWA_KV_HEADS
SWA_KV_WIDTH = SWA_KV_HEADS * SWA_HEAD_DIM
SWA_WINDOW = 128
SWA_BLOCK = 128
REL_BUCKETS = 32
REL_MAX_DIST = 128
GLA_WIDTH = 1024
GLA_HEADS = 4
GLA_VAL_DIM = 256
GLA_KEY_DIM = 128
GLA_KEY_WIDTH = GLA_HEADS * GLA_KEY_DIM
GLA_GATE_RANK = 16
GLA_GATE_NORMALIZER = 16.0
GLA_CHUNK = 64
SSD_WIDTH = 2048
SSD_HEAD_DIM = 64
SSD_HEADS = 32
SSD_GROUPS = 8
SSD_HEADS_PER_GROUP = SSD_HEADS // SSD_GROUPS
SSD_STATE = 128
SSD_CONV = 4
SSD_CHUNK = 128
SSD_BC_WIDTH = SSD_GROUPS * SSD_STATE
SSD_CONV_CH = SSD_WIDTH + 2 * SSD_BC_WIDTH
SSD_GROUP_WIDTH = SSD_WIDTH // SSD_GROUPS
X_HEADS = 4
X_HEAD_DIM = 128
X_WIDTH = X_HEADS * X_HEAD_DIM
FFN_HIDDEN = 11008

V7X_LANES = 128
V7X_VMEM_BYTES = 64 * 1024 * 1024
VMEM_REQUEST_CAP = (V7X_VMEM_BYTES * 7) // 8
V7X_MXU_WIDTH = 256
FFN_TILE = V7X_MXU_WIDTH
PROJ_TILE = 2 * V7X_MXU_WIDTH
GLA_TIME_BLOCK = 256
SSD_DT_LANE0 = V7X_LANES - SSD_HEADS


def _params(semantics, block_bytes):
    limit = min(VMEM_REQUEST_CAP, 2 * block_bytes + 16 * 1024 * 1024)
    return pltpu.CompilerParams(dimension_semantics=semantics, vmem_limit_bytes=int(limit))


def _nbytes(shape, dtype):
    return int(np.prod(shape)) * jnp.dtype(dtype).itemsize


def _split_bf16(x, terms):
    parts = []
    r = x
    for t in range(terms):
        p = r.astype(BF16)
        parts.append(p)
        if t + 1 < terms:
            r = r - p.astype(F32)
    return parts


def _dot(a, b):
    return jnp.dot(a, b, preferred_element_type=F32)


def _dot_nt(a, b):
    return lax.dot_general(a, b, (((1,), (1,)), ((), ())), preferred_element_type=F32)


def _dot_tn(a, b):
    return lax.dot_general(a, b, (((0,), (0,)), ((), ())), preferred_element_type=F32)


def _dot_exact_lhs(a_bf16, x, terms):
    acc = None
    for p in _split_bf16(x, terms):
        d = _dot(a_bf16, p)
        acc = d if acc is None else acc + d
    return acc


def _dot_exact_rhs(x, b_bf16, terms):
    acc = None
    for p in _split_bf16(x, terms):
        d = _dot(p, b_bf16)
        acc = d if acc is None else acc + d
    return acc


def _silu(x):
    return x / (1.0 + jnp.exp(-x))


def _softplus(x):
    return jnp.maximum(x, 0.0) + jnp.log1p(jnp.exp(-jnp.abs(x)))


def _rms(x, gain):
    return x * lax.rsqrt(jnp.mean(x * x, axis=-1, keepdims=True) + EPS) * gain


def _rmsnorm_body(x_ref, g_ref, o_ref):
    o_ref[...] = _rms(x_ref[...], g_ref[...]).astype(o_ref.dtype)


def rmsnorm(x, gain, tm=256):
    m, d = x.shape
    tm = min(tm, m)
    return pl.pallas_call(
        _rmsnorm_body,
        grid=(m // tm,),
        in_specs=[pl.BlockSpec((tm, d), lambda i: (i, 0)),
                  pl.BlockSpec((1, d), lambda i: (0, 0))],
        out_specs=pl.BlockSpec((tm, d), lambda i: (i, 0)),
        out_shape=jax.ShapeDtypeStruct((m, d), BF16),
        compiler_params=_params(("parallel",), _nbytes((tm, d), F32) + _nbytes((tm, d), BF16)),
        name="rmsnorm",
    )(x, gain.reshape(1, d))


def _rmsnorm_small_body(x_ref, g_ref, wa_ref, wb_ref, o_ref, small_ref, wt_ref):
    @pl.when(pl.program_id(0) == 0)
    def _():
        for t, w_ref in enumerate((wa_ref, wb_ref)):
            for k0 in range(0, w_ref.shape[2], _XPOSE_COLS):
                wt_ref[k0:k0 + _XPOSE_COLS, V7X_LANES * t:V7X_LANES * (t + 1)] = (
                    w_ref[0, :, k0:k0 + _XPOSE_COLS].T.astype(BF16))

    hn = _rms(x_ref[...], g_ref[...]).astype(BF16)
    o_ref[...] = hn
    small_ref[...] = _dot(hn, wt_ref[...])


def rmsnorm_small(x, gain, wt_all, layer, row_starts, tm=256):
    m, d = x.shape
    tm = min(tm, m)
    w_tile = lambda r0: pl.BlockSpec((pl.Element(1), pl.Element(V7X_LANES), pl.Element(d)),
                                     lambda i: (layer, r0, 0))
    nbytes = (_nbytes((tm, d), F32) + _nbytes((tm, d), BF16) + 2 * _nbytes((V7X_LANES, d), F32)
              + _nbytes((d, 2 * V7X_LANES), BF16) + _nbytes((tm, 2 * V7X_LANES), F32))
    return pl.pallas_call(
        _rmsnorm_small_body,
        grid=(m // tm,),
        in_specs=[pl.BlockSpec((tm, d), lambda i: (i, 0)),
                  pl.BlockSpec((1, d), lambda i: (0, 0)),
                  w_tile(row_starts[0]), w_tile(row_starts[1])],
        out_specs=[pl.BlockSpec((tm, d), lambda i: (i, 0)),
                   pl.BlockSpec((tm, 2 * V7X_LANES), lambda i: (i, 0))],
        out_shape=[jax.ShapeDtypeStruct((m, d), BF16), jax.ShapeDtypeStruct((m, 2 * V7X_LANES), F32)],
        scratch_shapes=[pltpu.VMEM((d, 2 * V7X_LANES), BF16)],
        compiler_params=_params(("arbitrary",), nbytes),
        name="rmsnorm_small",
    )(x, gain.reshape(1, d), wt_all, wt_all)


def _mm_body(*refs, nk, has_res):
    a_ref, w_ref = refs[0], refs[1]
    r_ref = refs[2] if has_res else None
    o_ref = refs[2 + has_res]
    part = _dot(a_ref[...], w_ref[...])
    if nk == 1:
        if has_res:
            part = part + r_ref[...]
        o_ref[...] = part.astype(o_ref.dtype)
        return
    k = pl.program_id(2)

    @pl.when(k == 0)
    def _():
        o_ref[...] = (part + r_ref[...]) if has_res else part

    @pl.when(k > 0)
    def _():
        o_ref[...] += part


def matmul(a, w, *, out_dtype, res=None, tm, tn, tk=None, layer=None, name="matmul"):
    m, kdim = a.shape
    n = w.shape[-1]
    tm, tn = min(tm, m), min(tn, n)
    tk = kdim if tk is None else tk
    nk = kdim // tk
    assert m % tm == 0 and n % tn == 0 and kdim % tk == 0
    assert nk == 1 or out_dtype == F32
    has_res = res is not None
    if layer is None:
        w_spec = pl.BlockSpec((tk, tn), lambda i, j, k: (k, j))
    else:
        w_spec = pl.BlockSpec((None, tk, tn), lambda i, j, k: (layer, k, j))
    in_specs = [pl.BlockSpec((tm, tk), lambda i, j, k: (i, k)), w_spec]
    args = [a, w]
    nbytes = _nbytes((tm, tk), BF16) + _nbytes((tk, tn), BF16) + _nbytes((tm, tn), out_dtype) + _nbytes((tm, tn), F32)
    if has_res:
        in_specs.append(pl.BlockSpec((tm, tn), lambda i, j, k: (i, j)))
        args.append(res)
        nbytes += _nbytes((tm, tn), F32)
    return pl.pallas_call(
        functools.partial(_mm_body, nk=nk, has_res=has_res),
        grid=(m // tm, n // tn, nk),
        in_specs=in_specs,
        out_specs=pl.BlockSpec((tm, tn), lambda i, j, k: (i, j)),
        out_shape=jax.ShapeDtypeStruct((m, n), out_dtype),
        compiler_params=_params(("parallel", "parallel", "arbitrary"), nbytes),
        name=name,
    )(*args)


def _mixout_body(ya_ref, yb_ref, yc_ref, w_ref, r_ref, o_ref, wb_ref):
    @pl.when(pl.program_id(1) == 0)
    def _():
        wb_ref[...] = w_ref[...].astype(BF16)

    acc = r_ref[...]
    acc = acc + _dot(ya_ref[...], wb_ref[0:SWA_WIDTH, :])
    acc = acc + _dot(yb_ref[...], wb_ref[SWA_WIDTH:SWA_WIDTH + GLA_WIDTH, :])
    acc = acc + _dot(yc_ref[...], wb_ref[SWA_WIDTH + GLA_WIDTH:, :])
    o_ref[...] = acc


def mix_out(ya, yb, yc, w_all, layer, res, tm=1024, tn=PROJ_TILE):
    m = ya.shape[0]
    _, kdim, n = w_all.shape
    tm = min(tm, m)
    nbytes = (_nbytes((tm, kdim), BF16) + _nbytes((kdim, tn), F32) + _nbytes((kdim, tn), BF16)
              + 3 * _nbytes((tm, tn), F32))
    return pl.pallas_call(
        _mixout_body,
        grid=(n // tn, m // tm),
        in_specs=[pl.BlockSpec((tm, SWA_WIDTH), lambda j, i: (i, 0)),
                  pl.BlockSpec((tm, GLA_WIDTH), lambda j, i: (i, 0)),
                  pl.BlockSpec((tm, SSD_WIDTH), lambda j, i: (i, 0)),
                  pl.BlockSpec((None, kdim, tn), lambda j, i: (layer, 0, j)),
                  pl.BlockSpec((tm, tn), lambda j, i: (i, j))],
        out_specs=pl.BlockSpec((tm, tn), lambda j, i: (i, j)),
        out_shape=jax.ShapeDtypeStruct((m, n), F32),
        scratch_shapes=[pltpu.VMEM((kdim, tn), BF16)],
        compiler_params=_params(("parallel", "arbitrary"), nbytes),
        name="mix_out",
    )(ya, yb, yc, w_all, res)


_XPOSE_ROWS, _XPOSE_COLS = 256, 512


def _proj_body(a_ref, w_ref, o_ref, wb_ref):
    _, tn, kdim = w_ref.shape

    @pl.when(pl.program_id(1) == 0)
    def _():
        for r0 in range(0, tn, _XPOSE_ROWS):
            r1 = min(r0 + _XPOSE_ROWS, tn)
            for k0 in range(0, kdim, _XPOSE_COLS):
                wb_ref[k0:k0 + _XPOSE_COLS, r0:r1] = w_ref[0, r0:r1, k0:k0 + _XPOSE_COLS].T.astype(BF16)

    o_ref[...] = _dot(a_ref[...], wb_ref[...]).astype(o_ref.dtype)


def proj_wcast(a, wt_all, layer, row_start, n_tiles, tn, *, out_dtype, tm=1024, name="proj"):
    m, kdim = a.shape
    tm = min(tm, m)
    nbytes = (_nbytes((tm, kdim), BF16) + _nbytes((kdim, tn), F32) + _nbytes((kdim, tn), BF16)
              + _nbytes((tm, tn), out_dtype) + _nbytes((tm, tn), F32))
    return pl.pallas_call(
        _proj_body,
        grid=(n_tiles, m // tm),
        in_specs=[pl.BlockSpec((tm, kdim), lambda j, i: (i, 0)),
                  pl.BlockSpec((pl.Element(1), pl.Element(tn), pl.Element(kdim)),
                               lambda j, i: (layer, pl.multiple_of(row_start(j), 8), 0))],
        out_specs=pl.BlockSpec((tm, tn), lambda j, i: (i, j)),
        out_shape=jax.ShapeDtypeStruct((m, n_tiles * tn), out_dtype),
        scratch_shapes=[pltpu.VMEM((kdim, tn), BF16)],
        compiler_params=_params(("parallel", "arbitrary"), nbytes),
        name=name,
    )(a, wt_all)


def _gateup_body(a_ref, wg_ref, wu_ref, o_ref, wgb_ref, wub_ref):
    @pl.when(pl.program_id(1) == 0)
    def _():
        wgb_ref[...] = wg_ref[...].astype(BF16)
        wub_ref[...] = wu_ref[...].astype(BF16)

    a = a_ref[...]
    g = _dot(a, wgb_ref[...])
    u = _dot(a, wub_ref[...])
    o_ref[...] = (_silu(g) * u).astype(o_ref.dtype)


def ffn_gate_up(a, wg_all, wu_all, layer, tm=1024, tn=FFN_TILE):
    m, kdim = a.shape
    n = wg_all.shape[2]
    tm = min(tm, m)
    nbytes = (_nbytes((tm, kdim), BF16) + 2 * _nbytes((kdim, tn), F32) + 2 * _nbytes((kdim, tn), BF16)
              + _nbytes((tm, tn), BF16) + 3 * _nbytes((tm, tn), F32))
    return pl.pallas_call(
        _gateup_body,
        grid=(n // tn, m // tm),
        in_specs=[pl.BlockSpec((tm, kdim), lambda j, i: (i, 0)),
                  pl.BlockSpec((None, kdim, tn), lambda j, i: (layer, 0, j)),
                  pl.BlockSpec((None, kdim, tn), lambda j, i: (layer, 0, j))],
        out_specs=pl.BlockSpec((tm, tn), lambda j, i: (i, j)),
        out_shape=jax.ShapeDtypeStruct((m, n), BF16),
        scratch_shapes=[pltpu.VMEM((kdim, tn), BF16), pltpu.VMEM((kdim, tn), BF16)],
        compiler_params=_params(("parallel", "arbitrary"), nbytes),
        name="ffn_gate_up",
    )(a, wg_all, wu_all)


def _swa_body(q_ref, kvc_ref, kvp_ref, bias_ref, sink_ref, qg_ref, kg_ref, og_ref, e_ref, et_ref, o_ref, acc_ref):
    hd, blk, grp = SWA_HEAD_DIM, SWA_BLOCK, SWA_GROUP
    kband = jnp.concatenate([kvp_ref[:, :SWA_KV_WIDTH], kvc_ref[:, :SWA_KV_WIDTH]], axis=0).astype(F32)
    vband = jnp.concatenate([kvp_ref[:, SWA_KV_WIDTH:], kvc_ref[:, SWA_KV_WIDTH:]], axis=0)
    ones = jnp.ones((2 * blk, V7X_LANES), BF16)

    q = q_ref[...].astype(F32)
    ssq = _dot_exact_rhs(q * q, e_ref[...], 2)
    inv = lax.rsqrt(ssq * (1.0 / hd) + EPS)
    qn = (q * _dot_exact_rhs(inv, et_ref[...], 3) * qg_ref[...]).astype(BF16)

    for j in range(SWA_KV_HEADS):
        kn = _rms(kband[:, hd * j:hd * (j + 1)], kg_ref[...]).astype(BF16)
        qs = jnp.concatenate([qn[:, hd * (grp * j + g):hd * (grp * j + g + 1)] for g in range(grp)], axis=0)
        s = _dot_nt(qs, kn) * (hd ** -0.5) + bias_ref[j]
        sink = sink_ref[j]
        s_prev, s_cur = s[:, :blk], s[:, blk:]
        row_max = jnp.max(jnp.maximum(s_prev, s_cur), axis=-1, keepdims=True)
        mx = jnp.maximum(jnp.broadcast_to(row_max, sink.shape), sink)
        p = jnp.concatenate([jnp.exp(s_prev - mx), jnp.exp(s_cur - mx)], axis=1).astype(BF16)
        total = _dot(p, ones) + jnp.exp(sink - mx)
        o = _dot(p, vband[:, hd * j:hd * (j + 1)]) / total[:, :hd]
        for g in range(grp):
            h = grp * j + g
            acc_ref[:, hd * h:hd * (h + 1)] = o[blk * g:blk * (g + 1), :]
    o_ref[...] = _rms(acc_ref[...], og_ref[...]).astype(o_ref.dtype)


def swa_mixer(qkv, bias, sinks, q_gain, k_gain, out_gain, batch, seq):
    nb = seq // SWA_BLOCK
    kv_blk = SWA_WIDTH // (2 * SWA_KV_WIDTH)
    rows = SWA_GROUP * SWA_BLOCK
    no_prev = jnp.arange(2 * SWA_BLOCK) < SWA_BLOCK
    bias_g = jnp.stack([jnp.where(no_prev, -jnp.inf, bias), bias]).reshape(2, SWA_KV_HEADS, rows, 2 * SWA_BLOCK)
    sink_col = jnp.broadcast_to(jnp.repeat(sinks.astype(F32), SWA_BLOCK)[:, None],
                                (SWA_HEADS * SWA_BLOCK, V7X_LANES)).reshape(SWA_KV_HEADS, rows, V7X_LANES)
    head_of_col = jnp.arange(SWA_WIDTH) // SWA_HEAD_DIM
    e = (head_of_col[:, None] == jnp.arange(V7X_LANES)[None, :]).astype(BF16)
    const = lambda shape: pl.BlockSpec(shape, lambda b, n: (0,) * len(shape))
    nbytes = (_nbytes((SWA_BLOCK, SWA_WIDTH), BF16) * 2 + 2 * _nbytes((SWA_BLOCK, 2 * SWA_KV_WIDTH), BF16)
              + _nbytes(bias.shape, F32) + _nbytes((SWA_KV_HEADS, rows, V7X_LANES), F32)
              + 2 * _nbytes(e.shape, BF16) + _nbytes((SWA_BLOCK, SWA_WIDTH), F32)
              + 4 * _nbytes((rows, 2 * SWA_BLOCK), F32))
    return pl.pallas_call(
        _swa_body,
        grid=(batch, nb),
        in_specs=[pl.BlockSpec((SWA_BLOCK, SWA_WIDTH), lambda b, n: (b * nb + n, 0)),
                  pl.BlockSpec((SWA_BLOCK, 2 * SWA_KV_WIDTH), lambda b, n: (b * nb + n, kv_blk)),
                  pl.BlockSpec((SWA_BLOCK, 2 * SWA_KV_WIDTH),
                               lambda b, n: (b * nb + jnp.maximum(n - 1, 0), kv_blk)),
                  pl.BlockSpec((None,) + bias_g.shape[1:], lambda b, n: (jnp.minimum(n, 1), 0, 0, 0)),
                  const(sink_col.shape),
                  const((1, SWA_WIDTH)), const((1, SWA_HEAD_DIM)), const((1, SWA_WIDTH)),
                  const(e.shape), const(e.T.shape)],
        out_specs=pl.BlockSpec((SWA_BLOCK, SWA_WIDTH), lambda b, n: (b * nb + n, 0)),
        out_shape=jax.ShapeDtypeStruct((batch * seq, SWA_WIDTH), BF16),
        scratch_shapes=[pltpu.VMEM((SWA_BLOCK, SWA_WIDTH), F32)],
        compiler_params=_params(("parallel", "parallel"), nbytes),
        name="swa_mixer",
    )(qkv, qkv, qkv, bias_g, sink_col, jnp.tile(q_gain, SWA_HEADS).reshape(1, -1), k_gain.reshape(1, -1),
      out_gain.reshape(1, -1), e, e.T)


def _t5_bucket(dist):
    n = jnp.maximum(dist, 0)
    max_exact = REL_BUCKETS // 2
    nf = jnp.maximum(n, 1).astype(F32)
    large = max_exact + (jnp.log(nf / max_exact) / math.log(REL_MAX_DIST / max_exact)
                         * (REL_BUCKETS - max_exact)).astype(jnp.int32)
    large = jnp.minimum(large, REL_BUCKETS - 1)
    return jnp.where(n < max_exact, n, large)


def swa_band_bias(rel_bias):
    i = jnp.arange(SWA_BLOCK, dtype=jnp.int32)[:, None]
    j = jnp.arange(2 * SWA_BLOCK, dtype=jnp.int32)[None, :]
    dist = i + SWA_BLOCK - j
    onehot = _t5_bucket(dist)[None] == jnp.arange(REL_BUCKETS, dtype=jnp.int32)[:, None, None]
    bias = jnp.sum(jnp.where(onehot[:, None], rel_bias.astype(F32)[:, :, None, None], 0.0), axis=0)
    in_window = (dist >= 0) & (dist < SWA_WINDOW)
    return jnp.where(in_window[None], bias, -jnp.inf)


def _gla_body(x_ref, gl_ref, wup_ref, bup_ref, gain_ref, o_ref, state_ref):
    tb = x_ref.shape[0]
    dk, dv, c = GLA_KEY_DIM, GLA_VAL_DIM, GLA_CHUNK

    @pl.when(pl.program_id(1) == 0)
    def _():
        state_ref[...] = jnp.zeros_like(state_ref)

    gl_hi, gl_lo = _split_bf16(gl_ref[...], 2)
    w_hi, w_lo = _split_bf16(wup_ref[...], 2)
    pre = _dot(gl_hi, w_hi) + _dot(gl_hi, w_lo) + _dot(gl_lo, w_hi) + bup_ref[...]
    g = (jnp.minimum(pre, 0.0) - jnp.log1p(jnp.exp(-jnp.abs(pre)))) * (1.0 / GLA_GATE_NORMALIZER)

    row = lax.broadcasted_iota(jnp.int32, (tb, tb), 0)
    colm = lax.broadcasted_iota(jnp.int32, (tb, tb), 1)
    same_chunk_lower = ((row // c) == (colm // c)) & (colm <= row)
    bcum_all = _dot_exact_lhs(same_chunk_lower.astype(BF16), g, 3)

    ri = lax.broadcasted_iota(jnp.int32, (c, c), 0)
    ci = lax.broadcasted_iota(jnp.int32, (c, c), 1)
    causal = ci <= ri
    for ch in range(tb // c):
        r0 = ch * c
        for h in range(GLA_HEADS):
            bcum = bcum_all[r0:r0 + c, dk * h:dk * (h + 1)]
            blast = bcum[c - 1:c, :]
            q = x_ref[r0:r0 + c, dk * h:dk * (h + 1)].astype(F32)
            k = x_ref[r0:r0 + c, GLA_KEY_WIDTH + dk * h:GLA_KEY_WIDTH + dk * (h + 1)].astype(F32)
            v = x_ref[r0:r0 + c, 2 * GLA_KEY_WIDTH + dv * h:2 * GLA_KEY_WIDTH + dv * (h + 1)]
            r = x_ref[r0:r0 + c, 2 * GLA_KEY_WIDTH + GLA_WIDTH + dv * h:
                      2 * GLA_KEY_WIDTH + GLA_WIDTH + dv * (h + 1)].astype(F32)
            qd = (q * (dk ** -0.5) * jnp.exp(bcum)).astype(BF16)
            kd = (k * jnp.exp(-bcum)).astype(BF16)
            kl = (k * jnp.exp(blast - bcum)).astype(BF16)
            att = jnp.where(causal, _dot_nt(qd, kd), 0.0)
            st = state_ref[h]
            o = _dot(att.astype(BF16), v) + _dot_nt(qd, st.astype(BF16))
            state_ref[h] = st * jnp.exp(blast) + _dot_tn(v, kl)
            o = _rms(o, gain_ref[...]) * _silu(r)
            o_ref[r0:r0 + c, dv * h:dv * (h + 1)] = o.astype(o_ref.dtype)


def gla_mixer(x, small, w_up_pad, b_up, norm_gain, batch, seq):
    tb = min(GLA_TIME_BLOCK, seq)
    nt = seq // tb
    width = x.shape[1]
    nbytes = (_nbytes((tb, width), BF16) + _nbytes((tb, V7X_LANES), F32) + _nbytes(w_up_pad.shape, F32)
              + _nbytes((tb, GLA_WIDTH), BF16) + _nbytes((GLA_HEADS, GLA_VAL_DIM, GLA_KEY_DIM), F32)
              + 8 * _nbytes((tb, GLA_KEY_WIDTH), F32))
    return pl.pallas_call(
        _gla_body,
        grid=(batch, nt),
        in_specs=[pl.BlockSpec((tb, width), lambda b, t: (b * nt + t, 0)),
                  pl.BlockSpec((tb, V7X_LANES), lambda b, t: (b * nt + t, 0)),
                  pl.BlockSpec(w_up_pad.shape, lambda b, t: (0, 0)),
                  pl.BlockSpec((1, GLA_KEY_WIDTH), lambda b, t: (0, 0)),
                  pl.BlockSpec((1, GLA_VAL_DIM), lambda b, t: (0, 0))],
        out_specs=pl.BlockSpec((tb, GLA_WIDTH), lambda b, t: (b * nt + t, 0)),
        out_shape=jax.ShapeDtypeStruct((batch * seq, GLA_WIDTH), BF16),
        scratch_shapes=[pltpu.VMEM((GLA_HEADS, GLA_VAL_DIM, GLA_KEY_DIM), F32)],
        compiler_params=_params(("parallel", "arbitrary"), nbytes),
        name="gla_mixer",
    )(x, small, w_up_pad, b_up.reshape(1, -1), norm_gain.reshape(1, -1))


def _ssd_body(x_ref, dt_ref, cw_ref, cb_ref, dtb_ref, alog_ref, dexp_ref, gain_ref, expand_ref,
              o_ref, state_ref, xe_ref, y_ref):
    L, P, N = SSD_CHUNK, SSD_HEAD_DIM, SSD_STATE
    gw = SSD_GROUP_WIDTH

    @pl.when(pl.program_id(1) == 0)
    def _():
        state_ref[...] = jnp.zeros_like(state_ref)
        xe_ref[0:8, :] = jnp.zeros((8, SSD_CONV_CH), F32)

    xe_ref[8:, :] = x_ref[:, SSD_WIDTH:].astype(F32)
    conv = cb_ref[...]
    for j in range(SSD_CONV):
        conv = conv + cw_ref[SSD_CONV - 1 - j:SSD_CONV - j, :] * xe_ref[8 - j:8 - j + L, :]
    xe_ref[0:8, :] = xe_ref[L:L + 8, :]
    xbc = _silu(conv)
    xs = xbc[:, :SSD_WIDTH]
    bm = xbc[:, SSD_WIDTH:SSD_WIDTH + SSD_BC_WIDTH].astype(BF16)
    cm = xbc[:, SSD_WIDTH + SSD_BC_WIDTH:].astype(BF16)

    dt = _softplus(dt_ref[...] + dtb_ref[...])
    dta = dt * (-jnp.exp(alog_ref[...]))
    ri = lax.broadcasted_iota(jnp.int32, (L, L), 0)
    ci = lax.broadcasted_iota(jnp.int32, (L, L), 1)
    causal = ci <= ri
    a_cum = _dot_exact_lhs(causal.astype(BF16), dta, 3)
    a_cum_t = a_cum.T
    expand = expand_ref[...]
    a_exp = _dot_exact_rhs(a_cum, expand, 3)
    dt_exp = _dot_exact_rhs(dt, expand, 2)
    a_last = a_exp[L - 1:L, :]
    xd = xs * dt_exp
    xdec = (xd * jnp.exp(a_last - a_exp)).astype(BF16)
    xd16 = xd.astype(BF16)
    out_scale = jnp.exp(a_exp)

    for g in range(SSD_GROUPS):
        bg = bm[:, N * g:N * (g + 1)]
        cg = cm[:, N * g:N * (g + 1)]
        cb = _dot_nt(cg, bg)
        for kk in range(SSD_HEADS_PER_GROUP):
            h = g * SSD_HEADS_PER_GROUP + kk
            hl = SSD_DT_LANE0 + h
            diff = a_cum[:, hl:hl + 1] - a_cum_t[hl:hl + 1, :]
            m = cb * jnp.exp(jnp.where(causal, diff, -jnp.inf))
            y_ref[:, P * h:P * (h + 1)] = _dot(m.astype(BF16), xd16[:, P * h:P * (h + 1)])
        sg = state_ref[:, gw * g:gw * (g + 1)]
        y_off = _dot(cg, sg.astype(BF16)) * out_scale[:, gw * g:gw * (g + 1)]
        state_ref[:, gw * g:gw * (g + 1)] = (sg * jnp.exp(a_last[:, gw * g:gw * (g + 1)])
                                             + _dot_tn(bg, xdec[:, gw * g:gw * (g + 1)]))
        yg = y_ref[:, gw * g:gw * (g + 1)] + y_off + xs[:, gw * g:gw * (g + 1)] * dexp_ref[:, gw * g:gw * (g + 1)]
        yg = yg * _silu(x_ref[:, gw * g:gw * (g + 1)].astype(F32))
        o_ref[:, gw * g:gw * (g + 1)] = _rms(yg, gain_ref[:, gw * g:gw * (g + 1)]).astype(o_ref.dtype)


def ssd_mixer(x, small, conv_w, conv_b, dt_bias, a_log, d_skip, norm_gain, batch, seq):
    L = SSD_CHUNK
    nc = seq // L
    width = x.shape[1]
    lane_pad = (SSD_DT_LANE0, V7X_LANES - SSD_DT_LANE0 - SSD_HEADS)
    dtb = jnp.pad(dt_bias, lane_pad).reshape(1, V7X_LANES)
    alog = jnp.pad(a_log, lane_pad).reshape(1, V7X_LANES)
    dexp = jnp.repeat(d_skip, SSD_HEAD_DIM).reshape(1, SSD_WIDTH)
    expand = (jnp.arange(V7X_LANES)[:, None] - SSD_DT_LANE0
              == (jnp.arange(SSD_WIDTH)[None, :] // SSD_HEAD_DIM)).astype(BF16)
    nbytes = (_nbytes((L, width), BF16) + _nbytes((L, V7X_LANES), F32) + _nbytes((L, SSD_WIDTH), BF16)
              + _nbytes(expand.shape, BF16) + 2 * _nbytes((SSD_STATE, SSD_WIDTH), F32)
              + 12 * _nbytes((L, SSD_CONV_CH), F32))
    return pl.pallas_call(
        _ssd_body,
        grid=(batch, nc),
        in_specs=[pl.BlockSpec((L, width), lambda b, c: (b * nc + c, 0)),
                  pl.BlockSpec((L, V7X_LANES), lambda b, c: (b * nc + c, 1)),
                  pl.BlockSpec((SSD_CONV, SSD_CONV_CH), lambda b, c: (0, 0)),
                  pl.BlockSpec((1, SSD_CONV_CH), lambda b, c: (0, 0)),
                  pl.BlockSpec((1, V7X_LANES), lambda b, c: (0, 0)),
                  pl.BlockSpec((1, V7X_LANES), lambda b, c: (0, 0)),
                  pl.BlockSpec((1, SSD_WIDTH), lambda b, c: (0, 0)),
                  pl.BlockSpec((1, SSD_WIDTH), lambda b, c: (0, 0)),
                  pl.BlockSpec(expand.shape, lambda b, c: (0, 0))],
        out_specs=pl.BlockSpec((L, SSD_WIDTH), lambda b, c: (b * nc + c, 0)),
        out_shape=jax.ShapeDtypeStruct((batch * seq, SSD_WIDTH), BF16),
        scratch_shapes=[pltpu.VMEM((SSD_STATE, SSD_WIDTH), F32),
                        pltpu.VMEM((8 + L, SSD_CONV_CH), F32),
                        pltpu.VMEM((L, SSD_WIDTH), F32)],
        compiler_params=_params(("parallel", "arbitrary"), nbytes),
        name="ssd_mixer",
    )(x, small, conv_w.reshape(SSD_CONV, SSD_CONV_CH), conv_b.reshape(1, -1), dtb, alog, dexp,
      norm_gain.reshape(1, -1), expand)


def _xattn_body(h_ref, kv_ref, wq_ref, wo_ref, lnx_ref, qg_ref, kg_ref, lnf_ref, o_ref, hf_ref, att_ref):
    hd = X_HEAD_DIM
    mem_len = kv_ref.shape[0]
    half = mem_len // 2
    ones = jnp.ones((mem_len, V7X_LANES), BF16)
    h = h_ref[...]
    q = _dot(_rms(h, lnx_ref[...]).astype(BF16), wq_ref[...])
    for i in range(X_HEADS):
        qh = _rms(q[:, hd * i:hd * (i + 1)], qg_ref[...]).astype(BF16)
        kh = _rms(kv_ref[:, hd * i:hd * (i + 1)], kg_ref[...]).astype(BF16)
        vh = kv_ref[:, X_WIDTH + hd * i:X_WIDTH + hd * (i + 1)].astype(BF16)
        s = _dot_nt(qh, kh) * (hd ** -0.5)
        s_lo, s_hi = s[:, :half], s[:, half:]
        mx = jnp.broadcast_to(jnp.max(jnp.maximum(s_lo, s_hi), axis=-1, keepdims=True), s_lo.shape)
        p = jnp.concatenate([jnp.exp(s_lo - mx), jnp.exp(s_hi - mx)], axis=1).astype(BF16)
        att_ref[:, hd * i:hd * (i + 1)] = (_dot(p, vh) / _dot(p, ones)).astype(BF16)
    h_new = h + _dot(att_ref[...], wo_ref[...])
    o_ref[...] = h_new
    hf_ref[...] = _rms(h_new, lnf_ref[...]).astype(hf_ref.dtype)


def cross_attention_block(h, kv, w_q, w_o, ln_x, q_gain, k_gain, ln_ffn, batch, seq, mem_len, tq=256):
    m, d = h.shape
    tq = min(tq, seq)
    nq = seq // tq
    assert mem_len == 2 * V7X_LANES and X_HEAD_DIM == V7X_LANES
    const = lambda shape: pl.BlockSpec(shape, lambda b, t: (0,) * len(shape))
    row_blk = lambda width: pl.BlockSpec((tq, width), lambda b, t: (b * nq + t, 0))
    nbytes = (2 * _nbytes((tq, d), F32) + _nbytes((tq, d), BF16) + _nbytes((mem_len, 2 * X_WIDTH), F32)
              + 2 * _nbytes((d, X_WIDTH), BF16) + 2 * _nbytes((tq, d), F32))
    return pl.pallas_call(
        _xattn_body,
        grid=(batch, nq),
        in_specs=[row_blk(d),
                  pl.BlockSpec((mem_len, 2 * X_WIDTH), lambda b, t: (b, 0)),
                  const((d, X_WIDTH)), const((X_WIDTH, d)), const((1, d)),
                  const((1, X_HEAD_DIM)), const((1, X_HEAD_DIM)), const((1, d))],
        out_specs=[row_blk(d), row_blk(d)],
        out_shape=[jax.ShapeDtypeStruct((m, d), F32), jax.ShapeDtypeStruct((m, d), BF16)],
        scratch_shapes=[pltpu.VMEM((tq, X_WIDTH), BF16)],
        compiler_params=_params(("parallel", "parallel"), nbytes),
        name="xattn_block",
    )(h, kv, w_q, w_o, ln_x.reshape(1, -1), q_gain.reshape(1, -1), k_gain.reshape(1, -1), ln_ffn.reshape(1, -1))


_IN_OFF = np.cumsum([0, SWA_WIDTH, SWA_KV_WIDTH, SWA_KV_WIDTH, GLA_KEY_WIDTH, GLA_KEY_WIDTH, GLA_WIDTH,
                     GLA_WIDTH, GLA_GATE_RANK, SSD_WIDTH, SSD_CONV_CH, SSD_HEADS]).tolist()
SWA_COL0, GLA_COL0, GLOW_COL0, SSD_COL0, DT_COL0 = _IN_OFF[0], _IN_OFF[3], _IN_OFF[7], _IN_OFF[8], _IN_OFF[10]


IN_DIM = _IN_OFF[-1]
SMALL_ROW_STARTS = (GLOW_COL0, IN_DIM - V7X_LANES)
assert SMALL_ROW_STARTS[1] + SSD_DT_LANE0 == DT_COL0


def kernel(x, mem, rel_bias, ln_mix, w_in, swa_q_gain, swa_k_gain, swa_sinks, swa_out_gain, gla_w_gk_up, gla_b_gk_up, gla_norm_gain, ssd_conv_w, ssd_conv_b, ssd_dt_bias, ssd_a_log, ssd_d, ssd_norm_gain, w_mix_out, ln_x, ln_mem, x_w_q, x_w_k, x_w_v, x_w_o, x_q_gain, x_k_gain, ln_ffn, ffn_w_gate, ffn_w_up, ffn_w_down):
    batch, seq, d = x.shape
    mem_len = mem.shape[1]
    m = batch * seq
    band_bias = swa_band_bias(rel_bias)
    h = x.reshape(m, d)
    mem2 = mem.reshape(batch * mem_len, d)
    w_in_t = jnp.swapaxes(w_in, 1, 2)
    w_down16 = ffn_w_down.astype(BF16)
    for l in range(DEPTH):
        w_up_pad = jnp.pad(gla_w_gk_up[l], ((0, V7X_LANES - GLA_GATE_RANK), (0, 0)))
        w_kv = jnp.concatenate([x_w_k[l], x_w_v[l]], axis=1).astype(BF16)
        hn, p_small = rmsnorm_small(h, ln_mix[l], w_in_t, l, SMALL_ROW_STARTS)
        swa_tile = (GLA_COL0 - SWA_COL0) // 2
        p_swa = proj_wcast(hn, w_in_t, l, lambda j: SWA_COL0 + swa_tile * j, 2, swa_tile,
                           out_dtype=BF16, name="proj_swa")
        p_gla = proj_wcast(hn, w_in_t, l, lambda j: GLA_COL0 + PROJ_TILE * j, (GLOW_COL0 - GLA_COL0) // PROJ_TILE,
                           PROJ_TILE, out_dtype=BF16, name="proj_gla")
        p_ssd = proj_wcast(hn, w_in_t, l, lambda j: SSD_COL0 + PROJ_TILE * j, (DT_COL0 - SSD_COL0) // PROJ_TILE,
                           PROJ_TILE, out_dtype=BF16, name="proj_ssd")
        y_a = swa_mixer(p_swa, band_bias, swa_sinks[l], swa_q_gain[l], swa_k_gain[l], swa_out_gain[l], batch, seq)
        y_b = gla_mixer(p_gla, p_small, w_up_pad, gla_b_gk_up[l], gla_norm_gain[l], batch, seq)
        y_c = ssd_mixer(p_ssd, p_small, ssd_conv_w[l], ssd_conv_b[l], ssd_dt_bias[l], ssd_a_log[l], ssd_d[l],
                        ssd_norm_gain[l], batch, seq)
        h = mix_out(y_a, y_b, y_c, w_mix_out, l, h)
        memn = rmsnorm(mem2, ln_mem[l])
        kv = matmul(memn, w_kv, out_dtype=F32, tm=1024, tn=512, name="xattn_kv")
        h, hf = cross_attention_block(h, kv, x_w_q[l].astype(BF16), x_w_o[l].astype(BF16), ln_x[l],
                                      x_q_gain[l], x_k_gain[l], ln_ffn[l], batch, seq, mem_len)
        hidden = ffn_gate_up(hf, ffn_w_gate, ffn_w_up, l)
        h = matmul(hidden, w_down16, layer=l, out_dtype=F32, res=h, tm=512, tn=256, name="ffn_down")
    return h.reshape(batch, seq, d)
```

```python
import functools
import math

import numpy as np
import jax
import jax.numpy as jnp
from jax import lax
from jax.experimental import pallas as pl
from jax.experimental.pallas import tpu as pltpu

F32 = jnp.float32
BF16 = jnp.bfloat16

D_MODEL = 4096
DEPTH = 2
EPS = 1e-6
SWA_WIDTH = 1024
SWA_HEAD_DIM = 64
SWA_HEADS = 16
SWA_KV_HEADS = 2
SWA_GROUP = SWA_HEADS // SWA_KV_HEADS
SWA_KV_WIDTH = SWA_KV_HEADS * SWA_HEAD_DIM
SWA_WINDOW = 128
SWA_BLOCK = 128
REL_BUCKETS = 32
REL_MAX_DIST = 128
GLA_WIDTH = 1024
GLA_HEADS = 4
GLA_VAL_DIM = 256
GLA_KEY_DIM = 128
GLA_KEY_WIDTH = GLA_HEADS * GLA_KEY_DIM
GLA_GATE_RANK = 16
GLA_GATE_NORMALIZER = 16.0
GLA_CHUNK = 64
SSD_WIDTH = 2048
SSD_HEAD_DIM = 64
SSD_HEADS = 32
SSD_GROUPS = 8
SSD_HEADS_PER_GROUP = SSD_HEADS // SSD_GROUPS
SSD_STATE = 128
SSD_CONV = 4
SSD_CHUNK = 128
SSD_BC_WIDTH = SSD_GROUPS * SSD_STATE
SSD_CONV_CH = SSD_WIDTH + 2 * SSD_BC_WIDTH
SSD_GROUP_WIDTH = SSD_WIDTH // SSD_GROUPS
X_HEADS = 4
X_HEAD_DIM = 128
X_WIDTH = X_HEADS * X_HEAD_DIM
FFN_HIDDEN = 11008

V7X_LANES = 128
V7X_VMEM_BYTES = 64 * 1024 * 1024
VMEM_REQUEST_CAP = (V7X_VMEM_BYTES * 7) // 8
V7X_MXU_WIDTH = 256
FFN_TILE = V7X_MXU_WIDTH
FFN_GROUP = 2
PROJ_TILE = 2 * V7X_MXU_WIDTH
GLA_TIME_BLOCK = 256
SSD_DT_LANE0 = V7X_LANES - SSD_HEADS


def _params(semantics, block_bytes):
    limit = min(VMEM_REQUEST_CAP, 2 * block_bytes + 16 * 1024 * 1024)
    return pltpu.CompilerParams(dimension_semantics=semantics, vmem_limit_bytes=int(limit))


def _nbytes(shape, dtype):
    return int(np.prod(shape)) * jnp.dtype(dtype).itemsize


def _split_bf16(x, terms):
    parts = []
    r = x
    for t in range(terms):
        p = r.astype(BF16)
        parts.append(p)
        if t + 1 < terms:
            r = r - p.astype(F32)
    return parts


def _dot(a, b):
    return jnp.dot(a, b, preferred_element_type=F32)


def _dot_nt(a, b):
    return lax.dot_general(a, b, (((1,), (1,)), ((), ())), preferred_element_type=F32)


def _dot_tn(a, b):
    return lax.dot_general(a, b, (((0,), (0,)), ((), ())), preferred_element_type=F32)


def _dot_exact_lhs(a_bf16, x, terms):
    acc = None
    for p in _split_bf16(x, terms):
        d = _dot(a_bf16, p)
        acc = d if acc is None else acc + d
    return acc


def _dot_exact_rhs(x, b_bf16, terms):
    acc = None
    for p in _split_bf16(x, terms):
        d = _dot(p, b_bf16)
        acc = d if acc is None else acc + d
    return acc


def _silu(x):
    return x / (1.0 + jnp.exp(-x))


def _softplus(x):
    return jnp.maximum(x, 0.0) + jnp.log1p(jnp.exp(-jnp.abs(x)))


def _rms(x, gain):
    return x * lax.rsqrt(jnp.mean(x * x, axis=-1, keepdims=True) + EPS) * gain


def _rmsnorm_body(x_ref, g_ref, o_ref):
    o_ref[...] = _rms(x_ref[...], g_ref[...]).astype(o_ref.dtype)


def rmsnorm(x, gain, tm=256):
    m, d = x.shape
    tm = min(tm, m)
    return pl.pallas_call(
        _rmsnorm_body,
        grid=(m // tm,),
        in_specs=[pl.BlockSpec((tm, d), lambda i: (i, 0)),
                  pl.BlockSpec((1, d), lambda i: (0, 0))],
        out_specs=pl.BlockSpec((tm, d), lambda i: (i, 0)),
        out_shape=jax.ShapeDtypeStruct((m, d), BF16),
        compiler_params=_params(("parallel",), _nbytes((tm, d), F32) + _nbytes((tm, d), BF16)),
        name="rmsnorm",
    )(x, gain.reshape(1, d))


def _rmsnorm_small_body(x_ref, g_ref, wa_ref, wb_ref, o_ref, small_ref, wt_ref):
    @pl.when(pl.program_id(0) == 0)
    def _():
        for t, w_ref in enumerate((wa_ref, wb_ref)):
            for k0 in range(0, w_ref.shape[2], _XPOSE_COLS):
                wt_ref[k0:k0 + _XPOSE_COLS, V7X_LANES * t:V7X_LANES * (t + 1)] = (
                    w_ref[0, :, k0:k0 + _XPOSE_COLS].T.astype(BF16))

    hn = _rms(x_ref[...], g_ref[...]).astype(BF16)
    o_ref[...] = hn
    small_ref[...] = _dot(hn, wt_ref[...])


def rmsnorm_small(x, gain, wt_all, layer, row_starts, tm=256):
    m, d = x.shape
    tm = min(tm, m)
    w_tile = lambda r0: pl.BlockSpec((pl.Element(1), pl.Element(V7X_LANES), pl.Element(d)),
                                     lambda i: (layer, r0, 0))
    nbytes = (_nbytes((tm, d), F32) + _nbytes((tm, d), BF16) + 2 * _nbytes((V7X_LANES, d), F32)
              + _nbytes((d, 2 * V7X_LANES), BF16) + _nbytes((tm, 2 * V7X_LANES), F32))
    return pl.pallas_call(
        _rmsnorm_small_body,
        grid=(m // tm,),
        in_specs=[pl.BlockSpec((tm, d), lambda i: (i, 0)),
                  pl.BlockSpec((1, d), lambda i: (0, 0)),
                  w_tile(row_starts[0]), w_tile(row_starts[1])],
        out_specs=[pl.BlockSpec((tm, d), lambda i: (i, 0)),
                   pl.BlockSpec((tm, 2 * V7X_LANES), lambda i: (i, 0))],
        out_shape=[jax.ShapeDtypeStruct((m, d), BF16), jax.ShapeDtypeStruct((m, 2 * V7X_LANES), F32)],
        scratch_shapes=[pltpu.VMEM((d, 2 * V7X_LANES), BF16)],
        compiler_params=_params(("arbitrary",), nbytes),
        name="rmsnorm_small",
    )(x, gain.reshape(1, d), wt_all, wt_all)


def _mm_body(*refs, nk, has_res):
    a_ref, w_ref = refs[0], refs[1]
    r_ref = refs[2] if has_res else None
    o_ref = refs[2 + has_res]
    part = _dot(a_ref[...], w_ref[...])
    if nk == 1:
        if has_res:
            part = part + r_ref[...]
        o_ref[...] = part.astype(o_ref.dtype)
        return
    k = pl.program_id(2)

    @pl.when(k == 0)
    def _():
        o_ref[...] = (part + r_ref[...]) if has_res else part

    @pl.when(k > 0)
    def _():
        o_ref[...] += part


def matmul(a, w, *, out_dtype, res=None, tm, tn, tk=None, layer=None, name="matmul"):
    m, kdim = a.shape
    n = w.shape[-1]
    tm, tn = min(tm, m), min(tn, n)
    tk = kdim if tk is None else tk
    nk = kdim // tk
    assert m % tm == 0 and n % tn == 0 and kdim % tk == 0
    assert nk == 1 or out_dtype == F32
    has_res = res is not None
    if layer is None:
        w_spec = pl.BlockSpec((tk, tn), lambda i, j, k: (k, j))
    else:
        w_spec = pl.BlockSpec((None, tk, tn), lambda i, j, k: (layer, k, j))
    in_specs = [pl.BlockSpec((tm, tk), lambda i, j, k: (i, k)), w_spec]
    args = [a, w]
    nbytes = _nbytes((tm, tk), BF16) + _nbytes((tk, tn), BF16) + _nbytes((tm, tn), out_dtype) + _nbytes((tm, tn), F32)
    if has_res:
        in_specs.append(pl.BlockSpec((tm, tn), lambda i, j, k: (i, j)))
        args.append(res)
        nbytes += _nbytes((tm, tn), F32)
    return pl.pallas_call(
        functools.partial(_mm_body, nk=nk, has_res=has_res),
        grid=(m // tm, n // tn, nk),
        in_specs=in_specs,
        out_specs=pl.BlockSpec((tm, tn), lambda i, j, k: (i, j)),
        out_shape=jax.ShapeDtypeStruct((m, n), out_dtype),
        compiler_params=_params(("parallel", "parallel", "arbitrary"), nbytes),
        name=name,
    )(*args)


def _mixout_body(ya_ref, yb_ref, yc_ref, w_ref, r_ref, o_ref, wb_ref):
    @pl.when(pl.program_id(1) == 0)
    def _():
        wb_ref[...] = w_ref[...].astype(BF16)

    acc = r_ref[...]
    acc = acc + _dot(ya_ref[...], wb_ref[0:SWA_WIDTH, :])
    acc = acc + _dot(yb_ref[...], wb_ref[SWA_WIDTH:SWA_WIDTH + GLA_WIDTH, :])
    acc = acc + _dot(yc_ref[...], wb_ref[SWA_WIDTH + GLA_WIDTH:, :])
    o_ref[...] = acc


def mix_out(ya, yb, yc, w_all, layer, res, tm=1024, tn=PROJ_TILE):
    m = ya.shape[0]
    _, kdim, n = w_all.shape
    tm = min(tm, m)
    nbytes = (_nbytes((tm, kdim), BF16) + _nbytes((kdim, tn), F32) + _nbytes((kdim, tn), BF16)
              + 3 * _nbytes((tm, tn), F32))
    return pl.pallas_call(
        _mixout_body,
        grid=(n // tn, m // tm),
        in_specs=[pl.BlockSpec((tm, SWA_WIDTH), lambda j, i: (i, 0)),
                  pl.BlockSpec((tm, GLA_WIDTH), lambda j, i: (i, 0)),
                  pl.BlockSpec((tm, SSD_WIDTH), lambda j, i: (i, 0)),
                  pl.BlockSpec((None, kdim, tn), lambda j, i: (layer, 0, j)),
                  pl.BlockSpec((tm, tn), lambda j, i: (i, j))],
        out_specs=pl.BlockSpec((tm, tn), lambda j, i: (i, j)),
        out_shape=jax.ShapeDtypeStruct((m, n), F32),
        scratch_shapes=[pltpu.VMEM((kdim, tn), BF16)],
        compiler_params=_params(("parallel", "arbitrary"), nbytes),
        name="mix_out",
    )(ya, yb, yc, w_all, res)


_XPOSE_ROWS, _XPOSE_COLS = 256, 512


def _proj_body(a_ref, w_ref, o_ref, wb_ref):
    _, tn, kdim = w_ref.shape

    @pl.when(pl.program_id(1) == 0)
    def _():
        for r0 in range(0, tn, _XPOSE_ROWS):
            r1 = min(r0 + _XPOSE_ROWS, tn)
            for k0 in range(0, kdim, _XPOSE_COLS):
                wb_ref[k0:k0 + _XPOSE_COLS, r0:r1] = w_ref[0, r0:r1, k0:k0 + _XPOSE_COLS].T.astype(BF16)

    o_ref[...] = _dot(a_ref[...], wb_ref[...]).astype(o_ref.dtype)


def proj_wcast(a, wt_all, layer, row_start, n_tiles, tn, *, out_dtype, tm=1024, name="proj"):
    m, kdim = a.shape
    tm = min(tm, m)
    nbytes = (_nbytes((tm, kdim), BF16) + _nbytes((kdim, tn), F32) + _nbytes((kdim, tn), BF16)
              + _nbytes((tm, tn), out_dtype) + _nbytes((tm, tn), F32))
    return pl.pallas_call(
        _proj_body,
        grid=(n_tiles, m // tm),
        in_specs=[pl.BlockSpec((tm, kdim), lambda j, i: (i, 0)),
                  pl.BlockSpec((pl.Element(1), pl.Element(tn), pl.Element(kdim)),
                               lambda j, i: (layer, pl.multiple_of(row_start(j), 8), 0))],
        out_specs=pl.BlockSpec((tm, tn), lambda j, i: (i, j)),
        out_shape=jax.ShapeDtypeStruct((m, n_tiles * tn), out_dtype),
        scratch_shapes=[pltpu.VMEM((kdim, tn), BF16)],
        compiler_params=_params(("parallel", "arbitrary"), nbytes),
        name=name,
    )(a, wt_all)


def _gateup_body(a_ref, wg_ref, wu_ref, o_ref, wgb_ref, wub_ref, *, n_tiles):
    s = pl.program_id(2)
    tile = pl.program_id(0) * FFN_GROUP + s

    @pl.when(tile < n_tiles)
    def _():
        @pl.when(pl.program_id(1) == 0)
        def _():
            wgb_ref[s] = wg_ref[...].astype(BF16)
            wub_ref[s] = wu_ref[...].astype(BF16)

        a = a_ref[...]
        g = _dot(a, wgb_ref[s])
        u = _dot(a, wub_ref[s])
        o_ref[...] = (_silu(g) * u).astype(o_ref.dtype)


def ffn_gate_up(a, wg_all, wu_all, layer, tm=1024, tn=FFN_TILE):
    m, kdim = a.shape
    n = wg_all.shape[2]
    tm = min(tm, m)
    n_tiles = n // tn
    n_groups = pl.cdiv(n_tiles, FFN_GROUP)
    tile_of = lambda g, s: jnp.minimum(g * FFN_GROUP + s, n_tiles - 1)
    w_index = lambda g, i, s: (layer, 0, jnp.where(i == 0, tile_of(g, s), tile_of(g, FFN_GROUP - 1)))
    nbytes = (_nbytes((tm, kdim), BF16) + 2 * _nbytes((kdim, tn), F32) + 2 * FFN_GROUP * _nbytes((kdim, tn), BF16)
              + _nbytes((tm, tn), BF16) + 3 * _nbytes((tm, tn), F32))
    return pl.pallas_call(
        functools.partial(_gateup_body, n_tiles=n_tiles),
        grid=(n_groups, m // tm, FFN_GROUP),
        in_specs=[pl.BlockSpec((tm, kdim), lambda g, i, s: (i, 0)),
                  pl.BlockSpec((None, kdim, tn), w_index),
                  pl.BlockSpec((None, kdim, tn), w_index)],
        out_specs=pl.BlockSpec((tm, tn), lambda g, i, s: (i, tile_of(g, s))),
        out_shape=jax.ShapeDtypeStruct((m, n), BF16),
        scratch_shapes=[pltpu.VMEM((FFN_GROUP, kdim, tn), BF16), pltpu.VMEM((FFN_GROUP, kdim, tn), BF16)],
        compiler_params=_params(("arbitrary", "arbitrary", "arbitrary"), nbytes),
        name="ffn_gate_up",
    )(a, wg_all, wu_all)


def _swa_body(q_ref, kvc_ref, kvp_ref, bias_ref, sink_ref, qg_ref, kg_ref, og_ref, e_ref, et_ref, o_ref, acc_ref):
    hd, blk, grp = SWA_HEAD_DIM, SWA_BLOCK, SWA_GROUP
    kband = jnp.concatenate([kvp_ref[:, :SWA_KV_WIDTH], kvc_ref[:, :SWA_KV_WIDTH]], axis=0).astype(F32)
    vband = jnp.concatenate([kvp_ref[:, SWA_KV_WIDTH:], kvc_ref[:, SWA_KV_WIDTH:]], axis=0)
    ones = jnp.ones((2 * blk, V7X_LANES), BF16)

    q = q_ref[...].astype(F32)
    ssq = _dot_exact_rhs(q * q, e_ref[...], 2)
    inv = lax.rsqrt(ssq * (1.0 / hd) + EPS)
    qn = (q * _dot_exact_rhs(inv, et_ref[...], 3) * qg_ref[...]).astype(BF16)

    for j in range(SWA_KV_HEADS):
        kn = _rms(kband[:, hd * j:hd * (j + 1)], kg_ref[...]).astype(BF16)
        qs = jnp.concatenate([qn[:, hd * (grp * j + g):hd * (grp * j + g + 1)] for g in range(grp)], axis=0)
        s = _dot_nt(qs, kn) * (hd ** -0.5) + bias_ref[j]
        sink = sink_ref[j]
        s_prev, s_cur = s[:, :blk], s[:, blk:]
        row_max = jnp.max(jnp.maximum(s_prev, s_cur), axis=-1, keepdims=True)
        mx = jnp.maximum(jnp.broadcast_to(row_max, sink.shape), sink)
        p = jnp.concatenate([jnp.exp(s_prev - mx), jnp.exp(s_cur - mx)], axis=1).astype(BF16)
        total = _dot(p, ones) + jnp.exp(sink - mx)
        o = _dot(p, vband[:, hd * j:hd * (j + 1)]) / total[:, :hd]
        for g in range(grp):
            h = grp * j + g
            acc_ref[:, hd * h:hd * (h + 1)] = o[blk * g:blk * (g + 1), :]
    o_ref[...] = _rms(acc_ref[...], og_ref[...]).astype(o_ref.dtype)


def swa_mixer(qkv, bias, sinks, q_gain, k_gain, out_gain, batch, seq):
    nb = seq // SWA_BLOCK
    kv_blk = SWA_WIDTH // (2 * SWA_KV_WIDTH)
    rows = SWA_GROUP * SWA_BLOCK
    no_prev = jnp.arange(2 * SWA_BLOCK) < SWA_BLOCK
    bias_g = jnp.stack([jnp.where(no_prev, -jnp.inf, bias), bias]).reshape(2, SWA_KV_HEADS, rows, 2 * SWA_BLOCK)
    sink_col = jnp.broadcast_to(jnp.repeat(sinks.astype(F32), SWA_BLOCK)[:, None],
                                (SWA_HEADS * SWA_BLOCK, V7X_LANES)).reshape(SWA_KV_HEADS, rows, V7X_LANES)
    head_of_col = jnp.arange(SWA_WIDTH) // SWA_HEAD_DIM
    e = (head_of_col[:, None] == jnp.arange(V7X_LANES)[None, :]).astype(BF16)
    const = lambda shape: pl.BlockSpec(shape, lambda b, n: (0,) * len(shape))
    nbytes = (_nbytes((SWA_BLOCK, SWA_WIDTH), BF16) * 2 + 2 * _nbytes((SWA_BLOCK, 2 * SWA_KV_WIDTH), BF16)
              + _nbytes(bias.shape, F32) + _nbytes((SWA_KV_HEADS, rows, V7X_LANES), F32)
              + 2 * _nbytes(e.shape, BF16) + _nbytes((SWA_BLOCK, SWA_WIDTH), F32)
              + 4 * _nbytes((rows, 2 * SWA_BLOCK), F32))
    return pl.pallas_call(
        _swa_body,
        grid=(batch, nb),
        in_specs=[pl.BlockSpec((SWA_BLOCK, SWA_WIDTH), lambda b, n: (b * nb + n, 0)),
                  pl.BlockSpec((SWA_BLOCK, 2 * SWA_KV_WIDTH), lambda b, n: (b * nb + n, kv_blk)),
                  pl.BlockSpec((SWA_BLOCK, 2 * SWA_KV_WIDTH),
                               lambda b, n: (b * nb + jnp.maximum(n - 1, 0), kv_blk)),
                  pl.BlockSpec((None,) + bias_g.shape[1:], lambda b, n: (jnp.minimum(n, 1), 0, 0, 0)),
                  const(sink_col.shape),
                  const((1, SWA_WIDTH)), const((1, SWA_HEAD_DIM)), const((1, SWA_WIDTH)),
                  const(e.shape), const(e.T.shape)],
        out_specs=pl.BlockSpec((SWA_BLOCK, SWA_WIDTH), lambda b, n: (b * nb + n, 0)),
        out_shape=jax.ShapeDtypeStruct((batch * seq, SWA_WIDTH), BF16),
        scratch_shapes=[pltpu.VMEM((SWA_BLOCK, SWA_WIDTH), F32)],
        compiler_params=_params(("parallel", "parallel"), nbytes),
        name="swa_mixer",
    )(qkv, qkv, qkv, bias_g, sink_col, jnp.tile(q_gain, SWA_HEADS).reshape(1, -1), k_gain.reshape(1, -1),
      out_gain.reshape(1, -1), e, e.T)


def _t5_bucket(dist):
    n = jnp.maximum(dist, 0)
    max_exact = REL_BUCKETS // 2
    nf = jnp.maximum(n, 1).astype(F32)
    large = max_exact + (jnp.log(nf / max_exact) / math.log(REL_MAX_DIST / max_exact)
                         * (REL_BUCKETS - max_exact)).astype(jnp.int32)
    large = jnp.minimum(large, REL_BUCKETS - 1)
    return jnp.where(n < max_exact, n, large)


def swa_band_bias(rel_bias):
    i = jnp.arange(SWA_BLOCK, dtype=jnp.int32)[:, None]
    j = jnp.arange(2 * SWA_BLOCK, dtype=jnp.int32)[None, :]
    dist = i + SWA_BLOCK - j
    onehot = _t5_bucket(dist)[None] == jnp.arange(REL_BUCKETS, dtype=jnp.int32)[:, None, None]
    bias = jnp.sum(jnp.where(onehot[:, None], rel_bias.astype(F32)[:, :, None, None], 0.0), axis=0)
    in_window = (dist >= 0) & (dist < SWA_WINDOW)
    return jnp.where(in_window[None], bias, -jnp.inf)


def _gla_body(x_ref, gl_ref, wup_ref, bup_ref, gain_ref, o_ref, state_ref):
    tb = x_ref.shape[0]
    dk, dv, c = GLA_KEY_DIM, GLA_VAL_DIM, GLA_CHUNK

    @pl.when(pl.program_id(1) == 0)
    def _():
        state_ref[...] = jnp.zeros_like(state_ref)

    gl_hi, gl_lo = _split_bf16(gl_ref[...], 2)
    w_hi, w_lo = _split_bf16(wup_ref[...], 2)
    pre = _dot(gl_hi, w_hi) + _dot(gl_hi, w_lo) + _dot(gl_lo, w_hi) + bup_ref[...]
    g = (jnp.minimum(pre, 0.0) - jnp.log1p(jnp.exp(-jnp.abs(pre)))) * (1.0 / GLA_GATE_NORMALIZER)

    row = lax.broadcasted_iota(jnp.int32, (tb, tb), 0)
    colm = lax.broadcasted_iota(jnp.int32, (tb, tb), 1)
    same_chunk_lower = ((row // c) == (colm // c)) & (colm <= row)
    bcum_all = _dot_exact_lhs(same_chunk_lower.astype(BF16), g, 3)

    ri = lax.broadcasted_iota(jnp.int32, (c, c), 0)
    ci = lax.broadcasted_iota(jnp.int32, (c, c), 1)
    causal = ci <= ri
    for ch in range(tb // c):
        r0 = ch * c
        for h in range(GLA_HEADS):
            bcum = bcum_all[r0:r0 + c, dk * h:dk * (h + 1)]
            blast = bcum[c - 1:c, :]
            q = x_ref[r0:r0 + c, dk * h:dk * (h + 1)].astype(F32)
            k = x_ref[r0:r0 + c, GLA_KEY_WIDTH + dk * h:GLA_KEY_WIDTH + dk * (h + 1)].astype(F32)
            v = x_ref[r0:r0 + c, 2 * GLA_KEY_WIDTH + dv * h:2 * GLA_KEY_WIDTH + dv * (h + 1)]
            r = x_ref[r0:r0 + c, 2 * GLA_KEY_WIDTH + GLA_WIDTH + dv * h:
                      2 * GLA_KEY_WIDTH + GLA_WIDTH + dv * (h + 1)].astype(F32)
            qd = (q * (dk ** -0.5) * jnp.exp(bcum)).astype(BF16)
            kd = (k * jnp.exp(-bcum)).astype(BF16)
            kl = (k * jnp.exp(blast - bcum)).astype(BF16)
            att = jnp.where(causal, _dot_nt(qd, kd), 0.0)
            st = state_ref[h]
            o = _dot(att.astype(BF16), v) + _dot_nt(qd, st.astype(BF16))
            state_ref[h] = st * jnp.exp(blast) + _dot_tn(v, kl)
            o = _rms(o, gain_ref[...]) * _silu(r)
            o_ref[r0:r0 + c, dv * h:dv * (h + 1)] = o.astype(o_ref.dtype)


def gla_mixer(x, small, w_up_pad, b_up, norm_gain, batch, seq):
    tb = min(GLA_TIME_BLOCK, seq)
    nt = seq // tb
    width = x.shape[1]
    nbytes = (_nbytes((tb, width), BF16) + _nbytes((tb, V7X_LANES), F32) + _nbytes(w_up_pad.shape, F32)
              + _nbytes((tb, GLA_WIDTH), BF16) + _nbytes((GLA_HEADS, GLA_VAL_DIM, GLA_KEY_DIM), F32)
              + 8 * _nbytes((tb, GLA_KEY_WIDTH), F32))
    return pl.pallas_call(
        _gla_body,
        grid=(batch, nt),
        in_specs=[pl.BlockSpec((tb, width), lambda b, t: (b * nt + t, 0)),
                  pl.BlockSpec((tb, V7X_LANES), lambda b, t: (b * nt + t, 0)),
                  pl.BlockSpec(w_up_pad.shape, lambda b, t: (0, 0)),
                  pl.BlockSpec((1, GLA_KEY_WIDTH), lambda b, t: (0, 0)),
                  pl.BlockSpec((1, GLA_VAL_DIM), lambda b, t: (0, 0))],
        out_specs=pl.BlockSpec((tb, GLA_WIDTH), lambda b, t: (b * nt + t, 0)),
        out_shape=jax.ShapeDtypeStruct((batch * seq, GLA_WIDTH), BF16),
        scratch_shapes=[pltpu.VMEM((GLA_HEADS, GLA_VAL_DIM, GLA_KEY_DIM), F32)],
        compiler_params=_params(("parallel", "arbitrary"), nbytes),
        name="gla_mixer",
    )(x, small, w_up_pad, b_up.reshape(1, -1), norm_gain.reshape(1, -1))


def _ssd_body(x_ref, dt_ref, cw_ref, cb_ref, dtb_ref, alog_ref, dexp_ref, gain_ref, expand_ref, shift_ref,
              wcast_in_ref, o_ref, wcast_out_ref, state_ref, tail_ref, y_ref, *, wcast_blocks):
    L, P, N = SSD_CHUNK, SSD_HEAD_DIM, SSD_STATE
    gw = SSD_GROUP_WIDTH

    @pl.when(pl.program_id(0) * pl.num_programs(1) + pl.program_id(1) < wcast_blocks)
    def _():
        wcast_out_ref[...] = wcast_in_ref[...].astype(BF16)

    @pl.when(pl.program_id(1) == 0)
    def _():
        state_ref[...] = jnp.zeros_like(state_ref)
        tail_ref[...] = jnp.zeros_like(tail_ref)

    xin = x_ref[:, SSD_WIDTH:]
    xin32 = xin.astype(F32)
    shifted = _dot(shift_ref[...], xin)
    conv = cb_ref[...] + cw_ref[SSD_CONV - 1:SSD_CONV, :] * xin32
    head = jnp.zeros((8, SSD_CONV_CH), F32)
    for j in range(1, SSD_CONV):
        wj = cw_ref[SSD_CONV - 1 - j:SSD_CONV - j, :]
        conv = conv + wj * shifted[L * (j - 1):L * j, :]
        head = head + wj * tail_ref[8 - j:16 - j, :]
    conv = jnp.concatenate([conv[0:8, :] + head, conv[8:, :]], axis=0)
    tail_ref[0:8, :] = xin32[L - 8:, :]
    xbc = _silu(conv)
    xs = xbc[:, :SSD_WIDTH]
    bm = xbc[:, SSD_WIDTH:SSD_WIDTH + SSD_BC_WIDTH].astype(BF16)
    cm = xbc[:, SSD_WIDTH + SSD_BC_WIDTH:].astype(BF16)

    dt = _softplus(dt_ref[...] + dtb_ref[...])
    dta = dt * (-jnp.exp(alog_ref[...]))
    ri = lax.broadcasted_iota(jnp.int32, (L, L), 0)
    ci = lax.broadcasted_iota(jnp.int32, (L, L), 1)
    causal = ci <= ri
    a_cum = _dot_exact_lhs(causal.astype(BF16), dta, 3)
    a_cum_t = a_cum.T
    expand = expand_ref[...]
    a_exp = _dot_exact_rhs(a_cum, expand, 3)
    dt_exp = _dot_exact_rhs(dt, expand, 2)
    a_last = a_exp[L - 1:L, :]
    xd = xs * dt_exp
    xdec = (xd * jnp.exp(a_last - a_exp)).astype(BF16)
    xd16 = xd.astype(BF16)
    out_scale = jnp.exp(a_exp)

    for g in range(SSD_GROUPS):
        bg = bm[:, N * g:N * (g + 1)]
        cg = cm[:, N * g:N * (g + 1)]
        cb = _dot_nt(cg, bg)
        for kk in range(SSD_HEADS_PER_GROUP):
            h = g * SSD_HEADS_PER_GROUP + kk
            hl = SSD_DT_LANE0 + h
            diff = a_cum[:, hl:hl + 1] - a_cum_t[hl:hl + 1, :]
            m = cb * jnp.exp(jnp.where(causal, diff, -jnp.inf))
            y_ref[:, P * h:P * (h + 1)] = _dot(m.astype(BF16), xd16[:, P * h:P * (h + 1)])
        sg = state_ref[:, gw * g:gw * (g + 1)]
        y_off = _dot(cg, sg.astype(BF16)) * out_scale[:, gw * g:gw * (g + 1)]
        state_ref[:, gw * g:gw * (g + 1)] = (sg * jnp.exp(a_last[:, gw * g:gw * (g + 1)])
                                             + _dot_tn(bg, xdec[:, gw * g:gw * (g + 1)]))
        yg = y_ref[:, gw * g:gw * (g + 1)] + y_off + xs[:, gw * g:gw * (g + 1)] * dexp_ref[:, gw * g:gw * (g + 1)]
        yg = yg * _silu(x_ref[:, gw * g:gw * (g + 1)].astype(F32))
        o_ref[:, gw * g:gw * (g + 1)] = _rms(yg, gain_ref[:, gw * g:gw * (g + 1)]).astype(o_ref.dtype)


def ssd_mixer(x, small, conv_w, conv_b, dt_bias, a_log, d_skip, norm_gain, batch, seq, wcast_all, layer):
    L = SSD_CHUNK
    nc = seq // L
    width = x.shape[1]
    _, wc_rows, wc_cols = wcast_all.shape
    wc_tile = V7X_MXU_WIDTH
    wc_blocks = wc_rows // wc_tile
    assert wc_rows % wc_tile == 0 and wc_blocks <= batch * nc
    wc_index = lambda b, c: jnp.minimum(b * nc + c, wc_blocks - 1)
    t = jnp.arange(L)
    shift = jnp.concatenate([(t[:, None] - j == t[None, :]) for j in range(1, SSD_CONV)], axis=0).astype(BF16)
    lane_pad = (SSD_DT_LANE0, V7X_LANES - SSD_DT_LANE0 - SSD_HEADS)
    dtb = jnp.pad(dt_bias, lane_pad).reshape(1, V7X_LANES)
    alog = jnp.pad(a_log, lane_pad).reshape(1, V7X_LANES)
    dexp = jnp.repeat(d_skip, SSD_HEAD_DIM).reshape(1, SSD_WIDTH)
    expand = (jnp.arange(V7X_LANES)[:, None] - SSD_DT_LANE0
              == (jnp.arange(SSD_WIDTH)[None, :] // SSD_HEAD_DIM)).astype(BF16)
    nbytes = (_nbytes((L, width), BF16) + _nbytes((L, V7X_LANES), F32) + _nbytes((L, SSD_WIDTH), BF16)
              + _nbytes(expand.shape, BF16) + 2 * _nbytes((SSD_STATE, SSD_WIDTH), F32)
              + 16 * _nbytes((L, SSD_CONV_CH), F32) + _nbytes((wc_tile, wc_cols), F32)
              + _nbytes((wc_tile, wc_cols), BF16))
    return pl.pallas_call(
        functools.partial(_ssd_body, wcast_blocks=wc_blocks),
        grid=(batch, nc),
        in_specs=[pl.BlockSpec((L, width), lambda b, c: (b * nc + c, 0)),
                  pl.BlockSpec((L, V7X_LANES), lambda b, c: (b * nc + c, 1)),
                  pl.BlockSpec((SSD_CONV, SSD_CONV_CH), lambda b, c: (0, 0)),
                  pl.BlockSpec((1, SSD_CONV_CH), lambda b, c: (0, 0)),
                  pl.BlockSpec((1, V7X_LANES), lambda b, c: (0, 0)),
                  pl.BlockSpec((1, V7X_LANES), lambda b, c: (0, 0)),
                  pl.BlockSpec((1, SSD_WIDTH), lambda b, c: (0, 0)),
                  pl.BlockSpec((1, SSD_WIDTH), lambda b, c: (0, 0)),
                  pl.BlockSpec(expand.shape, lambda b, c: (0, 0)),
                  pl.BlockSpec(shift.shape, lambda b, c: (0, 0)),
                  pl.BlockSpec((None, wc_tile, wc_cols), lambda b, c: (layer, wc_index(b, c), 0))],
        out_specs=[pl.BlockSpec((L, SSD_WIDTH), lambda b, c: (b * nc + c, 0)),
                   pl.BlockSpec((wc_tile, wc_cols), lambda b, c: (wc_index(b, c), 0))],
        out_shape=[jax.ShapeDtypeStruct((batch * seq, SSD_WIDTH), BF16),
                   jax.ShapeDtypeStruct((wc_rows, wc_cols), BF16)],
        scratch_shapes=[pltpu.VMEM((SSD_STATE, SSD_WIDTH), F32),
                        pltpu.VMEM((16, SSD_CONV_CH), F32),
                        pltpu.VMEM((L, SSD_WIDTH), F32)],
        compiler_params=_params(("arbitrary", "arbitrary"), nbytes),
        name="ssd_mixer",
    )(x, small, conv_w.reshape(SSD_CONV, SSD_CONV_CH), conv_b.reshape(1, -1), dtb, alog, dexp,
      norm_gain.reshape(1, -1), expand, shift, wcast_all)


def _xattn_body(h_ref, kv_ref, wq_ref, wo_ref, lnx_ref, qg_ref, kg_ref, lnf_ref, o_ref, hf_ref, att_ref):
    hd = X_HEAD_DIM
    mem_len = kv_ref.shape[0]
    half = mem_len // 2
    ones = jnp.ones((mem_len, V7X_LANES), BF16)
    h = h_ref[...]
    q = _dot(_rms(h, lnx_ref[...]).astype(BF16), wq_ref[...])
    for i in range(X_HEADS):
        qh = _rms(q[:, hd * i:hd * (i + 1)], qg_ref[...]).astype(BF16)
        kh = _rms(kv_ref[:, hd * i:hd * (i + 1)], kg_ref[...]).astype(BF16)
        vh = kv_ref[:, X_WIDTH + hd * i:X_WIDTH + hd * (i + 1)].astype(BF16)
        s = _dot_nt(qh, kh) * (hd ** -0.5)
        s_lo, s_hi = s[:, :half], s[:, half:]
        mx = jnp.broadcast_to(jnp.max(jnp.maximum(s_lo, s_hi), axis=-1, keepdims=True), s_lo.shape)
        p = jnp.concatenate([jnp.exp(s_lo - mx), jnp.exp(s_hi - mx)], axis=1).astype(BF16)
        att_ref[:, hd * i:hd * (i + 1)] = (_dot(p, vh) / _dot(p, ones)).astype(BF16)
    h_new = h + _dot(att_ref[...], wo_ref[...])
    o_ref[...] = h_new
    hf_ref[...] = _rms(h_new, lnf_ref[...]).astype(hf_ref.dtype)


def cross_attention_block(h, kv, w_q, w_o, ln_x, q_gain, k_gain, ln_ffn, batch, seq, mem_len, tq=256):
    m, d = h.shape
    tq = min(tq, seq)
    nq = seq // tq
    assert mem_len == 2 * V7X_LANES and X_HEAD_DIM == V7X_LANES
    const = lambda shape: pl.BlockSpec(shape, lambda b, t: (0,) * len(shape))
    row_blk = lambda width: pl.BlockSpec((tq, width), lambda b, t: (b * nq + t, 0))
    nbytes = (2 * _nbytes((tq, d), F32) + _nbytes((tq, d), BF16) + _nbytes((mem_len, 2 * X_WIDTH), F32)
              + 2 * _nbytes((d, X_WIDTH), BF16) + 2 * _nbytes((tq, d), F32))
    return pl.pallas_call(
        _xattn_body,
        grid=(batch, nq),
        in_specs=[row_blk(d),
                  pl.BlockSpec((mem_len, 2 * X_WIDTH), lambda b, t: (b, 0)),
                  const((d, X_WIDTH)), const((X_WIDTH, d)), const((1, d)),
                  const((1, X_HEAD_DIM)), const((1, X_HEAD_DIM)), const((1, d))],
        out_specs=[row_blk(d), row_blk(d)],
        out_shape=[jax.ShapeDtypeStruct((m, d), F32), jax.ShapeDtypeStruct((m, d), BF16)],
        scratch_shapes=[pltpu.VMEM((tq, X_WIDTH), BF16)],
        compiler_params=_params(("parallel", "parallel"), nbytes),
        name="xattn_block",
    )(h, kv, w_q, w_o, ln_x.reshape(1, -1), q_gain.reshape(1, -1), k_gain.reshape(1, -1), ln_ffn.reshape(1, -1))


_IN_OFF = np.cumsum([0, SWA_WIDTH, SWA_KV_WIDTH, SWA_KV_WIDTH, GLA_KEY_WIDTH, GLA_KEY_WIDTH, GLA_WIDTH,
                     GLA_WIDTH, GLA_GATE_RANK, SSD_WIDTH, SSD_CONV_CH, SSD_HEADS]).tolist()
SWA_COL0, GLA_COL0, GLOW_COL0, SSD_COL0, DT_COL0 = _IN_OFF[0], _IN_OFF[3], _IN_OFF[7], _IN_OFF[8], _IN_OFF[10]


IN_DIM = _IN_OFF[-1]
SMALL_ROW_STARTS = (GLOW_COL0, IN_DIM - V7X_LANES)
assert SMALL_ROW_STARTS[1] + SSD_DT_LANE0 == DT_COL0


def kernel(x, mem, rel_bias, ln_mix, w_in, swa_q_gain, swa_k_gain, swa_sinks, swa_out_gain, gla_w_gk_up, gla_b_gk_up, gla_norm_gain, ssd_conv_w, ssd_conv_b, ssd_dt_bias, ssd_a_log, ssd_d, ssd_norm_gain, w_mix_out, ln_x, ln_mem, x_w_q, x_w_k, x_w_v, x_w_o, x_q_gain, x_k_gain, ln_ffn, ffn_w_gate, ffn_w_up, ffn_w_down):
    batch, seq, d = x.shape
    mem_len = mem.shape[1]
    m = batch * seq
    band_bias = swa_band_bias(rel_bias)
    h = x.reshape(m, d)
    mem2 = mem.reshape(batch * mem_len, d)
    w_in_t = jnp.swapaxes(w_in, 1, 2)
    for l in range(DEPTH):
        w_up_pad = jnp.pad(gla_w_gk_up[l], ((0, V7X_LANES - GLA_GATE_RANK), (0, 0)))
        w_kv = jnp.concatenate([x_w_k[l], x_w_v[l]], axis=1).astype(BF16)
        hn, p_small = rmsnorm_small(h, ln_mix[l], w_in_t, l, SMALL_ROW_STARTS)
        swa_tile = (GLA_COL0 - SWA_COL0) // 2
        p_swa = proj_wcast(hn, w_in_t, l, lambda j: SWA_COL0 + swa_tile * j, 2, swa_tile,
                           out_dtype=BF16, name="proj_swa")
        p_gla = proj_wcast(hn, w_in_t, l, lambda j: GLA_COL0 + PROJ_TILE * j, (GLOW_COL0 - GLA_COL0) // PROJ_TILE,
                           PROJ_TILE, out_dtype=BF16, name="proj_gla")
        p_ssd = proj_wcast(hn, w_in_t, l, lambda j: SSD_COL0 + PROJ_TILE * j, (DT_COL0 - SSD_COL0) // PROJ_TILE,
                           PROJ_TILE, out_dtype=BF16, name="proj_ssd")
        y_a = swa_mixer(p_swa, band_bias, swa_sinks[l], swa_q_gain[l], swa_k_gain[l], swa_out_gain[l], batch, seq)
        y_b = gla_mixer(p_gla, p_small, w_up_pad, gla_b_gk_up[l], gla_norm_gain[l], batch, seq)
        y_c, w_down16 = ssd_mixer(p_ssd, p_small, ssd_conv_w[l], ssd_conv_b[l], ssd_dt_bias[l], ssd_a_log[l],
                                  ssd_d[l], ssd_norm_gain[l], batch, seq, ffn_w_down, l)
        h = mix_out(y_a, y_b, y_c, w_mix_out, l, h)
        memn = rmsnorm(mem2, ln_mem[l])
        kv = matmul(memn, w_kv, out_dtype=F32, tm=1024, tn=512, name="xattn_kv")
        h, hf = cross_attention_block(h, kv, x_w_q[l].astype(BF16), x_w_o[l].astype(BF16), ln_x[l],
                                      x_q_gain[l], x_k_gain[l], ln_ffn[l], batch, seq, mem_len)
        hidden = ffn_gate_up(hf, ffn_w_gate, ffn_w_up, l)
        h = matmul(hidden, w_down16, out_dtype=F32, res=h, tm=512, tn=256, name="ffn_down")
    return h.reshape(batch, seq, d)
```

```python
import functools
import math

import numpy as np
import jax
import jax.numpy as jnp
from jax import lax
from jax.experimental import pallas as pl
from jax.experimental.pallas import tpu as pltpu

F32 = jnp.float32
BF16 = jnp.bfloat16

D_MODEL = 4096
DEPTH = 2
EPS = 1e-6
SWA_WIDTH = 1024
SWA_HEAD_DIM = 64
SWA_HEADS = 16
SWA_KV_HEADS = 2
SWA_GROUP = SWA_HEADS // SWA_KV_HEADS
SWA_KV_WIDTH = SWA_KV_HEADS * SWA_HEAD_DIM
SWA_WINDOW = 128
SWA_BLOCK = 128
REL_BUCKETS = 32
REL_MAX_DIST = 128
GLA_WIDTH = 1024
GLA_HEADS = 4
GLA_VAL_DIM = 256
GLA_KEY_DIM = 128
GLA_KEY_WIDTH = GLA_HEADS * GLA_KEY_DIM
GLA_GATE_RANK = 16
GLA_GATE_NORMALIZER = 16.0
GLA_CHUNK = 64
SSD_WIDTH = 2048
SSD_HEAD_DIM = 64
SSD_HEADS = 32
SSD_GROUPS = 8
SSD_HEADS_PER_GROUP = SSD_HEADS // SSD_GROUPS
SSD_STATE = 128
SSD_CONV = 4
SSD_CHUNK = 128
SSD_BC_WIDTH = SSD_GROUPS * SSD_STATE
SSD_CONV_CH = SSD_WIDTH + 2 * SSD_BC_WIDTH
SSD_GROUP_WIDTH = SSD_WIDTH // SSD_GROUPS
X_HEADS = 4
X_HEAD_DIM = 128
X_WIDTH = X_HEADS * X_HEAD_DIM
FFN_HIDDEN = 11008

V7X_LANES = 128
V7X_VMEM_BYTES = 64 * 1024 * 1024
VMEM_REQUEST_CAP = (V7X_VMEM_BYTES * 7) // 8
V7X_MXU_WIDTH = 256
FFN_SLAB = 2 * V7X_MXU_WIDTH
PROJ_TILE = 2 * V7X_MXU_WIDTH
GLA_TIME_BLOCK = 256
SSD_DT_LANE0 = V7X_LANES - SSD_HEADS


def _params(semantics, block_bytes):
    limit = min(VMEM_REQUEST_CAP, 2 * block_bytes + 16 * 1024 * 1024)
    return pltpu.CompilerParams(dimension_semantics=semantics, vmem_limit_bytes=int(limit))


def _nbytes(shape, dtype):
    return int(np.prod(shape)) * jnp.dtype(dtype).itemsize


def _split_bf16(x, terms):
    parts = []
    r = x
    for t in range(terms):
        p = r.astype(BF16)
        parts.append(p)
        if t + 1 < terms:
            r = r - p.astype(F32)
    return parts


def _dot(a, b):
    return jnp.dot(a, b, preferred_element_type=F32)


def _dot_nt(a, b):
    return lax.dot_general(a, b, (((1,), (1,)), ((), ())), preferred_element_type=F32)


def _dot_tn(a, b):
    return lax.dot_general(a, b, (((0,), (0,)), ((), ())), preferred_element_type=F32)


def _dot_exact_lhs(a_bf16, x, terms):
    acc = None
    for p in _split_bf16(x, terms):
        d = _dot(a_bf16, p)
        acc = d if acc is None else acc + d
    return acc


def _dot_exact_rhs(x, b_bf16, terms):
    acc = None
    for p in _split_bf16(x, terms):
        d = _dot(p, b_bf16)
        acc = d if acc is None else acc + d
    return acc


def _silu(x):
    return x / (1.0 + jnp.exp(-x))


def _softplus(x):
    return jnp.maximum(x, 0.0) + jnp.log1p(jnp.exp(-jnp.abs(x)))


def _rms(x, gain):
    return x * lax.rsqrt(jnp.mean(x * x, axis=-1, keepdims=True) + EPS) * gain


def _rmsnorm_body(x_ref, g_ref, o_ref):
    o_ref[...] = _rms(x_ref[...], g_ref[...]).astype(o_ref.dtype)


def rmsnorm(x, gain, tm=256):
    m, d = x.shape
    tm = min(tm, m)
    return pl.pallas_call(
        _rmsnorm_body,
        grid=(m // tm,),
        in_specs=[pl.BlockSpec((tm, d), lambda i: (i, 0)),
                  pl.BlockSpec((1, d), lambda i: (0, 0))],
        out_specs=pl.BlockSpec((tm, d), lambda i: (i, 0)),
        out_shape=jax.ShapeDtypeStruct((m, d), BF16),
        compiler_params=_params(("parallel",), _nbytes((tm, d), F32) + _nbytes((tm, d), BF16)),
        name="rmsnorm",
    )(x, gain.reshape(1, d))


def _rmsnorm_small_body(x_ref, g_ref, wa_ref, wb_ref, o_ref, small_ref, wt_ref):
    @pl.when(pl.program_id(0) == 0)
    def _():
        for t, w_ref in enumerate((wa_ref, wb_ref)):
            for k0 in range(0, w_ref.shape[2], _XPOSE_COLS):
                wt_ref[k0:k0 + _XPOSE_COLS, V7X_LANES * t:V7X_LANES * (t + 1)] = (
                    w_ref[0, :, k0:k0 + _XPOSE_COLS].T.astype(BF16))

    hn = _rms(x_ref[...], g_ref[...]).astype(BF16)
    o_ref[...] = hn
    small_ref[...] = _dot(hn, wt_ref[...])


def rmsnorm_small(x, gain, wt_all, layer, row_starts, tm=256):
    m, d = x.shape
    tm = min(tm, m)
    w_tile = lambda r0: pl.BlockSpec((pl.Element(1), pl.Element(V7X_LANES), pl.Element(d)),
                                     lambda i: (layer, r0, 0))
    nbytes = (_nbytes((tm, d), F32) + _nbytes((tm, d), BF16) + 2 * _nbytes((V7X_LANES, d), F32)
              + _nbytes((d, 2 * V7X_LANES), BF16) + _nbytes((tm, 2 * V7X_LANES), F32))
    return pl.pallas_call(
        _rmsnorm_small_body,
        grid=(m // tm,),
        in_specs=[pl.BlockSpec((tm, d), lambda i: (i, 0)),
                  pl.BlockSpec((1, d), lambda i: (0, 0)),
                  w_tile(row_starts[0]), w_tile(row_starts[1])],
        out_specs=[pl.BlockSpec((tm, d), lambda i: (i, 0)),
                   pl.BlockSpec((tm, 2 * V7X_LANES), lambda i: (i, 0))],
        out_shape=[jax.ShapeDtypeStruct((m, d), BF16), jax.ShapeDtypeStruct((m, 2 * V7X_LANES), F32)],
        scratch_shapes=[pltpu.VMEM((d, 2 * V7X_LANES), BF16)],
        compiler_params=_params(("arbitrary",), nbytes),
        name="rmsnorm_small",
    )(x, gain.reshape(1, d), wt_all, wt_all)


def _mm_body(*refs, nk, has_res):
    a_ref, w_ref = refs[0], refs[1]
    r_ref = refs[2] if has_res else None
    o_ref = refs[2 + has_res]
    part = _dot(a_ref[...], w_ref[...])
    if nk == 1:
        if has_res:
            part = part + r_ref[...]
        o_ref[...] = part.astype(o_ref.dtype)
        return
    k = pl.program_id(2)

    @pl.when(k == 0)
    def _():
        o_ref[...] = (part + r_ref[...]) if has_res else part

    @pl.when(k > 0)
    def _():
        o_ref[...] += part


def matmul(a, w, *, out_dtype, res=None, tm, tn, tk=None, layer=None, name="matmul"):
    m, kdim = a.shape
    n = w.shape[-1]
    tm, tn = min(tm, m), min(tn, n)
    tk = kdim if tk is None else tk
    nk = kdim // tk
    assert m % tm == 0 and n % tn == 0 and kdim % tk == 0
    assert nk == 1 or out_dtype == F32
    has_res = res is not None
    if layer is None:
        w_spec = pl.BlockSpec((tk, tn), lambda i, j, k: (k, j))
    else:
        w_spec = pl.BlockSpec((None, tk, tn), lambda i, j, k: (layer, k, j))
    in_specs = [pl.BlockSpec((tm, tk), lambda i, j, k: (i, k)), w_spec]
    args = [a, w]
    nbytes = _nbytes((tm, tk), BF16) + _nbytes((tk, tn), BF16) + _nbytes((tm, tn), out_dtype) + _nbytes((tm, tn), F32)
    if has_res:
        in_specs.append(pl.BlockSpec((tm, tn), lambda i, j, k: (i, j)))
        args.append(res)
        nbytes += _nbytes((tm, tn), F32)
    return pl.pallas_call(
        functools.partial(_mm_body, nk=nk, has_res=has_res),
        grid=(m // tm, n // tn, nk),
        in_specs=in_specs,
        out_specs=pl.BlockSpec((tm, tn), lambda i, j, k: (i, j)),
        out_shape=jax.ShapeDtypeStruct((m, n), out_dtype),
        compiler_params=_params(("parallel", "parallel", "arbitrary"), nbytes),
        name=name,
    )(*args)


def _mixout_body(ya_ref, yb_ref, yc_ref, w_ref, r_ref, o_ref, wb_ref):
    @pl.when(pl.program_id(1) == 0)
    def _():
        wb_ref[...] = w_ref[...].astype(BF16)

    acc = r_ref[...]
    acc = acc + _dot(ya_ref[...], wb_ref[0:SWA_WIDTH, :])
    acc = acc + _dot(yb_ref[...], wb_ref[SWA_WIDTH:SWA_WIDTH + GLA_WIDTH, :])
    acc = acc + _dot(yc_ref[...], wb_ref[SWA_WIDTH + GLA_WIDTH:, :])
    o_ref[...] = acc


def mix_out(ya, yb, yc, w_all, layer, res, tm=1024, tn=PROJ_TILE):
    m = ya.shape[0]
    _, kdim, n = w_all.shape
    tm = min(tm, m)
    nbytes = (_nbytes((tm, kdim), BF16) + _nbytes((kdim, tn), F32) + _nbytes((kdim, tn), BF16)
              + 3 * _nbytes((tm, tn), F32))
    return pl.pallas_call(
        _mixout_body,
        grid=(n // tn, m // tm),
        in_specs=[pl.BlockSpec((tm, SWA_WIDTH), lambda j, i: (i, 0)),
                  pl.BlockSpec((tm, GLA_WIDTH), lambda j, i: (i, 0)),
                  pl.BlockSpec((tm, SSD_WIDTH), lambda j, i: (i, 0)),
                  pl.BlockSpec((None, kdim, tn), lambda j, i: (layer, 0, j)),
                  pl.BlockSpec((tm, tn), lambda j, i: (i, j))],
        out_specs=pl.BlockSpec((tm, tn), lambda j, i: (i, j)),
        out_shape=jax.ShapeDtypeStruct((m, n), F32),
        scratch_shapes=[pltpu.VMEM((kdim, tn), BF16)],
        compiler_params=_params(("parallel", "arbitrary"), nbytes),
        name="mix_out",
    )(ya, yb, yc, w_all, res)


_XPOSE_ROWS, _XPOSE_COLS = 256, 512


def _proj_body(a_ref, w_ref, o_ref, wb_ref):
    _, tn, kdim = w_ref.shape

    @pl.when(pl.program_id(1) == 0)
    def _():
        for r0 in range(0, tn, _XPOSE_ROWS):
            r1 = min(r0 + _XPOSE_ROWS, tn)
            for k0 in range(0, kdim, _XPOSE_COLS):
                wb_ref[k0:k0 + _XPOSE_COLS, r0:r1] = w_ref[0, r0:r1, k0:k0 + _XPOSE_COLS].T.astype(BF16)

    o_ref[...] = _dot(a_ref[...], wb_ref[...]).astype(o_ref.dtype)


def proj_wcast(a, wt_all, layer, row_start, n_tiles, tn, *, out_dtype, tm=1024, name="proj"):
    m, kdim = a.shape
    tm = min(tm, m)
    nbytes = (_nbytes((tm, kdim), BF16) + _nbytes((kdim, tn), F32) + _nbytes((kdim, tn), BF16)
              + _nbytes((tm, tn), out_dtype) + _nbytes((tm, tn), F32))
    return pl.pallas_call(
        _proj_body,
        grid=(n_tiles, m // tm),
        in_specs=[pl.BlockSpec((tm, kdim), lambda j, i: (i, 0)),
                  pl.BlockSpec((pl.Element(1), pl.Element(tn), pl.Element(kdim)),
                               lambda j, i: (layer, pl.multiple_of(row_start(j), 8), 0))],
        out_specs=pl.BlockSpec((tm, tn), lambda j, i: (i, j)),
        out_shape=jax.ShapeDtypeStruct((m, n_tiles * tn), out_dtype),
        scratch_shapes=[pltpu.VMEM((kdim, tn), BF16)],
        compiler_params=_params(("parallel", "arbitrary"), nbytes),
        name=name,
    )(a, wt_all)


def _gateup_copies(wg_hbm, wu_hbm, stage_ref, sem, layer, slab, width):
    c0 = pl.multiple_of(slab * FFN_SLAB, FFN_SLAB)
    return [pltpu.make_async_copy(w_hbm.at[layer, :, pl.ds(c0, width)], stage_ref.at[k, :, pl.ds(0, width)],
                                  sem.at[k])
            for k, w_hbm in enumerate((wg_hbm, wu_hbm))]


def _gateup_body(a_ref, wg_hbm, wu_hbm, o_ref, stage_ref, wgb_ref, wub_ref, sem, *, layer, last_width):
    slab, i = pl.program_id(0), pl.program_id(1)
    last = pl.num_programs(0) - 1

    def for_slab(s, action):
        @pl.when(s < last)
        def _():
            for c in _gateup_copies(wg_hbm, wu_hbm, stage_ref, sem, layer, s, FFN_SLAB):
                action(c)

        @pl.when(s == last)
        def _():
            for c in _gateup_copies(wg_hbm, wu_hbm, stage_ref, sem, layer, s, last_width):
                action(c)

    @pl.when(i == 0)
    def _():
        @pl.when(slab == 0)
        def _():
            for_slab(slab, lambda c: c.start())

        for_slab(slab, lambda c: c.wait())
        wgb_ref[...] = stage_ref[0].astype(BF16)
        wub_ref[...] = stage_ref[1].astype(BF16)

    @pl.when((i == 1) & (slab < last))
    def _():
        for_slab(slab + 1, lambda c: c.start())

    a = a_ref[...]
    g = _dot(a, wgb_ref[...])
    u = _dot(a, wub_ref[...])
    o_ref[...] = (_silu(g) * u).astype(o_ref.dtype)


def ffn_gate_up(a, wg_all, wu_all, layer, tm=1024):
    m, kdim = a.shape
    n = wg_all.shape[2]
    tm = min(tm, m)
    n_slabs = pl.cdiv(n, FFN_SLAB)
    last_width = n - (n_slabs - 1) * FFN_SLAB
    assert m // tm >= 2 and last_width % V7X_LANES == 0
    nbytes = (2 * _nbytes((tm, kdim), BF16) + 2 * _nbytes((kdim, FFN_SLAB), F32)
              + 2 * _nbytes((kdim, FFN_SLAB), BF16) + 2 * _nbytes((tm, FFN_SLAB), BF16)
              + 6 * _nbytes((tm, FFN_SLAB), F32))
    return pl.pallas_call(
        functools.partial(_gateup_body, layer=layer, last_width=last_width),
        grid=(n_slabs, m // tm),
        in_specs=[pl.BlockSpec((tm, kdim), lambda s, i: (i, 0)),
                  pl.BlockSpec(memory_space=pl.ANY),
                  pl.BlockSpec(memory_space=pl.ANY)],
        out_specs=pl.BlockSpec((tm, FFN_SLAB), lambda s, i: (i, s)),
        out_shape=jax.ShapeDtypeStruct((m, n), BF16),
        scratch_shapes=[pltpu.VMEM((2, kdim, FFN_SLAB), F32),
                        pltpu.VMEM((kdim, FFN_SLAB), BF16), pltpu.VMEM((kdim, FFN_SLAB), BF16),
                        pltpu.SemaphoreType.DMA((2,))],
        compiler_params=pltpu.CompilerParams(dimension_semantics=("arbitrary", "arbitrary"),
                                             vmem_limit_bytes=min(VMEM_REQUEST_CAP, nbytes)),
        name="ffn_gate_up",
    )(a, wg_all, wu_all)


def _swa_body(q_ref, kvc_ref, kvp_ref, bias_ref, sink_ref, qg_ref, kg_ref, og_ref, e_ref, et_ref, o_ref, acc_ref):
    hd, blk, grp = SWA_HEAD_DIM, SWA_BLOCK, SWA_GROUP
    kband = jnp.concatenate([kvp_ref[:, :SWA_KV_WIDTH], kvc_ref[:, :SWA_KV_WIDTH]], axis=0).astype(F32)
    vband = jnp.concatenate([kvp_ref[:, SWA_KV_WIDTH:], kvc_ref[:, SWA_KV_WIDTH:]], axis=0)
    ones = jnp.ones((2 * blk, V7X_LANES), BF16)

    q = q_ref[...].astype(F32)
    ssq = _dot_exact_rhs(q * q, e_ref[...], 2)
    inv = lax.rsqrt(ssq * (1.0 / hd) + EPS)
    qn = (q * _dot_exact_rhs(inv, et_ref[...], 3) * qg_ref[...]).astype(BF16)

    for j in range(SWA_KV_HEADS):
        kn = _rms(kband[:, hd * j:hd * (j + 1)], kg_ref[...]).astype(BF16)
        qs = jnp.concatenate([qn[:, hd * (grp * j + g):hd * (grp * j + g + 1)] for g in range(grp)], axis=0)
        s = _dot_nt(qs, kn) * (hd ** -0.5) + bias_ref[j]
        sink = sink_ref[j]
        s_prev, s_cur = s[:, :blk], s[:, blk:]
        row_max = jnp.max(jnp.maximum(s_prev, s_cur), axis=-1, keepdims=True)
        mx = jnp.maximum(jnp.broadcast_to(row_max, sink.shape), sink)
        p = jnp.concatenate([jnp.exp(s_prev - mx), jnp.exp(s_cur - mx)], axis=1).astype(BF16)
        total = _dot(p, ones) + jnp.exp(sink - mx)
        o = _dot(p, vband[:, hd * j:hd * (j + 1)]) / total[:, :hd]
        for g in range(grp):
            h = grp * j + g
            acc_ref[:, hd * h:hd * (h + 1)] = o[blk * g:blk * (g + 1), :]
    o_ref[...] = _rms(acc_ref[...], og_ref[...]).astype(o_ref.dtype)


def swa_mixer(qkv, bias, sinks, q_gain, k_gain, out_gain, batch, seq):
    nb = seq // SWA_BLOCK
    kv_blk = SWA_WIDTH // (2 * SWA_KV_WIDTH)
    rows = SWA_GROUP * SWA_BLOCK
    no_prev = jnp.arange(2 * SWA_BLOCK) < SWA_BLOCK
    bias_g = jnp.stack([jnp.where(no_prev, -jnp.inf, bias), bias]).reshape(2, SWA_KV_HEADS, rows, 2 * SWA_BLOCK)
    sink_col = jnp.broadcast_to(jnp.repeat(sinks.astype(F32), SWA_BLOCK)[:, None],
                                (SWA_HEADS * SWA_BLOCK, V7X_LANES)).reshape(SWA_KV_HEADS, rows, V7X_LANES)
    head_of_col = jnp.arange(SWA_WIDTH) // SWA_HEAD_DIM
    e = (head_of_col[:, None] == jnp.arange(V7X_LANES)[None, :]).astype(BF16)
    const = lambda shape: pl.BlockSpec(shape, lambda b, n: (0,) * len(shape))
    nbytes = (_nbytes((SWA_BLOCK, SWA_WIDTH), BF16) * 2 + 2 * _nbytes((SWA_BLOCK, 2 * SWA_KV_WIDTH), BF16)
              + _nbytes(bias.shape, F32) + _nbytes((SWA_KV_HEADS, rows, V7X_LANES), F32)
              + 2 * _nbytes(e.shape, BF16) + _nbytes((SWA_BLOCK, SWA_WIDTH), F32)
              + 4 * _nbytes((rows, 2 * SWA_BLOCK), F32))
    return pl.pallas_call(
        _swa_body,
        grid=(batch, nb),
        in_specs=[pl.BlockSpec((SWA_BLOCK, SWA_WIDTH), lambda b, n: (b * nb + n, 0)),
                  pl.BlockSpec((SWA_BLOCK, 2 * SWA_KV_WIDTH), lambda b, n: (b * nb + n, kv_blk)),
                  pl.BlockSpec((SWA_BLOCK, 2 * SWA_KV_WIDTH),
                               lambda b, n: (b * nb + jnp.maximum(n - 1, 0), kv_blk)),
                  pl.BlockSpec((None,) + bias_g.shape[1:], lambda b, n: (jnp.minimum(n, 1), 0, 0, 0)),
                  const(sink_col.shape),
                  const((1, SWA_WIDTH)), const((1, SWA_HEAD_DIM)), const((1, SWA_WIDTH)),
                  const(e.shape), const(e.T.shape)],
        out_specs=pl.BlockSpec((SWA_BLOCK, SWA_WIDTH), lambda b, n: (b * nb + n, 0)),
        out_shape=jax.ShapeDtypeStruct((batch * seq, SWA_WIDTH), BF16),
        scratch_shapes=[pltpu.VMEM((SWA_BLOCK, SWA_WIDTH), F32)],
        compiler_params=_params(("parallel", "parallel"), nbytes),
        name="swa_mixer",
    )(qkv, qkv, qkv, bias_g, sink_col, jnp.tile(q_gain, SWA_HEADS).reshape(1, -1), k_gain.reshape(1, -1),
      out_gain.reshape(1, -1), e, e.T)


def _t5_bucket(dist):
    n = jnp.maximum(dist, 0)
    max_exact = REL_BUCKETS // 2
    nf = jnp.maximum(n, 1).astype(F32)
    large = max_exact + (jnp.log(nf / max_exact) / math.log(REL_MAX_DIST / max_exact)
                         * (REL_BUCKETS - max_exact)).astype(jnp.int32)
    large = jnp.minimum(large, REL_BUCKETS - 1)
    return jnp.where(n < max_exact, n, large)


def swa_band_bias(rel_bias):
    i = jnp.arange(SWA_BLOCK, dtype=jnp.int32)[:, None]
    j = jnp.arange(2 * SWA_BLOCK, dtype=jnp.int32)[None, :]
    dist = i + SWA_BLOCK - j
    onehot = _t5_bucket(dist)[None] == jnp.arange(REL_BUCKETS, dtype=jnp.int32)[:, None, None]
    bias = jnp.sum(jnp.where(onehot[:, None], rel_bias.astype(F32)[:, :, None, None], 0.0), axis=0)
    in_window = (dist >= 0) & (dist < SWA_WINDOW)
    return jnp.where(in_window[None], bias, -jnp.inf)


def _gla_body(x_ref, gl_ref, wup_ref, bup_ref, gain_ref, o_ref, state_ref):
    tb = x_ref.shape[0]
    dk, dv, c = GLA_KEY_DIM, GLA_VAL_DIM, GLA_CHUNK

    @pl.when(pl.program_id(1) == 0)
    def _():
        state_ref[...] = jnp.zeros_like(state_ref)

    gl_hi, gl_lo = _split_bf16(gl_ref[...], 2)
    w_hi, w_lo = _split_bf16(wup_ref[...], 2)
    pre = _dot(gl_hi, w_hi) + _dot(gl_hi, w_lo) + _dot(gl_lo, w_hi) + bup_ref[...]
    g = (jnp.minimum(pre, 0.0) - jnp.log1p(jnp.exp(-jnp.abs(pre)))) * (1.0 / GLA_GATE_NORMALIZER)

    row = lax.broadcasted_iota(jnp.int32, (tb, tb), 0)
    colm = lax.broadcasted_iota(jnp.int32, (tb, tb), 1)
    same_chunk_lower = ((row // c) == (colm // c)) & (colm <= row)
    bcum_all = _dot_exact_lhs(same_chunk_lower.astype(BF16), g, 3)

    ri = lax.broadcasted_iota(jnp.int32, (c, c), 0)
    ci = lax.broadcasted_iota(jnp.int32, (c, c), 1)
    causal = ci <= ri
    for ch in range(tb // c):
        r0 = ch * c
        for h in range(GLA_HEADS):
            bcum = bcum_all[r0:r0 + c, dk * h:dk * (h + 1)]
            blast = bcum[c - 1:c, :]
            q = x_ref[r0:r0 + c, dk * h:dk * (h + 1)].astype(F32)
            k = x_ref[r0:r0 + c, GLA_KEY_WIDTH + dk * h:GLA_KEY_WIDTH + dk * (h + 1)].astype(F32)
            v = x_ref[r0:r0 + c, 2 * GLA_KEY_WIDTH + dv * h:2 * GLA_KEY_WIDTH + dv * (h + 1)]
            r = x_ref[r0:r0 + c, 2 * GLA_KEY_WIDTH + GLA_WIDTH + dv * h:
                      2 * GLA_KEY_WIDTH + GLA_WIDTH + dv * (h + 1)].astype(F32)
            qd = (q * (dk ** -0.5) * jnp.exp(bcum)).astype(BF16)
            kd = (k * jnp.exp(-bcum)).astype(BF16)
            kl = (k * jnp.exp(blast - bcum)).astype(BF16)
            att = jnp.where(causal, _dot_nt(qd, kd), 0.0)
            st = state_ref[h]
            o = _dot(att.astype(BF16), v) + _dot_nt(qd, st.astype(BF16))
            state_ref[h] = st * jnp.exp(blast) + _dot_tn(v, kl)
            o = _rms(o, gain_ref[...]) * _silu(r)
            o_ref[r0:r0 + c, dv * h:dv * (h + 1)] = o.astype(o_ref.dtype)


def gla_mixer(x, small, w_up_pad, b_up, norm_gain, batch, seq):
    tb = min(GLA_TIME_BLOCK, seq)
    nt = seq // tb
    width = x.shape[1]
    nbytes = (_nbytes((tb, width), BF16) + _nbytes((tb, V7X_LANES), F32) + _nbytes(w_up_pad.shape, F32)
              + _nbytes((tb, GLA_WIDTH), BF16) + _nbytes((GLA_HEADS, GLA_VAL_DIM, GLA_KEY_DIM), F32)
              + 8 * _nbytes((tb, GLA_KEY_WIDTH), F32))
    return pl.pallas_call(
        _gla_body,
        grid=(batch, nt),
        in_specs=[pl.BlockSpec((tb, width), lambda b, t: (b * nt + t, 0)),
                  pl.BlockSpec((tb, V7X_LANES), lambda b, t: (b * nt + t, 0)),
                  pl.BlockSpec(w_up_pad.shape, lambda b, t: (0, 0)),
                  pl.BlockSpec((1, GLA_KEY_WIDTH), lambda b, t: (0, 0)),
                  pl.BlockSpec((1, GLA_VAL_DIM), lambda b, t: (0, 0))],
        out_specs=pl.BlockSpec((tb, GLA_WIDTH), lambda b, t: (b * nt + t, 0)),
        out_shape=jax.ShapeDtypeStruct((batch * seq, GLA_WIDTH), BF16),
        scratch_shapes=[pltpu.VMEM((GLA_HEADS, GLA_VAL_DIM, GLA_KEY_DIM), F32)],
        compiler_params=_params(("parallel", "arbitrary"), nbytes),
        name="gla_mixer",
    )(x, small, w_up_pad, b_up.reshape(1, -1), norm_gain.reshape(1, -1))


def _ssd_body(x_ref, dt_ref, cw_ref, cb_ref, dtb_ref, alog_ref, dexp_ref, gain_ref, expand_ref, shift_ref,
              wcast_in_ref, o_ref, wcast_out_ref, state_ref, tail_ref, y_ref, *, wcast_blocks):
    L, P, N = SSD_CHUNK, SSD_HEAD_DIM, SSD_STATE
    gw = SSD_GROUP_WIDTH

    @pl.when(pl.program_id(0) * pl.num_programs(1) + pl.program_id(1) < wcast_blocks)
    def _():
        wcast_out_ref[...] = wcast_in_ref[...].astype(BF16)

    @pl.when(pl.program_id(1) == 0)
    def _():
        state_ref[...] = jnp.zeros_like(state_ref)
        tail_ref[...] = jnp.zeros_like(tail_ref)

    xin = x_ref[:, SSD_WIDTH:]
    xin32 = xin.astype(F32)
    shifted = _dot(shift_ref[...], xin)
    conv = cb_ref[...] + cw_ref[SSD_CONV - 1:SSD_CONV, :] * xin32
    head = jnp.zeros((8, SSD_CONV_CH), F32)
    for j in range(1, SSD_CONV):
        wj = cw_ref[SSD_CONV - 1 - j:SSD_CONV - j, :]
        conv = conv + wj * shifted[L * (j - 1):L * j, :]
        head = head + wj * tail_ref[8 - j:16 - j, :]
    conv = jnp.concatenate([conv[0:8, :] + head, conv[8:, :]], axis=0)
    tail_ref[0:8, :] = xin32[L - 8:, :]
    xbc = _silu(conv)
    xs = xbc[:, :SSD_WIDTH]
    bm = xbc[:, SSD_WIDTH:SSD_WIDTH + SSD_BC_WIDTH].astype(BF16)
    cm = xbc[:, SSD_WIDTH + SSD_BC_WIDTH:].astype(BF16)

    dt = _softplus(dt_ref[...] + dtb_ref[...])
    dta = dt * (-jnp.exp(alog_ref[...]))
    ri = lax.broadcasted_iota(jnp.int32, (L, L), 0)
    ci = lax.broadcasted_iota(jnp.int32, (L, L), 1)
    causal = ci <= ri
    a_cum = _dot_exact_lhs(causal.astype(BF16), dta, 3)
    a_cum_t = a_cum.T
    expand = expand_ref[...]
    a_exp = _dot_exact_rhs(a_cum, expand, 3)
    dt_exp = _dot_exact_rhs(dt, expand, 2)
    a_last = a_exp[L - 1:L, :]
    xd = xs * dt_exp
    xdec = (xd * jnp.exp(a_last - a_exp)).astype(BF16)
    xd16 = xd.astype(BF16)
    out_scale = jnp.exp(a_exp)

    for g in range(SSD_GROUPS):
        bg = bm[:, N * g:N * (g + 1)]
        cg = cm[:, N * g:N * (g + 1)]
        cb = _dot_nt(cg, bg)
        for kk in range(SSD_HEADS_PER_GROUP):
            h = g * SSD_HEADS_PER_GROUP + kk
            hl = SSD_DT_LANE0 + h
            diff = a_cum[:, hl:hl + 1] - a_cum_t[hl:hl + 1, :]
            m = cb * jnp.exp(jnp.where(causal, diff, -jnp.inf))
            y_ref[:, P * h:P * (h + 1)] = _dot(m.astype(BF16), xd16[:, P * h:P * (h + 1)])
        sg = state_ref[:, gw * g:gw * (g + 1)]
        y_off = _dot(cg, sg.astype(BF16)) * out_scale[:, gw * g:gw * (g + 1)]
        state_ref[:, gw * g:gw * (g + 1)] = (sg * jnp.exp(a_last[:, gw * g:gw * (g + 1)])
                                             + _dot_tn(bg, xdec[:, gw * g:gw * (g + 1)]))
        yg = y_ref[:, gw * g:gw * (g + 1)] + y_off + xs[:, gw * g:gw * (g + 1)] * dexp_ref[:, gw * g:gw * (g + 1)]
        yg = yg * _silu(x_ref[:, gw * g:gw * (g + 1)].astype(F32))
        o_ref[:, gw * g:gw * (g + 1)] = _rms(yg, gain_ref[:, gw * g:gw * (g + 1)]).astype(o_ref.dtype)


def ssd_mixer(x, small, conv_w, conv_b, dt_bias, a_log, d_skip, norm_gain, batch, seq, wcast_all, layer):
    L = SSD_CHUNK
    nc = seq // L
    width = x.shape[1]
    _, wc_rows, wc_cols = wcast_all.shape
    wc_tile = V7X_MXU_WIDTH
    wc_blocks = wc_rows // wc_tile
    assert wc_rows % wc_tile == 0 and wc_blocks <= batch * nc
    wc_index = lambda b, c: jnp.minimum(b * nc + c, wc_blocks - 1)
    t = jnp.arange(L)
    shift = jnp.concatenate([(t[:, None] - j == t[None, :]) for j in range(1, SSD_CONV)], axis=0).astype(BF16)
    lane_pad = (SSD_DT_LANE0, V7X_LANES - SSD_DT_LANE0 - SSD_HEADS)
    dtb = jnp.pad(dt_bias, lane_pad).reshape(1, V7X_LANES)
    alog = jnp.pad(a_log, lane_pad).reshape(1, V7X_LANES)
    dexp = jnp.repeat(d_skip, SSD_HEAD_DIM).reshape(1, SSD_WIDTH)
    expand = (jnp.arange(V7X_LANES)[:, None] - SSD_DT_LANE0
              == (jnp.arange(SSD_WIDTH)[None, :] // SSD_HEAD_DIM)).astype(BF16)
    nbytes = (_nbytes((L, width), BF16) + _nbytes((L, V7X_LANES), F32) + _nbytes((L, SSD_WIDTH), BF16)
              + _nbytes(expand.shape, BF16) + 2 * _nbytes((SSD_STATE, SSD_WIDTH), F32)
              + 16 * _nbytes((L, SSD_CONV_CH), F32) + _nbytes((wc_tile, wc_cols), F32)
              + _nbytes((wc_tile, wc_cols), BF16))
    return pl.pallas_call(
        functools.partial(_ssd_body, wcast_blocks=wc_blocks),
        grid=(batch, nc),
        in_specs=[pl.BlockSpec((L, width), lambda b, c: (b * nc + c, 0)),
                  pl.BlockSpec((L, V7X_LANES), lambda b, c: (b * nc + c, 1)),
                  pl.BlockSpec((SSD_CONV, SSD_CONV_CH), lambda b, c: (0, 0)),
                  pl.BlockSpec((1, SSD_CONV_CH), lambda b, c: (0, 0)),
                  pl.BlockSpec((1, V7X_LANES), lambda b, c: (0, 0)),
                  pl.BlockSpec((1, V7X_LANES), lambda b, c: (0, 0)),
                  pl.BlockSpec((1, SSD_WIDTH), lambda b, c: (0, 0)),
                  pl.BlockSpec((1, SSD_WIDTH), lambda b, c: (0, 0)),
                  pl.BlockSpec(expand.shape, lambda b, c: (0, 0)),
                  pl.BlockSpec(shift.shape, lambda b, c: (0, 0)),
                  pl.BlockSpec((None, wc_tile, wc_cols), lambda b, c: (layer, wc_index(b, c), 0))],
        out_specs=[pl.BlockSpec((L, SSD_WIDTH), lambda b, c: (b * nc + c, 0)),
                   pl.BlockSpec((wc_tile, wc_cols), lambda b, c: (wc_index(b, c), 0))],
        out_shape=[jax.ShapeDtypeStruct((batch * seq, SSD_WIDTH), BF16),
                   jax.ShapeDtypeStruct((wc_rows, wc_cols), BF16)],
        scratch_shapes=[pltpu.VMEM((SSD_STATE, SSD_WIDTH), F32),
                        pltpu.VMEM((16, SSD_CONV_CH), F32),
                        pltpu.VMEM((L, SSD_WIDTH), F32)],
        compiler_params=_params(("arbitrary", "arbitrary"), nbytes),
        name="ssd_mixer",
    )(x, small, conv_w.reshape(SSD_CONV, SSD_CONV_CH), conv_b.reshape(1, -1), dtb, alog, dexp,
      norm_gain.reshape(1, -1), expand, shift, wcast_all)


def _xattn_body(h_ref, kv_ref, wq_ref, wo_ref, lnx_ref, qg_ref, kg_ref, lnf_ref, o_ref, hf_ref, att_ref):
    hd = X_HEAD_DIM
    mem_len = kv_ref.shape[0]
    half = mem_len // 2
    ones = jnp.ones((mem_len, V7X_LANES), BF16)
    h = h_ref[...]
    q = _dot(_rms(h, lnx_ref[...]).astype(BF16), wq_ref[...])
    for i in range(X_HEADS):
        qh = _rms(q[:, hd * i:hd * (i + 1)], qg_ref[...]).astype(BF16)
        kh = _rms(kv_ref[:, hd * i:hd * (i + 1)], kg_ref[...]).astype(BF16)
        vh = kv_ref[:, X_WIDTH + hd * i:X_WIDTH + hd * (i + 1)].astype(BF16)
        s = _dot_nt(qh, kh) * (hd ** -0.5)
        s_lo, s_hi = s[:, :half], s[:, half:]
        mx = jnp.broadcast_to(jnp.max(jnp.maximum(s_lo, s_hi), axis=-1, keepdims=True), s_lo.shape)
        p = jnp.concatenate([jnp.exp(s_lo - mx), jnp.exp(s_hi - mx)], axis=1).astype(BF16)
        att_ref[:, hd * i:hd * (i + 1)] = (_dot(p, vh) / _dot(p, ones)).astype(BF16)
    h_new = h + _dot(att_ref[...], wo_ref[...])
    o_ref[...] = h_new
    hf_ref[...] = _rms(h_new, lnf_ref[...]).astype(hf_ref.dtype)


def cross_attention_block(h, kv, w_q, w_o, ln_x, q_gain, k_gain, ln_ffn, batch, seq, mem_len, tq=256):
    m, d = h.shape
    tq = min(tq, seq)
    nq = seq // tq
    assert mem_len == 2 * V7X_LANES and X_HEAD_DIM == V7X_LANES
    const = lambda shape: pl.BlockSpec(shape, lambda b, t: (0,) * len(shape))
    row_blk = lambda width: pl.BlockSpec((tq, width), lambda b, t: (b * nq + t, 0))
    nbytes = (2 * _nbytes((tq, d), F32) + _nbytes((tq, d), BF16) + _nbytes((mem_len, 2 * X_WIDTH), F32)
              + 2 * _nbytes((d, X_WIDTH), BF16) + 2 * _nbytes((tq, d), F32))
    return pl.pallas_call(
        _xattn_body,
        grid=(batch, nq),
        in_specs=[row_blk(d),
                  pl.BlockSpec((mem_len, 2 * X_WIDTH), lambda b, t: (b, 0)),
                  const((d, X_WIDTH)), const((X_WIDTH, d)), const((1, d)),
                  const((1, X_HEAD_DIM)), const((1, X_HEAD_DIM)), const((1, d))],
        out_specs=[row_blk(d), row_blk(d)],
        out_shape=[jax.ShapeDtypeStruct((m, d), F32), jax.ShapeDtypeStruct((m, d), BF16)],
        scratch_shapes=[pltpu.VMEM((tq, X_WIDTH), BF16)],
        compiler_params=_params(("parallel", "parallel"), nbytes),
        name="xattn_block",
    )(h, kv, w_q, w_o, ln_x.reshape(1, -1), q_gain.reshape(1, -1), k_gain.reshape(1, -1), ln_ffn.reshape(1, -1))


_IN_OFF = np.cumsum([0, SWA_WIDTH, SWA_KV_WIDTH, SWA_KV_WIDTH, GLA_KEY_WIDTH, GLA_KEY_WIDTH, GLA_WIDTH,
                     GLA_WIDTH, GLA_GATE_RANK, SSD_WIDTH, SSD_CONV_CH, SSD_HEADS]).tolist()
SWA_COL0, GLA_COL0, GLOW_COL0, SSD_COL0, DT_COL0 = _IN_OFF[0], _IN_OFF[3], _IN_OFF[7], _IN_OFF[8], _IN_OFF[10]


IN_DIM = _IN_OFF[-1]
SMALL_ROW_STARTS = (GLOW_COL0, IN_DIM - V7X_LANES)
assert SMALL_ROW_STARTS[1] + SSD_DT_LANE0 == DT_COL0


def kernel(x, mem, rel_bias, ln_mix, w_in, swa_q_gain, swa_k_gain, swa_sinks, swa_out_gain, gla_w_gk_up, gla_b_gk_up, gla_norm_gain, ssd_conv_w, ssd_conv_b, ssd_dt_bias, ssd_a_log, ssd_d, ssd_norm_gain, w_mix_out, ln_x, ln_mem, x_w_q, x_w_k, x_w_v, x_w_o, x_q_gain, x_k_gain, ln_ffn, ffn_w_gate, ffn_w_up, ffn_w_down):
    batch, seq, d = x.shape
    mem_len = mem.shape[1]
    m = batch * seq
    band_bias = swa_band_bias(rel_bias)
    h = x.reshape(m, d)
    mem2 = mem.reshape(batch * mem_len, d)
    w_in_t = jnp.swapaxes(w_in, 1, 2)
    for l in range(DEPTH):
        w_up_pad = jnp.pad(gla_w_gk_up[l], ((0, V7X_LANES - GLA_GATE_RANK), (0, 0)))
        w_kv = jnp.concatenate([x_w_k[l], x_w_v[l]], axis=1).astype(BF16)
        hn, p_small = rmsnorm_small(h, ln_mix[l], w_in_t, l, SMALL_ROW_STARTS)
        swa_tile = (GLA_COL0 - SWA_COL0) // 2
        p_swa = proj_wcast(hn, w_in_t, l, lambda j: SWA_COL0 + swa_tile * j, 2, swa_tile,
                           out_dtype=BF16, name="proj_swa")
        p_gla = proj_wcast(hn, w_in_t, l, lambda j: GLA_COL0 + PROJ_TILE * j, (GLOW_COL0 - GLA_COL0) // PROJ_TILE,
                           PROJ_TILE, out_dtype=BF16, name="proj_gla")
        p_ssd = proj_wcast(hn, w_in_t, l, lambda j: SSD_COL0 + PROJ_TILE * j, (DT_COL0 - SSD_COL0) // PROJ_TILE,
                           PROJ_TILE, out_dtype=BF16, name="proj_ssd")
        y_a = swa_mixer(p_swa, band_bias, swa_sinks[l], swa_q_gain[l], swa_k_gain[l], swa_out_gain[l], batch, seq)
        y_b = gla_mixer(p_gla, p_small, w_up_pad, gla_b_gk_up[l], gla_norm_gain[l], batch, seq)
        y_c, w_down16 = ssd_mixer(p_ssd, p_small, ssd_conv_w[l], ssd_conv_b[l], ssd_dt_bias[l], ssd_a_log[l],
                                  ssd_d[l], ssd_norm_gain[l], batch, seq, ffn_w_down, l)
        h = mix_out(y_a, y_b, y_c, w_mix_out, l, h)
        memn = rmsnorm(mem2, ln_mem[l])
        kv = matmul(memn, w_kv, out_dtype=F32, tm=1024, tn=512, name="xattn_kv")
        h, hf = cross_attention_block(h, kv, x_w_q[l].astype(BF16), x_w_o[l].astype(BF16), ln_x[l],
                                      x_q_gain[l], x_k_gain[l], ln_ffn[l], batch, seq, mem_len)
        hidden = ffn_gate_up(hf, ffn_w_gate, ffn_w_up, l)
        h = matmul(hidden, w_down16, out_dtype=F32, res=h, tm=512, tn=512, name="ffn_down")
    return h.reshape(batch, seq, d)
```

```python
import functools
import math

import numpy as np
import jax
import jax.numpy as jnp
from jax import lax
from jax.experimental import pallas as pl
from jax.experimental.pallas import tpu as pltpu

F32 = jnp.float32
BF16 = jnp.bfloat16

D_MODEL = 4096
DEPTH = 2
EPS = 1e-6
SWA_WIDTH = 1024
SWA_HEAD_DIM = 64
SWA_HEADS = 16
SWA_KV_HEADS = 2
SWA_GROUP = SWA_HEADS // SWA_KV_HEADS
SWA_KV_WIDTH = SWA_KV_HEADS * SWA_HEAD_DIM
SWA_WINDOW = 128
SWA_BLOCK = 128
REL_BUCKETS = 32
REL_MAX_DIST = 128
GLA_WIDTH = 1024
GLA_HEADS = 4
GLA_VAL_DIM = 256
GLA_KEY_DIM = 128
GLA_KEY_WIDTH = GLA_HEADS * GLA_KEY_DIM
GLA_GATE_RANK = 16
GLA_GATE_NORMALIZER = 16.0
GLA_CHUNK = 64
SSD_WIDTH = 2048
SSD_HEAD_DIM = 64
SSD_HEADS = 32
SSD_GROUPS = 8
SSD_HEADS_PER_GROUP = SSD_HEADS // SSD_GROUPS
SSD_STATE = 128
SSD_CONV = 4
SSD_CHUNK = 128
SSD_BC_WIDTH = SSD_GROUPS * SSD_STATE
SSD_CONV_CH = SSD_WIDTH + 2 * SSD_BC_WIDTH
SSD_GROUP_WIDTH = SSD_WIDTH // SSD_GROUPS
X_HEADS = 4
X_HEAD_DIM = 128
X_WIDTH = X_HEADS * X_HEAD_DIM
FFN_HIDDEN = 11008

V7X_LANES = 128
V7X_VMEM_BYTES = 64 * 1024 * 1024
VMEM_REQUEST_CAP = (V7X_VMEM_BYTES * 7) // 8
V7X_MXU_WIDTH = 256
FFN_SLAB = 2 * V7X_MXU_WIDTH
PROJ_SLAB = 4 * V7X_MXU_WIDTH
MIX_SLAB = 2 * V7X_MXU_WIDTH
GLA_TIME_BLOCK = 256
SSD_DT_LANE0 = V7X_LANES - SSD_HEADS


def _params(semantics, block_bytes):
    limit = min(VMEM_REQUEST_CAP, 2 * block_bytes + 16 * 1024 * 1024)
    return pltpu.CompilerParams(dimension_semantics=semantics, vmem_limit_bytes=int(limit))


def _nbytes(shape, dtype):
    return int(np.prod(shape)) * jnp.dtype(dtype).itemsize


def _split_bf16(x, terms):
    parts = []
    r = x
    for t in range(terms):
        p = r.astype(BF16)
        parts.append(p)
        if t + 1 < terms:
            r = r - p.astype(F32)
    return parts


def _dot(a, b):
    return jnp.dot(a, b, preferred_element_type=F32)


def _dot_nt(a, b):
    return lax.dot_general(a, b, (((1,), (1,)), ((), ())), preferred_element_type=F32)


def _dot_tn(a, b):
    return lax.dot_general(a, b, (((0,), (0,)), ((), ())), preferred_element_type=F32)


def _dot_exact_lhs(a_bf16, x, terms):
    acc = None
    for p in _split_bf16(x, terms):
        d = _dot(a_bf16, p)
        acc = d if acc is None else acc + d
    return acc


def _dot_exact_rhs(x, b_bf16, terms):
    acc = None
    for p in _split_bf16(x, terms):
        d = _dot(p, b_bf16)
        acc = d if acc is None else acc + d
    return acc


def _silu(x):
    return x / (1.0 + jnp.exp(-x))


def _softplus(x):
    return jnp.maximum(x, 0.0) + jnp.log1p(jnp.exp(-jnp.abs(x)))


def _rms(x, gain):
    return x * lax.rsqrt(jnp.mean(x * x, axis=-1, keepdims=True) + EPS) * gain


def _rmsnorm_body(x_ref, g_ref, o_ref):
    o_ref[...] = _rms(x_ref[...], g_ref[...]).astype(o_ref.dtype)


def rmsnorm(x, gain, tm=256):
    m, d = x.shape
    tm = min(tm, m)
    return pl.pallas_call(
        _rmsnorm_body,
        grid=(m // tm,),
        in_specs=[pl.BlockSpec((tm, d), lambda i: (i, 0)),
                  pl.BlockSpec((1, d), lambda i: (0, 0))],
        out_specs=pl.BlockSpec((tm, d), lambda i: (i, 0)),
        out_shape=jax.ShapeDtypeStruct((m, d), BF16),
        compiler_params=_params(("parallel",), _nbytes((tm, d), F32) + _nbytes((tm, d), BF16)),
        name="rmsnorm",
    )(x, gain.reshape(1, d))


def _rmsnorm_small_body(x_ref, g_ref, wa_ref, wb_ref, o_ref, small_ref, wt_ref):
    @pl.when(pl.program_id(0) == 0)
    def _():
        for t, w_ref in enumerate((wa_ref, wb_ref)):
            for k0 in range(0, w_ref.shape[2], _XPOSE_COLS):
                wt_ref[k0:k0 + _XPOSE_COLS, V7X_LANES * t:V7X_LANES * (t + 1)] = (
                    w_ref[0, :, k0:k0 + _XPOSE_COLS].T.astype(BF16))

    hn = _rms(x_ref[...], g_ref[...]).astype(BF16)
    o_ref[...] = hn
    small_ref[...] = _dot(hn, wt_ref[...])


def rmsnorm_small(x, gain, wt_all, layer, row_starts, tm=256):
    m, d = x.shape
    tm = min(tm, m)
    w_tile = lambda r0: pl.BlockSpec((pl.Element(1), pl.Element(V7X_LANES), pl.Element(d)),
                                     lambda i: (layer, r0, 0))
    nbytes = (_nbytes((tm, d), F32) + _nbytes((tm, d), BF16) + 2 * _nbytes((V7X_LANES, d), F32)
              + _nbytes((d, 2 * V7X_LANES), BF16) + _nbytes((tm, 2 * V7X_LANES), F32))
    return pl.pallas_call(
        _rmsnorm_small_body,
        grid=(m // tm,),
        in_specs=[pl.BlockSpec((tm, d), lambda i: (i, 0)),
                  pl.BlockSpec((1, d), lambda i: (0, 0)),
                  w_tile(row_starts[0]), w_tile(row_starts[1])],
        out_specs=[pl.BlockSpec((tm, d), lambda i: (i, 0)),
                   pl.BlockSpec((tm, 2 * V7X_LANES), lambda i: (i, 0))],
        out_shape=[jax.ShapeDtypeStruct((m, d), BF16), jax.ShapeDtypeStruct((m, 2 * V7X_LANES), F32)],
        scratch_shapes=[pltpu.VMEM((d, 2 * V7X_LANES), BF16)],
        compiler_params=_params(("arbitrary",), nbytes),
        name="rmsnorm_small",
    )(x, gain.reshape(1, d), wt_all, wt_all)


def _mm_body(*refs, nk, has_res):
    a_ref, w_ref = refs[0], refs[1]
    r_ref = refs[2] if has_res else None
    o_ref = refs[2 + has_res]
    part = _dot(a_ref[...], w_ref[...])
    if nk == 1:
        if has_res:
            part = part + r_ref[...]
        o_ref[...] = part.astype(o_ref.dtype)
        return
    k = pl.program_id(2)

    @pl.when(k == 0)
    def _():
        o_ref[...] = (part + r_ref[...]) if has_res else part

    @pl.when(k > 0)
    def _():
        o_ref[...] += part


def matmul(a, w, *, out_dtype, res=None, tm, tn, tk=None, layer=None, name="matmul"):
    m, kdim = a.shape
    n = w.shape[-1]
    tm, tn = min(tm, m), min(tn, n)
    tk = kdim if tk is None else tk
    nk = kdim // tk
    assert m % tm == 0 and n % tn == 0 and kdim % tk == 0
    assert nk == 1 or out_dtype == F32
    has_res = res is not None
    if layer is None:
        w_spec = pl.BlockSpec((tk, tn), lambda i, j, k: (k, j))
    else:
        w_spec = pl.BlockSpec((None, tk, tn), lambda i, j, k: (layer, k, j))
    in_specs = [pl.BlockSpec((tm, tk), lambda i, j, k: (i, k)), w_spec]
    args = [a, w]
    nbytes = _nbytes((tm, tk), BF16) + _nbytes((tk, tn), BF16) + _nbytes((tm, tn), out_dtype) + _nbytes((tm, tn), F32)
    if has_res:
        in_specs.append(pl.BlockSpec((tm, tn), lambda i, j, k: (i, j)))
        args.append(res)
        nbytes += _nbytes((tm, tn), F32)
    return pl.pallas_call(
        functools.partial(_mm_body, nk=nk, has_res=has_res),
        grid=(m // tm, n // tn, nk),
        in_specs=in_specs,
        out_specs=pl.BlockSpec((tm, tn), lambda i, j, k: (i, j)),
        out_shape=jax.ShapeDtypeStruct((m, n), out_dtype),
        compiler_params=_params(("parallel", "parallel", "arbitrary"), nbytes),
        name=name,
    )(*args)


_XPOSE_ROWS, _XPOSE_COLS = 256, 512


def _slab_copy(w_hbm, stage_ref, sem, layer, slab, start0, tn, features_on_rows):
    if features_on_rows:
        src = w_hbm.at[layer, pl.ds(pl.multiple_of(start0 + slab * tn, 8), tn), :]
    else:
        src = w_hbm.at[layer, :, pl.ds(pl.multiple_of(start0 + slab * tn, V7X_LANES), tn)]
    return pltpu.make_async_copy(src, stage_ref, sem.at[0])


def _slab_body(*refs, n_a, has_res, layer, start0, features_on_rows):
    a_refs, w_hbm = refs[:n_a], refs[n_a]
    r_ref = refs[n_a + 1] if has_res else None
    o_ref, stage_ref, wb_ref, sem = refs[n_a + 1 + has_res:]
    kdim, tn = wb_ref.shape
    slab, i = pl.program_id(0), pl.program_id(1)
    copy = lambda s: _slab_copy(w_hbm, stage_ref, sem, layer, s, start0, tn, features_on_rows)

    @pl.when(i == 0)
    def _():
        @pl.when(slab == 0)
        def _():
            copy(slab).start()

        copy(slab).wait()
        if features_on_rows:
            for r0 in range(0, tn, _XPOSE_ROWS):
                r1 = min(r0 + _XPOSE_ROWS, tn)
                for k0 in range(0, kdim, _XPOSE_COLS):
                    wb_ref[k0:k0 + _XPOSE_COLS, r0:r1] = stage_ref[r0:r1, k0:k0 + _XPOSE_COLS].T.astype(BF16)
        else:
            wb_ref[...] = stage_ref[...].astype(BF16)

    @pl.when((i == 1) & (slab + 1 < pl.num_programs(0)))
    def _():
        copy(slab + 1).start()

    acc = r_ref[...] if has_res else None
    k0 = 0
    for a_ref in a_refs:
        k1 = k0 + a_ref.shape[1]
        part = _dot(a_ref[...], wb_ref[k0:k1, :])
        acc = part if acc is None else acc + part
        k0 = k1
    o_ref[...] = acc.astype(o_ref.dtype)


def slab_matmul(a_list, w_all, layer, start0, n_slabs, tn, *, features_on_rows, out_dtype, res=None,
                tm=1024, name="slab_matmul"):
    m = a_list[0].shape[0]
    kdim = sum(a.shape[1] for a in a_list)
    tm = min(tm, m)
    assert m // tm >= 2
    has_res = res is not None
    in_specs = [pl.BlockSpec((tm, a.shape[1]), lambda s, i: (i, 0)) for a in a_list]
    in_specs.append(pl.BlockSpec(memory_space=pl.ANY))
    args = list(a_list) + [w_all]
    nbytes = (2 * _nbytes((tm, kdim), BF16) + _nbytes((kdim, tn), F32) + _nbytes((kdim, tn), BF16)
              + 2 * _nbytes((tm, tn), out_dtype) + 4 * _nbytes((tm, tn), F32))
    if has_res:
        in_specs.append(pl.BlockSpec((tm, tn), lambda s, i: (i, s)))
        args.append(res)
        nbytes += 2 * _nbytes((tm, tn), F32)
    return pl.pallas_call(
        functools.partial(_slab_body, n_a=len(a_list), has_res=has_res, layer=layer, start0=start0,
                          features_on_rows=features_on_rows),
        grid=(n_slabs, m // tm),
        in_specs=in_specs,
        out_specs=pl.BlockSpec((tm, tn), lambda s, i: (i, s)),
        out_shape=jax.ShapeDtypeStruct((m, n_slabs * tn), out_dtype),
        scratch_shapes=[pltpu.VMEM((tn, kdim) if features_on_rows else (kdim, tn), F32),
                        pltpu.VMEM((kdim, tn), BF16),
                        pltpu.SemaphoreType.DMA((1,))],
        compiler_params=pltpu.CompilerParams(dimension_semantics=("arbitrary", "arbitrary"),
                                             vmem_limit_bytes=min(VMEM_REQUEST_CAP, nbytes)),
        name=name,
    )(*args)


def _gateup_copies(wg_hbm, wu_hbm, stage_ref, sem, layer, slab, width):
    c0 = pl.multiple_of(slab * FFN_SLAB, FFN_SLAB)
    return [pltpu.make_async_copy(w_hbm.at[layer, :, pl.ds(c0, width)], stage_ref.at[k, :, pl.ds(0, width)],
                                  sem.at[k])
            for k, w_hbm in enumerate((wg_hbm, wu_hbm))]


def _gateup_body(a_ref, wg_hbm, wu_hbm, o_ref, stage_ref, wgb_ref, wub_ref, sem, *, layer, last_width):
    slab, i = pl.program_id(0), pl.program_id(1)
    last = pl.num_programs(0) - 1

    def for_slab(s, action):
        @pl.when(s < last)
        def _():
            for c in _gateup_copies(wg_hbm, wu_hbm, stage_ref, sem, layer, s, FFN_SLAB):
                action(c)

        @pl.when(s == last)
        def _():
            for c in _gateup_copies(wg_hbm, wu_hbm, stage_ref, sem, layer, s, last_width):
                action(c)

    @pl.when(i == 0)
    def _():
        @pl.when(slab == 0)
        def _():
            for_slab(slab, lambda c: c.start())

        for_slab(slab, lambda c: c.wait())
        wgb_ref[...] = stage_ref[0].astype(BF16)
        wub_ref[...] = stage_ref[1].astype(BF16)

    @pl.when((i == 1) & (slab < last))
    def _():
        for_slab(slab + 1, lambda c: c.start())

    a = a_ref[...]
    g = _dot(a, wgb_ref[...])
    u = _dot(a, wub_ref[...])
    o_ref[...] = (_silu(g) * u).astype(o_ref.dtype)


def ffn_gate_up(a, wg_all, wu_all, layer, tm=1024):
    m, kdim = a.shape
    n = wg_all.shape[2]
    tm = min(tm, m)
    n_slabs = pl.cdiv(n, FFN_SLAB)
    last_width = n - (n_slabs - 1) * FFN_SLAB
    assert m // tm >= 2 and last_width % V7X_LANES == 0
    nbytes = (2 * _nbytes((tm, kdim), BF16) + 2 * _nbytes((kdim, FFN_SLAB), F32)
              + 2 * _nbytes((kdim, FFN_SLAB), BF16) + 2 * _nbytes((tm, FFN_SLAB), BF16)
              + 6 * _nbytes((tm, FFN_SLAB), F32))
    return pl.pallas_call(
        functools.partial(_gateup_body, layer=layer, last_width=last_width),
        grid=(n_slabs, m // tm),
        in_specs=[pl.BlockSpec((tm, kdim), lambda s, i: (i, 0)),
                  pl.BlockSpec(memory_space=pl.ANY),
                  pl.BlockSpec(memory_space=pl.ANY)],
        out_specs=pl.BlockSpec((tm, FFN_SLAB), lambda s, i: (i, s)),
        out_shape=jax.ShapeDtypeStruct((m, n), BF16),
        scratch_shapes=[pltpu.VMEM((2, kdim, FFN_SLAB), F32),
                        pltpu.VMEM((kdim, FFN_SLAB), BF16), pltpu.VMEM((kdim, FFN_SLAB), BF16),
                        pltpu.SemaphoreType.DMA((2,))],
        compiler_params=pltpu.CompilerParams(dimension_semantics=("arbitrary", "arbitrary"),
                                             vmem_limit_bytes=min(VMEM_REQUEST_CAP, nbytes)),
        name="ffn_gate_up",
    )(a, wg_all, wu_all)


def _swa_body(q_ref, kvc_ref, kvp_ref, bias_ref, sink_ref, qg_ref, kg_ref, og_ref, e_ref, et_ref, o_ref, acc_ref):
    hd, blk, grp = SWA_HEAD_DIM, SWA_BLOCK, SWA_GROUP
    kband = jnp.concatenate([kvp_ref[:, :SWA_KV_WIDTH], kvc_ref[:, :SWA_KV_WIDTH]], axis=0).astype(F32)
    vband = jnp.concatenate([kvp_ref[:, SWA_KV_WIDTH:], kvc_ref[:, SWA_KV_WIDTH:]], axis=0)
    ones = jnp.ones((2 * blk, V7X_LANES), BF16)

    q = q_ref[...].astype(F32)
    ssq = _dot_exact_rhs(q * q, e_ref[...], 2)
    inv = lax.rsqrt(ssq * (1.0 / hd) + EPS)
    qn = (q * _dot_exact_rhs(inv, et_ref[...], 3) * qg_ref[...]).astype(BF16)

    for j in range(SWA_KV_HEADS):
        kn = _rms(kband[:, hd * j:hd * (j + 1)], kg_ref[...]).astype(BF16)
        qs = jnp.concatenate([qn[:, hd * (grp * j + g):hd * (grp * j + g + 1)] for g in range(grp)], axis=0)
        s = _dot_nt(qs, kn) * (hd ** -0.5) + bias_ref[j]
        sink = sink_ref[j]
        s_prev, s_cur = s[:, :blk], s[:, blk:]
        row_max = jnp.max(jnp.maximum(s_prev, s_cur), axis=-1, keepdims=True)
        mx = jnp.maximum(jnp.broadcast_to(row_max, sink.shape), sink)
        p = jnp.concatenate([jnp.exp(s_prev - mx), jnp.exp(s_cur - mx)], axis=1).astype(BF16)
        total = _dot(p, ones) + jnp.exp(sink - mx)
        o = _dot(p, vband[:, hd * j:hd * (j + 1)]) / total[:, :hd]
        for g in range(grp):
            h = grp * j + g
            acc_ref[:, hd * h:hd * (h + 1)] = o[blk * g:blk * (g + 1), :]
    o_ref[...] = _rms(acc_ref[...], og_ref[...]).astype(o_ref.dtype)


def swa_mixer(qkv, bias, sinks, q_gain, k_gain, out_gain, batch, seq):
    nb = seq // SWA_BLOCK
    kv_blk = SWA_WIDTH // (2 * SWA_KV_WIDTH)
    rows = SWA_GROUP * SWA_BLOCK
    no_prev = jnp.arange(2 * SWA_BLOCK) < SWA_BLOCK
    bias_g = jnp.stack([jnp.where(no_prev, -jnp.inf, bias), bias]).reshape(2, SWA_KV_HEADS, rows, 2 * SWA_BLOCK)
    sink_col = jnp.broadcast_to(jnp.repeat(sinks.astype(F32), SWA_BLOCK)[:, None],
                                (SWA_HEADS * SWA_BLOCK, V7X_LANES)).reshape(SWA_KV_HEADS, rows, V7X_LANES)
    head_of_col = jnp.arange(SWA_WIDTH) // SWA_HEAD_DIM
    e = (head_of_col[:, None] == jnp.arange(V7X_LANES)[None, :]).astype(BF16)
    const = lambda shape: pl.BlockSpec(shape, lambda b, n: (0,) * len(shape))
    nbytes = (_nbytes((SWA_BLOCK, SWA_WIDTH), BF16) * 2 + 2 * _nbytes((SWA_BLOCK, 2 * SWA_KV_WIDTH), BF16)
              + _nbytes(bias.shape, F32) + _nbytes((SWA_KV_HEADS, rows, V7X_LANES), F32)
              + 2 * _nbytes(e.shape, BF16) + _nbytes((SWA_BLOCK, SWA_WIDTH), F32)
              + 4 * _nbytes((rows, 2 * SWA_BLOCK), F32))
    return pl.pallas_call(
        _swa_body,
        grid=(batch, nb),
        in_specs=[pl.BlockSpec((SWA_BLOCK, SWA_WIDTH), lambda b, n: (b * nb + n, 0)),
                  pl.BlockSpec((SWA_BLOCK, 2 * SWA_KV_WIDTH), lambda b, n: (b * nb + n, kv_blk)),
                  pl.BlockSpec((SWA_BLOCK, 2 * SWA_KV_WIDTH),
                               lambda b, n: (b * nb + jnp.maximum(n - 1, 0), kv_blk)),
                  pl.BlockSpec((None,) + bias_g.shape[1:], lambda b, n: (jnp.minimum(n, 1), 0, 0, 0)),
                  const(sink_col.shape),
                  const((1, SWA_WIDTH)), const((1, SWA_HEAD_DIM)), const((1, SWA_WIDTH)),
                  const(e.shape), const(e.T.shape)],
        out_specs=pl.BlockSpec((SWA_BLOCK, SWA_WIDTH), lambda b, n: (b * nb + n, 0)),
        out_shape=jax.ShapeDtypeStruct((batch * seq, SWA_WIDTH), BF16),
        scratch_shapes=[pltpu.VMEM((SWA_BLOCK, SWA_WIDTH), F32)],
        compiler_params=_params(("parallel", "parallel"), nbytes),
        name="swa_mixer",
    )(qkv, qkv, qkv, bias_g, sink_col, jnp.tile(q_gain, SWA_HEADS).reshape(1, -1), k_gain.reshape(1, -1),
      out_gain.reshape(1, -1), e, e.T)


def _t5_bucket(dist):
    n = jnp.maximum(dist, 0)
    max_exact = REL_BUCKETS // 2
    nf = jnp.maximum(n, 1).astype(F32)
    large = max_exact + (jnp.log(nf / max_exact) / math.log(REL_MAX_DIST / max_exact)
                         * (REL_BUCKETS - max_exact)).astype(jnp.int32)
    large = jnp.minimum(large, REL_BUCKETS - 1)
    return jnp.where(n < max_exact, n, large)


def swa_band_bias(rel_bias):
    i = jnp.arange(SWA_BLOCK, dtype=jnp.int32)[:, None]
    j = jnp.arange(2 * SWA_BLOCK, dtype=jnp.int32)[None, :]
    dist = i + SWA_BLOCK - j
    onehot = _t5_bucket(dist)[None] == jnp.arange(REL_BUCKETS, dtype=jnp.int32)[:, None, None]
    bias = jnp.sum(jnp.where(onehot[:, None], rel_bias.astype(F32)[:, :, None, None], 0.0), axis=0)
    in_window = (dist >= 0) & (dist < SWA_WINDOW)
    return jnp.where(in_window[None], bias, -jnp.inf)


def _gla_body(x_ref, gl_ref, wup_ref, bup_ref, gain_ref, o_ref, state_ref):
    tb = x_ref.shape[0]
    dk, dv, c = GLA_KEY_DIM, GLA_VAL_DIM, GLA_CHUNK

    @pl.when(pl.program_id(1) == 0)
    def _():
        state_ref[...] = jnp.zeros_like(state_ref)

    gl_hi, gl_lo = _split_bf16(gl_ref[...], 2)
    w_hi, w_lo = _split_bf16(wup_ref[...], 2)
    pre = _dot(gl_hi, w_hi) + _dot(gl_hi, w_lo) + _dot(gl_lo, w_hi) + bup_ref[...]
    g = (jnp.minimum(pre, 0.0) - jnp.log1p(jnp.exp(-jnp.abs(pre)))) * (1.0 / GLA_GATE_NORMALIZER)

    row = lax.broadcasted_iota(jnp.int32, (tb, tb), 0)
    colm = lax.broadcasted_iota(jnp.int32, (tb, tb), 1)
    same_chunk_lower = ((row // c) == (colm // c)) & (colm <= row)
    bcum_all = _dot_exact_lhs(same_chunk_lower.astype(BF16), g, 3)

    ri = lax.broadcasted_iota(jnp.int32, (c, c), 0)
    ci = lax.broadcasted_iota(jnp.int32, (c, c), 1)
    causal = ci <= ri
    for ch in range(tb // c):
        r0 = ch * c
        for h in range(GLA_HEADS):
            bcum = bcum_all[r0:r0 + c, dk * h:dk * (h + 1)]
            blast = bcum[c - 1:c, :]
            q = x_ref[r0:r0 + c, dk * h:dk * (h + 1)].astype(F32)
            k = x_ref[r0:r0 + c, GLA_KEY_WIDTH + dk * h:GLA_KEY_WIDTH + dk * (h + 1)].astype(F32)
            v = x_ref[r0:r0 + c, 2 * GLA_KEY_WIDTH + dv * h:2 * GLA_KEY_WIDTH + dv * (h + 1)]
            r = x_ref[r0:r0 + c, 2 * GLA_KEY_WIDTH + GLA_WIDTH + dv * h:
                      2 * GLA_KEY_WIDTH + GLA_WIDTH + dv * (h + 1)].astype(F32)
            qd = (q * (dk ** -0.5) * jnp.exp(bcum)).astype(BF16)
            kd = (k * jnp.exp(-bcum)).astype(BF16)
            kl = (k * jnp.exp(blast - bcum)).astype(BF16)
            att = jnp.where(causal, _dot_nt(qd, kd), 0.0)
            st = state_ref[h]
            o = _dot(att.astype(BF16), v) + _dot_nt(qd, st.astype(BF16))
            state_ref[h] = st * jnp.exp(blast) + _dot_tn(v, kl)
            o = _rms(o, gain_ref[...]) * _silu(r)
            o_ref[r0:r0 + c, dv * h:dv * (h + 1)] = o.astype(o_ref.dtype)


def gla_mixer(x, small, w_up_pad, b_up, norm_gain, batch, seq):
    tb = min(GLA_TIME_BLOCK, seq)
    nt = seq // tb
    width = x.shape[1]
    nbytes = (_nbytes((tb, width), BF16) + _nbytes((tb, V7X_LANES), F32) + _nbytes(w_up_pad.shape, F32)
              + _nbytes((tb, GLA_WIDTH), BF16) + _nbytes((GLA_HEADS, GLA_VAL_DIM, GLA_KEY_DIM), F32)
              + 8 * _nbytes((tb, GLA_KEY_WIDTH), F32))
    return pl.pallas_call(
        _gla_body,
        grid=(batch, nt),
        in_specs=[pl.BlockSpec((tb, width), lambda b, t: (b * nt + t, 0)),
                  pl.BlockSpec((tb, V7X_LANES), lambda b, t: (b * nt + t, 0)),
                  pl.BlockSpec(w_up_pad.shape, lambda b, t: (0, 0)),
                  pl.BlockSpec((1, GLA_KEY_WIDTH), lambda b, t: (0, 0)),
                  pl.BlockSpec((1, GLA_VAL_DIM), lambda b, t: (0, 0))],
        out_specs=pl.BlockSpec((tb, GLA_WIDTH), lambda b, t: (b * nt + t, 0)),
        out_shape=jax.ShapeDtypeStruct((batch * seq, GLA_WIDTH), BF16),
        scratch_shapes=[pltpu.VMEM((GLA_HEADS, GLA_VAL_DIM, GLA_KEY_DIM), F32)],
        compiler_params=_params(("parallel", "arbitrary"), nbytes),
        name="gla_mixer",
    )(x, small, w_up_pad, b_up.reshape(1, -1), norm_gain.reshape(1, -1))


def _ssd_body(x_ref, dt_ref, cw_ref, cb_ref, dtb_ref, alog_ref, dexp_ref, gain_ref, expand_ref, shift_ref,
              wcast_in_ref, o_ref, wcast_out_ref, state_ref, tail_ref, y_ref, *, wcast_blocks):
    L, P, N = SSD_CHUNK, SSD_HEAD_DIM, SSD_STATE
    gw = SSD_GROUP_WIDTH

    @pl.when(pl.program_id(0) * pl.num_programs(1) + pl.program_id(1) < wcast_blocks)
    def _():
        wcast_out_ref[...] = wcast_in_ref[...].astype(BF16)

    @pl.when(pl.program_id(1) == 0)
    def _():
        state_ref[...] = jnp.zeros_like(state_ref)
        tail_ref[...] = jnp.zeros_like(tail_ref)

    xin = x_ref[:, SSD_WIDTH:]
    xin32 = xin.astype(F32)
    shifted = _dot(shift_ref[...], xin)
    conv = cb_ref[...] + cw_ref[SSD_CONV - 1:SSD_CONV, :] * xin32
    head = jnp.zeros((8, SSD_CONV_CH), F32)
    for j in range(1, SSD_CONV):
        wj = cw_ref[SSD_CONV - 1 - j:SSD_CONV - j, :]
        conv = conv + wj * shifted[L * (j - 1):L * j, :]
        head = head + wj * tail_ref[8 - j:16 - j, :]
    conv = jnp.concatenate([conv[0:8, :] + head, conv[8:, :]], axis=0)
    tail_ref[0:8, :] = xin32[L - 8:, :]
    xbc = _silu(conv)
    xs = xbc[:, :SSD_WIDTH]
    bm = xbc[:, SSD_WIDTH:SSD_WIDTH + SSD_BC_WIDTH].astype(BF16)
    cm = xbc[:, SSD_WIDTH + SSD_BC_WIDTH:].astype(BF16)

    dt = _softplus(dt_ref[...] + dtb_ref[...])
    dta = dt * (-jnp.exp(alog_ref[...]))
    ri = lax.broadcasted_iota(jnp.int32, (L, L), 0)
    ci = lax.broadcasted_iota(jnp.int32, (L, L), 1)
    causal = ci <= ri
    a_cum = _dot_exact_lhs(causal.astype(BF16), dta, 3)
    a_cum_t = a_cum.T
    expand = expand_ref[...]
    a_exp = _dot_exact_rhs(a_cum, expand, 3)
    dt_exp = _dot_exact_rhs(dt, expand, 2)
    a_last = a_exp[L - 1:L, :]
    xd = xs * dt_exp
    xdec = (xd * jnp.exp(a_last - a_exp)).astype(BF16)
    xd16 = xd.astype(BF16)
    out_scale = jnp.exp(a_exp)

    for g in range(SSD_GROUPS):
        bg = bm[:, N * g:N * (g + 1)]
        cg = cm[:, N * g:N * (g + 1)]
        cb = _dot_nt(cg, bg)
        for kk in range(SSD_HEADS_PER_GROUP):
            h = g * SSD_HEADS_PER_GROUP + kk
            hl = SSD_DT_LANE0 + h
            diff = a_cum[:, hl:hl + 1] - a_cum_t[hl:hl + 1, :]
            m = cb * jnp.exp(jnp.where(causal, diff, -jnp.inf))
            y_ref[:, P * h:P * (h + 1)] = _dot(m.astype(BF16), xd16[:, P * h:P * (h + 1)])
        sg = state_ref[:, gw * g:gw * (g + 1)]
        y_off = _dot(cg, sg.astype(BF16)) * out_scale[:, gw * g:gw * (g + 1)]
        state_ref[:, gw * g:gw * (g + 1)] = (sg * jnp.exp(a_last[:, gw * g:gw * (g + 1)])
                                             + _dot_tn(bg, xdec[:, gw * g:gw * (g + 1)]))
        yg = y_ref[:, gw * g:gw * (g + 1)] + y_off + xs[:, gw * g:gw * (g + 1)] * dexp_ref[:, gw * g:gw * (g + 1)]
        yg = yg * _silu(x_ref[:, gw * g:gw * (g + 1)].astype(F32))
        o_ref[:, gw * g:gw * (g + 1)] = _rms(yg, gain_ref[:, gw * g:gw * (g + 1)]).astype(o_ref.dtype)


def ssd_mixer(x, small, conv_w, conv_b, dt_bias, a_log, d_skip, norm_gain, batch, seq, wcast_all, layer):
    L = SSD_CHUNK
    nc = seq // L
    width = x.shape[1]
    _, wc_rows, wc_cols = wcast_all.shape
    wc_tile = V7X_MXU_WIDTH
    wc_blocks = wc_rows // wc_tile
    assert wc_rows % wc_tile == 0 and wc_blocks <= batch * nc
    wc_index = lambda b, c: jnp.minimum(b * nc + c, wc_blocks - 1)
    t = jnp.arange(L)
    shift = jnp.concatenate([(t[:, None] - j == t[None, :]) for j in range(1, SSD_CONV)], axis=0).astype(BF16)
    lane_pad = (SSD_DT_LANE0, V7X_LANES - SSD_DT_LANE0 - SSD_HEADS)
    dtb = jnp.pad(dt_bias, lane_pad).reshape(1, V7X_LANES)
    alog = jnp.pad(a_log, lane_pad).reshape(1, V7X_LANES)
    dexp = jnp.repeat(d_skip, SSD_HEAD_DIM).reshape(1, SSD_WIDTH)
    expand = (jnp.arange(V7X_LANES)[:, None] - SSD_DT_LANE0
              == (jnp.arange(SSD_WIDTH)[None, :] // SSD_HEAD_DIM)).astype(BF16)
    nbytes = (_nbytes((L, width), BF16) + _nbytes((L, V7X_LANES), F32) + _nbytes((L, SSD_WIDTH), BF16)
              + _nbytes(expand.shape, BF16) + 2 * _nbytes((SSD_STATE, SSD_WIDTH), F32)
              + 16 * _nbytes((L, SSD_CONV_CH), F32) + _nbytes((wc_tile, wc_cols), F32)
              + _nbytes((wc_tile, wc_cols), BF16))
    return pl.pallas_call(
        functools.partial(_ssd_body, wcast_blocks=wc_blocks),
        grid=(batch, nc),
        in_specs=[pl.BlockSpec((L, width), lambda b, c: (b * nc + c, 0)),
                  pl.BlockSpec((L, V7X_LANES), lambda b, c: (b * nc + c, 1)),
                  pl.BlockSpec((SSD_CONV, SSD_CONV_CH), lambda b, c: (0, 0)),
                  pl.BlockSpec((1, SSD_CONV_CH), lambda b, c: (0, 0)),
                  pl.BlockSpec((1, V7X_LANES), lambda b, c: (0, 0)),
                  pl.BlockSpec((1, V7X_LANES), lambda b, c: (0, 0)),
                  pl.BlockSpec((1, SSD_WIDTH), lambda b, c: (0, 0)),
                  pl.BlockSpec((1, SSD_WIDTH), lambda b, c: (0, 0)),
                  pl.BlockSpec(expand.shape, lambda b, c: (0, 0)),
                  pl.BlockSpec(shift.shape, lambda b, c: (0, 0)),
                  pl.BlockSpec((None, wc_tile, wc_cols), lambda b, c: (layer, wc_index(b, c), 0))],
        out_specs=[pl.BlockSpec((L, SSD_WIDTH), lambda b, c: (b * nc + c, 0)),
                   pl.BlockSpec((wc_tile, wc_cols), lambda b, c: (wc_index(b, c), 0))],
        out_shape=[jax.ShapeDtypeStruct((batch * seq, SSD_WIDTH), BF16),
                   jax.ShapeDtypeStruct((wc_rows, wc_cols), BF16)],
        scratch_shapes=[pltpu.VMEM((SSD_STATE, SSD_WIDTH), F32),
                        pltpu.VMEM((16, SSD_CONV_CH), F32),
                        pltpu.VMEM((L, SSD_WIDTH), F32)],
        compiler_params=_params(("arbitrary", "arbitrary"), nbytes),
        name="ssd_mixer",
    )(x, small, conv_w.reshape(SSD_CONV, SSD_CONV_CH), conv_b.reshape(1, -1), dtb, alog, dexp,
      norm_gain.reshape(1, -1), expand, shift, wcast_all)


def _xattn_body(h_ref, kv_ref, wq_ref, wo_ref, lnx_ref, qg_ref, kg_ref, lnf_ref, o_ref, hf_ref, att_ref):
    hd = X_HEAD_DIM
    mem_len = kv_ref.shape[0]
    half = mem_len // 2
    ones = jnp.ones((mem_len, V7X_LANES), BF16)
    h = h_ref[...]
    q = _dot(_rms(h, lnx_ref[...]).astype(BF16), wq_ref[...])
    for i in range(X_HEADS):
        qh = _rms(q[:, hd * i:hd * (i + 1)], qg_ref[...]).astype(BF16)
        kh = _rms(kv_ref[:, hd * i:hd * (i + 1)], kg_ref[...]).astype(BF16)
        vh = kv_ref[:, X_WIDTH + hd * i:X_WIDTH + hd * (i + 1)].astype(BF16)
        s = _dot_nt(qh, kh) * (hd ** -0.5)
        s_lo, s_hi = s[:, :half], s[:, half:]
        mx = jnp.broadcast_to(jnp.max(jnp.maximum(s_lo, s_hi), axis=-1, keepdims=True), s_lo.shape)
        p = jnp.concatenate([jnp.exp(s_lo - mx), jnp.exp(s_hi - mx)], axis=1).astype(BF16)
        att_ref[:, hd * i:hd * (i + 1)] = (_dot(p, vh) / _dot(p, ones)).astype(BF16)
    h_new = h + _dot(att_ref[...], wo_ref[...])
    o_ref[...] = h_new
    hf_ref[...] = _rms(h_new, lnf_ref[...]).astype(hf_ref.dtype)


def cross_attention_block(h, kv, w_q, w_o, ln_x, q_gain, k_gain, ln_ffn, batch, seq, mem_len, tq=256):
    m, d = h.shape
    tq = min(tq, seq)
    nq = seq // tq
    assert mem_len == 2 * V7X_LANES and X_HEAD_DIM == V7X_LANES
    const = lambda shape: pl.BlockSpec(shape, lambda b, t: (0,) * len(shape))
    row_blk = lambda width: pl.BlockSpec((tq, width), lambda b, t: (b * nq + t, 0))
    nbytes = (2 * _nbytes((tq, d), F32) + _nbytes((tq, d), BF16) + _nbytes((mem_len, 2 * X_WIDTH), F32)
              + 2 * _nbytes((d, X_WIDTH), BF16) + 2 * _nbytes((tq, d), F32))
    return pl.pallas_call(
        _xattn_body,
        grid=(batch, nq),
        in_specs=[row_blk(d),
                  pl.BlockSpec((mem_len, 2 * X_WIDTH), lambda b, t: (b, 0)),
                  const((d, X_WIDTH)), const((X_WIDTH, d)), const((1, d)),
                  const((1, X_HEAD_DIM)), const((1, X_HEAD_DIM)), const((1, d))],
        out_specs=[row_blk(d), row_blk(d)],
        out_shape=[jax.ShapeDtypeStruct((m, d), F32), jax.ShapeDtypeStruct((m, d), BF16)],
        scratch_shapes=[pltpu.VMEM((tq, X_WIDTH), BF16)],
        compiler_params=_params(("parallel", "parallel"), nbytes),
        name="xattn_block",
    )(h, kv, w_q, w_o, ln_x.reshape(1, -1), q_gain.reshape(1, -1), k_gain.reshape(1, -1), ln_ffn.reshape(1, -1))


_IN_OFF = np.cumsum([0, SWA_WIDTH, SWA_KV_WIDTH, SWA_KV_WIDTH, GLA_KEY_WIDTH, GLA_KEY_WIDTH, GLA_WIDTH,
                     GLA_WIDTH, GLA_GATE_RANK, SSD_WIDTH, SSD_CONV_CH, SSD_HEADS]).tolist()
SWA_COL0, GLA_COL0, GLOW_COL0, SSD_COL0, DT_COL0 = _IN_OFF[0], _IN_OFF[3], _IN_OFF[7], _IN_OFF[8], _IN_OFF[10]


IN_DIM = _IN_OFF[-1]
SMALL_ROW_STARTS = (GLOW_COL0, IN_DIM - V7X_LANES)
assert SMALL_ROW_STARTS[1] + SSD_DT_LANE0 == DT_COL0


def kernel(x, mem, rel_bias, ln_mix, w_in, swa_q_gain, swa_k_gain, swa_sinks, swa_out_gain, gla_w_gk_up, gla_b_gk_up, gla_norm_gain, ssd_conv_w, ssd_conv_b, ssd_dt_bias, ssd_a_log, ssd_d, ssd_norm_gain, w_mix_out, ln_x, ln_mem, x_w_q, x_w_k, x_w_v, x_w_o, x_q_gain, x_k_gain, ln_ffn, ffn_w_gate, ffn_w_up, ffn_w_down):
    batch, seq, d = x.shape
    mem_len = mem.shape[1]
    m = batch * seq
    band_bias = swa_band_bias(rel_bias)
    h = x.reshape(m, d)
    mem2 = mem.reshape(batch * mem_len, d)
    w_in_t = jnp.swapaxes(w_in, 1, 2)
    for l in range(DEPTH):
        w_up_pad = jnp.pad(gla_w_gk_up[l], ((0, V7X_LANES - GLA_GATE_RANK), (0, 0)))
        w_kv = jnp.concatenate([x_w_k[l], x_w_v[l]], axis=1).astype(BF16)
        hn, p_small = rmsnorm_small(h, ln_mix[l], w_in_t, l, SMALL_ROW_STARTS)
        proj = functools.partial(slab_matmul, [hn], w_in_t, l, features_on_rows=True, out_dtype=BF16)
        p_swa = proj(SWA_COL0, 2, (GLA_COL0 - SWA_COL0) // 2, name="proj_swa")
        p_gla = proj(GLA_COL0, (GLOW_COL0 - GLA_COL0) // PROJ_SLAB, PROJ_SLAB, name="proj_gla")
        p_ssd = proj(SSD_COL0, (DT_COL0 - SSD_COL0) // PROJ_SLAB, PROJ_SLAB, name="proj_ssd")
        y_a = swa_mixer(p_swa, band_bias, swa_sinks[l], swa_q_gain[l], swa_k_gain[l], swa_out_gain[l], batch, seq)
        y_b = gla_mixer(p_gla, p_small, w_up_pad, gla_b_gk_up[l], gla_norm_gain[l], batch, seq)
        y_c, w_down16 = ssd_mixer(p_ssd, p_small, ssd_conv_w[l], ssd_conv_b[l], ssd_dt_bias[l], ssd_a_log[l],
                                  ssd_d[l], ssd_norm_gain[l], batch, seq, ffn_w_down, l)
        h = slab_matmul([y_a, y_b, y_c], w_mix_out, l, 0, d // MIX_SLAB, MIX_SLAB, features_on_rows=False,
                        out_dtype=F32, res=h, name="mix_out")
        memn = rmsnorm(mem2, ln_mem[l])
        kv = matmul(memn, w_kv, out_dtype=F32, tm=1024, tn=512, name="xattn_kv")
        h, hf = cross_attention_block(h, kv, x_w_q[l].astype(BF16), x_w_o[l].astype(BF16), ln_x[l],
                                      x_q_gain[l], x_k_gain[l], ln_ffn[l], batch, seq, mem_len)
        hidden = ffn_gate_up(hf, ffn_w_gate, ffn_w_up, l)
        h = matmul(hidden, w_down16, out_dtype=F32, res=h, tm=512, tn=512, name="ffn_down")
    return h.reshape(batch, seq, d)
```

```python
import functools
import math

import numpy as np
import jax
import jax.numpy as jnp
from jax import lax
from jax.experimental import pallas as pl
from jax.experimental.pallas import tpu as pltpu

F32 = jnp.float32
BF16 = jnp.bfloat16

D_MODEL = 4096
DEPTH = 2
EPS = 1e-6
SWA_WIDTH = 1024
SWA_HEAD_DIM = 64
SWA_HEADS = 16
SWA_KV_HEADS = 2
SWA_GROUP = SWA_HEADS // SWA_KV_HEADS
SWA_KV_WIDTH = SWA_KV_HEADS * SWA_HEAD_DIM
SWA_WINDOW = 128
SWA_BLOCK = 128
REL_BUCKETS = 32
REL_MAX_DIST = 128
GLA_WIDTH = 1024
GLA_HEADS = 4
GLA_VAL_DIM = 256
GLA_KEY_DIM = 128
GLA_KEY_WIDTH = GLA_HEADS * GLA_KEY_DIM
GLA_GATE_RANK = 16
GLA_GATE_NORMALIZER = 16.0
GLA_CHUNK = 64
SSD_WIDTH = 2048
SSD_HEAD_DIM = 64
SSD_HEADS = 32
SSD_GROUPS = 8
SSD_HEADS_PER_GROUP = SSD_HEADS // SSD_GROUPS
SSD_STATE = 128
SSD_CONV = 4
SSD_CHUNK = 128
SSD_BC_WIDTH = SSD_GROUPS * SSD_STATE
SSD_CONV_CH = SSD_WIDTH + 2 * SSD_BC_WIDTH
SSD_GROUP_WIDTH = SSD_WIDTH // SSD_GROUPS
X_HEADS = 4
X_HEAD_DIM = 128
X_WIDTH = X_HEADS * X_HEAD_DIM
FFN_HIDDEN = 11008

V7X_LANES = 128
V7X_VMEM_BYTES = 64 * 1024 * 1024
VMEM_REQUEST_CAP = (V7X_VMEM_BYTES * 7) // 8
V7X_MXU_WIDTH = 256
FFN_SLAB = 2 * V7X_MXU_WIDTH
PROJ_SLAB = 4 * V7X_MXU_WIDTH
MIX_SLAB = 4 * V7X_MXU_WIDTH
MIX_TOKENS = 512
GLA_TIME_BLOCK = 256
SSD_DT_LANE0 = V7X_LANES - SSD_HEADS


def _params(semantics, block_bytes):
    limit = min(VMEM_REQUEST_CAP, 2 * block_bytes + 16 * 1024 * 1024)
    return pltpu.CompilerParams(dimension_semantics=semantics, vmem_limit_bytes=int(limit))


def _nbytes(shape, dtype):
    return int(np.prod(shape)) * jnp.dtype(dtype).itemsize


def _split_bf16(x, terms):
    parts = []
    r = x
    for t in range(terms):
        p = r.astype(BF16)
        parts.append(p)
        if t + 1 < terms:
            r = r - p.astype(F32)
    return parts


def _dot(a, b):
    return jnp.dot(a, b, preferred_element_type=F32)


def _dot_nt(a, b):
    return lax.dot_general(a, b, (((1,), (1,)), ((), ())), preferred_element_type=F32)


def _dot_tn(a, b):
    return lax.dot_general(a, b, (((0,), (0,)), ((), ())), preferred_element_type=F32)


def _dot_exact_lhs(a_bf16, x, terms):
    acc = None
    for p in _split_bf16(x, terms):
        d = _dot(a_bf16, p)
        acc = d if acc is None else acc + d
    return acc


def _dot_exact_rhs(x, b_bf16, terms):
    acc = None
    for p in _split_bf16(x, terms):
        d = _dot(p, b_bf16)
        acc = d if acc is None else acc + d
    return acc


def _silu(x):
    return x / (1.0 + jnp.exp(-x))


def _softplus(x):
    return jnp.maximum(x, 0.0) + jnp.log1p(jnp.exp(-jnp.abs(x)))


def _rms(x, gain):
    return x * lax.rsqrt(jnp.mean(x * x, axis=-1, keepdims=True) + EPS) * gain


def _rmsnorm_body(x_ref, g_ref, o_ref):
    o_ref[...] = _rms(x_ref[...], g_ref[...]).astype(o_ref.dtype)


def rmsnorm(x, gain, tm=256):
    m, d = x.shape
    tm = min(tm, m)
    return pl.pallas_call(
        _rmsnorm_body,
        grid=(m // tm,),
        in_specs=[pl.BlockSpec((tm, d), lambda i: (i, 0)),
                  pl.BlockSpec((1, d), lambda i: (0, 0))],
        out_specs=pl.BlockSpec((tm, d), lambda i: (i, 0)),
        out_shape=jax.ShapeDtypeStruct((m, d), BF16),
        compiler_params=_params(("parallel",), _nbytes((tm, d), F32) + _nbytes((tm, d), BF16)),
        name="rmsnorm",
    )(x, gain.reshape(1, d))


def _rmsnorm_small_body(x_ref, g_ref, wa_ref, wb_ref, o_ref, small_ref, wt_ref):
    @pl.when(pl.program_id(0) == 0)
    def _():
        for t, w_ref in enumerate((wa_ref, wb_ref)):
            for k0 in range(0, w_ref.shape[2], _XPOSE_COLS):
                wt_ref[k0:k0 + _XPOSE_COLS, V7X_LANES * t:V7X_LANES * (t + 1)] = (
                    w_ref[0, :, k0:k0 + _XPOSE_COLS].T.astype(BF16))

    hn = _rms(x_ref[...], g_ref[...]).astype(BF16)
    o_ref[...] = hn
    small_ref[...] = _dot(hn, wt_ref[...])


def rmsnorm_small(x, gain, wt_all, layer, row_starts, tm=512):
    m, d = x.shape
    tm = min(tm, m)
    w_tile = lambda r0: pl.BlockSpec((pl.Element(1), pl.Element(V7X_LANES), pl.Element(d)),
                                     lambda i: (layer, r0, 0))
    nbytes = (_nbytes((tm, d), F32) + _nbytes((tm, d), BF16) + 2 * _nbytes((V7X_LANES, d), F32)
              + _nbytes((d, 2 * V7X_LANES), BF16) + _nbytes((tm, 2 * V7X_LANES), F32))
    return pl.pallas_call(
        _rmsnorm_small_body,
        grid=(m // tm,),
        in_specs=[pl.BlockSpec((tm, d), lambda i: (i, 0)),
                  pl.BlockSpec((1, d), lambda i: (0, 0)),
                  w_tile(row_starts[0]), w_tile(row_starts[1])],
        out_specs=[pl.BlockSpec((tm, d), lambda i: (i, 0)),
                   pl.BlockSpec((tm, 2 * V7X_LANES), lambda i: (i, 0))],
        out_shape=[jax.ShapeDtypeStruct((m, d), BF16), jax.ShapeDtypeStruct((m, 2 * V7X_LANES), F32)],
        scratch_shapes=[pltpu.VMEM((d, 2 * V7X_LANES), BF16)],
        compiler_params=_params(("arbitrary",), nbytes),
        name="rmsnorm_small",
    )(x, gain.reshape(1, d), wt_all, wt_all)


def _mm_body(*refs, nk, has_res):
    a_ref, w_ref = refs[0], refs[1]
    r_ref = refs[2] if has_res else None
    o_ref = refs[2 + has_res]
    part = _dot(a_ref[...], w_ref[...])
    if nk == 1:
        if has_res:
            part = part + r_ref[...]
        o_ref[...] = part.astype(o_ref.dtype)
        return
    k = pl.program_id(2)

    @pl.when(k == 0)
    def _():
        o_ref[...] = (part + r_ref[...]) if has_res else part

    @pl.when(k > 0)
    def _():
        o_ref[...] += part


def matmul(a, w, *, out_dtype, res=None, tm, tn, tk=None, layer=None, name="matmul"):
    m, kdim = a.shape
    n = w.shape[-1]
    tm, tn = min(tm, m), min(tn, n)
    tk = kdim if tk is None else tk
    nk = kdim // tk
    assert m % tm == 0 and n % tn == 0 and kdim % tk == 0
    assert nk == 1 or out_dtype == F32
    has_res = res is not None
    if layer is None:
        w_spec = pl.BlockSpec((tk, tn), lambda i, j, k: (k, j))
    else:
        w_spec = pl.BlockSpec((None, tk, tn), lambda i, j, k: (layer, k, j))
    in_specs = [pl.BlockSpec((tm, tk), lambda i, j, k: (i, k)), w_spec]
    args = [a, w]
    nbytes = _nbytes((tm, tk), BF16) + _nbytes((tk, tn), BF16) + _nbytes((tm, tn), out_dtype) + _nbytes((tm, tn), F32)
    if has_res:
        in_specs.append(pl.BlockSpec((tm, tn), lambda i, j, k: (i, j)))
        args.append(res)
        nbytes += _nbytes((tm, tn), F32)
    return pl.pallas_call(
        functools.partial(_mm_body, nk=nk, has_res=has_res),
        grid=(m // tm, n // tn, nk),
        in_specs=in_specs,
        out_specs=pl.BlockSpec((tm, tn), lambda i, j, k: (i, j)),
        out_shape=jax.ShapeDtypeStruct((m, n), out_dtype),
        compiler_params=_params(("parallel", "parallel", "arbitrary"), nbytes),
        name=name,
    )(*args)


_XPOSE_ROWS, _XPOSE_COLS = 256, 512


def _slab_copy(w_hbm, stage_ref, sem, layer, slab, start0, tn, features_on_rows):
    if features_on_rows:
        src = w_hbm.at[layer, pl.ds(pl.multiple_of(start0 + slab * tn, 8), tn), :]
    else:
        src = w_hbm.at[layer, :, pl.ds(pl.multiple_of(start0 + slab * tn, V7X_LANES), tn)]
    return pltpu.make_async_copy(src, stage_ref, sem.at[0])


def _slab_body(*refs, n_a, has_res, layer, start0, features_on_rows):
    a_refs, w_hbm = refs[:n_a], refs[n_a]
    r_ref = refs[n_a + 1] if has_res else None
    o_ref, stage_ref, wb_ref, sem = refs[n_a + 1 + has_res:]
    kdim, tn = wb_ref.shape
    slab, i = pl.program_id(0), pl.program_id(1)
    copy = lambda s: _slab_copy(w_hbm, stage_ref, sem, layer, s, start0, tn, features_on_rows)

    @pl.when(i == 0)
    def _():
        @pl.when(slab == 0)
        def _():
            copy(slab).start()

        copy(slab).wait()
        if features_on_rows:
            for r0 in range(0, tn, _XPOSE_ROWS):
                r1 = min(r0 + _XPOSE_ROWS, tn)
                for k0 in range(0, kdim, _XPOSE_COLS):
                    wb_ref[k0:k0 + _XPOSE_COLS, r0:r1] = stage_ref[r0:r1, k0:k0 + _XPOSE_COLS].T.astype(BF16)
        else:
            wb_ref[...] = stage_ref[...].astype(BF16)

    @pl.when((i == 1) & (slab + 1 < pl.num_programs(0)))
    def _():
        copy(slab + 1).start()

    acc = r_ref[...] if has_res else None
    k0 = 0
    for a_ref in a_refs:
        k1 = k0 + a_ref.shape[1]
        part = _dot(a_ref[...], wb_ref[k0:k1, :])
        acc = part if acc is None else acc + part
        k0 = k1
    o_ref[...] = acc.astype(o_ref.dtype)


def slab_matmul(a_list, w_all, layer, start0, n_slabs, tn, *, features_on_rows, out_dtype, res=None,
                tm=1024, name="slab_matmul"):
    m = a_list[0].shape[0]
    kdim = sum(a.shape[1] for a in a_list)
    tm = min(tm, m)
    assert m // tm >= 2
    has_res = res is not None
    in_specs = [pl.BlockSpec((tm, a.shape[1]), lambda s, i: (i, 0)) for a in a_list]
    in_specs.append(pl.BlockSpec(memory_space=pl.ANY))
    args = list(a_list) + [w_all]
    nbytes = (2 * _nbytes((tm, kdim), BF16) + _nbytes((kdim, tn), F32) + _nbytes((kdim, tn), BF16)
              + 2 * _nbytes((tm, tn), out_dtype) + 4 * _nbytes((tm, tn), F32))
    if has_res:
        in_specs.append(pl.BlockSpec((tm, tn), lambda s, i: (i, s)))
        args.append(res)
        nbytes += 2 * _nbytes((tm, tn), F32)
    return pl.pallas_call(
        functools.partial(_slab_body, n_a=len(a_list), has_res=has_res, layer=layer, start0=start0,
                          features_on_rows=features_on_rows),
        grid=(n_slabs, m // tm),
        in_specs=in_specs,
        out_specs=pl.BlockSpec((tm, tn), lambda s, i: (i, s)),
        out_shape=jax.ShapeDtypeStruct((m, n_slabs * tn), out_dtype),
        scratch_shapes=[pltpu.VMEM((tn, kdim) if features_on_rows else (kdim, tn), F32),
                        pltpu.VMEM((kdim, tn), BF16),
                        pltpu.SemaphoreType.DMA((1,))],
        compiler_params=pltpu.CompilerParams(dimension_semantics=("arbitrary", "arbitrary"),
                                             vmem_limit_bytes=min(VMEM_REQUEST_CAP, nbytes)),
        name=name,
    )(*args)


def _gateup_copies(wg_hbm, wu_hbm, stage_ref, sem, layer, slab, width):
    c0 = pl.multiple_of(slab * FFN_SLAB, FFN_SLAB)
    return [pltpu.make_async_copy(w_hbm.at[layer, :, pl.ds(c0, width)], stage_ref.at[k, :, pl.ds(0, width)],
                                  sem.at[k])
            for k, w_hbm in enumerate((wg_hbm, wu_hbm))]


def _gateup_body(a_ref, wg_hbm, wu_hbm, o_ref, stage_ref, wgb_ref, wub_ref, sem, *, layer, last_width):
    slab, i = pl.program_id(0), pl.program_id(1)
    last = pl.num_programs(0) - 1

    def for_slab(s, action):
        @pl.when(s < last)
        def _():
            for c in _gateup_copies(wg_hbm, wu_hbm, stage_ref, sem, layer, s, FFN_SLAB):
                action(c)

        @pl.when(s == last)
        def _():
            for c in _gateup_copies(wg_hbm, wu_hbm, stage_ref, sem, layer, s, last_width):
                action(c)

    @pl.when(i == 0)
    def _():
        @pl.when(slab == 0)
        def _():
            for_slab(slab, lambda c: c.start())

        for_slab(slab, lambda c: c.wait())
        wgb_ref[...] = stage_ref[0].astype(BF16)
        wub_ref[...] = stage_ref[1].astype(BF16)

    @pl.when((i == 1) & (slab < last))
    def _():
        for_slab(slab + 1, lambda c: c.start())

    a = a_ref[...]
    g = _dot(a, wgb_ref[...])
    u = _dot(a, wub_ref[...])
    o_ref[...] = (_silu(g) * u).astype(o_ref.dtype)


def ffn_gate_up(a, wg_all, wu_all, layer, tm=1024):
    m, kdim = a.shape
    n = wg_all.shape[2]
    tm = min(tm, m)
    n_slabs = pl.cdiv(n, FFN_SLAB)
    last_width = n - (n_slabs - 1) * FFN_SLAB
    assert m // tm >= 2 and last_width % V7X_LANES == 0
    nbytes = (2 * _nbytes((tm, kdim), BF16) + 2 * _nbytes((kdim, FFN_SLAB), F32)
              + 2 * _nbytes((kdim, FFN_SLAB), BF16) + 2 * _nbytes((tm, FFN_SLAB), BF16)
              + 6 * _nbytes((tm, FFN_SLAB), F32))
    return pl.pallas_call(
        functools.partial(_gateup_body, layer=layer, last_width=last_width),
        grid=(n_slabs, m // tm),
        in_specs=[pl.BlockSpec((tm, kdim), lambda s, i: (i, 0)),
                  pl.BlockSpec(memory_space=pl.ANY),
                  pl.BlockSpec(memory_space=pl.ANY)],
        out_specs=pl.BlockSpec((tm, FFN_SLAB), lambda s, i: (i, s)),
        out_shape=jax.ShapeDtypeStruct((m, n), BF16),
        scratch_shapes=[pltpu.VMEM((2, kdim, FFN_SLAB), F32),
                        pltpu.VMEM((kdim, FFN_SLAB), BF16), pltpu.VMEM((kdim, FFN_SLAB), BF16),
                        pltpu.SemaphoreType.DMA((2,))],
        compiler_params=pltpu.CompilerParams(dimension_semantics=("arbitrary", "arbitrary"),
                                             vmem_limit_bytes=min(VMEM_REQUEST_CAP, nbytes)),
        name="ffn_gate_up",
    )(a, wg_all, wu_all)


def _swa_body(q_ref, kvc_ref, kvp_ref, bias_ref, sink_ref, qg_ref, kg_ref, og_ref, e_ref, et_ref, o_ref, acc_ref):
    hd, blk, grp = SWA_HEAD_DIM, SWA_BLOCK, SWA_GROUP
    kband = jnp.concatenate([kvp_ref[:, :SWA_KV_WIDTH], kvc_ref[:, :SWA_KV_WIDTH]], axis=0).astype(F32)
    vband = jnp.concatenate([kvp_ref[:, SWA_KV_WIDTH:], kvc_ref[:, SWA_KV_WIDTH:]], axis=0)
    ones = jnp.ones((2 * blk, V7X_LANES), BF16)

    q = q_ref[...].astype(F32)
    ssq = _dot_exact_rhs(q * q, e_ref[...], 2)
    inv = lax.rsqrt(ssq * (1.0 / hd) + EPS)
    qn = (q * _dot_exact_rhs(inv, et_ref[...], 3) * qg_ref[...]).astype(BF16)

    for j in range(SWA_KV_HEADS):
        kn = _rms(kband[:, hd * j:hd * (j + 1)], kg_ref[...]).astype(BF16)
        qs = jnp.concatenate([qn[:, hd * (grp * j + g):hd * (grp * j + g + 1)] for g in range(grp)], axis=0)
        s = _dot_nt(qs, kn) * (hd ** -0.5) + bias_ref[j]
        sink = sink_ref[j]
        s_prev, s_cur = s[:, :blk], s[:, blk:]
        row_max = jnp.max(jnp.maximum(s_prev, s_cur), axis=-1, keepdims=True)
        mx = jnp.maximum(jnp.broadcast_to(row_max, sink.shape), sink)
        p = jnp.concatenate([jnp.exp(s_prev - mx), jnp.exp(s_cur - mx)], axis=1).astype(BF16)
        total = _dot(p, ones) + jnp.exp(sink - mx)
        o = _dot(p, vband[:, hd * j:hd * (j + 1)]) / total[:, :hd]
        for g in range(grp):
            h = grp * j + g
            acc_ref[:, hd * h:hd * (h + 1)] = o[blk * g:blk * (g + 1), :]
    o_ref[...] = _rms(acc_ref[...], og_ref[...]).astype(o_ref.dtype)


def swa_mixer(qkv, bias, sinks, q_gain, k_gain, out_gain, batch, seq):
    nb = seq // SWA_BLOCK
    kv_blk = SWA_WIDTH // (2 * SWA_KV_WIDTH)
    rows = SWA_GROUP * SWA_BLOCK
    no_prev = jnp.arange(2 * SWA_BLOCK) < SWA_BLOCK
    bias_g = jnp.stack([jnp.where(no_prev, -jnp.inf, bias), bias]).reshape(2, SWA_KV_HEADS, rows, 2 * SWA_BLOCK)
    sink_col = jnp.broadcast_to(jnp.repeat(sinks.astype(F32), SWA_BLOCK)[:, None],
                                (SWA_HEADS * SWA_BLOCK, V7X_LANES)).reshape(SWA_KV_HEADS, rows, V7X_LANES)
    head_of_col = jnp.arange(SWA_WIDTH) // SWA_HEAD_DIM
    e = (head_of_col[:, None] == jnp.arange(V7X_LANES)[None, :]).astype(BF16)
    const = lambda shape: pl.BlockSpec(shape, lambda b, n: (0,) * len(shape))
    nbytes = (_nbytes((SWA_BLOCK, SWA_WIDTH), BF16) * 2 + 2 * _nbytes((SWA_BLOCK, 2 * SWA_KV_WIDTH), BF16)
              + _nbytes(bias.shape, F32) + _nbytes((SWA_KV_HEADS, rows, V7X_LANES), F32)
              + 2 * _nbytes(e.shape, BF16) + _nbytes((SWA_BLOCK, SWA_WIDTH), F32)
              + 4 * _nbytes((rows, 2 * SWA_BLOCK), F32))
    return pl.pallas_call(
        _swa_body,
        grid=(batch, nb),
        in_specs=[pl.BlockSpec((SWA_BLOCK, SWA_WIDTH), lambda b, n: (b * nb + n, 0)),
                  pl.BlockSpec((SWA_BLOCK, 2 * SWA_KV_WIDTH), lambda b, n: (b * nb + n, kv_blk)),
                  pl.BlockSpec((SWA_BLOCK, 2 * SWA_KV_WIDTH),
                               lambda b, n: (b * nb + jnp.maximum(n - 1, 0), kv_blk)),
                  pl.BlockSpec((None,) + bias_g.shape[1:], lambda b, n: (jnp.minimum(n, 1), 0, 0, 0)),
                  const(sink_col.shape),
                  const((1, SWA_WIDTH)), const((1, SWA_HEAD_DIM)), const((1, SWA_WIDTH)),
                  const(e.shape), const(e.T.shape)],
        out_specs=pl.BlockSpec((SWA_BLOCK, SWA_WIDTH), lambda b, n: (b * nb + n, 0)),
        out_shape=jax.ShapeDtypeStruct((batch * seq, SWA_WIDTH), BF16),
        scratch_shapes=[pltpu.VMEM((SWA_BLOCK, SWA_WIDTH), F32)],
        compiler_params=_params(("parallel", "parallel"), nbytes),
        name="swa_mixer",
    )(qkv, qkv, qkv, bias_g, sink_col, jnp.tile(q_gain, SWA_HEADS).reshape(1, -1), k_gain.reshape(1, -1),
      out_gain.reshape(1, -1), e, e.T)


def _t5_bucket(dist):
    n = jnp.maximum(dist, 0)
    max_exact = REL_BUCKETS // 2
    nf = jnp.maximum(n, 1).astype(F32)
    large = max_exact + (jnp.log(nf / max_exact) / math.log(REL_MAX_DIST / max_exact)
                         * (REL_BUCKETS - max_exact)).astype(jnp.int32)
    large = jnp.minimum(large, REL_BUCKETS - 1)
    return jnp.where(n < max_exact, n, large)


def swa_band_bias(rel_bias):
    i = jnp.arange(SWA_BLOCK, dtype=jnp.int32)[:, None]
    j = jnp.arange(2 * SWA_BLOCK, dtype=jnp.int32)[None, :]
    dist = i + SWA_BLOCK - j
    onehot = _t5_bucket(dist)[None] == jnp.arange(REL_BUCKETS, dtype=jnp.int32)[:, None, None]
    bias = jnp.sum(jnp.where(onehot[:, None], rel_bias.astype(F32)[:, :, None, None], 0.0), axis=0)
    in_window = (dist >= 0) & (dist < SWA_WINDOW)
    return jnp.where(in_window[None], bias, -jnp.inf)


def _gla_body(x_ref, gl_ref, wup_ref, bup_ref, gain_ref, o_ref, state_ref):
    tb = x_ref.shape[0]
    dk, dv, c = GLA_KEY_DIM, GLA_VAL_DIM, GLA_CHUNK

    @pl.when(pl.program_id(1) == 0)
    def _():
        state_ref[...] = jnp.zeros_like(state_ref)

    gl_hi, gl_lo = _split_bf16(gl_ref[...], 2)
    w_hi, w_lo = _split_bf16(wup_ref[...], 2)
    pre = _dot(gl_hi, w_hi) + _dot(gl_hi, w_lo) + _dot(gl_lo, w_hi) + bup_ref[...]
    g = (jnp.minimum(pre, 0.0) - jnp.log1p(jnp.exp(-jnp.abs(pre)))) * (1.0 / GLA_GATE_NORMALIZER)

    row = lax.broadcasted_iota(jnp.int32, (tb, tb), 0)
    colm = lax.broadcasted_iota(jnp.int32, (tb, tb), 1)
    same_chunk_lower = ((row // c) == (colm // c)) & (colm <= row)
    bcum_all = _dot_exact_lhs(same_chunk_lower.astype(BF16), g, 3)

    ri = lax.broadcasted_iota(jnp.int32, (c, c), 0)
    ci = lax.broadcasted_iota(jnp.int32, (c, c), 1)
    causal = ci <= ri
    for ch in range(tb // c):
        r0 = ch * c
        for h in range(GLA_HEADS):
            bcum = bcum_all[r0:r0 + c, dk * h:dk * (h + 1)]
            blast = bcum[c - 1:c, :]
            q = x_ref[r0:r0 + c, dk * h:dk * (h + 1)].astype(F32)
            k = x_ref[r0:r0 + c, GLA_KEY_WIDTH + dk * h:GLA_KEY_WIDTH + dk * (h + 1)].astype(F32)
            v = x_ref[r0:r0 + c, 2 * GLA_KEY_WIDTH + dv * h:2 * GLA_KEY_WIDTH + dv * (h + 1)]
            r = x_ref[r0:r0 + c, 2 * GLA_KEY_WIDTH + GLA_WIDTH + dv * h:
                      2 * GLA_KEY_WIDTH + GLA_WIDTH + dv * (h + 1)].astype(F32)
            qd = (q * (dk ** -0.5) * jnp.exp(bcum)).astype(BF16)
            kd = (k * jnp.exp(-bcum)).astype(BF16)
            kl = (k * jnp.exp(blast - bcum)).astype(BF16)
            att = jnp.where(causal, _dot_nt(qd, kd), 0.0)
            st = state_ref[h]
            o = _dot(att.astype(BF16), v) + _dot_nt(qd, st.astype(BF16))
            state_ref[h] = st * jnp.exp(blast) + _dot_tn(v, kl)
            o = _rms(o, gain_ref[...]) * _silu(r)
            o_ref[r0:r0 + c, dv * h:dv * (h + 1)] = o.astype(o_ref.dtype)


def gla_mixer(x, small, w_up_pad, b_up, norm_gain, batch, seq):
    tb = min(GLA_TIME_BLOCK, seq)
    nt = seq // tb
    width = x.shape[1]
    nbytes = (_nbytes((tb, width), BF16) + _nbytes((tb, V7X_LANES), F32) + _nbytes(w_up_pad.shape, F32)
              + _nbytes((tb, GLA_WIDTH), BF16) + _nbytes((GLA_HEADS, GLA_VAL_DIM, GLA_KEY_DIM), F32)
              + 8 * _nbytes((tb, GLA_KEY_WIDTH), F32))
    return pl.pallas_call(
        _gla_body,
        grid=(batch, nt),
        in_specs=[pl.BlockSpec((tb, width), lambda b, t: (b * nt + t, 0)),
                  pl.BlockSpec((tb, V7X_LANES), lambda b, t: (b * nt + t, 0)),
                  pl.BlockSpec(w_up_pad.shape, lambda b, t: (0, 0)),
                  pl.BlockSpec((1, GLA_KEY_WIDTH), lambda b, t: (0, 0)),
                  pl.BlockSpec((1, GLA_VAL_DIM), lambda b, t: (0, 0))],
        out_specs=pl.BlockSpec((tb, GLA_WIDTH), lambda b, t: (b * nt + t, 0)),
        out_shape=jax.ShapeDtypeStruct((batch * seq, GLA_WIDTH), BF16),
        scratch_shapes=[pltpu.VMEM((GLA_HEADS, GLA_VAL_DIM, GLA_KEY_DIM), F32)],
        compiler_params=_params(("parallel", "arbitrary"), nbytes),
        name="gla_mixer",
    )(x, small, w_up_pad, b_up.reshape(1, -1), norm_gain.reshape(1, -1))


def _ssd_body(x_ref, dt_ref, cw_ref, cb_ref, dtb_ref, alog_ref, dexp_ref, gain_ref, expand_ref, shift_ref,
              wcast_in_ref, o_ref, wcast_out_ref, state_ref, tail_ref, y_ref, *, wcast_blocks):
    L, P, N = SSD_CHUNK, SSD_HEAD_DIM, SSD_STATE
    gw = SSD_GROUP_WIDTH

    @pl.when(pl.program_id(0) * pl.num_programs(1) + pl.program_id(1) < wcast_blocks)
    def _():
        wcast_out_ref[...] = wcast_in_ref[...].astype(BF16)

    @pl.when(pl.program_id(1) == 0)
    def _():
        state_ref[...] = jnp.zeros_like(state_ref)
        tail_ref[...] = jnp.zeros_like(tail_ref)

    xin = x_ref[:, SSD_WIDTH:]
    xin32 = xin.astype(F32)
    shifted = _dot(shift_ref[...], xin)
    conv = cb_ref[...] + cw_ref[SSD_CONV - 1:SSD_CONV, :] * xin32
    head = jnp.zeros((8, SSD_CONV_CH), F32)
    for j in range(1, SSD_CONV):
        wj = cw_ref[SSD_CONV - 1 - j:SSD_CONV - j, :]
        conv = conv + wj * shifted[L * (j - 1):L * j, :]
        head = head + wj * tail_ref[8 - j:16 - j, :]
    conv = jnp.concatenate([conv[0:8, :] + head, conv[8:, :]], axis=0)
    tail_ref[0:8, :] = xin32[L - 8:, :]
    xbc = _silu(conv)
    xs = xbc[:, :SSD_WIDTH]
    bm = xbc[:, SSD_WIDTH:SSD_WIDTH + SSD_BC_WIDTH].astype(BF16)
    cm = xbc[:, SSD_WIDTH + SSD_BC_WIDTH:].astype(BF16)

    dt = _softplus(dt_ref[...] + dtb_ref[...])
    dta = dt * (-jnp.exp(alog_ref[...]))
    ri = lax.broadcasted_iota(jnp.int32, (L, L), 0)
    ci = lax.broadcasted_iota(jnp.int32, (L, L), 1)
    causal = ci <= ri
    a_cum = _dot_exact_lhs(causal.astype(BF16), dta, 3)
    a_cum_t = a_cum.T
    expand = expand_ref[...]
    a_exp = _dot_exact_rhs(a_cum, expand, 3)
    dt_exp = _dot_exact_rhs(dt, expand, 2)
    a_last = a_exp[L - 1:L, :]
    xd = xs * dt_exp
    xdec = (xd * jnp.exp(a_last - a_exp)).astype(BF16)
    xd16 = xd.astype(BF16)
    out_scale = jnp.exp(a_exp)

    for g in range(SSD_GROUPS):
        bg = bm[:, N * g:N * (g + 1)]
        cg = cm[:, N * g:N * (g + 1)]
        cb = _dot_nt(cg, bg)
        for kk in range(SSD_HEADS_PER_GROUP):
            h = g * SSD_HEADS_PER_GROUP + kk
            hl = SSD_DT_LANE0 + h
            diff = a_cum[:, hl:hl + 1] - a_cum_t[hl:hl + 1, :]
            m = cb * jnp.exp(jnp.where(causal, diff, -jnp.inf))
            y_ref[:, P * h:P * (h + 1)] = _dot(m.astype(BF16), xd16[:, P * h:P * (h + 1)])
        sg = state_ref[:, gw * g:gw * (g + 1)]
        y_off = _dot(cg, sg.astype(BF16)) * out_scale[:, gw * g:gw * (g + 1)]
        state_ref[:, gw * g:gw * (g + 1)] = (sg * jnp.exp(a_last[:, gw * g:gw * (g + 1)])
                                             + _dot_tn(bg, xdec[:, gw * g:gw * (g + 1)]))
        yg = y_ref[:, gw * g:gw * (g + 1)] + y_off + xs[:, gw * g:gw * (g + 1)] * dexp_ref[:, gw * g:gw * (g + 1)]
        yg = yg * _silu(x_ref[:, gw * g:gw * (g + 1)].astype(F32))
        o_ref[:, gw * g:gw * (g + 1)] = _rms(yg, gain_ref[:, gw * g:gw * (g + 1)]).astype(o_ref.dtype)


def ssd_mixer(x, small, conv_w, conv_b, dt_bias, a_log, d_skip, norm_gain, batch, seq, wcast_all, layer):
    L = SSD_CHUNK
    nc = seq // L
    width = x.shape[1]
    _, wc_rows, wc_cols = wcast_all.shape
    wc_tile = V7X_MXU_WIDTH
    wc_blocks = wc_rows // wc_tile
    assert wc_rows % wc_tile == 0 and wc_blocks <= batch * nc
    wc_index = lambda b, c: jnp.minimum(b * nc + c, wc_blocks - 1)
    t = jnp.arange(L)
    shift = jnp.concatenate([(t[:, None] - j == t[None, :]) for j in range(1, SSD_CONV)], axis=0).astype(BF16)
    lane_pad = (SSD_DT_LANE0, V7X_LANES - SSD_DT_LANE0 - SSD_HEADS)
    dtb = jnp.pad(dt_bias, lane_pad).reshape(1, V7X_LANES)
    alog = jnp.pad(a_log, lane_pad).reshape(1, V7X_LANES)
    dexp = jnp.repeat(d_skip, SSD_HEAD_DIM).reshape(1, SSD_WIDTH)
    expand = (jnp.arange(V7X_LANES)[:, None] - SSD_DT_LANE0
              == (jnp.arange(SSD_WIDTH)[None, :] // SSD_HEAD_DIM)).astype(BF16)
    nbytes = (_nbytes((L, width), BF16) + _nbytes((L, V7X_LANES), F32) + _nbytes((L, SSD_WIDTH), BF16)
              + _nbytes(expand.shape, BF16) + 2 * _nbytes((SSD_STATE, SSD_WIDTH), F32)
              + 16 * _nbytes((L, SSD_CONV_CH), F32) + _nbytes((wc_tile, wc_cols), F32)
              + _nbytes((wc_tile, wc_cols), BF16))
    return pl.pallas_call(
        functools.partial(_ssd_body, wcast_blocks=wc_blocks),
        grid=(batch, nc),
        in_specs=[pl.BlockSpec((L, width), lambda b, c: (b * nc + c, 0)),
                  pl.BlockSpec((L, V7X_LANES), lambda b, c: (b * nc + c, 1)),
                  pl.BlockSpec((SSD_CONV, SSD_CONV_CH), lambda b, c: (0, 0)),
                  pl.BlockSpec((1, SSD_CONV_CH), lambda b, c: (0, 0)),
                  pl.BlockSpec((1, V7X_LANES), lambda b, c: (0, 0)),
                  pl.BlockSpec((1, V7X_LANES), lambda b, c: (0, 0)),
                  pl.BlockSpec((1, SSD_WIDTH), lambda b, c: (0, 0)),
                  pl.BlockSpec((1, SSD_WIDTH), lambda b, c: (0, 0)),
                  pl.BlockSpec(expand.shape, lambda b, c: (0, 0)),
                  pl.BlockSpec(shift.shape, lambda b, c: (0, 0)),
                  pl.BlockSpec((None, wc_tile, wc_cols), lambda b, c: (layer, wc_index(b, c), 0))],
        out_specs=[pl.BlockSpec((L, SSD_WIDTH), lambda b, c: (b * nc + c, 0)),
                   pl.BlockSpec((wc_tile, wc_cols), lambda b, c: (wc_index(b, c), 0))],
        out_shape=[jax.ShapeDtypeStruct((batch * seq, SSD_WIDTH), BF16),
                   jax.ShapeDtypeStruct((wc_rows, wc_cols), BF16)],
        scratch_shapes=[pltpu.VMEM((SSD_STATE, SSD_WIDTH), F32),
                        pltpu.VMEM((16, SSD_CONV_CH), F32),
                        pltpu.VMEM((L, SSD_WIDTH), F32)],
        compiler_params=_params(("arbitrary", "arbitrary"), nbytes),
        name="ssd_mixer",
    )(x, small, conv_w.reshape(SSD_CONV, SSD_CONV_CH), conv_b.reshape(1, -1), dtb, alog, dexp,
      norm_gain.reshape(1, -1), expand, shift, wcast_all)


def _xattn_body(h_ref, kv_ref, wq_ref, wo_ref, lnx_ref, qg_ref, kg_ref, lnf_ref, o_ref, hf_ref, att_ref):
    hd = X_HEAD_DIM
    mem_len = kv_ref.shape[0]
    half = mem_len // 2
    ones = jnp.ones((mem_len, V7X_LANES), BF16)
    h = h_ref[...]
    q = _dot(_rms(h, lnx_ref[...]).astype(BF16), wq_ref[...])
    for i in range(X_HEADS):
        qh = _rms(q[:, hd * i:hd * (i + 1)], qg_ref[...]).astype(BF16)
        kh = _rms(kv_ref[:, hd * i:hd * (i + 1)], kg_ref[...]).astype(BF16)
        vh = kv_ref[:, X_WIDTH + hd * i:X_WIDTH + hd * (i + 1)].astype(BF16)
        s = _dot_nt(qh, kh) * (hd ** -0.5)
        s_lo, s_hi = s[:, :half], s[:, half:]
        mx = jnp.broadcast_to(jnp.max(jnp.maximum(s_lo, s_hi), axis=-1, keepdims=True), s_lo.shape)
        p = jnp.concatenate([jnp.exp(s_lo - mx), jnp.exp(s_hi - mx)], axis=1).astype(BF16)
        att_ref[:, hd * i:hd * (i + 1)] = (_dot(p, vh) / _dot(p, ones)).astype(BF16)
    h_new = h + _dot(att_ref[...], wo_ref[...])
    o_ref[...] = h_new
    hf_ref[...] = _rms(h_new, lnf_ref[...]).astype(hf_ref.dtype)


def cross_attention_block(h, kv, w_q, w_o, ln_x, q_gain, k_gain, ln_ffn, batch, seq, mem_len, tq=256):
    m, d = h.shape
    tq = min(tq, seq)
    nq = seq // tq
    assert mem_len == 2 * V7X_LANES and X_HEAD_DIM == V7X_LANES
    const = lambda shape: pl.BlockSpec(shape, lambda b, t: (0,) * len(shape))
    row_blk = lambda width: pl.BlockSpec((tq, width), lambda b, t: (b * nq + t, 0))
    nbytes = (2 * _nbytes((tq, d), F32) + _nbytes((tq, d), BF16) + _nbytes((mem_len, 2 * X_WIDTH), F32)
              + 2 * _nbytes((d, X_WIDTH), BF16) + 2 * _nbytes((tq, d), F32))
    return pl.pallas_call(
        _xattn_body,
        grid=(batch, nq),
        in_specs=[row_blk(d),
                  pl.BlockSpec((mem_len, 2 * X_WIDTH), lambda b, t: (b, 0)),
                  const((d, X_WIDTH)), const((X_WIDTH, d)), const((1, d)),
                  const((1, X_HEAD_DIM)), const((1, X_HEAD_DIM)), const((1, d))],
        out_specs=[row_blk(d), row_blk(d)],
        out_shape=[jax.ShapeDtypeStruct((m, d), F32), jax.ShapeDtypeStruct((m, d), BF16)],
        scratch_shapes=[pltpu.VMEM((tq, X_WIDTH), BF16)],
        compiler_params=_params(("parallel", "parallel"), nbytes),
        name="xattn_block",
    )(h, kv, w_q, w_o, ln_x.reshape(1, -1), q_gain.reshape(1, -1), k_gain.reshape(1, -1), ln_ffn.reshape(1, -1))


_IN_OFF = np.cumsum([0, SWA_WIDTH, SWA_KV_WIDTH, SWA_KV_WIDTH, GLA_KEY_WIDTH, GLA_KEY_WIDTH, GLA_WIDTH,
                     GLA_WIDTH, GLA_GATE_RANK, SSD_WIDTH, SSD_CONV_CH, SSD_HEADS]).tolist()
SWA_COL0, GLA_COL0, GLOW_COL0, SSD_COL0, DT_COL0 = _IN_OFF[0], _IN_OFF[3], _IN_OFF[7], _IN_OFF[8], _IN_OFF[10]


IN_DIM = _IN_OFF[-1]
SMALL_ROW_STARTS = (GLOW_COL0, IN_DIM - V7X_LANES)
assert SMALL_ROW_STARTS[1] + SSD_DT_LANE0 == DT_COL0


def kernel(x, mem, rel_bias, ln_mix, w_in, swa_q_gain, swa_k_gain, swa_sinks, swa_out_gain, gla_w_gk_up, gla_b_gk_up, gla_norm_gain, ssd_conv_w, ssd_conv_b, ssd_dt_bias, ssd_a_log, ssd_d, ssd_norm_gain, w_mix_out, ln_x, ln_mem, x_w_q, x_w_k, x_w_v, x_w_o, x_q_gain, x_k_gain, ln_ffn, ffn_w_gate, ffn_w_up, ffn_w_down):
    batch, seq, d = x.shape
    mem_len = mem.shape[1]
    m = batch * seq
    band_bias = swa_band_bias(rel_bias)
    h = x.reshape(m, d)
    mem2 = mem.reshape(batch * mem_len, d)
    w_in_t = jnp.swapaxes(w_in, 1, 2)
    for l in range(DEPTH):
        w_up_pad = jnp.pad(gla_w_gk_up[l], ((0, V7X_LANES - GLA_GATE_RANK), (0, 0)))
        w_kv = jnp.concatenate([x_w_k[l], x_w_v[l]], axis=1).astype(BF16)
        hn, p_small = rmsnorm_small(h, ln_mix[l], w_in_t, l, SMALL_ROW_STARTS)
        proj = functools.partial(slab_matmul, [hn], w_in_t, l, features_on_rows=True, out_dtype=BF16)
        p_swa = proj(SWA_COL0, 2, (GLA_COL0 - SWA_COL0) // 2, name="proj_swa")
        p_gla = proj(GLA_COL0, (GLOW_COL0 - GLA_COL0) // PROJ_SLAB, PROJ_SLAB, name="proj_gla")
        p_ssd = proj(SSD_COL0, (DT_COL0 - SSD_COL0) // PROJ_SLAB, PROJ_SLAB, name="proj_ssd")
        y_a = swa_mixer(p_swa, band_bias, swa_sinks[l], swa_q_gain[l], swa_k_gain[l], swa_out_gain[l], batch, seq)
        y_b = gla_mixer(p_gla, p_small, w_up_pad, gla_b_gk_up[l], gla_norm_gain[l], batch, seq)
        y_c, w_down16 = ssd_mixer(p_ssd, p_small, ssd_conv_w[l], ssd_conv_b[l], ssd_dt_bias[l], ssd_a_log[l],
                                  ssd_d[l], ssd_norm_gain[l], batch, seq, ffn_w_down, l)
        h = slab_matmul([y_a, y_b, y_c], w_mix_out, l, 0, d // MIX_SLAB, MIX_SLAB, features_on_rows=False,
                        out_dtype=F32, res=h, tm=MIX_TOKENS, name="mix_out")
        memn = rmsnorm(mem2, ln_mem[l])
        kv = matmul(memn, w_kv, out_dtype=F32, tm=1024, tn=512, name="xattn_kv")
        h, hf = cross_attention_block(h, kv, x_w_q[l].astype(BF16), x_w_o[l].astype(BF16), ln_x[l],
                                      x_q_gain[l], x_k_gain[l], ln_ffn[l], batch, seq, mem_len)
        hidden = ffn_gate_up(hf, ffn_w_gate, ffn_w_up, l)
        h = matmul(hidden, w_down16, out_dtype=F32, res=h, tm=512, tn=512, name="ffn_down")
    return h.reshape(batch, seq, d)
```

```python
import functools
import math

import numpy as np
import jax
import jax.numpy as jnp
from jax import lax
from jax.experimental import pallas as pl
from jax.experimental.pallas import tpu as pltpu

F32 = jnp.float32
BF16 = jnp.bfloat16

D_MODEL = 4096
DEPTH = 2
EPS = 1e-6
SWA_WIDTH = 1024
SWA_HEAD_DIM = 64
SWA_HEADS = 16
SWA_KV_HEADS = 2
SWA_GROUP = SWA_HEADS // SWA_KV_HEADS
SWA_KV_WIDTH = SWA_KV_HEADS * SWA_HEAD_DIM
SWA_WINDOW = 128
SWA_BLOCK = 128
REL_BUCKETS = 32
REL_MAX_DIST = 128
GLA_WIDTH = 1024
GLA_HEADS = 4
GLA_VAL_DIM = 256
GLA_KEY_DIM = 128
GLA_KEY_WIDTH = GLA_HEADS * GLA_KEY_DIM
GLA_GATE_RANK = 16
GLA_GATE_NORMALIZER = 16.0
GLA_CHUNK = 64
SSD_WIDTH = 2048
SSD_HEAD_DIM = 64
SSD_HEADS = 32
SSD_GROUPS = 8
SSD_HEADS_PER_GROUP = SSD_HEADS // SSD_GROUPS
SSD_STATE = 128
SSD_CONV = 4
SSD_CHUNK = 128
SSD_BC_WIDTH = SSD_GROUPS * SSD_STATE
SSD_CONV_CH = SSD_WIDTH + 2 * SSD_BC_WIDTH
SSD_GROUP_WIDTH = SSD_WIDTH // SSD_GROUPS
X_HEADS = 4
X_HEAD_DIM = 128
X_WIDTH = X_HEADS * X_HEAD_DIM
FFN_HIDDEN = 11008

V7X_LANES = 128
V7X_SUBLANES = 8
V7X_VMEM_BYTES = 64 * 1024 * 1024
VMEM_REQUEST_CAP = (V7X_VMEM_BYTES * 7) // 8
V7X_MXU_WIDTH = 256
FFN_SLAB = 2 * V7X_MXU_WIDTH
PROJ_SLAB = 4 * V7X_MXU_WIDTH
MIX_SLAB = 4 * V7X_MXU_WIDTH
MIX_TOKENS = 512
GLA_TIME_BLOCK = 256
SSD_DT_LANE0 = V7X_LANES - SSD_HEADS


VMEM_TEMP_ALLOWANCE = 16 * 1024 * 1024


def _params(semantics, block_bytes):
    limit = min(VMEM_REQUEST_CAP, 2 * block_bytes + VMEM_TEMP_ALLOWANCE)
    return pltpu.CompilerParams(dimension_semantics=semantics, vmem_limit_bytes=int(limit))


def _nbytes(shape, dtype):
    return int(np.prod(shape)) * jnp.dtype(dtype).itemsize


def _split_bf16(x, terms):
    parts = []
    r = x
    for t in range(terms):
        p = r.astype(BF16)
        parts.append(p)
        if t + 1 < terms:
            r = r - p.astype(F32)
    return parts


def _dot(a, b):
    return jnp.dot(a, b, preferred_element_type=F32)


def _dot_nt(a, b):
    return lax.dot_general(a, b, (((1,), (1,)), ((), ())), preferred_element_type=F32)


def _dot_tn(a, b):
    return lax.dot_general(a, b, (((0,), (0,)), ((), ())), preferred_element_type=F32)


def _dot_exact_lhs(a_bf16, x, terms):
    acc = None
    for p in _split_bf16(x, terms):
        d = _dot(a_bf16, p)
        acc = d if acc is None else acc + d
    return acc


def _dot_exact_rhs(x, b_bf16, terms):
    acc = None
    for p in _split_bf16(x, terms):
        d = _dot(p, b_bf16)
        acc = d if acc is None else acc + d
    return acc


def _silu(x):
    return x / (1.0 + jnp.exp(-x))


def _softplus(x):
    return jnp.maximum(x, 0.0) + jnp.log1p(jnp.exp(-jnp.abs(x)))


def _rms(x, gain):
    return x * lax.rsqrt(jnp.mean(x * x, axis=-1, keepdims=True) + EPS) * gain


def _rmsnorm_body(x_ref, g_ref, o_ref):
    o_ref[...] = _rms(x_ref[...], g_ref[...]).astype(o_ref.dtype)


def rmsnorm(x, gain, tm=256):
    m, d = x.shape
    tm = min(tm, m)
    return pl.pallas_call(
        _rmsnorm_body,
        grid=(m // tm,),
        in_specs=[pl.BlockSpec((tm, d), lambda i: (i, 0)),
                  pl.BlockSpec((1, d), lambda i: (0, 0))],
        out_specs=pl.BlockSpec((tm, d), lambda i: (i, 0)),
        out_shape=jax.ShapeDtypeStruct((m, d), BF16),
        compiler_params=_params(("parallel",), _nbytes((tm, d), F32) + _nbytes((tm, d), BF16)),
        name="rmsnorm",
    )(x, gain.reshape(1, d))


def _rmsnorm_small_body(x_ref, g_ref, wa_ref, wb_ref, o_ref, small_ref, wt_ref):
    @pl.when(pl.program_id(0) == 0)
    def _():
        for t, w_ref in enumerate((wa_ref, wb_ref)):
            for k0 in range(0, w_ref.shape[2], _XPOSE_COLS):
                wt_ref[k0:k0 + _XPOSE_COLS, V7X_LANES * t:V7X_LANES * (t + 1)] = (
                    w_ref[0, :, k0:k0 + _XPOSE_COLS].T.astype(BF16))

    hn = _rms(x_ref[...], g_ref[...]).astype(BF16)
    o_ref[...] = hn
    small_ref[...] = _dot(hn, wt_ref[...])


def rmsnorm_small(x, gain, wt_all, layer, row_starts, tm=512):
    m, d = x.shape
    tm = min(tm, m)
    w_tile = lambda r0: pl.BlockSpec((pl.Element(1), pl.Element(V7X_LANES), pl.Element(d)),
                                     lambda i: (layer, r0, 0))
    nbytes = (_nbytes((tm, d), F32) + _nbytes((tm, d), BF16) + 2 * _nbytes((V7X_LANES, d), F32)
              + _nbytes((d, 2 * V7X_LANES), BF16) + _nbytes((tm, 2 * V7X_LANES), F32))
    return pl.pallas_call(
        _rmsnorm_small_body,
        grid=(m // tm,),
        in_specs=[pl.BlockSpec((tm, d), lambda i: (i, 0)),
                  pl.BlockSpec((1, d), lambda i: (0, 0)),
                  w_tile(row_starts[0]), w_tile(row_starts[1])],
        out_specs=[pl.BlockSpec((tm, d), lambda i: (i, 0)),
                   pl.BlockSpec((tm, 2 * V7X_LANES), lambda i: (i, 0))],
        out_shape=[jax.ShapeDtypeStruct((m, d), BF16), jax.ShapeDtypeStruct((m, 2 * V7X_LANES), F32)],
        scratch_shapes=[pltpu.VMEM((d, 2 * V7X_LANES), BF16)],
        compiler_params=_params(("arbitrary",), nbytes),
        name="rmsnorm_small",
    )(x, gain.reshape(1, d), wt_all, wt_all)


def _mm_body(*refs, has_res):
    a_ref, w_ref = refs[0], refs[1]
    o_ref = refs[2 + has_res]
    part = _dot(a_ref[...], w_ref[...])
    if has_res:
        part = part + refs[2][...]
    o_ref[...] = part.astype(o_ref.dtype)


def matmul(a, w, *, out_dtype, res=None, tm, tn, name="matmul"):
    m, kdim = a.shape
    n = w.shape[1]
    tm, tn = min(tm, m), min(tn, n)
    assert m % tm == 0 and n % tn == 0
    has_res = res is not None
    in_specs = [pl.BlockSpec((tm, kdim), lambda i, j: (i, 0)), pl.BlockSpec((kdim, tn), lambda i, j: (0, j))]
    args = [a, w]
    nbytes = _nbytes((tm, kdim), BF16) + _nbytes((kdim, tn), BF16) + _nbytes((tm, tn), out_dtype) + _nbytes((tm, tn), F32)
    if has_res:
        in_specs.append(pl.BlockSpec((tm, tn), lambda i, j: (i, j)))
        args.append(res)
        nbytes += _nbytes((tm, tn), F32)
    return pl.pallas_call(
        functools.partial(_mm_body, has_res=has_res),
        grid=(m // tm, n // tn),
        in_specs=in_specs,
        out_specs=pl.BlockSpec((tm, tn), lambda i, j: (i, j)),
        out_shape=jax.ShapeDtypeStruct((m, n), out_dtype),
        compiler_params=_params(("parallel", "parallel"), nbytes),
        name=name,
    )(*args)


_XPOSE_ROWS, _XPOSE_COLS = 256, 512


def _slab_copy(w_hbm, stage_ref, sem, layer, slab, start0, tn, features_on_rows):
    if features_on_rows:
        src = w_hbm.at[layer, pl.ds(pl.multiple_of(start0 + slab * tn, V7X_SUBLANES), tn), :]
    else:
        src = w_hbm.at[layer, :, pl.ds(pl.multiple_of(start0 + slab * tn, V7X_LANES), tn)]
    return pltpu.make_async_copy(src, stage_ref, sem.at[0])


def _slab_body(*refs, n_a, has_res, layer, start0, features_on_rows):
    a_refs, w_hbm = refs[:n_a], refs[n_a]
    r_ref = refs[n_a + 1] if has_res else None
    o_ref, stage_ref, wb_ref, sem = refs[n_a + 1 + has_res:]
    kdim, tn = wb_ref.shape
    slab, i = pl.program_id(0), pl.program_id(1)
    copy = lambda s: _slab_copy(w_hbm, stage_ref, sem, layer, s, start0, tn, features_on_rows)

    @pl.when(i == 0)
    def _():
        @pl.when(slab == 0)
        def _():
            copy(slab).start()

        copy(slab).wait()
        if features_on_rows:
            for r0 in range(0, tn, _XPOSE_ROWS):
                r1 = min(r0 + _XPOSE_ROWS, tn)
                for k0 in range(0, kdim, _XPOSE_COLS):
                    wb_ref[k0:k0 + _XPOSE_COLS, r0:r1] = stage_ref[r0:r1, k0:k0 + _XPOSE_COLS].T.astype(BF16)
        else:
            wb_ref[...] = stage_ref[...].astype(BF16)

    @pl.when((i == 1) & (slab + 1 < pl.num_programs(0)))
    def _():
        copy(slab + 1).start()

    acc = r_ref[...] if has_res else None
    k0 = 0
    for a_ref in a_refs:
        k1 = k0 + a_ref.shape[1]
        part = _dot(a_ref[...], wb_ref[k0:k1, :])
        acc = part if acc is None else acc + part
        k0 = k1
    o_ref[...] = acc.astype(o_ref.dtype)


def slab_matmul(a_list, w_all, layer, start0, n_slabs, tn, *, features_on_rows, out_dtype, res=None,
                tm=1024, name="slab_matmul"):
    m = a_list[0].shape[0]
    kdim = sum(a.shape[1] for a in a_list)
    tm = min(tm, m)
    assert m // tm >= 2
    has_res = res is not None
    in_specs = [pl.BlockSpec((tm, a.shape[1]), lambda s, i: (i, 0)) for a in a_list]
    in_specs.append(pl.BlockSpec(memory_space=pl.ANY))
    args = list(a_list) + [w_all]
    nbytes = (2 * _nbytes((tm, kdim), BF16) + _nbytes((kdim, tn), F32) + _nbytes((kdim, tn), BF16)
              + 2 * _nbytes((tm, tn), out_dtype) + 4 * _nbytes((tm, tn), F32))
    if has_res:
        in_specs.append(pl.BlockSpec((tm, tn), lambda s, i: (i, s)))
        args.append(res)
        nbytes += 2 * _nbytes((tm, tn), F32)
    return pl.pallas_call(
        functools.partial(_slab_body, n_a=len(a_list), has_res=has_res, layer=layer, start0=start0,
                          features_on_rows=features_on_rows),
        grid=(n_slabs, m // tm),
        in_specs=in_specs,
        out_specs=pl.BlockSpec((tm, tn), lambda s, i: (i, s)),
        out_shape=jax.ShapeDtypeStruct((m, n_slabs * tn), out_dtype),
        scratch_shapes=[pltpu.VMEM((tn, kdim) if features_on_rows else (kdim, tn), F32),
                        pltpu.VMEM((kdim, tn), BF16),
                        pltpu.SemaphoreType.DMA((1,))],
        compiler_params=pltpu.CompilerParams(dimension_semantics=("arbitrary", "arbitrary"),
                                             vmem_limit_bytes=min(VMEM_REQUEST_CAP, nbytes)),
        name=name,
    )(*args)


def _gateup_copies(wg_hbm, wu_hbm, stage_ref, sem, layer, slab, width):
    c0 = pl.multiple_of(slab * FFN_SLAB, FFN_SLAB)
    return [pltpu.make_async_copy(w_hbm.at[layer, :, pl.ds(c0, width)], stage_ref.at[k, :, pl.ds(0, width)],
                                  sem.at[k])
            for k, w_hbm in enumerate((wg_hbm, wu_hbm))]


def _gateup_body(a_ref, wg_hbm, wu_hbm, o_ref, stage_ref, wgb_ref, wub_ref, sem, *, layer, last_width):
    slab, i = pl.program_id(0), pl.program_id(1)
    last = pl.num_programs(0) - 1

    def for_slab(s, action):
        @pl.when(s < last)
        def _():
            for c in _gateup_copies(wg_hbm, wu_hbm, stage_ref, sem, layer, s, FFN_SLAB):
                action(c)

        @pl.when(s == last)
        def _():
            for c in _gateup_copies(wg_hbm, wu_hbm, stage_ref, sem, layer, s, last_width):
                action(c)

    @pl.when(i == 0)
    def _():
        @pl.when(slab == 0)
        def _():
            for_slab(slab, lambda c: c.start())

        for_slab(slab, lambda c: c.wait())
        wgb_ref[...] = stage_ref[0].astype(BF16)
        wub_ref[...] = stage_ref[1].astype(BF16)

    @pl.when((i == 1) & (slab < last))
    def _():
        for_slab(slab + 1, lambda c: c.start())

    a = a_ref[...]
    g = _dot(a, wgb_ref[...])
    u = _dot(a, wub_ref[...])
    o_ref[...] = (_silu(g) * u).astype(o_ref.dtype)


def ffn_gate_up(a, wg_all, wu_all, layer, tm=1024):
    m, kdim = a.shape
    n = wg_all.shape[2]
    tm = min(tm, m)
    n_slabs = pl.cdiv(n, FFN_SLAB)
    last_width = n - (n_slabs - 1) * FFN_SLAB
    assert m // tm >= 2 and last_width % V7X_LANES == 0
    nbytes = (2 * _nbytes((tm, kdim), BF16) + 2 * _nbytes((kdim, FFN_SLAB), F32)
              + 2 * _nbytes((kdim, FFN_SLAB), BF16) + 2 * _nbytes((tm, FFN_SLAB), BF16)
              + 6 * _nbytes((tm, FFN_SLAB), F32))
    return pl.pallas_call(
        functools.partial(_gateup_body, layer=layer, last_width=last_width),
        grid=(n_slabs, m // tm),
        in_specs=[pl.BlockSpec((tm, kdim), lambda s, i: (i, 0)),
                  pl.BlockSpec(memory_space=pl.ANY),
                  pl.BlockSpec(memory_space=pl.ANY)],
        out_specs=pl.BlockSpec((tm, FFN_SLAB), lambda s, i: (i, s)),
        out_shape=jax.ShapeDtypeStruct((m, n), BF16),
        scratch_shapes=[pltpu.VMEM((2, kdim, FFN_SLAB), F32),
                        pltpu.VMEM((kdim, FFN_SLAB), BF16), pltpu.VMEM((kdim, FFN_SLAB), BF16),
                        pltpu.SemaphoreType.DMA((2,))],
        compiler_params=pltpu.CompilerParams(dimension_semantics=("arbitrary", "arbitrary"),
                                             vmem_limit_bytes=min(VMEM_REQUEST_CAP, nbytes)),
        name="ffn_gate_up",
    )(a, wg_all, wu_all)


def _swa_body(q_ref, kvc_ref, kvp_ref, bias_ref, sink_ref, qg_ref, kg_ref, og_ref, e_ref, et_ref, o_ref, acc_ref):
    hd, blk, grp = SWA_HEAD_DIM, SWA_BLOCK, SWA_GROUP
    kband = jnp.concatenate([kvp_ref[:, :SWA_KV_WIDTH], kvc_ref[:, :SWA_KV_WIDTH]], axis=0).astype(F32)
    vband = jnp.concatenate([kvp_ref[:, SWA_KV_WIDTH:], kvc_ref[:, SWA_KV_WIDTH:]], axis=0)
    ones = jnp.ones((2 * blk, V7X_LANES), BF16)

    q = q_ref[...].astype(F32)
    ssq = _dot_exact_rhs(q * q, e_ref[...], 2)
    inv = lax.rsqrt(ssq * (1.0 / hd) + EPS)
    qn = (q * _dot_exact_rhs(inv, et_ref[...], 3) * qg_ref[...]).astype(BF16)

    for j in range(SWA_KV_HEADS):
        kn = _rms(kband[:, hd * j:hd * (j + 1)], kg_ref[...]).astype(BF16)
        qs = jnp.concatenate([qn[:, hd * (grp * j + g):hd * (grp * j + g + 1)] for g in range(grp)], axis=0)
        s = _dot_nt(qs, kn) * (hd ** -0.5) + bias_ref[j]
        sink = sink_ref[j]
        s_prev, s_cur = s[:, :blk], s[:, blk:]
        row_max = jnp.max(jnp.maximum(s_prev, s_cur), axis=-1, keepdims=True)
        mx = jnp.maximum(jnp.broadcast_to(row_max, sink.shape), sink)
        p = jnp.concatenate([jnp.exp(s_prev - mx), jnp.exp(s_cur - mx)], axis=1).astype(BF16)
        total = _dot(p, ones) + jnp.exp(sink - mx)
        o = _dot(p, vband[:, hd * j:hd * (j + 1)]) / total[:, :hd]
        for g in range(grp):
            h = grp * j + g
            acc_ref[:, hd * h:hd * (h + 1)] = o[blk * g:blk * (g + 1), :]
    o_ref[...] = _rms(acc_ref[...], og_ref[...]).astype(o_ref.dtype)


def swa_mixer(qkv, bias, sinks, q_gain, k_gain, out_gain, batch, seq):
    nb = seq // SWA_BLOCK
    kv_blk = SWA_WIDTH // (2 * SWA_KV_WIDTH)
    rows = SWA_GROUP * SWA_BLOCK
    no_prev = jnp.arange(2 * SWA_BLOCK) < SWA_BLOCK
    bias_g = jnp.stack([jnp.where(no_prev, -jnp.inf, bias), bias]).reshape(2, SWA_KV_HEADS, rows, 2 * SWA_BLOCK)
    sink_col = jnp.broadcast_to(jnp.repeat(sinks, SWA_BLOCK)[:, None],
                                (SWA_HEADS * SWA_BLOCK, V7X_LANES)).reshape(SWA_KV_HEADS, rows, V7X_LANES)
    head_of_col = jnp.arange(SWA_WIDTH) // SWA_HEAD_DIM
    e = (head_of_col[:, None] == jnp.arange(V7X_LANES)[None, :]).astype(BF16)
    const = lambda shape: pl.BlockSpec(shape, lambda b, n: (0,) * len(shape))
    nbytes = (_nbytes((SWA_BLOCK, SWA_WIDTH), BF16) * 2 + 2 * _nbytes((SWA_BLOCK, 2 * SWA_KV_WIDTH), BF16)
              + _nbytes(bias.shape, F32) + _nbytes((SWA_KV_HEADS, rows, V7X_LANES), F32)
              + 2 * _nbytes(e.shape, BF16) + _nbytes((SWA_BLOCK, SWA_WIDTH), F32)
              + 4 * _nbytes((rows, 2 * SWA_BLOCK), F32))
    return pl.pallas_call(
        _swa_body,
        grid=(batch, nb),
        in_specs=[pl.BlockSpec((SWA_BLOCK, SWA_WIDTH), lambda b, n: (b * nb + n, 0)),
                  pl.BlockSpec((SWA_BLOCK, 2 * SWA_KV_WIDTH), lambda b, n: (b * nb + n, kv_blk)),
                  pl.BlockSpec((SWA_BLOCK, 2 * SWA_KV_WIDTH),
                               lambda b, n: (b * nb + jnp.maximum(n - 1, 0), kv_blk)),
                  pl.BlockSpec((None,) + bias_g.shape[1:], lambda b, n: (jnp.minimum(n, 1), 0, 0, 0)),
                  const(sink_col.shape),
                  const((1, SWA_WIDTH)), const((1, SWA_HEAD_DIM)), const((1, SWA_WIDTH)),
                  const(e.shape), const(e.T.shape)],
        out_specs=pl.BlockSpec((SWA_BLOCK, SWA_WIDTH), lambda b, n: (b * nb + n, 0)),
        out_shape=jax.ShapeDtypeStruct((batch * seq, SWA_WIDTH), BF16),
        scratch_shapes=[pltpu.VMEM((SWA_BLOCK, SWA_WIDTH), F32)],
        compiler_params=_params(("parallel", "parallel"), nbytes),
        name="swa_mixer",
    )(qkv, qkv, qkv, bias_g, sink_col, jnp.tile(q_gain, SWA_HEADS).reshape(1, -1), k_gain.reshape(1, -1),
      out_gain.reshape(1, -1), e, e.T)


def _t5_bucket(dist):
    n = jnp.maximum(dist, 0)
    max_exact = REL_BUCKETS // 2
    nf = jnp.maximum(n, 1).astype(F32)
    large = max_exact + (jnp.log(nf / max_exact) / math.log(REL_MAX_DIST / max_exact)
                         * (REL_BUCKETS - max_exact)).astype(jnp.int32)
    large = jnp.minimum(large, REL_BUCKETS - 1)
    return jnp.where(n < max_exact, n, large)


def swa_band_bias(rel_bias):
    i = jnp.arange(SWA_BLOCK, dtype=jnp.int32)[:, None]
    j = jnp.arange(2 * SWA_BLOCK, dtype=jnp.int32)[None, :]
    dist = i + SWA_BLOCK - j
    onehot = _t5_bucket(dist)[None] == jnp.arange(REL_BUCKETS, dtype=jnp.int32)[:, None, None]
    bias = jnp.sum(jnp.where(onehot[:, None], rel_bias[:, :, None, None], 0.0), axis=0)
    in_window = (dist >= 0) & (dist < SWA_WINDOW)
    return jnp.where(in_window[None], bias, -jnp.inf)


def _gla_body(x_ref, gl_ref, wup_ref, bup_ref, gain_ref, o_ref, state_ref):
    tb = x_ref.shape[0]
    dk, dv, c = GLA_KEY_DIM, GLA_VAL_DIM, GLA_CHUNK

    @pl.when(pl.program_id(1) == 0)
    def _():
        state_ref[...] = jnp.zeros_like(state_ref)

    gl_hi, gl_lo = _split_bf16(gl_ref[...], 2)
    w_hi, w_lo = _split_bf16(wup_ref[...], 2)
    pre = _dot(gl_hi, w_hi) + _dot(gl_hi, w_lo) + _dot(gl_lo, w_hi) + bup_ref[...]
    g = (jnp.minimum(pre, 0.0) - jnp.log1p(jnp.exp(-jnp.abs(pre)))) * (1.0 / GLA_GATE_NORMALIZER)

    row = lax.broadcasted_iota(jnp.int32, (tb, tb), 0)
    colm = lax.broadcasted_iota(jnp.int32, (tb, tb), 1)
    same_chunk_lower = ((row // c) == (colm // c)) & (colm <= row)
    bcum_all = _dot_exact_lhs(same_chunk_lower.astype(BF16), g, 3)

    ri = lax.broadcasted_iota(jnp.int32, (c, c), 0)
    ci = lax.broadcasted_iota(jnp.int32, (c, c), 1)
    causal = ci <= ri
    for ch in range(tb // c):
        r0 = ch * c
        for h in range(GLA_HEADS):
            bcum = bcum_all[r0:r0 + c, dk * h:dk * (h + 1)]
            blast = bcum[c - 1:c, :]
            q = x_ref[r0:r0 + c, dk * h:dk * (h + 1)].astype(F32)
            k = x_ref[r0:r0 + c, GLA_KEY_WIDTH + dk * h:GLA_KEY_WIDTH + dk * (h + 1)].astype(F32)
            v = x_ref[r0:r0 + c, 2 * GLA_KEY_WIDTH + dv * h:2 * GLA_KEY_WIDTH + dv * (h + 1)]
            r = x_ref[r0:r0 + c, 2 * GLA_KEY_WIDTH + GLA_WIDTH + dv * h:
                      2 * GLA_KEY_WIDTH + GLA_WIDTH + dv * (h + 1)].astype(F32)
            qd = (q * (dk ** -0.5) * jnp.exp(bcum)).astype(BF16)
            kd = (k * jnp.exp(-bcum)).astype(BF16)
            kl = (k * jnp.exp(blast - bcum)).astype(BF16)
            att = jnp.where(causal, _dot_nt(qd, kd), 0.0)
            st = state_ref[h]
            o = _dot(att.astype(BF16), v) + _dot_nt(qd, st.astype(BF16))
            state_ref[h] = st * jnp.exp(blast) + _dot_tn(v, kl)
            o = _rms(o, gain_ref[...]) * _silu(r)
            o_ref[r0:r0 + c, dv * h:dv * (h + 1)] = o.astype(o_ref.dtype)


def gla_mixer(x, small, w_up_pad, b_up, norm_gain, batch, seq):
    tb = min(GLA_TIME_BLOCK, seq)
    nt = seq // tb
    width = x.shape[1]
    nbytes = (_nbytes((tb, width), BF16) + _nbytes((tb, V7X_LANES), F32) + _nbytes(w_up_pad.shape, F32)
              + _nbytes((tb, GLA_WIDTH), BF16) + _nbytes((GLA_HEADS, GLA_VAL_DIM, GLA_KEY_DIM), F32)
              + 8 * _nbytes((tb, GLA_KEY_WIDTH), F32))
    return pl.pallas_call(
        _gla_body,
        grid=(batch, nt),
        in_specs=[pl.BlockSpec((tb, width), lambda b, t: (b * nt + t, 0)),
                  pl.BlockSpec((tb, V7X_LANES), lambda b, t: (b * nt + t, 0)),
                  pl.BlockSpec(w_up_pad.shape, lambda b, t: (0, 0)),
                  pl.BlockSpec((1, GLA_KEY_WIDTH), lambda b, t: (0, 0)),
                  pl.BlockSpec((1, GLA_VAL_DIM), lambda b, t: (0, 0))],
        out_specs=pl.BlockSpec((tb, GLA_WIDTH), lambda b, t: (b * nt + t, 0)),
        out_shape=jax.ShapeDtypeStruct((batch * seq, GLA_WIDTH), BF16),
        scratch_shapes=[pltpu.VMEM((GLA_HEADS, GLA_VAL_DIM, GLA_KEY_DIM), F32)],
        compiler_params=_params(("parallel", "arbitrary"), nbytes),
        name="gla_mixer",
    )(x, small, w_up_pad, b_up.reshape(1, -1), norm_gain.reshape(1, -1))


def _ssd_body(x_ref, dt_ref, cw_ref, cb_ref, dtb_ref, alog_ref, dexp_ref, gain_ref, expand_ref, shift_ref,
              wcast_in_ref, o_ref, wcast_out_ref, state_ref, tail_ref, y_ref, *, wcast_blocks):
    L, P, N = SSD_CHUNK, SSD_HEAD_DIM, SSD_STATE
    gw = SSD_GROUP_WIDTH

    @pl.when(pl.program_id(0) * pl.num_programs(1) + pl.program_id(1) < wcast_blocks)
    def _():
        wcast_out_ref[...] = wcast_in_ref[...].astype(BF16)

    @pl.when(pl.program_id(1) == 0)
    def _():
        state_ref[...] = jnp.zeros_like(state_ref)
        tail_ref[...] = jnp.zeros_like(tail_ref)

    xin = x_ref[:, SSD_WIDTH:]
    xin32 = xin.astype(F32)
    shifted = _dot(shift_ref[...], xin)
    conv = cb_ref[...] + cw_ref[SSD_CONV - 1:SSD_CONV, :] * xin32
    head = jnp.zeros((8, SSD_CONV_CH), F32)
    for j in range(1, SSD_CONV):
        wj = cw_ref[SSD_CONV - 1 - j:SSD_CONV - j, :]
        conv = conv + wj * shifted[L * (j - 1):L * j, :]
        head = head + wj * tail_ref[8 - j:16 - j, :]
    conv = jnp.concatenate([conv[0:8, :] + head, conv[8:, :]], axis=0)
    tail_ref[0:8, :] = xin32[L - 8:, :]
    xbc = _silu(conv)
    xs = xbc[:, :SSD_WIDTH]
    bm = xbc[:, SSD_WIDTH:SSD_WIDTH + SSD_BC_WIDTH].astype(BF16)
    cm = xbc[:, SSD_WIDTH + SSD_BC_WIDTH:].astype(BF16)

    dt = _softplus(dt_ref[...] + dtb_ref[...])
    dta = dt * (-jnp.exp(alog_ref[...]))
    ri = lax.broadcasted_iota(jnp.int32, (L, L), 0)
    ci = lax.broadcasted_iota(jnp.int32, (L, L), 1)
    causal = ci <= ri
    a_cum = _dot_exact_lhs(causal.astype(BF16), dta, 3)
    a_cum_t = a_cum.T
    expand = expand_ref[...]
    a_exp = _dot_exact_rhs(a_cum, expand, 3)
    dt_exp = _dot_exact_rhs(dt, expand, 2)
    a_last = a_exp[L - 1:L, :]
    xd = xs * dt_exp
    xdec = (xd * jnp.exp(a_last - a_exp)).astype(BF16)
    xd16 = xd.astype(BF16)
    out_scale = jnp.exp(a_exp)

    for g in range(SSD_GROUPS):
        bg = bm[:, N * g:N * (g + 1)]
        cg = cm[:, N * g:N * (g + 1)]
        cb = _dot_nt(cg, bg)
        for kk in range(SSD_HEADS_PER_GROUP):
            h = g * SSD_HEADS_PER_GROUP + kk
            hl = SSD_DT_LANE0 + h
            diff = a_cum[:, hl:hl + 1] - a_cum_t[hl:hl + 1, :]
            m = cb * jnp.exp(jnp.where(causal, diff, -jnp.inf))
            y_ref[:, P * h:P * (h + 1)] = _dot(m.astype(BF16), xd16[:, P * h:P * (h + 1)])
        sg = state_ref[:, gw * g:gw * (g + 1)]
        y_off = _dot(cg, sg.astype(BF16)) * out_scale[:, gw * g:gw * (g + 1)]
        state_ref[:, gw * g:gw * (g + 1)] = (sg * jnp.exp(a_last[:, gw * g:gw * (g + 1)])
                                             + _dot_tn(bg, xdec[:, gw * g:gw * (g + 1)]))
        yg = y_ref[:, gw * g:gw * (g + 1)] + y_off + xs[:, gw * g:gw * (g + 1)] * dexp_ref[:, gw * g:gw * (g + 1)]
        yg = yg * _silu(x_ref[:, gw * g:gw * (g + 1)].astype(F32))
        o_ref[:, gw * g:gw * (g + 1)] = _rms(yg, gain_ref[:, gw * g:gw * (g + 1)]).astype(o_ref.dtype)


def ssd_mixer(x, small, conv_w, conv_b, dt_bias, a_log, d_skip, norm_gain, batch, seq, wcast_all, layer):
    L = SSD_CHUNK
    nc = seq // L
    width = x.shape[1]
    _, wc_rows, wc_cols = wcast_all.shape
    wc_tile = V7X_MXU_WIDTH
    wc_blocks = wc_rows // wc_tile
    assert wc_rows % wc_tile == 0 and wc_blocks <= batch * nc
    wc_index = lambda b, c: jnp.minimum(b * nc + c, wc_blocks - 1)
    t = jnp.arange(L)
    shift = jnp.concatenate([(t[:, None] - j == t[None, :]) for j in range(1, SSD_CONV)], axis=0).astype(BF16)
    lane_pad = (SSD_DT_LANE0, V7X_LANES - SSD_DT_LANE0 - SSD_HEADS)
    dtb = jnp.pad(dt_bias, lane_pad).reshape(1, V7X_LANES)
    alog = jnp.pad(a_log, lane_pad).reshape(1, V7X_LANES)
    dexp = jnp.repeat(d_skip, SSD_HEAD_DIM).reshape(1, SSD_WIDTH)
    expand = (jnp.arange(V7X_LANES)[:, None] - SSD_DT_LANE0
              == (jnp.arange(SSD_WIDTH)[None, :] // SSD_HEAD_DIM)).astype(BF16)
    nbytes = (_nbytes((L, width), BF16) + _nbytes((L, V7X_LANES), F32) + _nbytes((L, SSD_WIDTH), BF16)
              + _nbytes(expand.shape, BF16) + 2 * _nbytes((SSD_STATE, SSD_WIDTH), F32)
              + 16 * _nbytes((L, SSD_CONV_CH), F32) + _nbytes((wc_tile, wc_cols), F32)
              + _nbytes((wc_tile, wc_cols), BF16))
    return pl.pallas_call(
        functools.partial(_ssd_body, wcast_blocks=wc_blocks),
        grid=(batch, nc),
        in_specs=[pl.BlockSpec((L, width), lambda b, c: (b * nc + c, 0)),
                  pl.BlockSpec((L, V7X_LANES), lambda b, c: (b * nc + c, 1)),
                  pl.BlockSpec((SSD_CONV, SSD_CONV_CH), lambda b, c: (0, 0)),
                  pl.BlockSpec((1, SSD_CONV_CH), lambda b, c: (0, 0)),
                  pl.BlockSpec((1, V7X_LANES), lambda b, c: (0, 0)),
                  pl.BlockSpec((1, V7X_LANES), lambda b, c: (0, 0)),
                  pl.BlockSpec((1, SSD_WIDTH), lambda b, c: (0, 0)),
                  pl.BlockSpec((1, SSD_WIDTH), lambda b, c: (0, 0)),
                  pl.BlockSpec(expand.shape, lambda b, c: (0, 0)),
                  pl.BlockSpec(shift.shape, lambda b, c: (0, 0)),
                  pl.BlockSpec((None, wc_tile, wc_cols), lambda b, c: (layer, wc_index(b, c), 0))],
        out_specs=[pl.BlockSpec((L, SSD_WIDTH), lambda b, c: (b * nc + c, 0)),
                   pl.BlockSpec((wc_tile, wc_cols), lambda b, c: (wc_index(b, c), 0))],
        out_shape=[jax.ShapeDtypeStruct((batch * seq, SSD_WIDTH), BF16),
                   jax.ShapeDtypeStruct((wc_rows, wc_cols), BF16)],
        scratch_shapes=[pltpu.VMEM((SSD_STATE, SSD_WIDTH), F32),
                        pltpu.VMEM((16, SSD_CONV_CH), F32),
                        pltpu.VMEM((L, SSD_WIDTH), F32)],
        compiler_params=_params(("arbitrary", "arbitrary"), nbytes),
        name="ssd_mixer",
    )(x, small, conv_w.reshape(SSD_CONV, SSD_CONV_CH), conv_b.reshape(1, -1), dtb, alog, dexp,
      norm_gain.reshape(1, -1), expand, shift, wcast_all)


def _xattn_body(h_ref, kv_ref, wq_ref, wo_ref, lnx_ref, qg_ref, kg_ref, lnf_ref, o_ref, hf_ref, att_ref):
    hd = X_HEAD_DIM
    mem_len = kv_ref.shape[0]
    half = mem_len // 2
    ones = jnp.ones((mem_len, V7X_LANES), BF16)
    h = h_ref[...]
    q = _dot(_rms(h, lnx_ref[...]).astype(BF16), wq_ref[...])
    for i in range(X_HEADS):
        qh = _rms(q[:, hd * i:hd * (i + 1)], qg_ref[...]).astype(BF16)
        kh = _rms(kv_ref[:, hd * i:hd * (i + 1)], kg_ref[...]).astype(BF16)
        vh = kv_ref[:, X_WIDTH + hd * i:X_WIDTH + hd * (i + 1)].astype(BF16)
        s = _dot_nt(qh, kh) * (hd ** -0.5)
        s_lo, s_hi = s[:, :half], s[:, half:]
        mx = jnp.broadcast_to(jnp.max(jnp.maximum(s_lo, s_hi), axis=-1, keepdims=True), s_lo.shape)
        p = jnp.concatenate([jnp.exp(s_lo - mx), jnp.exp(s_hi - mx)], axis=1).astype(BF16)
        att_ref[:, hd * i:hd * (i + 1)] = (_dot(p, vh) / _dot(p, ones)).astype(BF16)
    h_new = h + _dot(att_ref[...], wo_ref[...])
    o_ref[...] = h_new
    hf_ref[...] = _rms(h_new, lnf_ref[...]).astype(hf_ref.dtype)


def cross_attention_block(h, kv, w_q, w_o, ln_x, q_gain, k_gain, ln_ffn, batch, seq, mem_len, tq=256):
    m, d = h.shape
    tq = min(tq, seq)
    nq = seq // tq
    assert mem_len == 2 * V7X_LANES and X_HEAD_DIM == V7X_LANES
    const = lambda shape: pl.BlockSpec(shape, lambda b, t: (0,) * len(shape))
    row_blk = lambda width: pl.BlockSpec((tq, width), lambda b, t: (b * nq + t, 0))
    nbytes = (2 * _nbytes((tq, d), F32) + _nbytes((tq, d), BF16) + _nbytes((mem_len, 2 * X_WIDTH), F32)
              + 2 * _nbytes((d, X_WIDTH), BF16) + 2 * _nbytes((tq, d), F32))
    return pl.pallas_call(
        _xattn_body,
        grid=(batch, nq),
        in_specs=[row_blk(d),
                  pl.BlockSpec((mem_len, 2 * X_WIDTH), lambda b, t: (b, 0)),
                  const((d, X_WIDTH)), const((X_WIDTH, d)), const((1, d)),
                  const((1, X_HEAD_DIM)), const((1, X_HEAD_DIM)), const((1, d))],
        out_specs=[row_blk(d), row_blk(d)],
        out_shape=[jax.ShapeDtypeStruct((m, d), F32), jax.ShapeDtypeStruct((m, d), BF16)],
        scratch_shapes=[pltpu.VMEM((tq, X_WIDTH), BF16)],
        compiler_params=_params(("parallel", "parallel"), nbytes),
        name="xattn_block",
    )(h, kv, w_q, w_o, ln_x.reshape(1, -1), q_gain.reshape(1, -1), k_gain.reshape(1, -1), ln_ffn.reshape(1, -1))


_IN_OFF = np.cumsum([0, SWA_WIDTH, SWA_KV_WIDTH, SWA_KV_WIDTH, GLA_KEY_WIDTH, GLA_KEY_WIDTH, GLA_WIDTH,
                     GLA_WIDTH, GLA_GATE_RANK, SSD_WIDTH, SSD_CONV_CH, SSD_HEADS]).tolist()
SWA_COL0, GLA_COL0, GLOW_COL0, SSD_COL0, DT_COL0 = _IN_OFF[0], _IN_OFF[3], _IN_OFF[7], _IN_OFF[8], _IN_OFF[10]


IN_DIM = _IN_OFF[-1]
SMALL_ROW_STARTS = (GLOW_COL0, IN_DIM - V7X_LANES)
assert SMALL_ROW_STARTS[1] + SSD_DT_LANE0 == DT_COL0


def kernel(x, mem, rel_bias, ln_mix, w_in, swa_q_gain, swa_k_gain, swa_sinks, swa_out_gain, gla_w_gk_up, gla_b_gk_up, gla_norm_gain, ssd_conv_w, ssd_conv_b, ssd_dt_bias, ssd_a_log, ssd_d, ssd_norm_gain, w_mix_out, ln_x, ln_mem, x_w_q, x_w_k, x_w_v, x_w_o, x_q_gain, x_k_gain, ln_ffn, ffn_w_gate, ffn_w_up, ffn_w_down):
    batch, seq, d = x.shape
    mem_len = mem.shape[1]
    m = batch * seq
    band_bias = swa_band_bias(rel_bias)
    h = x.reshape(m, d)
    mem2 = mem.reshape(batch * mem_len, d)
    w_in_t = jnp.swapaxes(w_in, 1, 2)
    for l in range(DEPTH):
        w_up_pad = jnp.pad(gla_w_gk_up[l], ((0, V7X_LANES - GLA_GATE_RANK), (0, 0)))
        w_kv = jnp.concatenate([x_w_k[l], x_w_v[l]], axis=1).astype(BF16)
        hn, p_small = rmsnorm_small(h, ln_mix[l], w_in_t, l, SMALL_ROW_STARTS)
        proj = functools.partial(slab_matmul, [hn], w_in_t, l, features_on_rows=True, out_dtype=BF16)
        p_swa = proj(SWA_COL0, 1, GLA_COL0 - SWA_COL0, tm=512, name="proj_swa")
        p_gla = proj(GLA_COL0, (GLOW_COL0 - GLA_COL0) // PROJ_SLAB, PROJ_SLAB, name="proj_gla")
        p_ssd = proj(SSD_COL0, (DT_COL0 - SSD_COL0) // PROJ_SLAB, PROJ_SLAB, name="proj_ssd")
        y_a = swa_mixer(p_swa, band_bias, swa_sinks[l], swa_q_gain[l], swa_k_gain[l], swa_out_gain[l], batch, seq)
        y_b = gla_mixer(p_gla, p_small, w_up_pad, gla_b_gk_up[l], gla_norm_gain[l], batch, seq)
        y_c, w_down16 = ssd_mixer(p_ssd, p_small, ssd_conv_w[l], ssd_conv_b[l], ssd_dt_bias[l], ssd_a_log[l],
                                  ssd_d[l], ssd_norm_gain[l], batch, seq, ffn_w_down, l)
        h = slab_matmul([y_a, y_b, y_c], w_mix_out, l, 0, d // MIX_SLAB, MIX_SLAB, features_on_rows=False,
                        out_dtype=F32, res=h, tm=MIX_TOKENS, name="mix_out")
        memn = rmsnorm(mem2, ln_mem[l])
        kv = matmul(memn, w_kv, out_dtype=F32, tm=1024, tn=512, name="xattn_kv")
        h, hf = cross_attention_block(h, kv, x_w_q[l].astype(BF16), x_w_o[l].astype(BF16), ln_x[l],
                                      x_q_gain[l], x_k_gain[l], ln_ffn[l], batch, seq, mem_len)
        hidden = ffn_gate_up(hf, ffn_w_gate, ffn_w_up, l)
        h = matmul(hidden, w_down16, out_dtype=F32, res=h, tm=512, tn=512, name="ffn_down")
    return h.reshape(batch, seq, d)
```

```python
import functools
import math

import numpy as np
import jax
import jax.numpy as jnp
from jax import lax
from jax.experimental import pallas as pl
from jax.experimental.pallas import tpu as pltpu

F32 = jnp.float32
BF16 = jnp.bfloat16

D_MODEL = 4096
DEPTH = 2
EPS = 1e-6
SWA_WIDTH = 1024
SWA_HEAD_DIM = 64
SWA_HEADS = 16
SWA_KV_HEADS = 2
SWA_GROUP = SWA_HEADS // SWA_KV_HEADS
SWA_KV_WIDTH = SWA_KV_HEADS * SWA_HEAD_DIM
SWA_WINDOW = 128
SWA_BLOCK = 128
REL_BUCKETS = 32
REL_MAX_DIST = 128
GLA_WIDTH = 1024
GLA_HEADS = 4
GLA_VAL_DIM = 256
GLA_KEY_DIM = 128
GLA_KEY_WIDTH = GLA_HEADS * GLA_KEY_DIM
GLA_GATE_RANK = 16
GLA_GATE_NORMALIZER = 16.0
GLA_CHUNK = 64
SSD_WIDTH = 2048
SSD_HEAD_DIM = 64
SSD_HEADS = 32
SSD_GROUPS = 8
SSD_HEADS_PER_GROUP = SSD_HEADS // SSD_GROUPS
SSD_STATE = 128
SSD_CONV = 4
SSD_CHUNK = 128
SSD_BC_WIDTH = SSD_GROUPS * SSD_STATE
SSD_CONV_CH = SSD_WIDTH + 2 * SSD_BC_WIDTH
SSD_GROUP_WIDTH = SSD_WIDTH // SSD_GROUPS
X_HEADS = 4
X_HEAD_DIM = 128
X_WIDTH = X_HEADS * X_HEAD_DIM
FFN_HIDDEN = 11008

V7X_LANES = 128
V7X_SUBLANES = 8
V7X_VMEM_BYTES = 64 * 1024 * 1024
VMEM_REQUEST_CAP = (V7X_VMEM_BYTES * 7) // 8
V7X_MXU_WIDTH = 256
FFN_SLAB = 2 * V7X_MXU_WIDTH
PROJ_SLAB = 4 * V7X_MXU_WIDTH
MIX_SLAB = 4 * V7X_MXU_WIDTH
MIX_TOKENS = 512
GLA_TIME_BLOCK = 256
SSD_DT_LANE0 = V7X_LANES - SSD_HEADS


VMEM_TEMP_ALLOWANCE = 16 * 1024 * 1024


def _params(semantics, block_bytes):
    limit = min(VMEM_REQUEST_CAP, 2 * block_bytes + VMEM_TEMP_ALLOWANCE)
    return pltpu.CompilerParams(dimension_semantics=semantics, vmem_limit_bytes=int(limit))


def _nbytes(shape, dtype):
    return int(np.prod(shape)) * jnp.dtype(dtype).itemsize


def _split_bf16(x, terms):
    parts = []
    r = x
    for t in range(terms):
        p = r.astype(BF16)
        parts.append(p)
        if t + 1 < terms:
            r = r - p.astype(F32)
    return parts


def _dot(a, b):
    return jnp.dot(a, b, preferred_element_type=F32)


def _dot_nt(a, b):
    return lax.dot_general(a, b, (((1,), (1,)), ((), ())), preferred_element_type=F32)


def _dot_tn(a, b):
    return lax.dot_general(a, b, (((0,), (0,)), ((), ())), preferred_element_type=F32)


def _dot_exact_lhs(a_bf16, x, terms):
    acc = None
    for p in _split_bf16(x, terms):
        d = _dot(a_bf16, p)
        acc = d if acc is None else acc + d
    return acc


def _dot_exact_rhs(x, b_bf16, terms):
    acc = None
    for p in _split_bf16(x, terms):
        d = _dot(p, b_bf16)
        acc = d if acc is None else acc + d
    return acc


def _silu(x):
    return x / (1.0 + jnp.exp(-x))


def _softplus(x):
    return jnp.maximum(x, 0.0) + jnp.log1p(jnp.exp(-jnp.abs(x)))


def _rms(x, gain):
    return x * lax.rsqrt(jnp.mean(x * x, axis=-1, keepdims=True) + EPS) * gain


def _rmsnorm_body(x_ref, g_ref, o_ref):
    o_ref[...] = _rms(x_ref[...], g_ref[...]).astype(o_ref.dtype)


def rmsnorm(x, gain, tm=256):
    m, d = x.shape
    tm = min(tm, m)
    return pl.pallas_call(
        _rmsnorm_body,
        grid=(m // tm,),
        in_specs=[pl.BlockSpec((tm, d), lambda i: (i, 0)),
                  pl.BlockSpec((1, d), lambda i: (0, 0))],
        out_specs=pl.BlockSpec((tm, d), lambda i: (i, 0)),
        out_shape=jax.ShapeDtypeStruct((m, d), BF16),
        compiler_params=_params(("parallel",), _nbytes((tm, d), F32) + _nbytes((tm, d), BF16)),
        name="rmsnorm",
    )(x, gain.reshape(1, d))


def _rmsnorm_small_body(x_ref, g_ref, wa_ref, wb_ref, o_ref, small_ref, wt_ref):
    @pl.when(pl.program_id(0) == 0)
    def _():
        for t, w_ref in enumerate((wa_ref, wb_ref)):
            for k0 in range(0, w_ref.shape[2], _XPOSE_COLS):
                wt_ref[k0:k0 + _XPOSE_COLS, V7X_LANES * t:V7X_LANES * (t + 1)] = (
                    w_ref[0, :, k0:k0 + _XPOSE_COLS].T.astype(BF16))

    hn = _rms(x_ref[...], g_ref[...]).astype(BF16)
    o_ref[...] = hn
    small_ref[...] = _dot(hn, wt_ref[...])


def rmsnorm_small(x, gain, wt_all, layer, row_starts, tm=512):
    m, d = x.shape
    tm = min(tm, m)
    w_tile = lambda r0: pl.BlockSpec((pl.Element(1), pl.Element(V7X_LANES), pl.Element(d)),
                                     lambda i: (layer, r0, 0))
    nbytes = (_nbytes((tm, d), F32) + _nbytes((tm, d), BF16) + 2 * _nbytes((V7X_LANES, d), F32)
              + _nbytes((d, 2 * V7X_LANES), BF16) + _nbytes((tm, 2 * V7X_LANES), F32))
    return pl.pallas_call(
        _rmsnorm_small_body,
        grid=(m // tm,),
        in_specs=[pl.BlockSpec((tm, d), lambda i: (i, 0)),
                  pl.BlockSpec((1, d), lambda i: (0, 0)),
                  w_tile(row_starts[0]), w_tile(row_starts[1])],
        out_specs=[pl.BlockSpec((tm, d), lambda i: (i, 0)),
                   pl.BlockSpec((tm, 2 * V7X_LANES), lambda i: (i, 0))],
        out_shape=[jax.ShapeDtypeStruct((m, d), BF16), jax.ShapeDtypeStruct((m, 2 * V7X_LANES), F32)],
        scratch_shapes=[pltpu.VMEM((d, 2 * V7X_LANES), BF16)],
        compiler_params=_params(("arbitrary",), nbytes),
        name="rmsnorm_small",
    )(x, gain.reshape(1, d), wt_all, wt_all)


def _mm_body(*refs, has_res):
    a_ref, w_ref = refs[0], refs[1]
    o_ref = refs[2 + has_res]
    part = _dot(a_ref[...], w_ref[...])
    if has_res:
        part = part + refs[2][...]
    o_ref[...] = part.astype(o_ref.dtype)


def matmul(a, w, *, out_dtype, res=None, tm, tn, name="matmul"):
    m, kdim = a.shape
    n = w.shape[1]
    tm, tn = min(tm, m), min(tn, n)
    assert m % tm == 0 and n % tn == 0
    has_res = res is not None
    in_specs = [pl.BlockSpec((tm, kdim), lambda i, j: (i, 0)), pl.BlockSpec((kdim, tn), lambda i, j: (0, j))]
    args = [a, w]
    nbytes = _nbytes((tm, kdim), BF16) + _nbytes((kdim, tn), BF16) + _nbytes((tm, tn), out_dtype) + _nbytes((tm, tn), F32)
    if has_res:
        in_specs.append(pl.BlockSpec((tm, tn), lambda i, j: (i, j)))
        args.append(res)
        nbytes += _nbytes((tm, tn), F32)
    return pl.pallas_call(
        functools.partial(_mm_body, has_res=has_res),
        grid=(m // tm, n // tn),
        in_specs=in_specs,
        out_specs=pl.BlockSpec((tm, tn), lambda i, j: (i, j)),
        out_shape=jax.ShapeDtypeStruct((m, n), out_dtype),
        compiler_params=_params(("parallel", "parallel"), nbytes),
        name=name,
    )(*args)


_XPOSE_ROWS, _XPOSE_COLS = 256, 512


def _slab_copy(w_hbm, stage_ref, sem, layer, slab, start0, tn, features_on_rows):
    if features_on_rows:
        src = w_hbm.at[layer, pl.ds(pl.multiple_of(start0 + slab * tn, V7X_SUBLANES), tn), :]
    else:
        src = w_hbm.at[layer, :, pl.ds(pl.multiple_of(start0 + slab * tn, V7X_LANES), tn)]
    return pltpu.make_async_copy(src, stage_ref, sem.at[0])


def _slab_body(*refs, n_a, has_res, layer, start0, features_on_rows):
    a_refs, w_hbm = refs[:n_a], refs[n_a]
    r_ref = refs[n_a + 1] if has_res else None
    o_ref, stage_ref, wb_ref, sem = refs[n_a + 1 + has_res:]
    kdim, tn = wb_ref.shape
    slab, i = pl.program_id(0), pl.program_id(1)
    copy = lambda s: _slab_copy(w_hbm, stage_ref, sem, layer, s, start0, tn, features_on_rows)

    @pl.when(i == 0)
    def _():
        @pl.when(slab == 0)
        def _():
            copy(slab).start()

        copy(slab).wait()
        if features_on_rows:
            for r0 in range(0, tn, _XPOSE_ROWS):
                r1 = min(r0 + _XPOSE_ROWS, tn)
                for k0 in range(0, kdim, _XPOSE_COLS):
                    wb_ref[k0:k0 + _XPOSE_COLS, r0:r1] = stage_ref[r0:r1, k0:k0 + _XPOSE_COLS].T.astype(BF16)
        else:
            wb_ref[...] = stage_ref[...].astype(BF16)

    @pl.when((i == 1) & (slab + 1 < pl.num_programs(0)))
    def _():
        copy(slab + 1).start()

    acc = r_ref[...] if has_res else None
    k0 = 0
    for a_ref in a_refs:
        k1 = k0 + a_ref.shape[1]
        part = _dot(a_ref[...], wb_ref[k0:k1, :])
        acc = part if acc is None else acc + part
        k0 = k1
    o_ref[...] = acc.astype(o_ref.dtype)


def slab_matmul(a_list, w_all, layer, start0, n_slabs, tn, *, features_on_rows, out_dtype, res=None,
                tm=1024, name="slab_matmul"):
    m = a_list[0].shape[0]
    kdim = sum(a.shape[1] for a in a_list)
    tm = min(tm, m)
    assert m // tm >= 2
    has_res = res is not None
    in_specs = [pl.BlockSpec((tm, a.shape[1]), lambda s, i: (i, 0)) for a in a_list]
    in_specs.append(pl.BlockSpec(memory_space=pl.ANY))
    args = list(a_list) + [w_all]
    nbytes = (2 * _nbytes((tm, kdim), BF16) + _nbytes((kdim, tn), F32) + _nbytes((kdim, tn), BF16)
              + 2 * _nbytes((tm, tn), out_dtype) + 4 * _nbytes((tm, tn), F32))
    if has_res:
        in_specs.append(pl.BlockSpec((tm, tn), lambda s, i: (i, s)))
        args.append(res)
        nbytes += 2 * _nbytes((tm, tn), F32)
    return pl.pallas_call(
        functools.partial(_slab_body, n_a=len(a_list), has_res=has_res, layer=layer, start0=start0,
                          features_on_rows=features_on_rows),
        grid=(n_slabs, m // tm),
        in_specs=in_specs,
        out_specs=pl.BlockSpec((tm, tn), lambda s, i: (i, s)),
        out_shape=jax.ShapeDtypeStruct((m, n_slabs * tn), out_dtype),
        scratch_shapes=[pltpu.VMEM((tn, kdim) if features_on_rows else (kdim, tn), F32),
                        pltpu.VMEM((kdim, tn), BF16),
                        pltpu.SemaphoreType.DMA((1,))],
        compiler_params=pltpu.CompilerParams(dimension_semantics=("arbitrary", "arbitrary"),
                                             vmem_limit_bytes=min(VMEM_REQUEST_CAP, nbytes)),
        name=name,
    )(*args)


def _gateup_copies(wg_hbm, wu_hbm, stage_ref, sem, layer, slab, width):
    c0 = pl.multiple_of(slab * FFN_SLAB, FFN_SLAB)
    return [pltpu.make_async_copy(w_hbm.at[layer, :, pl.ds(c0, width)], stage_ref.at[k, :, pl.ds(0, width)],
                                  sem.at[k])
            for k, w_hbm in enumerate((wg_hbm, wu_hbm))]


def _gateup_body(a_ref, wg_hbm, wu_hbm, o_ref, stage_ref, wgb_ref, wub_ref, sem, *, layer, last_width):
    slab, i = pl.program_id(0), pl.program_id(1)
    last = pl.num_programs(0) - 1

    def for_slab(s, action):
        @pl.when(s < last)
        def _():
            for c in _gateup_copies(wg_hbm, wu_hbm, stage_ref, sem, layer, s, FFN_SLAB):
                action(c)

        @pl.when(s == last)
        def _():
            for c in _gateup_copies(wg_hbm, wu_hbm, stage_ref, sem, layer, s, last_width):
                action(c)

    @pl.when(i == 0)
    def _():
        @pl.when(slab == 0)
        def _():
            for_slab(slab, lambda c: c.start())

        for_slab(slab, lambda c: c.wait())
        wgb_ref[...] = stage_ref[0].astype(BF16)
        wub_ref[...] = stage_ref[1].astype(BF16)

    @pl.when((i == 1) & (slab < last))
    def _():
        for_slab(slab + 1, lambda c: c.start())

    a = a_ref[...]
    g = _dot(a, wgb_ref[...])
    u = _dot(a, wub_ref[...])
    o_ref[...] = (_silu(g) * u).astype(o_ref.dtype)


def ffn_gate_up(a, wg_all, wu_all, layer, tm=1024):
    m, kdim = a.shape
    n = wg_all.shape[2]
    tm = min(tm, m)
    n_slabs = pl.cdiv(n, FFN_SLAB)
    last_width = n - (n_slabs - 1) * FFN_SLAB
    assert m // tm >= 2 and last_width % V7X_LANES == 0
    nbytes = (2 * _nbytes((tm, kdim), BF16) + 2 * _nbytes((kdim, FFN_SLAB), F32)
              + 2 * _nbytes((kdim, FFN_SLAB), BF16) + 2 * _nbytes((tm, FFN_SLAB), BF16)
              + 6 * _nbytes((tm, FFN_SLAB), F32))
    return pl.pallas_call(
        functools.partial(_gateup_body, layer=layer, last_width=last_width),
        grid=(n_slabs, m // tm),
        in_specs=[pl.BlockSpec((tm, kdim), lambda s, i: (i, 0)),
                  pl.BlockSpec(memory_space=pl.ANY),
                  pl.BlockSpec(memory_space=pl.ANY)],
        out_specs=pl.BlockSpec((tm, FFN_SLAB), lambda s, i: (i, s)),
        out_shape=jax.ShapeDtypeStruct((m, n), BF16),
        scratch_shapes=[pltpu.VMEM((2, kdim, FFN_SLAB), F32),
                        pltpu.VMEM((kdim, FFN_SLAB), BF16), pltpu.VMEM((kdim, FFN_SLAB), BF16),
                        pltpu.SemaphoreType.DMA((2,))],
        compiler_params=pltpu.CompilerParams(dimension_semantics=("arbitrary", "arbitrary"),
                                             vmem_limit_bytes=min(VMEM_REQUEST_CAP, nbytes)),
        name="ffn_gate_up",
    )(a, wg_all, wu_all)


def _swa_body(q_ref, kvc_ref, kvp_ref, bias_ref, sink_ref, qg_ref, kg_ref, og_ref, e_ref, et_ref, o_ref, acc_ref):
    hd, blk, grp = SWA_HEAD_DIM, SWA_BLOCK, SWA_GROUP
    kband = jnp.concatenate([kvp_ref[:, :SWA_KV_WIDTH], kvc_ref[:, :SWA_KV_WIDTH]], axis=0).astype(F32)
    vband = jnp.concatenate([kvp_ref[:, SWA_KV_WIDTH:], kvc_ref[:, SWA_KV_WIDTH:]], axis=0)
    ones = jnp.ones((2 * blk, V7X_LANES), BF16)

    q = q_ref[...].astype(F32)
    ssq = _dot_exact_rhs(q * q, e_ref[...], 2)
    inv = lax.rsqrt(ssq * (1.0 / hd) + EPS)
    qn = (q * _dot_exact_rhs(inv, et_ref[...], 3) * qg_ref[...]).astype(BF16)

    for j in range(SWA_KV_HEADS):
        kn = _rms(kband[:, hd * j:hd * (j + 1)], kg_ref[...]).astype(BF16)
        qs = jnp.concatenate([qn[:, hd * (grp * j + g):hd * (grp * j + g + 1)] for g in range(grp)], axis=0)
        s = _dot_nt(qs, kn) * (hd ** -0.5) + bias_ref[j]
        sink = sink_ref[j]
        s_prev, s_cur = s[:, :blk], s[:, blk:]
        row_max = jnp.max(jnp.maximum(s_prev, s_cur), axis=-1, keepdims=True)
        mx = jnp.maximum(jnp.broadcast_to(row_max, sink.shape), sink)
        p = jnp.concatenate([jnp.exp(s_prev - mx), jnp.exp(s_cur - mx)], axis=1).astype(BF16)
        total = _dot(p, ones) + jnp.exp(sink - mx)
        o = _dot(p, vband[:, hd * j:hd * (j + 1)]) / total[:, :hd]
        for g in range(grp):
            h = grp * j + g
            acc_ref[:, hd * h:hd * (h + 1)] = o[blk * g:blk * (g + 1), :]
    o_ref[...] = _rms(acc_ref[...], og_ref[...]).astype(o_ref.dtype)


def swa_mixer(qkv, bias, sinks, q_gain, k_gain, out_gain, batch, seq):
    nb = seq // SWA_BLOCK
    kv_blk = SWA_WIDTH // (2 * SWA_KV_WIDTH)
    rows = SWA_GROUP * SWA_BLOCK
    no_prev = jnp.arange(2 * SWA_BLOCK) < SWA_BLOCK
    bias_g = jnp.stack([jnp.where(no_prev, -jnp.inf, bias), bias]).reshape(2, SWA_KV_HEADS, rows, 2 * SWA_BLOCK)
    sink_col = jnp.broadcast_to(jnp.repeat(sinks, SWA_BLOCK)[:, None],
                                (SWA_HEADS * SWA_BLOCK, V7X_LANES)).reshape(SWA_KV_HEADS, rows, V7X_LANES)
    head_of_col = jnp.arange(SWA_WIDTH) // SWA_HEAD_DIM
    e = (head_of_col[:, None] == jnp.arange(V7X_LANES)[None, :]).astype(BF16)
    const = lambda shape: pl.BlockSpec(shape, lambda b, n: (0,) * len(shape))
    nbytes = (_nbytes((SWA_BLOCK, SWA_WIDTH), BF16) * 2 + 2 * _nbytes((SWA_BLOCK, 2 * SWA_KV_WIDTH), BF16)
              + _nbytes(bias.shape, F32) + _nbytes((SWA_KV_HEADS, rows, V7X_LANES), F32)
              + 2 * _nbytes(e.shape, BF16) + _nbytes((SWA_BLOCK, SWA_WIDTH), F32)
              + 4 * _nbytes((rows, 2 * SWA_BLOCK), F32))
    return pl.pallas_call(
        _swa_body,
        grid=(batch, nb),
        in_specs=[pl.BlockSpec((SWA_BLOCK, SWA_WIDTH), lambda b, n: (b * nb + n, 0)),
                  pl.BlockSpec((SWA_BLOCK, 2 * SWA_KV_WIDTH), lambda b, n: (b * nb + n, kv_blk)),
                  pl.BlockSpec((SWA_BLOCK, 2 * SWA_KV_WIDTH),
                               lambda b, n: (b * nb + jnp.maximum(n - 1, 0), kv_blk)),
                  pl.BlockSpec((None,) + bias_g.shape[1:], lambda b, n: (jnp.minimum(n, 1), 0, 0, 0)),
                  const(sink_col.shape),
                  const((1, SWA_WIDTH)), const((1, SWA_HEAD_DIM)), const((1, SWA_WIDTH)),
                  const(e.shape), const(e.T.shape)],
        out_specs=pl.BlockSpec((SWA_BLOCK, SWA_WIDTH), lambda b, n: (b * nb + n, 0)),
        out_shape=jax.ShapeDtypeStruct((batch * seq, SWA_WIDTH), BF16),
        scratch_shapes=[pltpu.VMEM((SWA_BLOCK, SWA_WIDTH), F32)],
        compiler_params=_params(("parallel", "parallel"), nbytes),
        name="swa_mixer",
    )(qkv, qkv, qkv, bias_g, sink_col, jnp.tile(q_gain, SWA_HEADS).reshape(1, -1), k_gain.reshape(1, -1),
      out_gain.reshape(1, -1), e, e.T)


def _t5_bucket(dist):
    n = jnp.maximum(dist, 0)
    max_exact = REL_BUCKETS // 2
    nf = jnp.maximum(n, 1).astype(F32)
    large = max_exact + (jnp.log(nf / max_exact) / math.log(REL_MAX_DIST / max_exact)
                         * (REL_BUCKETS - max_exact)).astype(jnp.int32)
    large = jnp.minimum(large, REL_BUCKETS - 1)
    return jnp.where(n < max_exact, n, large)


def swa_band_bias(rel_bias):
    i = jnp.arange(SWA_BLOCK, dtype=jnp.int32)[:, None]
    j = jnp.arange(2 * SWA_BLOCK, dtype=jnp.int32)[None, :]
    dist = i + SWA_BLOCK - j
    onehot = _t5_bucket(dist)[None] == jnp.arange(REL_BUCKETS, dtype=jnp.int32)[:, None, None]
    bias = jnp.sum(jnp.where(onehot[:, None], rel_bias[:, :, None, None], 0.0), axis=0)
    in_window = (dist >= 0) & (dist < SWA_WINDOW)
    return jnp.where(in_window[None], bias, -jnp.inf)


def _gla_body(x_ref, gl_ref, wup_ref, bup_ref, gain_ref, wq_ref, wk_ref, wv_ref, wo_ref,
              o_ref, wq16_ref, wkv16_ref, wo16_ref, state_ref):
    tb = x_ref.shape[0]
    dk, dv, c = GLA_KEY_DIM, GLA_VAL_DIM, GLA_CHUNK

    wq16_ref[...] = wq_ref[...].astype(BF16)
    wkv16_ref[:, :X_WIDTH] = wk_ref[...].astype(BF16)
    wkv16_ref[:, X_WIDTH:] = wv_ref[...].astype(BF16)
    wo16_ref[...] = wo_ref[...].astype(BF16)

    @pl.when(pl.program_id(1) == 0)
    def _():
        state_ref[...] = jnp.zeros_like(state_ref)

    gl_hi, gl_lo = _split_bf16(gl_ref[...], 2)
    w_hi, w_lo = _split_bf16(wup_ref[...], 2)
    pre = _dot(gl_hi, w_hi) + _dot(gl_hi, w_lo) + _dot(gl_lo, w_hi) + bup_ref[...]
    g = (jnp.minimum(pre, 0.0) - jnp.log1p(jnp.exp(-jnp.abs(pre)))) * (1.0 / GLA_GATE_NORMALIZER)

    row = lax.broadcasted_iota(jnp.int32, (tb, tb), 0)
    colm = lax.broadcasted_iota(jnp.int32, (tb, tb), 1)
    same_chunk_lower = ((row // c) == (colm // c)) & (colm <= row)
    bcum_all = _dot_exact_lhs(same_chunk_lower.astype(BF16), g, 3)

    ri = lax.broadcasted_iota(jnp.int32, (c, c), 0)
    ci = lax.broadcasted_iota(jnp.int32, (c, c), 1)
    causal = ci <= ri
    for ch in range(tb // c):
        r0 = ch * c
        for h in range(GLA_HEADS):
            bcum = bcum_all[r0:r0 + c, dk * h:dk * (h + 1)]
            blast = bcum[c - 1:c, :]
            q = x_ref[r0:r0 + c, dk * h:dk * (h + 1)].astype(F32)
            k = x_ref[r0:r0 + c, GLA_KEY_WIDTH + dk * h:GLA_KEY_WIDTH + dk * (h + 1)].astype(F32)
            v = x_ref[r0:r0 + c, 2 * GLA_KEY_WIDTH + dv * h:2 * GLA_KEY_WIDTH + dv * (h + 1)]
            r = x_ref[r0:r0 + c, 2 * GLA_KEY_WIDTH + GLA_WIDTH + dv * h:
                      2 * GLA_KEY_WIDTH + GLA_WIDTH + dv * (h + 1)].astype(F32)
            qd = (q * (dk ** -0.5) * jnp.exp(bcum)).astype(BF16)
            kd = (k * jnp.exp(-bcum)).astype(BF16)
            kl = (k * jnp.exp(blast - bcum)).astype(BF16)
            att = jnp.where(causal, _dot_nt(qd, kd), 0.0)
            st = state_ref[h]
            o = _dot(att.astype(BF16), v) + _dot_nt(qd, st.astype(BF16))
            state_ref[h] = st * jnp.exp(blast) + _dot_tn(v, kl)
            o = _rms(o, gain_ref[...]) * _silu(r)
            o_ref[r0:r0 + c, dv * h:dv * (h + 1)] = o.astype(o_ref.dtype)


def gla_mixer(x, small, w_up_pad, b_up, norm_gain, batch, seq, x_w_q, x_w_k, x_w_v, x_w_o, layer):
    tb = min(GLA_TIME_BLOCK, seq)
    nt = seq // tb
    steps = batch * nt
    width = x.shape[1]
    d, xw = x_w_q.shape[1:]
    assert d % (steps * V7X_SUBLANES) == 0 and xw % (steps * 2 * V7X_SUBLANES) == 0
    row_blk = lambda rows, cols: pl.BlockSpec((rows // steps, cols), lambda b, t: (b * nt + t, 0))
    layer_blk = lambda rows, cols: pl.BlockSpec((None, rows // steps, cols), lambda b, t: (layer, b * nt + t, 0))
    nbytes = (_nbytes((tb, width), BF16) + _nbytes((tb, V7X_LANES), F32) + _nbytes(w_up_pad.shape, F32)
              + _nbytes((tb, GLA_WIDTH), BF16) + _nbytes((GLA_HEADS, GLA_VAL_DIM, GLA_KEY_DIM), F32)
              + 8 * _nbytes((tb, GLA_KEY_WIDTH), F32) + 6 * _nbytes((d // steps, xw), F32))
    return pl.pallas_call(
        _gla_body,
        grid=(batch, nt),
        in_specs=[pl.BlockSpec((tb, width), lambda b, t: (b * nt + t, 0)),
                  pl.BlockSpec((tb, V7X_LANES), lambda b, t: (b * nt + t, 0)),
                  pl.BlockSpec(w_up_pad.shape, lambda b, t: (0, 0)),
                  pl.BlockSpec((1, GLA_KEY_WIDTH), lambda b, t: (0, 0)),
                  pl.BlockSpec((1, GLA_VAL_DIM), lambda b, t: (0, 0)),
                  layer_blk(d, xw), layer_blk(d, xw), layer_blk(d, xw), layer_blk(xw, d)],
        out_specs=[pl.BlockSpec((tb, GLA_WIDTH), lambda b, t: (b * nt + t, 0)),
                   row_blk(d, xw), row_blk(d, 2 * xw), row_blk(xw, d)],
        out_shape=[jax.ShapeDtypeStruct((batch * seq, GLA_WIDTH), BF16),
                   jax.ShapeDtypeStruct((d, xw), BF16), jax.ShapeDtypeStruct((d, 2 * xw), BF16),
                   jax.ShapeDtypeStruct((xw, d), BF16)],
        scratch_shapes=[pltpu.VMEM((GLA_HEADS, GLA_VAL_DIM, GLA_KEY_DIM), F32)],
        compiler_params=_params(("arbitrary", "arbitrary"), nbytes),
        name="gla_mixer",
    )(x, small, w_up_pad, b_up.reshape(1, -1), norm_gain.reshape(1, -1), x_w_q, x_w_k, x_w_v, x_w_o)


def _ssd_body(x_ref, dt_ref, cw_ref, cb_ref, dtb_ref, alog_ref, dexp_ref, gain_ref, expand_ref, shift_ref,
              wcast_in_ref, o_ref, wcast_out_ref, state_ref, tail_ref, y_ref, *, wcast_blocks):
    L, P, N = SSD_CHUNK, SSD_HEAD_DIM, SSD_STATE
    gw = SSD_GROUP_WIDTH

    @pl.when(pl.program_id(0) * pl.num_programs(1) + pl.program_id(1) < wcast_blocks)
    def _():
        wcast_out_ref[...] = wcast_in_ref[...].astype(BF16)

    @pl.when(pl.program_id(1) == 0)
    def _():
        state_ref[...] = jnp.zeros_like(state_ref)
        tail_ref[...] = jnp.zeros_like(tail_ref)

    xin = x_ref[:, SSD_WIDTH:]
    xin32 = xin.astype(F32)
    shifted = _dot(shift_ref[...], xin)
    conv = cb_ref[...] + cw_ref[SSD_CONV - 1:SSD_CONV, :] * xin32
    head = jnp.zeros((8, SSD_CONV_CH), F32)
    for j in range(1, SSD_CONV):
        wj = cw_ref[SSD_CONV - 1 - j:SSD_CONV - j, :]
        conv = conv + wj * shifted[L * (j - 1):L * j, :]
        head = head + wj * tail_ref[8 - j:16 - j, :]
    conv = jnp.concatenate([conv[0:8, :] + head, conv[8:, :]], axis=0)
    tail_ref[0:8, :] = xin32[L - 8:, :]
    xbc = _silu(conv)
    xs = xbc[:, :SSD_WIDTH]
    bm = xbc[:, SSD_WIDTH:SSD_WIDTH + SSD_BC_WIDTH].astype(BF16)
    cm = xbc[:, SSD_WIDTH + SSD_BC_WIDTH:].astype(BF16)

    dt = _softplus(dt_ref[...] + dtb_ref[...])
    dta = dt * (-jnp.exp(alog_ref[...]))
    ri = lax.broadcasted_iota(jnp.int32, (L, L), 0)
    ci = lax.broadcasted_iota(jnp.int32, (L, L), 1)
    causal = ci <= ri
    a_cum = _dot_exact_lhs(causal.astype(BF16), dta, 3)
    a_cum_t = a_cum.T
    expand = expand_ref[...]
    a_exp = _dot_exact_rhs(a_cum, expand, 3)
    dt_exp = _dot_exact_rhs(dt, expand, 2)
    a_last = a_exp[L - 1:L, :]
    xd = xs * dt_exp
    xdec = (xd * jnp.exp(a_last - a_exp)).astype(BF16)
    xd16 = xd.astype(BF16)
    out_scale = jnp.exp(a_exp)

    for g in range(SSD_GROUPS):
        bg = bm[:, N * g:N * (g + 1)]
        cg = cm[:, N * g:N * (g + 1)]
        cb = _dot_nt(cg, bg)
        for kk in range(SSD_HEADS_PER_GROUP):
            h = g * SSD_HEADS_PER_GROUP + kk
            hl = SSD_DT_LANE0 + h
            diff = a_cum[:, hl:hl + 1] - a_cum_t[hl:hl + 1, :]
            m = cb * jnp.exp(jnp.where(causal, diff, -jnp.inf))
            y_ref[:, P * h:P * (h + 1)] = _dot(m.astype(BF16), xd16[:, P * h:P * (h + 1)])
        sg = state_ref[:, gw * g:gw * (g + 1)]
        y_off = _dot(cg, sg.astype(BF16)) * out_scale[:, gw * g:gw * (g + 1)]
        state_ref[:, gw * g:gw * (g + 1)] = (sg * jnp.exp(a_last[:, gw * g:gw * (g + 1)])
                                             + _dot_tn(bg, xdec[:, gw * g:gw * (g + 1)]))
        yg = y_ref[:, gw * g:gw * (g + 1)] + y_off + xs[:, gw * g:gw * (g + 1)] * dexp_ref[:, gw * g:gw * (g + 1)]
        yg = yg * _silu(x_ref[:, gw * g:gw * (g + 1)].astype(F32))
        o_ref[:, gw * g:gw * (g + 1)] = _rms(yg, gain_ref[:, gw * g:gw * (g + 1)]).astype(o_ref.dtype)


def ssd_mixer(x, small, conv_w, conv_b, dt_bias, a_log, d_skip, norm_gain, batch, seq, wcast_all, layer):
    L = SSD_CHUNK
    nc = seq // L
    width = x.shape[1]
    _, wc_rows, wc_cols = wcast_all.shape
    wc_tile = V7X_MXU_WIDTH
    wc_blocks = wc_rows // wc_tile
    assert wc_rows % wc_tile == 0 and wc_blocks <= batch * nc
    wc_index = lambda b, c: jnp.minimum(b * nc + c, wc_blocks - 1)
    t = jnp.arange(L)
    shift = jnp.concatenate([(t[:, None] - j == t[None, :]) for j in range(1, SSD_CONV)], axis=0).astype(BF16)
    lane_pad = (SSD_DT_LANE0, V7X_LANES - SSD_DT_LANE0 - SSD_HEADS)
    dtb = jnp.pad(dt_bias, lane_pad).reshape(1, V7X_LANES)
    alog = jnp.pad(a_log, lane_pad).reshape(1, V7X_LANES)
    dexp = jnp.repeat(d_skip, SSD_HEAD_DIM).reshape(1, SSD_WIDTH)
    expand = (jnp.arange(V7X_LANES)[:, None] - SSD_DT_LANE0
              == (jnp.arange(SSD_WIDTH)[None, :] // SSD_HEAD_DIM)).astype(BF16)
    nbytes = (_nbytes((L, width), BF16) + _nbytes((L, V7X_LANES), F32) + _nbytes((L, SSD_WIDTH), BF16)
              + _nbytes(expand.shape, BF16) + 2 * _nbytes((SSD_STATE, SSD_WIDTH), F32)
              + 16 * _nbytes((L, SSD_CONV_CH), F32) + _nbytes((wc_tile, wc_cols), F32)
              + _nbytes((wc_tile, wc_cols), BF16))
    return pl.pallas_call(
        functools.partial(_ssd_body, wcast_blocks=wc_blocks),
        grid=(batch, nc),
        in_specs=[pl.BlockSpec((L, width), lambda b, c: (b * nc + c, 0)),
                  pl.BlockSpec((L, V7X_LANES), lambda b, c: (b * nc + c, 1)),
                  pl.BlockSpec((SSD_CONV, SSD_CONV_CH), lambda b, c: (0, 0)),
                  pl.BlockSpec((1, SSD_CONV_CH), lambda b, c: (0, 0)),
                  pl.BlockSpec((1, V7X_LANES), lambda b, c: (0, 0)),
                  pl.BlockSpec((1, V7X_LANES), lambda b, c: (0, 0)),
                  pl.BlockSpec((1, SSD_WIDTH), lambda b, c: (0, 0)),
                  pl.BlockSpec((1, SSD_WIDTH), lambda b, c: (0, 0)),
                  pl.BlockSpec(expand.shape, lambda b, c: (0, 0)),
                  pl.BlockSpec(shift.shape, lambda b, c: (0, 0)),
                  pl.BlockSpec((None, wc_tile, wc_cols), lambda b, c: (layer, wc_index(b, c), 0))],
        out_specs=[pl.BlockSpec((L, SSD_WIDTH), lambda b, c: (b * nc + c, 0)),
                   pl.BlockSpec((wc_tile, wc_cols), lambda b, c: (wc_index(b, c), 0))],
        out_shape=[jax.ShapeDtypeStruct((batch * seq, SSD_WIDTH), BF16),
                   jax.ShapeDtypeStruct((wc_rows, wc_cols), BF16)],
        scratch_shapes=[pltpu.VMEM((SSD_STATE, SSD_WIDTH), F32),
                        pltpu.VMEM((16, SSD_CONV_CH), F32),
                        pltpu.VMEM((L, SSD_WIDTH), F32)],
        compiler_params=_params(("arbitrary", "arbitrary"), nbytes),
        name="ssd_mixer",
    )(x, small, conv_w.reshape(SSD_CONV, SSD_CONV_CH), conv_b.reshape(1, -1), dtb, alog, dexp,
      norm_gain.reshape(1, -1), expand, shift, wcast_all)


def _xattn_body(h_ref, kv_ref, wq_ref, wo_ref, lnx_ref, qg_ref, kg_ref, lnf_ref, o_ref, hf_ref, att_ref):
    hd = X_HEAD_DIM
    mem_len = kv_ref.shape[0]
    half = mem_len // 2
    ones = jnp.ones((mem_len, V7X_LANES), BF16)
    h = h_ref[...]
    q = _dot(_rms(h, lnx_ref[...]).astype(BF16), wq_ref[...])
    for i in range(X_HEADS):
        qh = _rms(q[:, hd * i:hd * (i + 1)], qg_ref[...]).astype(BF16)
        kh = _rms(kv_ref[:, hd * i:hd * (i + 1)], kg_ref[...]).astype(BF16)
        vh = kv_ref[:, X_WIDTH + hd * i:X_WIDTH + hd * (i + 1)].astype(BF16)
        s = _dot_nt(qh, kh) * (hd ** -0.5)
        s_lo, s_hi = s[:, :half], s[:, half:]
        mx = jnp.broadcast_to(jnp.max(jnp.maximum(s_lo, s_hi), axis=-1, keepdims=True), s_lo.shape)
        p = jnp.concatenate([jnp.exp(s_lo - mx), jnp.exp(s_hi - mx)], axis=1).astype(BF16)
        att_ref[:, hd * i:hd * (i + 1)] = (_dot(p, vh) / _dot(p, ones)).astype(BF16)
    h_new = h + _dot(att_ref[...], wo_ref[...])
    o_ref[...] = h_new
    hf_ref[...] = _rms(h_new, lnf_ref[...]).astype(hf_ref.dtype)


def cross_attention_block(h, kv, w_q, w_o, ln_x, q_gain, k_gain, ln_ffn, batch, seq, mem_len, tq=256):
    m, d = h.shape
    tq = min(tq, seq)
    nq = seq // tq
    assert mem_len == 2 * V7X_LANES and X_HEAD_DIM == V7X_LANES
    const = lambda shape: pl.BlockSpec(shape, lambda b, t: (0,) * len(shape))
    row_blk = lambda width: pl.BlockSpec((tq, width), lambda b, t: (b * nq + t, 0))
    nbytes = (2 * _nbytes((tq, d), F32) + _nbytes((tq, d), BF16) + _nbytes((mem_len, 2 * X_WIDTH), F32)
              + 2 * _nbytes((d, X_WIDTH), BF16) + 2 * _nbytes((tq, d), F32))
    return pl.pallas_call(
        _xattn_body,
        grid=(batch, nq),
        in_specs=[row_blk(d),
                  pl.BlockSpec((mem_len, 2 * X_WIDTH), lambda b, t: (b, 0)),
                  const((d, X_WIDTH)), const((X_WIDTH, d)), const((1, d)),
                  const((1, X_HEAD_DIM)), const((1, X_HEAD_DIM)), const((1, d))],
        out_specs=[row_blk(d), row_blk(d)],
        out_shape=[jax.ShapeDtypeStruct((m, d), F32), jax.ShapeDtypeStruct((m, d), BF16)],
        scratch_shapes=[pltpu.VMEM((tq, X_WIDTH), BF16)],
        compiler_params=_params(("parallel", "parallel"), nbytes),
        name="xattn_block",
    )(h, kv, w_q, w_o, ln_x.reshape(1, -1), q_gain.reshape(1, -1), k_gain.reshape(1, -1), ln_ffn.reshape(1, -1))


_IN_OFF = np.cumsum([0, SWA_WIDTH, SWA_KV_WIDTH, SWA_KV_WIDTH, GLA_KEY_WIDTH, GLA_KEY_WIDTH, GLA_WIDTH,
                     GLA_WIDTH, GLA_GATE_RANK, SSD_WIDTH, SSD_CONV_CH, SSD_HEADS]).tolist()
SWA_COL0, GLA_COL0, GLOW_COL0, SSD_COL0, DT_COL0 = _IN_OFF[0], _IN_OFF[3], _IN_OFF[7], _IN_OFF[8], _IN_OFF[10]


IN_DIM = _IN_OFF[-1]
SMALL_ROW_STARTS = (GLOW_COL0, IN_DIM - V7X_LANES)
assert SMALL_ROW_STARTS[1] + SSD_DT_LANE0 == DT_COL0


def kernel(x, mem, rel_bias, ln_mix, w_in, swa_q_gain, swa_k_gain, swa_sinks, swa_out_gain, gla_w_gk_up, gla_b_gk_up, gla_norm_gain, ssd_conv_w, ssd_conv_b, ssd_dt_bias, ssd_a_log, ssd_d, ssd_norm_gain, w_mix_out, ln_x, ln_mem, x_w_q, x_w_k, x_w_v, x_w_o, x_q_gain, x_k_gain, ln_ffn, ffn_w_gate, ffn_w_up, ffn_w_down):
    batch, seq, d = x.shape
    mem_len = mem.shape[1]
    m = batch * seq
    band_bias = swa_band_bias(rel_bias)
    h = x.reshape(m, d)
    mem2 = mem.reshape(batch * mem_len, d)
    w_in_t = jnp.swapaxes(w_in, 1, 2)
    for l in range(DEPTH):
        w_up_pad = jnp.pad(gla_w_gk_up[l], ((0, V7X_LANES - GLA_GATE_RANK), (0, 0)))
        hn, p_small = rmsnorm_small(h, ln_mix[l], w_in_t, l, SMALL_ROW_STARTS)
        proj = functools.partial(slab_matmul, [hn], w_in_t, l, features_on_rows=True, out_dtype=BF16)
        p_swa = proj(SWA_COL0, 1, GLA_COL0 - SWA_COL0, tm=512, name="proj_swa")
        p_gla = proj(GLA_COL0, (GLOW_COL0 - GLA_COL0) // PROJ_SLAB, PROJ_SLAB, name="proj_gla")
        p_ssd = proj(SSD_COL0, (DT_COL0 - SSD_COL0) // PROJ_SLAB, PROJ_SLAB, name="proj_ssd")
        y_a = swa_mixer(p_swa, band_bias, swa_sinks[l], swa_q_gain[l], swa_k_gain[l], swa_out_gain[l], batch, seq)
        y_b, w_q16, w_kv16, w_o16 = gla_mixer(p_gla, p_small, w_up_pad, gla_b_gk_up[l], gla_norm_gain[l], batch, seq,
                                              x_w_q, x_w_k, x_w_v, x_w_o, l)
        y_c, w_down16 = ssd_mixer(p_ssd, p_small, ssd_conv_w[l], ssd_conv_b[l], ssd_dt_bias[l], ssd_a_log[l],
                                  ssd_d[l], ssd_norm_gain[l], batch, seq, ffn_w_down, l)
        h = slab_matmul([y_a, y_b, y_c], w_mix_out, l, 0, d // MIX_SLAB, MIX_SLAB, features_on_rows=False,
                        out_dtype=F32, res=h, tm=MIX_TOKENS, name="mix_out")
        memn = rmsnorm(mem2, ln_mem[l])
        kv = matmul(memn, w_kv16, out_dtype=F32, tm=1024, tn=512, name="xattn_kv")
        h, hf = cross_attention_block(h, kv, w_q16, w_o16, ln_x[l], x_q_gain[l], x_k_gain[l], ln_ffn[l],
                                      batch, seq, mem_len)
        hidden = ffn_gate_up(hf, ffn_w_gate, ffn_w_up, l)
        h = matmul(hidden, w_down16, out_dtype=F32, res=h, tm=512, tn=512, name="ffn_down")
    return h.reshape(batch, seq, d)
```

```python
import functools
import math

import numpy as np
import jax
import jax.numpy as jnp
from jax import lax
from jax.experimental import pallas as pl
from jax.experimental.pallas import tpu as pltpu

F32 = jnp.float32
BF16 = jnp.bfloat16

D_MODEL = 4096
DEPTH = 2
EPS = 1e-6
SWA_WIDTH = 1024
SWA_HEAD_DIM = 64
SWA_HEADS = 16
SWA_KV_HEADS = 2
SWA_GROUP = SWA_HEADS // SWA_KV_HEADS
SWA_KV_WIDTH = SWA_KV_HEADS * SWA_HEAD_DIM
SWA_WINDOW = 128
SWA_BLOCK = 128
REL_BUCKETS = 32
REL_MAX_DIST = 128
GLA_WIDTH = 1024
GLA_HEADS = 4
GLA_VAL_DIM = 256
GLA_KEY_DIM = 128
GLA_KEY_WIDTH = GLA_HEADS * GLA_KEY_DIM
GLA_GATE_RANK = 16
GLA_GATE_NORMALIZER = 16.0
GLA_CHUNK = 64
SSD_WIDTH = 2048
SSD_HEAD_DIM = 64
SSD_HEADS = 32
SSD_GROUPS = 8
SSD_HEADS_PER_GROUP = SSD_HEADS // SSD_GROUPS
SSD_STATE = 128
SSD_CONV = 4
SSD_CHUNK = 128
SSD_BC_WIDTH = SSD_GROUPS * SSD_STATE
SSD_CONV_CH = SSD_WIDTH + 2 * SSD_BC_WIDTH
SSD_GROUP_WIDTH = SSD_WIDTH // SSD_GROUPS
X_HEADS = 4
X_HEAD_DIM = 128
X_WIDTH = X_HEADS * X_HEAD_DIM
FFN_HIDDEN = 11008

V7X_LANES = 128
V7X_SUBLANES = 8
V7X_VMEM_BYTES = 64 * 1024 * 1024
VMEM_REQUEST_CAP = (V7X_VMEM_BYTES * 7) // 8
V7X_MXU_WIDTH = 256
FFN_SLAB = 2 * V7X_MXU_WIDTH
PROJ_SLAB = 4 * V7X_MXU_WIDTH
MIX_SLAB = 4 * V7X_MXU_WIDTH
MIX_TOKENS = 512
GLA_TIME_BLOCK = 256
SWA_BLOCKS_PER_STEP = 4
SSD_DT_LANE0 = V7X_LANES - SSD_HEADS


VMEM_TEMP_ALLOWANCE = 16 * 1024 * 1024


def _params(semantics, block_bytes):
    limit = min(VMEM_REQUEST_CAP, 2 * block_bytes + VMEM_TEMP_ALLOWANCE)
    return pltpu.CompilerParams(dimension_semantics=semantics, vmem_limit_bytes=int(limit))


def _nbytes(shape, dtype):
    return int(np.prod(shape)) * jnp.dtype(dtype).itemsize


def _split_bf16(x, terms):
    parts = []
    r = x
    for t in range(terms):
        p = r.astype(BF16)
        parts.append(p)
        if t + 1 < terms:
            r = r - p.astype(F32)
    return parts


def _dot(a, b):
    return jnp.dot(a, b, preferred_element_type=F32)


def _dot_nt(a, b):
    return lax.dot_general(a, b, (((1,), (1,)), ((), ())), preferred_element_type=F32)


def _dot_tn(a, b):
    return lax.dot_general(a, b, (((0,), (0,)), ((), ())), preferred_element_type=F32)


def _dot_exact_lhs(a_bf16, x, terms):
    acc = None
    for p in _split_bf16(x, terms):
        d = _dot(a_bf16, p)
        acc = d if acc is None else acc + d
    return acc


def _dot_exact_rhs(x, b_bf16, terms):
    acc = None
    for p in _split_bf16(x, terms):
        d = _dot(p, b_bf16)
        acc = d if acc is None else acc + d
    return acc


def _silu(x):
    return x / (1.0 + jnp.exp(-x))


def _softplus(x):
    return jnp.maximum(x, 0.0) + jnp.log1p(jnp.exp(-jnp.abs(x)))


def _rms(x, gain):
    return x * lax.rsqrt(jnp.mean(x * x, axis=-1, keepdims=True) + EPS) * gain


def _rmsnorm_body(x_ref, g_ref, o_ref):
    o_ref[...] = _rms(x_ref[...], g_ref[...]).astype(o_ref.dtype)


def rmsnorm(x, gain, tm=256):
    m, d = x.shape
    tm = min(tm, m)
    return pl.pallas_call(
        _rmsnorm_body,
        grid=(m // tm,),
        in_specs=[pl.BlockSpec((tm, d), lambda i: (i, 0)),
                  pl.BlockSpec((1, d), lambda i: (0, 0))],
        out_specs=pl.BlockSpec((tm, d), lambda i: (i, 0)),
        out_shape=jax.ShapeDtypeStruct((m, d), BF16),
        compiler_params=_params(("parallel",), _nbytes((tm, d), F32) + _nbytes((tm, d), BF16)),
        name="rmsnorm",
    )(x, gain.reshape(1, d))


def _rmsnorm_small_body(x_ref, g_ref, wa_ref, wb_ref, o_ref, small_ref, wt_ref):
    @pl.when(pl.program_id(0) == 0)
    def _():
        for t, w_ref in enumerate((wa_ref, wb_ref)):
            for k0 in range(0, w_ref.shape[2], _XPOSE_COLS):
                wt_ref[k0:k0 + _XPOSE_COLS, V7X_LANES * t:V7X_LANES * (t + 1)] = (
                    w_ref[0, :, k0:k0 + _XPOSE_COLS].T.astype(BF16))

    hn = _rms(x_ref[...], g_ref[...]).astype(BF16)
    o_ref[...] = hn
    small_ref[...] = _dot(hn, wt_ref[...])


def rmsnorm_small(x, gain, wt_all, layer, row_starts, tm=512):
    m, d = x.shape
    tm = min(tm, m)
    w_tile = lambda r0: pl.BlockSpec((pl.Element(1), pl.Element(V7X_LANES), pl.Element(d)),
                                     lambda i: (layer, r0, 0))
    nbytes = (_nbytes((tm, d), F32) + _nbytes((tm, d), BF16) + 2 * _nbytes((V7X_LANES, d), F32)
              + _nbytes((d, 2 * V7X_LANES), BF16) + _nbytes((tm, 2 * V7X_LANES), F32))
    return pl.pallas_call(
        _rmsnorm_small_body,
        grid=(m // tm,),
        in_specs=[pl.BlockSpec((tm, d), lambda i: (i, 0)),
                  pl.BlockSpec((1, d), lambda i: (0, 0)),
                  w_tile(row_starts[0]), w_tile(row_starts[1])],
        out_specs=[pl.BlockSpec((tm, d), lambda i: (i, 0)),
                   pl.BlockSpec((tm, 2 * V7X_LANES), lambda i: (i, 0))],
        out_shape=[jax.ShapeDtypeStruct((m, d), BF16), jax.ShapeDtypeStruct((m, 2 * V7X_LANES), F32)],
        scratch_shapes=[pltpu.VMEM((d, 2 * V7X_LANES), BF16)],
        compiler_params=_params(("arbitrary",), nbytes),
        name="rmsnorm_small",
    )(x, gain.reshape(1, d), wt_all, wt_all)


def _mm_body(*refs, has_res):
    a_ref, w_ref = refs[0], refs[1]
    o_ref = refs[2 + has_res]
    part = _dot(a_ref[...], w_ref[...])
    if has_res:
        part = part + refs[2][...]
    o_ref[...] = part.astype(o_ref.dtype)


def matmul(a, w, *, out_dtype, res=None, tm, tn, name="matmul"):
    m, kdim = a.shape
    n = w.shape[1]
    tm, tn = min(tm, m), min(tn, n)
    assert m % tm == 0 and n % tn == 0
    has_res = res is not None
    in_specs = [pl.BlockSpec((tm, kdim), lambda i, j: (i, 0)), pl.BlockSpec((kdim, tn), lambda i, j: (0, j))]
    args = [a, w]
    nbytes = _nbytes((tm, kdim), BF16) + _nbytes((kdim, tn), BF16) + _nbytes((tm, tn), out_dtype) + _nbytes((tm, tn), F32)
    if has_res:
        in_specs.append(pl.BlockSpec((tm, tn), lambda i, j: (i, j)))
        args.append(res)
        nbytes += _nbytes((tm, tn), F32)
    return pl.pallas_call(
        functools.partial(_mm_body, has_res=has_res),
        grid=(m // tm, n // tn),
        in_specs=in_specs,
        out_specs=pl.BlockSpec((tm, tn), lambda i, j: (i, j)),
        out_shape=jax.ShapeDtypeStruct((m, n), out_dtype),
        compiler_params=_params(("parallel", "parallel"), nbytes),
        name=name,
    )(*args)


_XPOSE_ROWS, _XPOSE_COLS = 256, 512


def _slab_copy(w_hbm, stage_ref, sem, layer, slab, start0, tn, features_on_rows):
    if features_on_rows:
        src = w_hbm.at[layer, pl.ds(pl.multiple_of(start0 + slab * tn, V7X_SUBLANES), tn), :]
    else:
        src = w_hbm.at[layer, :, pl.ds(pl.multiple_of(start0 + slab * tn, V7X_LANES), tn)]
    return pltpu.make_async_copy(src, stage_ref, sem.at[0])


def _slab_body(*refs, n_a, has_res, layer, start0, features_on_rows):
    a_refs, w_hbm = refs[:n_a], refs[n_a]
    r_ref = refs[n_a + 1] if has_res else None
    o_ref, stage_ref, wb_ref, sem = refs[n_a + 1 + has_res:]
    kdim, tn = wb_ref.shape
    slab, i = pl.program_id(0), pl.program_id(1)
    copy = lambda s: _slab_copy(w_hbm, stage_ref, sem, layer, s, start0, tn, features_on_rows)

    @pl.when(i == 0)
    def _():
        @pl.when(slab == 0)
        def _():
            copy(slab).start()

        copy(slab).wait()
        if features_on_rows:
            for r0 in range(0, tn, _XPOSE_ROWS):
                r1 = min(r0 + _XPOSE_ROWS, tn)
                for k0 in range(0, kdim, _XPOSE_COLS):
                    wb_ref[k0:k0 + _XPOSE_COLS, r0:r1] = stage_ref[r0:r1, k0:k0 + _XPOSE_COLS].T.astype(BF16)
        else:
            wb_ref[...] = stage_ref[...].astype(BF16)

    @pl.when((i == 1) & (slab + 1 < pl.num_programs(0)))
    def _():
        copy(slab + 1).start()

    acc = r_ref[...] if has_res else None
    k0 = 0
    for a_ref in a_refs:
        k1 = k0 + a_ref.shape[1]
        part = _dot(a_ref[...], wb_ref[k0:k1, :])
        acc = part if acc is None else acc + part
        k0 = k1
    o_ref[...] = acc.astype(o_ref.dtype)


def slab_matmul(a_list, w_all, layer, start0, n_slabs, tn, *, features_on_rows, out_dtype, res=None,
                tm=1024, name="slab_matmul"):
    m = a_list[0].shape[0]
    kdim = sum(a.shape[1] for a in a_list)
    tm = min(tm, m)
    assert m // tm >= 2
    has_res = res is not None
    in_specs = [pl.BlockSpec((tm, a.shape[1]), lambda s, i: (i, 0)) for a in a_list]
    in_specs.append(pl.BlockSpec(memory_space=pl.ANY))
    args = list(a_list) + [w_all]
    nbytes = (2 * _nbytes((tm, kdim), BF16) + _nbytes((kdim, tn), F32) + _nbytes((kdim, tn), BF16)
              + 2 * _nbytes((tm, tn), out_dtype) + 4 * _nbytes((tm, tn), F32))
    if has_res:
        in_specs.append(pl.BlockSpec((tm, tn), lambda s, i: (i, s)))
        args.append(res)
        nbytes += 2 * _nbytes((tm, tn), F32)
    return pl.pallas_call(
        functools.partial(_slab_body, n_a=len(a_list), has_res=has_res, layer=layer, start0=start0,
                          features_on_rows=features_on_rows),
        grid=(n_slabs, m // tm),
        in_specs=in_specs,
        out_specs=pl.BlockSpec((tm, tn), lambda s, i: (i, s)),
        out_shape=jax.ShapeDtypeStruct((m, n_slabs * tn), out_dtype),
        scratch_shapes=[pltpu.VMEM((tn, kdim) if features_on_rows else (kdim, tn), F32),
                        pltpu.VMEM((kdim, tn), BF16),
                        pltpu.SemaphoreType.DMA((1,))],
        compiler_params=pltpu.CompilerParams(dimension_semantics=("arbitrary", "arbitrary"),
                                             vmem_limit_bytes=min(VMEM_REQUEST_CAP, nbytes)),
        name=name,
    )(*args)


def _gateup_copies(wg_hbm, wu_hbm, stage_ref, sem, layer, slab, width):
    c0 = pl.multiple_of(slab * FFN_SLAB, FFN_SLAB)
    return [pltpu.make_async_copy(w_hbm.at[layer, :, pl.ds(c0, width)], stage_ref.at[k, :, pl.ds(0, width)],
                                  sem.at[k])
            for k, w_hbm in enumerate((wg_hbm, wu_hbm))]


def _gateup_body(a_ref, wg_hbm, wu_hbm, o_ref, stage_ref, wgb_ref, wub_ref, sem, *, layer, last_width):
    slab, i = pl.program_id(0), pl.program_id(1)
    last = pl.num_programs(0) - 1

    def for_slab(s, action):
        @pl.when(s < last)
        def _():
            for c in _gateup_copies(wg_hbm, wu_hbm, stage_ref, sem, layer, s, FFN_SLAB):
                action(c)

        @pl.when(s == last)
        def _():
            for c in _gateup_copies(wg_hbm, wu_hbm, stage_ref, sem, layer, s, last_width):
                action(c)

    @pl.when(i == 0)
    def _():
        @pl.when(slab == 0)
        def _():
            for_slab(slab, lambda c: c.start())

        for_slab(slab, lambda c: c.wait())
        wgb_ref[...] = stage_ref[0].astype(BF16)
        wub_ref[...] = stage_ref[1].astype(BF16)

    @pl.when((i == 1) & (slab < last))
    def _():
        for_slab(slab + 1, lambda c: c.start())

    a = a_ref[...]
    g = _dot(a, wgb_ref[...])
    u = _dot(a, wub_ref[...])
    o_ref[...] = (_silu(g) * u).astype(o_ref.dtype)


def ffn_gate_up(a, wg_all, wu_all, layer, tm=1024):
    m, kdim = a.shape
    n = wg_all.shape[2]
    tm = min(tm, m)
    n_slabs = pl.cdiv(n, FFN_SLAB)
    last_width = n - (n_slabs - 1) * FFN_SLAB
    assert m // tm >= 2 and last_width % V7X_LANES == 0
    nbytes = (2 * _nbytes((tm, kdim), BF16) + 2 * _nbytes((kdim, FFN_SLAB), F32)
              + 2 * _nbytes((kdim, FFN_SLAB), BF16) + 2 * _nbytes((tm, FFN_SLAB), BF16)
              + 6 * _nbytes((tm, FFN_SLAB), F32))
    return pl.pallas_call(
        functools.partial(_gateup_body, layer=layer, last_width=last_width),
        grid=(n_slabs, m // tm),
        in_specs=[pl.BlockSpec((tm, kdim), lambda s, i: (i, 0)),
                  pl.BlockSpec(memory_space=pl.ANY),
                  pl.BlockSpec(memory_space=pl.ANY)],
        out_specs=pl.BlockSpec((tm, FFN_SLAB), lambda s, i: (i, s)),
        out_shape=jax.ShapeDtypeStruct((m, n), BF16),
        scratch_shapes=[pltpu.VMEM((2, kdim, FFN_SLAB), F32),
                        pltpu.VMEM((kdim, FFN_SLAB), BF16), pltpu.VMEM((kdim, FFN_SLAB), BF16),
                        pltpu.SemaphoreType.DMA((2,))],
        compiler_params=pltpu.CompilerParams(dimension_semantics=("arbitrary", "arbitrary"),
                                             vmem_limit_bytes=min(VMEM_REQUEST_CAP, nbytes)),
        name="ffn_gate_up",
    )(a, wg_all, wu_all)


def _swa_body(q_ref, kvc_ref, kvp_ref, bias_ref, sink_ref, qg_ref, kg_ref, og_ref, e_ref, et_ref, o_ref, acc_ref):
    hd, blk, grp = SWA_HEAD_DIM, SWA_BLOCK, SWA_GROUP
    ones = jnp.ones((2 * blk, V7X_LANES), BF16)
    first_variant = jnp.minimum(pl.program_id(1), 1)

    for sub in range(SWA_BLOCKS_PER_STEP):
        rows = slice(blk * sub, blk * (sub + 1))
        kv_prev = kvp_ref if sub == 0 else kvc_ref.at[blk * (sub - 1):blk * sub]
        kband = jnp.concatenate([kv_prev[:, :SWA_KV_WIDTH], kvc_ref[rows, :SWA_KV_WIDTH]], axis=0).astype(F32)
        vband = jnp.concatenate([kv_prev[:, SWA_KV_WIDTH:], kvc_ref[rows, SWA_KV_WIDTH:]], axis=0)
        variant = first_variant if sub == 0 else 1

        q = q_ref[rows, :].astype(F32)
        ssq = _dot_exact_rhs(q * q, e_ref[...], 2)
        inv = lax.rsqrt(ssq * (1.0 / hd) + EPS)
        qn = (q * _dot_exact_rhs(inv, et_ref[...], 3) * qg_ref[...]).astype(BF16)

        for j in range(SWA_KV_HEADS):
            kn = _rms(kband[:, hd * j:hd * (j + 1)], kg_ref[...]).astype(BF16)
            qs = jnp.concatenate([qn[:, hd * (grp * j + g):hd * (grp * j + g + 1)] for g in range(grp)], axis=0)
            s = _dot_nt(qs, kn) * (hd ** -0.5) + bias_ref[variant, j]
            sink = sink_ref[j]
            s_prev, s_cur = s[:, :blk], s[:, blk:]
            row_max = jnp.max(jnp.maximum(s_prev, s_cur), axis=-1, keepdims=True)
            mx = jnp.maximum(jnp.broadcast_to(row_max, sink.shape), sink)
            p = jnp.concatenate([jnp.exp(s_prev - mx), jnp.exp(s_cur - mx)], axis=1).astype(BF16)
            total = _dot(p, ones) + jnp.exp(sink - mx)
            o = _dot(p, vband[:, hd * j:hd * (j + 1)]) / total[:, :hd]
            for g in range(grp):
                h = grp * j + g
                acc_ref[sub, :, hd * h:hd * (h + 1)] = o[blk * g:blk * (g + 1), :]
        o_ref[rows, :] = _rms(acc_ref[sub], og_ref[...]).astype(o_ref.dtype)


def swa_mixer(qkv, bias, sinks, q_gain, k_gain, out_gain, batch, seq):
    nb = seq // SWA_BLOCK
    kv_blk = SWA_WIDTH // (2 * SWA_KV_WIDTH)
    rows = SWA_GROUP * SWA_BLOCK
    no_prev = jnp.arange(2 * SWA_BLOCK) < SWA_BLOCK
    bias_g = jnp.stack([jnp.where(no_prev, -jnp.inf, bias), bias]).reshape(2, SWA_KV_HEADS, rows, 2 * SWA_BLOCK)
    sink_col = jnp.broadcast_to(jnp.repeat(sinks, SWA_BLOCK)[:, None],
                                (SWA_HEADS * SWA_BLOCK, V7X_LANES)).reshape(SWA_KV_HEADS, rows, V7X_LANES)
    head_of_col = jnp.arange(SWA_WIDTH) // SWA_HEAD_DIM
    e = (head_of_col[:, None] == jnp.arange(V7X_LANES)[None, :]).astype(BF16)
    const = lambda shape: pl.BlockSpec(shape, lambda b, n: (0,) * len(shape))
    nsub = SWA_BLOCKS_PER_STEP
    step_rows = nsub * SWA_BLOCK
    ns = seq // step_rows
    assert seq % step_rows == 0
    nbytes = (_nbytes((step_rows, SWA_WIDTH), BF16) * 2 + 2 * _nbytes((step_rows, 2 * SWA_KV_WIDTH), BF16)
              + _nbytes(bias_g.shape, F32) + _nbytes((SWA_KV_HEADS, rows, V7X_LANES), F32)
              + 2 * _nbytes(e.shape, BF16) + _nbytes((step_rows, SWA_WIDTH), F32)
              + 4 * _nbytes((rows, 2 * SWA_BLOCK), F32))
    return pl.pallas_call(
        _swa_body,
        grid=(batch, ns),
        in_specs=[pl.BlockSpec((step_rows, SWA_WIDTH), lambda b, n: (b * ns + n, 0)),
                  pl.BlockSpec((step_rows, 2 * SWA_KV_WIDTH), lambda b, n: (b * ns + n, kv_blk)),
                  pl.BlockSpec((SWA_BLOCK, 2 * SWA_KV_WIDTH),
                               lambda b, n: (b * nb + jnp.maximum(n * nsub - 1, 0), kv_blk)),
                  const(bias_g.shape), const(sink_col.shape),
                  const((1, SWA_WIDTH)), const((1, SWA_HEAD_DIM)), const((1, SWA_WIDTH)),
                  const(e.shape), const(e.T.shape)],
        out_specs=pl.BlockSpec((step_rows, SWA_WIDTH), lambda b, n: (b * ns + n, 0)),
        out_shape=jax.ShapeDtypeStruct((batch * seq, SWA_WIDTH), BF16),
        scratch_shapes=[pltpu.VMEM((nsub, SWA_BLOCK, SWA_WIDTH), F32)],
        compiler_params=_params(("parallel", "parallel"), nbytes),
        name="swa_mixer",
    )(qkv, qkv, qkv, bias_g, sink_col, jnp.tile(q_gain, SWA_HEADS).reshape(1, -1), k_gain.reshape(1, -1),
      out_gain.reshape(1, -1), e, e.T)


def _t5_bucket(dist):
    n = jnp.maximum(dist, 0)
    max_exact = REL_BUCKETS // 2
    nf = jnp.maximum(n, 1).astype(F32)
    large = max_exact + (jnp.log(nf / max_exact) / math.log(REL_MAX_DIST / max_exact)
                         * (REL_BUCKETS - max_exact)).astype(jnp.int32)
    large = jnp.minimum(large, REL_BUCKETS - 1)
    return jnp.where(n < max_exact, n, large)


def swa_band_bias(rel_bias):
    i = jnp.arange(SWA_BLOCK, dtype=jnp.int32)[:, None]
    j = jnp.arange(2 * SWA_BLOCK, dtype=jnp.int32)[None, :]
    dist = i + SWA_BLOCK - j
    onehot = _t5_bucket(dist)[None] == jnp.arange(REL_BUCKETS, dtype=jnp.int32)[:, None, None]
    bias = jnp.sum(jnp.where(onehot[:, None], rel_bias[:, :, None, None], 0.0), axis=0)
    in_window = (dist >= 0) & (dist < SWA_WINDOW)
    return jnp.where(in_window[None], bias, -jnp.inf)


def _gla_body(x_ref, gl_ref, wup_ref, bup_ref, gain_ref, wq_ref, wk_ref, wv_ref, wo_ref,
              o_ref, wq16_ref, wkv16_ref, wo16_ref, state_ref):
    tb = x_ref.shape[0]
    dk, dv, c = GLA_KEY_DIM, GLA_VAL_DIM, GLA_CHUNK

    wq16_ref[...] = wq_ref[...].astype(BF16)
    wkv16_ref[:, :X_WIDTH] = wk_ref[...].astype(BF16)
    wkv16_ref[:, X_WIDTH:] = wv_ref[...].astype(BF16)
    wo16_ref[...] = wo_ref[...].astype(BF16)

    @pl.when(pl.program_id(1) == 0)
    def _():
        state_ref[...] = jnp.zeros_like(state_ref)

    gl_hi, gl_lo = _split_bf16(gl_ref[...], 2)
    w_hi, w_lo = _split_bf16(wup_ref[...], 2)
    pre = _dot(gl_hi, w_hi) + _dot(gl_hi, w_lo) + _dot(gl_lo, w_hi) + bup_ref[...]
    g = (jnp.minimum(pre, 0.0) - jnp.log1p(jnp.exp(-jnp.abs(pre)))) * (1.0 / GLA_GATE_NORMALIZER)

    row = lax.broadcasted_iota(jnp.int32, (tb, tb), 0)
    colm = lax.broadcasted_iota(jnp.int32, (tb, tb), 1)
    same_chunk_lower = ((row // c) == (colm // c)) & (colm <= row)
    bcum_all = _dot_exact_lhs(same_chunk_lower.astype(BF16), g, 3)

    ri = lax.broadcasted_iota(jnp.int32, (c, c), 0)
    ci = lax.broadcasted_iota(jnp.int32, (c, c), 1)
    causal = ci <= ri
    for ch in range(tb // c):
        r0 = ch * c
        for h in range(GLA_HEADS):
            bcum = bcum_all[r0:r0 + c, dk * h:dk * (h + 1)]
            blast = bcum[c - 1:c, :]
            q = x_ref[r0:r0 + c, dk * h:dk * (h + 1)].astype(F32)
            k = x_ref[r0:r0 + c, GLA_KEY_WIDTH + dk * h:GLA_KEY_WIDTH + dk * (h + 1)].astype(F32)
            v = x_ref[r0:r0 + c, 2 * GLA_KEY_WIDTH + dv * h:2 * GLA_KEY_WIDTH + dv * (h + 1)]
            r = x_ref[r0:r0 + c, 2 * GLA_KEY_WIDTH + GLA_WIDTH + dv * h:
                      2 * GLA_KEY_WIDTH + GLA_WIDTH + dv * (h + 1)].astype(F32)
            qd = (q * (dk ** -0.5) * jnp.exp(bcum)).astype(BF16)
            kd = (k * jnp.exp(-bcum)).astype(BF16)
            kl = (k * jnp.exp(blast - bcum)).astype(BF16)
            att = jnp.where(causal, _dot_nt(qd, kd), 0.0)
            st = state_ref[h]
            o = _dot(att.astype(BF16), v) + _dot_nt(qd, st.astype(BF16))
            state_ref[h] = st * jnp.exp(blast) + _dot_tn(v, kl)
            o = _rms(o, gain_ref[...]) * _silu(r)
            o_ref[r0:r0 + c, dv * h:dv * (h + 1)] = o.astype(o_ref.dtype)


def gla_mixer(x, small, w_up_pad, b_up, norm_gain, batch, seq, x_w_q, x_w_k, x_w_v, x_w_o, layer):
    tb = min(GLA_TIME_BLOCK, seq)
    nt = seq // tb
    steps = batch * nt
    width = x.shape[1]
    d, xw = x_w_q.shape[1:]
    assert d % (steps * V7X_SUBLANES) == 0 and xw % (steps * 2 * V7X_SUBLANES) == 0
    row_blk = lambda rows, cols: pl.BlockSpec((rows // steps, cols), lambda b, t: (b * nt + t, 0))
    layer_blk = lambda rows, cols: pl.BlockSpec((None, rows // steps, cols), lambda b, t: (layer, b * nt + t, 0))
    nbytes = (_nbytes((tb, width), BF16) + _nbytes((tb, V7X_LANES), F32) + _nbytes(w_up_pad.shape, F32)
              + _nbytes((tb, GLA_WIDTH), BF16) + _nbytes((GLA_HEADS, GLA_VAL_DIM, GLA_KEY_DIM), F32)
              + 8 * _nbytes((tb, GLA_KEY_WIDTH), F32) + 6 * _nbytes((d // steps, xw), F32))
    return pl.pallas_call(
        _gla_body,
        grid=(batch, nt),
        in_specs=[pl.BlockSpec((tb, width), lambda b, t: (b * nt + t, 0)),
                  pl.BlockSpec((tb, V7X_LANES), lambda b, t: (b * nt + t, 0)),
                  pl.BlockSpec(w_up_pad.shape, lambda b, t: (0, 0)),
                  pl.BlockSpec((1, GLA_KEY_WIDTH), lambda b, t: (0, 0)),
                  pl.BlockSpec((1, GLA_VAL_DIM), lambda b, t: (0, 0)),
                  layer_blk(d, xw), layer_blk(d, xw), layer_blk(d, xw), layer_blk(xw, d)],
        out_specs=[pl.BlockSpec((tb, GLA_WIDTH), lambda b, t: (b * nt + t, 0)),
                   row_blk(d, xw), row_blk(d, 2 * xw), row_blk(xw, d)],
        out_shape=[jax.ShapeDtypeStruct((batch * seq, GLA_WIDTH), BF16),
                   jax.ShapeDtypeStruct((d, xw), BF16), jax.ShapeDtypeStruct((d, 2 * xw), BF16),
                   jax.ShapeDtypeStruct((xw, d), BF16)],
        scratch_shapes=[pltpu.VMEM((GLA_HEADS, GLA_VAL_DIM, GLA_KEY_DIM), F32)],
        compiler_params=_params(("arbitrary", "arbitrary"), nbytes),
        name="gla_mixer",
    )(x, small, w_up_pad, b_up.reshape(1, -1), norm_gain.reshape(1, -1), x_w_q, x_w_k, x_w_v, x_w_o)


def _ssd_body(x_ref, dt_ref, cw_ref, cb_ref, dtb_ref, alog_ref, dexp_ref, gain_ref, expand_ref, shift_ref,
              wcast_in_ref, o_ref, wcast_out_ref, state_ref, tail_ref, y_ref, *, wcast_blocks):
    L, P, N = SSD_CHUNK, SSD_HEAD_DIM, SSD_STATE
    gw = SSD_GROUP_WIDTH

    @pl.when(pl.program_id(0) * pl.num_programs(1) + pl.program_id(1) < wcast_blocks)
    def _():
        wcast_out_ref[...] = wcast_in_ref[...].astype(BF16)

    @pl.when(pl.program_id(1) == 0)
    def _():
        state_ref[...] = jnp.zeros_like(state_ref)
        tail_ref[...] = jnp.zeros_like(tail_ref)

    xin = x_ref[:, SSD_WIDTH:]
    xin32 = xin.astype(F32)
    shifted = _dot(shift_ref[...], xin)
    conv = cb_ref[...] + cw_ref[SSD_CONV - 1:SSD_CONV, :] * xin32
    head = jnp.zeros((8, SSD_CONV_CH), F32)
    for j in range(1, SSD_CONV):
        wj = cw_ref[SSD_CONV - 1 - j:SSD_CONV - j, :]
        conv = conv + wj * shifted[L * (j - 1):L * j, :]
        head = head + wj * tail_ref[8 - j:16 - j, :]
    conv = jnp.concatenate([conv[0:8, :] + head, conv[8:, :]], axis=0)
    tail_ref[0:8, :] = xin32[L - 8:, :]
    xbc = _silu(conv)
    xs = xbc[:, :SSD_WIDTH]
    bm = xbc[:, SSD_WIDTH:SSD_WIDTH + SSD_BC_WIDTH].astype(BF16)
    cm = xbc[:, SSD_WIDTH + SSD_BC_WIDTH:].astype(BF16)

    dt = _softplus(dt_ref[...] + dtb_ref[...])
    dta = dt * (-jnp.exp(alog_ref[...]))
    ri = lax.broadcasted_iota(jnp.int32, (L, L), 0)
    ci = lax.broadcasted_iota(jnp.int32, (L, L), 1)
    causal = ci <= ri
    a_cum = _dot_exact_lhs(causal.astype(BF16), dta, 3)
    a_cum_t = a_cum.T
    expand = expand_ref[...]
    a_exp = _dot_exact_rhs(a_cum, expand, 3)
    dt_exp = _dot_exact_rhs(dt, expand, 2)
    a_last = a_exp[L - 1:L, :]
    xd = xs * dt_exp
    xdec = (xd * jnp.exp(a_last - a_exp)).astype(BF16)
    xd16 = xd.astype(BF16)
    out_scale = jnp.exp(a_exp)

    for g in range(SSD_GROUPS):
        bg = bm[:, N * g:N * (g + 1)]
        cg = cm[:, N * g:N * (g + 1)]
        cb = _dot_nt(cg, bg)
        for kk in range(SSD_HEADS_PER_GROUP):
            h = g * SSD_HEADS_PER_GROUP + kk
            hl = SSD_DT_LANE0 + h
            diff = a_cum[:, hl:hl + 1] - a_cum_t[hl:hl + 1, :]
            m = cb * jnp.exp(jnp.where(causal, diff, -jnp.inf))
            y_ref[:, P * h:P * (h + 1)] = _dot(m.astype(BF16), xd16[:, P * h:P * (h + 1)])
        sg = state_ref[:, gw * g:gw * (g + 1)]
        y_off = _dot(cg, sg.astype(BF16)) * out_scale[:, gw * g:gw * (g + 1)]
        state_ref[:, gw * g:gw * (g + 1)] = (sg * jnp.exp(a_last[:, gw * g:gw * (g + 1)])
                                             + _dot_tn(bg, xdec[:, gw * g:gw * (g + 1)]))
        yg = y_ref[:, gw * g:gw * (g + 1)] + y_off + xs[:, gw * g:gw * (g + 1)] * dexp_ref[:, gw * g:gw * (g + 1)]
        yg = yg * _silu(x_ref[:, gw * g:gw * (g + 1)].astype(F32))
        o_ref[:, gw * g:gw * (g + 1)] = _rms(yg, gain_ref[:, gw * g:gw * (g + 1)]).astype(o_ref.dtype)


def ssd_mixer(x, small, conv_w, conv_b, dt_bias, a_log, d_skip, norm_gain, batch, seq, wcast_all, layer):
    L = SSD_CHUNK
    nc = seq // L
    width = x.shape[1]
    _, wc_rows, wc_cols = wcast_all.shape
    wc_tile = V7X_MXU_WIDTH
    wc_blocks = wc_rows // wc_tile
    assert wc_rows % wc_tile == 0 and wc_blocks <= batch * nc
    wc_index = lambda b, c: jnp.minimum(b * nc + c, wc_blocks - 1)
    t = jnp.arange(L)
    shift = jnp.concatenate([(t[:, None] - j == t[None, :]) for j in range(1, SSD_CONV)], axis=0).astype(BF16)
    lane_pad = (SSD_DT_LANE0, V7X_LANES - SSD_DT_LANE0 - SSD_HEADS)
    dtb = jnp.pad(dt_bias, lane_pad).reshape(1, V7X_LANES)
    alog = jnp.pad(a_log, lane_pad).reshape(1, V7X_LANES)
    dexp = jnp.repeat(d_skip, SSD_HEAD_DIM).reshape(1, SSD_WIDTH)
    expand = (jnp.arange(V7X_LANES)[:, None] - SSD_DT_LANE0
              == (jnp.arange(SSD_WIDTH)[None, :] // SSD_HEAD_DIM)).astype(BF16)
    nbytes = (_nbytes((L, width), BF16) + _nbytes((L, V7X_LANES), F32) + _nbytes((L, SSD_WIDTH), BF16)
              + _nbytes(expand.shape, BF16) + 2 * _nbytes((SSD_STATE, SSD_WIDTH), F32)
              + 16 * _nbytes((L, SSD_CONV_CH), F32) + _nbytes((wc_tile, wc_cols), F32)
              + _nbytes((wc_tile, wc_cols), BF16))
    return pl.pallas_call(
        functools.partial(_ssd_body, wcast_blocks=wc_blocks),
        grid=(batch, nc),
        in_specs=[pl.BlockSpec((L, width), lambda b, c: (b * nc + c, 0)),
                  pl.BlockSpec((L, V7X_LANES), lambda b, c: (b * nc + c, 1)),
                  pl.BlockSpec((SSD_CONV, SSD_CONV_CH), lambda b, c: (0, 0)),
                  pl.BlockSpec((1, SSD_CONV_CH), lambda b, c: (0, 0)),
                  pl.BlockSpec((1, V7X_LANES), lambda b, c: (0, 0)),
                  pl.BlockSpec((1, V7X_LANES), lambda b, c: (0, 0)),
                  pl.BlockSpec((1, SSD_WIDTH), lambda b, c: (0, 0)),
                  pl.BlockSpec((1, SSD_WIDTH), lambda b, c: (0, 0)),
                  pl.BlockSpec(expand.shape, lambda b, c: (0, 0)),
                  pl.BlockSpec(shift.shape, lambda b, c: (0, 0)),
                  pl.BlockSpec((None, wc_tile, wc_cols), lambda b, c: (layer, wc_index(b, c), 0))],
        out_specs=[pl.BlockSpec((L, SSD_WIDTH), lambda b, c: (b * nc + c, 0)),
                   pl.BlockSpec((wc_tile, wc_cols), lambda b, c: (wc_index(b, c), 0))],
        out_shape=[jax.ShapeDtypeStruct((batch * seq, SSD_WIDTH), BF16),
                   jax.ShapeDtypeStruct((wc_rows, wc_cols), BF16)],
        scratch_shapes=[pltpu.VMEM((SSD_STATE, SSD_WIDTH), F32),
                        pltpu.VMEM((16, SSD_CONV_CH), F32),
                        pltpu.VMEM((L, SSD_WIDTH), F32)],
        compiler_params=_params(("arbitrary", "arbitrary"), nbytes),
        name="ssd_mixer",
    )(x, small, conv_w.reshape(SSD_CONV, SSD_CONV_CH), conv_b.reshape(1, -1), dtb, alog, dexp,
      norm_gain.reshape(1, -1), expand, shift, wcast_all)


def _xattn_body(h_ref, kv_ref, wq_ref, wo_ref, lnx_ref, qg_ref, kg_ref, lnf_ref, o_ref, hf_ref, att_ref):
    hd = X_HEAD_DIM
    mem_len = kv_ref.shape[0]
    half = mem_len // 2
    ones = jnp.ones((mem_len, V7X_LANES), BF16)
    h = h_ref[...]
    q = _dot(_rms(h, lnx_ref[...]).astype(BF16), wq_ref[...])
    for i in range(X_HEADS):
        qh = _rms(q[:, hd * i:hd * (i + 1)], qg_ref[...]).astype(BF16)
        kh = _rms(kv_ref[:, hd * i:hd * (i + 1)], kg_ref[...]).astype(BF16)
        vh = kv_ref[:, X_WIDTH + hd * i:X_WIDTH + hd * (i + 1)].astype(BF16)
        s = _dot_nt(qh, kh) * (hd ** -0.5)
        s_lo, s_hi = s[:, :half], s[:, half:]
        mx = jnp.broadcast_to(jnp.max(jnp.maximum(s_lo, s_hi), axis=-1, keepdims=True), s_lo.shape)
        p = jnp.concatenate([jnp.exp(s_lo - mx), jnp.exp(s_hi - mx)], axis=1).astype(BF16)
        att_ref[:, hd * i:hd * (i + 1)] = (_dot(p, vh) / _dot(p, ones)).astype(BF16)
    h_new = h + _dot(att_ref[...], wo_ref[...])
    o_ref[...] = h_new
    hf_ref[...] = _rms(h_new, lnf_ref[...]).astype(hf_ref.dtype)


def cross_attention_block(h, kv, w_q, w_o, ln_x, q_gain, k_gain, ln_ffn, batch, seq, mem_len, tq=256):
    m, d = h.shape
    tq = min(tq, seq)
    nq = seq // tq
    assert mem_len == 2 * V7X_LANES and X_HEAD_DIM == V7X_LANES
    const = lambda shape: pl.BlockSpec(shape, lambda b, t: (0,) * len(shape))
    row_blk = lambda width: pl.BlockSpec((tq, width), lambda b, t: (b * nq + t, 0))
    nbytes = (2 * _nbytes((tq, d), F32) + _nbytes((tq, d), BF16) + _nbytes((mem_len, 2 * X_WIDTH), F32)
              + 2 * _nbytes((d, X_WIDTH), BF16) + 2 * _nbytes((tq, d), F32))
    return pl.pallas_call(
        _xattn_body,
        grid=(batch, nq),
        in_specs=[row_blk(d),
                  pl.BlockSpec((mem_len, 2 * X_WIDTH), lambda b, t: (b, 0)),
                  const((d, X_WIDTH)), const((X_WIDTH, d)), const((1, d)),
                  const((1, X_HEAD_DIM)), const((1, X_HEAD_DIM)), const((1, d))],
        out_specs=[row_blk(d), row_blk(d)],
        out_shape=[jax.ShapeDtypeStruct((m, d), F32), jax.ShapeDtypeStruct((m, d), BF16)],
        scratch_shapes=[pltpu.VMEM((tq, X_WIDTH), BF16)],
        compiler_params=_params(("parallel", "parallel"), nbytes),
        name="xattn_block",
    )(h, kv, w_q, w_o, ln_x.reshape(1, -1), q_gain.reshape(1, -1), k_gain.reshape(1, -1), ln_ffn.reshape(1, -1))


_IN_OFF = np.cumsum([0, SWA_WIDTH, SWA_KV_WIDTH, SWA_KV_WIDTH, GLA_KEY_WIDTH, GLA_KEY_WIDTH, GLA_WIDTH,
                     GLA_WIDTH, GLA_GATE_RANK, SSD_WIDTH, SSD_CONV_CH, SSD_HEADS]).tolist()
SWA_COL0, GLA_COL0, GLOW_COL0, SSD_COL0, DT_COL0 = _IN_OFF[0], _IN_OFF[3], _IN_OFF[7], _IN_OFF[8], _IN_OFF[10]


IN_DIM = _IN_OFF[-1]
SMALL_ROW_STARTS = (GLOW_COL0, IN_DIM - V7X_LANES)
assert SMALL_ROW_STARTS[1] + SSD_DT_LANE0 == DT_COL0


def kernel(x, mem, rel_bias, ln_mix, w_in, swa_q_gain, swa_k_gain, swa_sinks, swa_out_gain, gla_w_gk_up, gla_b_gk_up, gla_norm_gain, ssd_conv_w, ssd_conv_b, ssd_dt_bias, ssd_a_log, ssd_d, ssd_norm_gain, w_mix_out, ln_x, ln_mem, x_w_q, x_w_k, x_w_v, x_w_o, x_q_gain, x_k_gain, ln_ffn, ffn_w_gate, ffn_w_up, ffn_w_down):
    batch, seq, d = x.shape
    mem_len = mem.shape[1]
    m = batch * seq
    band_bias = swa_band_bias(rel_bias)
    h = x.reshape(m, d)
    mem2 = mem.reshape(batch * mem_len, d)
    w_in_t = jnp.swapaxes(w_in, 1, 2)
    for l in range(DEPTH):
        w_up_pad = jnp.pad(gla_w_gk_up[l], ((0, V7X_LANES - GLA_GATE_RANK), (0, 0)))
        hn, p_small = rmsnorm_small(h, ln_mix[l], w_in_t, l, SMALL_ROW_STARTS)
        proj = functools.partial(slab_matmul, [hn], w_in_t, l, features_on_rows=True, out_dtype=BF16)
        p_swa = proj(SWA_COL0, 1, GLA_COL0 - SWA_COL0, tm=512, name="proj_swa")
        p_gla = proj(GLA_COL0, (GLOW_COL0 - GLA_COL0) // PROJ_SLAB, PROJ_SLAB, name="proj_gla")
        p_ssd = proj(SSD_COL0, (DT_COL0 - SSD_COL0) // PROJ_SLAB, PROJ_SLAB, name="proj_ssd")
        y_a = swa_mixer(p_swa, band_bias, swa_sinks[l], swa_q_gain[l], swa_k_gain[l], swa_out_gain[l], batch, seq)
        y_b, w_q16, w_kv16, w_o16 = gla_mixer(p_gla, p_small, w_up_pad, gla_b_gk_up[l], gla_norm_gain[l], batch, seq,
                                              x_w_q, x_w_k, x_w_v, x_w_o, l)
        y_c, w_down16 = ssd_mixer(p_ssd, p_small, ssd_conv_w[l], ssd_conv_b[l], ssd_dt_bias[l], ssd_a_log[l],
                                  ssd_d[l], ssd_norm_gain[l], batch, seq, ffn_w_down, l)
        h = slab_matmul([y_a, y_b, y_c], w_mix_out, l, 0, d // MIX_SLAB, MIX_SLAB, features_on_rows=False,
                        out_dtype=F32, res=h, tm=MIX_TOKENS, name="mix_out")
        memn = rmsnorm(mem2, ln_mem[l])
        kv = matmul(memn, w_kv16, out_dtype=F32, tm=1024, tn=512, name="xattn_kv")
        h, hf = cross_attention_block(h, kv, w_q16, w_o16, ln_x[l], x_q_gain[l], x_k_gain[l], ln_ffn[l],
                                      batch, seq, mem_len)
        hidden = ffn_gate_up(hf, ffn_w_gate, ffn_w_up, l)
        h = matmul(hidden, w_down16, out_dtype=F32, res=h, tm=512, tn=512, name="ffn_down")
    return h.reshape(batch, seq, d)
```

```python
import functools
import math

import numpy as np
import jax
import jax.numpy as jnp
from jax import lax
from jax.experimental import pallas as pl
from jax.experimental.pallas import tpu as pltpu

F32 = jnp.float32
BF16 = jnp.bfloat16

D_MODEL = 4096
DEPTH = 2
EPS = 1e-6
SWA_WIDTH = 1024
SWA_HEAD_DIM = 64
SWA_HEADS = 16
SWA_KV_HEADS = 2
SWA_GROUP = SWA_HEADS // SWA_KV_HEADS
SWA_KV_WIDTH = SWA_KV_HEADS * SWA_HEAD_DIM
SWA_WINDOW = 128
SWA_BLOCK = 128
REL_BUCKETS = 32
REL_MAX_DIST = 128
GLA_WIDTH = 1024
GLA_HEADS = 4
GLA_VAL_DIM = 256
GLA_KEY_DIM = 128
GLA_KEY_WIDTH = GLA_HEADS * GLA_KEY_DIM
GLA_GATE_RANK = 16
GLA_GATE_NORMALIZER = 16.0
GLA_CHUNK = 64
SSD_WIDTH = 2048
SSD_HEAD_DIM = 64
SSD_HEADS = 32
SSD_GROUPS = 8
SSD_HEADS_PER_GROUP = SSD_HEADS // SSD_GROUPS
SSD_STATE = 128
SSD_CONV = 4
SSD_CHUNK = 128
SSD_BC_WIDTH = SSD_GROUPS * SSD_STATE
SSD_CONV_CH = SSD_WIDTH + 2 * SSD_BC_WIDTH
SSD_GROUP_WIDTH = SSD_WIDTH // SSD_GROUPS
X_HEADS = 4
X_HEAD_DIM = 128
X_WIDTH = X_HEADS * X_HEAD_DIM
FFN_HIDDEN = 11008

V7X_LANES = 128
V7X_SUBLANES = 8
V7X_VMEM_BYTES = 64 * 1024 * 1024
VMEM_REQUEST_CAP = (V7X_VMEM_BYTES * 7) // 8
V7X_MXU_WIDTH = 256
FFN_SLAB = 2 * V7X_MXU_WIDTH
PROJ_SLAB = 4 * V7X_MXU_WIDTH
MIX_SLAB = 4 * V7X_MXU_WIDTH
MIX_TOKENS = 512
GLA_TIME_BLOCK = 256
GLA_BLOCKS_PER_STEP = 2
SWA_BLOCKS_PER_STEP = 4
SSD_DT_LANE0 = V7X_LANES - SSD_HEADS


VMEM_TEMP_ALLOWANCE = 16 * 1024 * 1024


def _params(semantics, block_bytes):
    limit = min(VMEM_REQUEST_CAP, 2 * block_bytes + VMEM_TEMP_ALLOWANCE)
    return pltpu.CompilerParams(dimension_semantics=semantics, vmem_limit_bytes=int(limit))


def _nbytes(shape, dtype):
    return int(np.prod(shape)) * jnp.dtype(dtype).itemsize


def _split_bf16(x, terms):
    parts = []
    r = x
    for t in range(terms):
        p = r.astype(BF16)
        parts.append(p)
        if t + 1 < terms:
            r = r - p.astype(F32)
    return parts


def _dot(a, b):
    return jnp.dot(a, b, preferred_element_type=F32)


def _dot_nt(a, b):
    return lax.dot_general(a, b, (((1,), (1,)), ((), ())), preferred_element_type=F32)


def _dot_tn(a, b):
    return lax.dot_general(a, b, (((0,), (0,)), ((), ())), preferred_element_type=F32)


def _dot_exact_lhs(a_bf16, x, terms):
    acc = None
    for p in _split_bf16(x, terms):
        d = _dot(a_bf16, p)
        acc = d if acc is None else acc + d
    return acc


def _dot_exact_rhs(x, b_bf16, terms):
    acc = None
    for p in _split_bf16(x, terms):
        d = _dot(p, b_bf16)
        acc = d if acc is None else acc + d
    return acc


def _silu(x):
    return x / (1.0 + jnp.exp(-x))


def _softplus(x):
    return jnp.maximum(x, 0.0) + jnp.log1p(jnp.exp(-jnp.abs(x)))


def _rms(x, gain):
    return x * lax.rsqrt(jnp.mean(x * x, axis=-1, keepdims=True) + EPS) * gain


def _rmsnorm_body(x_ref, g_ref, o_ref):
    o_ref[...] = _rms(x_ref[...], g_ref[...]).astype(o_ref.dtype)


def rmsnorm(x, gain, tm=256):
    m, d = x.shape
    tm = min(tm, m)
    return pl.pallas_call(
        _rmsnorm_body,
        grid=(m // tm,),
        in_specs=[pl.BlockSpec((tm, d), lambda i: (i, 0)),
                  pl.BlockSpec((1, d), lambda i: (0, 0))],
        out_specs=pl.BlockSpec((tm, d), lambda i: (i, 0)),
        out_shape=jax.ShapeDtypeStruct((m, d), BF16),
        compiler_params=_params(("parallel",), _nbytes((tm, d), F32) + _nbytes((tm, d), BF16)),
        name="rmsnorm",
    )(x, gain.reshape(1, d))


def _rmsnorm_small_body(x_ref, g_ref, wa_ref, wb_ref, o_ref, small_ref, wt_ref):
    @pl.when(pl.program_id(0) == 0)
    def _():
        for t, w_ref in enumerate((wa_ref, wb_ref)):
            for k0 in range(0, w_ref.shape[2], _XPOSE_COLS):
                wt_ref[k0:k0 + _XPOSE_COLS, V7X_LANES * t:V7X_LANES * (t + 1)] = (
                    w_ref[0, :, k0:k0 + _XPOSE_COLS].T.astype(BF16))

    hn = _rms(x_ref[...], g_ref[...]).astype(BF16)
    o_ref[...] = hn
    small_ref[...] = _dot(hn, wt_ref[...])


def rmsnorm_small(x, gain, wt_all, layer, row_starts, tm=512):
    m, d = x.shape
    tm = min(tm, m)
    w_tile = lambda r0: pl.BlockSpec((pl.Element(1), pl.Element(V7X_LANES), pl.Element(d)),
                                     lambda i: (layer, r0, 0))
    nbytes = (_nbytes((tm, d), F32) + _nbytes((tm, d), BF16) + 2 * _nbytes((V7X_LANES, d), F32)
              + _nbytes((d, 2 * V7X_LANES), BF16) + _nbytes((tm, 2 * V7X_LANES), F32))
    return pl.pallas_call(
        _rmsnorm_small_body,
        grid=(m // tm,),
        in_specs=[pl.BlockSpec((tm, d), lambda i: (i, 0)),
                  pl.BlockSpec((1, d), lambda i: (0, 0)),
                  w_tile(row_starts[0]), w_tile(row_starts[1])],
        out_specs=[pl.BlockSpec((tm, d), lambda i: (i, 0)),
                   pl.BlockSpec((tm, 2 * V7X_LANES), lambda i: (i, 0))],
        out_shape=[jax.ShapeDtypeStruct((m, d), BF16), jax.ShapeDtypeStruct((m, 2 * V7X_LANES), F32)],
        scratch_shapes=[pltpu.VMEM((d, 2 * V7X_LANES), BF16)],
        compiler_params=_params(("arbitrary",), nbytes),
        name="rmsnorm_small",
    )(x, gain.reshape(1, d), wt_all, wt_all)


def _mm_body(*refs, has_res):
    a_ref, w_ref = refs[0], refs[1]
    o_ref = refs[2 + has_res]
    part = _dot(a_ref[...], w_ref[...])
    if has_res:
        part = part + refs[2][...]
    o_ref[...] = part.astype(o_ref.dtype)


def matmul(a, w, *, out_dtype, res=None, tm, tn, name="matmul"):
    m, kdim = a.shape
    n = w.shape[1]
    tm, tn = min(tm, m), min(tn, n)
    assert m % tm == 0 and n % tn == 0
    has_res = res is not None
    in_specs = [pl.BlockSpec((tm, kdim), lambda i, j: (i, 0)), pl.BlockSpec((kdim, tn), lambda i, j: (0, j))]
    args = [a, w]
    nbytes = _nbytes((tm, kdim), BF16) + _nbytes((kdim, tn), BF16) + _nbytes((tm, tn), out_dtype) + _nbytes((tm, tn), F32)
    if has_res:
        in_specs.append(pl.BlockSpec((tm, tn), lambda i, j: (i, j)))
        args.append(res)
        nbytes += _nbytes((tm, tn), F32)
    return pl.pallas_call(
        functools.partial(_mm_body, has_res=has_res),
        grid=(m // tm, n // tn),
        in_specs=in_specs,
        out_specs=pl.BlockSpec((tm, tn), lambda i, j: (i, j)),
        out_shape=jax.ShapeDtypeStruct((m, n), out_dtype),
        compiler_params=_params(("parallel", "parallel"), nbytes),
        name=name,
    )(*args)


_XPOSE_ROWS, _XPOSE_COLS = 256, 512


def _slab_copy(w_hbm, stage_ref, sem, layer, slab, start0, tn, features_on_rows):
    if features_on_rows:
        src = w_hbm.at[layer, pl.ds(pl.multiple_of(start0 + slab * tn, V7X_SUBLANES), tn), :]
    else:
        src = w_hbm.at[layer, :, pl.ds(pl.multiple_of(start0 + slab * tn, V7X_LANES), tn)]
    return pltpu.make_async_copy(src, stage_ref, sem.at[0])


def _slab_body(*refs, n_a, has_res, layer, start0, features_on_rows):
    a_refs, w_hbm = refs[:n_a], refs[n_a]
    r_ref = refs[n_a + 1] if has_res else None
    o_ref, stage_ref, wb_ref, sem = refs[n_a + 1 + has_res:]
    kdim, tn = wb_ref.shape
    slab, i = pl.program_id(0), pl.program_id(1)
    copy = lambda s: _slab_copy(w_hbm, stage_ref, sem, layer, s, start0, tn, features_on_rows)

    @pl.when(i == 0)
    def _():
        @pl.when(slab == 0)
        def _():
            copy(slab).start()

        copy(slab).wait()
        if features_on_rows:
            for r0 in range(0, tn, _XPOSE_ROWS):
                r1 = min(r0 + _XPOSE_ROWS, tn)
                for k0 in range(0, kdim, _XPOSE_COLS):
                    wb_ref[k0:k0 + _XPOSE_COLS, r0:r1] = stage_ref[r0:r1, k0:k0 + _XPOSE_COLS].T.astype(BF16)
        else:
            wb_ref[...] = stage_ref[...].astype(BF16)

    @pl.when((i == 1) & (slab + 1 < pl.num_programs(0)))
    def _():
        copy(slab + 1).start()

    acc = r_ref[...] if has_res else None
    k0 = 0
    for a_ref in a_refs:
        k1 = k0 + a_ref.shape[1]
        part = _dot(a_ref[...], wb_ref[k0:k1, :])
        acc = part if acc is None else acc + part
        k0 = k1
    o_ref[...] = acc.astype(o_ref.dtype)


def slab_matmul(a_list, w_all, layer, start0, n_slabs, tn, *, features_on_rows, out_dtype, res=None,
                tm=1024, name="slab_matmul"):
    m = a_list[0].shape[0]
    kdim = sum(a.shape[1] for a in a_list)
    tm = min(tm, m)
    assert m // tm >= 2
    has_res = res is not None
    in_specs = [pl.BlockSpec((tm, a.shape[1]), lambda s, i: (i, 0)) for a in a_list]
    in_specs.append(pl.BlockSpec(memory_space=pl.ANY))
    args = list(a_list) + [w_all]
    nbytes = (2 * _nbytes((tm, kdim), BF16) + _nbytes((kdim, tn), F32) + _nbytes((kdim, tn), BF16)
              + 2 * _nbytes((tm, tn), out_dtype) + 4 * _nbytes((tm, tn), F32))
    if has_res:
        in_specs.append(pl.BlockSpec((tm, tn), lambda s, i: (i, s)))
        args.append(res)
        nbytes += 2 * _nbytes((tm, tn), F32)
    return pl.pallas_call(
        functools.partial(_slab_body, n_a=len(a_list), has_res=has_res, layer=layer, start0=start0,
                          features_on_rows=features_on_rows),
        grid=(n_slabs, m // tm),
        in_specs=in_specs,
        out_specs=pl.BlockSpec((tm, tn), lambda s, i: (i, s)),
        out_shape=jax.ShapeDtypeStruct((m, n_slabs * tn), out_dtype),
        scratch_shapes=[pltpu.VMEM((tn, kdim) if features_on_rows else (kdim, tn), F32),
                        pltpu.VMEM((kdim, tn), BF16),
                        pltpu.SemaphoreType.DMA((1,))],
        compiler_params=pltpu.CompilerParams(dimension_semantics=("arbitrary", "arbitrary"),
                                             vmem_limit_bytes=min(VMEM_REQUEST_CAP, nbytes)),
        name=name,
    )(*args)


def _gateup_copies(wg_hbm, wu_hbm, stage_ref, sem, layer, slab, width):
    c0 = pl.multiple_of(slab * FFN_SLAB, FFN_SLAB)
    return [pltpu.make_async_copy(w_hbm.at[layer, :, pl.ds(c0, width)], stage_ref.at[k, :, pl.ds(0, width)],
                                  sem.at[k])
            for k, w_hbm in enumerate((wg_hbm, wu_hbm))]


def _gateup_body(a_ref, wg_hbm, wu_hbm, o_ref, stage_ref, wgb_ref, wub_ref, sem, *, layer, last_width):
    slab, i = pl.program_id(0), pl.program_id(1)
    last = pl.num_programs(0) - 1

    def for_slab(s, action):
        @pl.when(s < last)
        def _():
            for c in _gateup_copies(wg_hbm, wu_hbm, stage_ref, sem, layer, s, FFN_SLAB):
                action(c)

        @pl.when(s == last)
        def _():
            for c in _gateup_copies(wg_hbm, wu_hbm, stage_ref, sem, layer, s, last_width):
                action(c)

    @pl.when(i == 0)
    def _():
        @pl.when(slab == 0)
        def _():
            for_slab(slab, lambda c: c.start())

        for_slab(slab, lambda c: c.wait())
        wgb_ref[...] = stage_ref[0].astype(BF16)
        wub_ref[...] = stage_ref[1].astype(BF16)

    @pl.when((i == 1) & (slab < last))
    def _():
        for_slab(slab + 1, lambda c: c.start())

    a = a_ref[...]
    g = _dot(a, wgb_ref[...])
    u = _dot(a, wub_ref[...])
    o_ref[...] = (_silu(g) * u).astype(o_ref.dtype)


def ffn_gate_up(a, wg_all, wu_all, layer, tm=1024):
    m, kdim = a.shape
    n = wg_all.shape[2]
    tm = min(tm, m)
    n_slabs = pl.cdiv(n, FFN_SLAB)
    last_width = n - (n_slabs - 1) * FFN_SLAB
    assert m // tm >= 2 and last_width % V7X_LANES == 0
    nbytes = (2 * _nbytes((tm, kdim), BF16) + 2 * _nbytes((kdim, FFN_SLAB), F32)
              + 2 * _nbytes((kdim, FFN_SLAB), BF16) + 2 * _nbytes((tm, FFN_SLAB), BF16)
              + 6 * _nbytes((tm, FFN_SLAB), F32))
    return pl.pallas_call(
        functools.partial(_gateup_body, layer=layer, last_width=last_width),
        grid=(n_slabs, m // tm),
        in_specs=[pl.BlockSpec((tm, kdim), lambda s, i: (i, 0)),
                  pl.BlockSpec(memory_space=pl.ANY),
                  pl.BlockSpec(memory_space=pl.ANY)],
        out_specs=pl.BlockSpec((tm, FFN_SLAB), lambda s, i: (i, s)),
        out_shape=jax.ShapeDtypeStruct((m, n), BF16),
        scratch_shapes=[pltpu.VMEM((2, kdim, FFN_SLAB), F32),
                        pltpu.VMEM((kdim, FFN_SLAB), BF16), pltpu.VMEM((kdim, FFN_SLAB), BF16),
                        pltpu.SemaphoreType.DMA((2,))],
        compiler_params=pltpu.CompilerParams(dimension_semantics=("arbitrary", "arbitrary"),
                                             vmem_limit_bytes=min(VMEM_REQUEST_CAP, nbytes)),
        name="ffn_gate_up",
    )(a, wg_all, wu_all)


def _swa_body(q_ref, kvc_ref, kvp_ref, bias_ref, sink_ref, qg_ref, kg_ref, og_ref, e_ref, et_ref, o_ref, acc_ref):
    hd, blk, grp = SWA_HEAD_DIM, SWA_BLOCK, SWA_GROUP
    ones = jnp.ones((2 * blk, V7X_LANES), BF16)
    first_variant = jnp.minimum(pl.program_id(1), 1)

    for sub in range(SWA_BLOCKS_PER_STEP):
        rows = slice(blk * sub, blk * (sub + 1))
        kv_prev = kvp_ref if sub == 0 else kvc_ref.at[blk * (sub - 1):blk * sub]
        kband = jnp.concatenate([kv_prev[:, :SWA_KV_WIDTH], kvc_ref[rows, :SWA_KV_WIDTH]], axis=0).astype(F32)
        vband = jnp.concatenate([kv_prev[:, SWA_KV_WIDTH:], kvc_ref[rows, SWA_KV_WIDTH:]], axis=0)
        variant = first_variant if sub == 0 else 1

        q = q_ref[rows, :].astype(F32)
        ssq = _dot_exact_rhs(q * q, e_ref[...], 2)
        inv = lax.rsqrt(ssq * (1.0 / hd) + EPS)
        qn = (q * _dot_exact_rhs(inv, et_ref[...], 3) * qg_ref[...]).astype(BF16)

        for j in range(SWA_KV_HEADS):
            kn = _rms(kband[:, hd * j:hd * (j + 1)], kg_ref[...]).astype(BF16)
            qs = jnp.concatenate([qn[:, hd * (grp * j + g):hd * (grp * j + g + 1)] for g in range(grp)], axis=0)
            s = _dot_nt(qs, kn) * (hd ** -0.5) + bias_ref[variant, j]
            sink = sink_ref[j]
            s_prev, s_cur = s[:, :blk], s[:, blk:]
            row_max = jnp.max(jnp.maximum(s_prev, s_cur), axis=-1, keepdims=True)
            mx = jnp.maximum(jnp.broadcast_to(row_max, sink.shape), sink)
            p = jnp.concatenate([jnp.exp(s_prev - mx), jnp.exp(s_cur - mx)], axis=1).astype(BF16)
            total = _dot(p, ones) + jnp.exp(sink - mx)
            o = _dot(p, vband[:, hd * j:hd * (j + 1)]) / total[:, :hd]
            for g in range(grp):
                h = grp * j + g
                acc_ref[sub, :, hd * h:hd * (h + 1)] = o[blk * g:blk * (g + 1), :]
        o_ref[rows, :] = _rms(acc_ref[sub], og_ref[...]).astype(o_ref.dtype)


def swa_mixer(qkv, bias, sinks, q_gain, k_gain, out_gain, batch, seq):
    nb = seq // SWA_BLOCK
    kv_blk = SWA_WIDTH // (2 * SWA_KV_WIDTH)
    rows = SWA_GROUP * SWA_BLOCK
    no_prev = jnp.arange(2 * SWA_BLOCK) < SWA_BLOCK
    bias_g = jnp.stack([jnp.where(no_prev, -jnp.inf, bias), bias]).reshape(2, SWA_KV_HEADS, rows, 2 * SWA_BLOCK)
    sink_col = jnp.broadcast_to(jnp.repeat(sinks, SWA_BLOCK)[:, None],
                                (SWA_HEADS * SWA_BLOCK, V7X_LANES)).reshape(SWA_KV_HEADS, rows, V7X_LANES)
    head_of_col = jnp.arange(SWA_WIDTH) // SWA_HEAD_DIM
    e = (head_of_col[:, None] == jnp.arange(V7X_LANES)[None, :]).astype(BF16)
    const = lambda shape: pl.BlockSpec(shape, lambda b, n: (0,) * len(shape))
    nsub = SWA_BLOCKS_PER_STEP
    step_rows = nsub * SWA_BLOCK
    ns = seq // step_rows
    assert seq % step_rows == 0
    nbytes = (_nbytes((step_rows, SWA_WIDTH), BF16) * 2 + 2 * _nbytes((step_rows, 2 * SWA_KV_WIDTH), BF16)
              + _nbytes(bias_g.shape, F32) + _nbytes((SWA_KV_HEADS, rows, V7X_LANES), F32)
              + 2 * _nbytes(e.shape, BF16) + _nbytes((step_rows, SWA_WIDTH), F32)
              + 4 * _nbytes((rows, 2 * SWA_BLOCK), F32))
    return pl.pallas_call(
        _swa_body,
        grid=(batch, ns),
        in_specs=[pl.BlockSpec((step_rows, SWA_WIDTH), lambda b, n: (b * ns + n, 0)),
                  pl.BlockSpec((step_rows, 2 * SWA_KV_WIDTH), lambda b, n: (b * ns + n, kv_blk)),
                  pl.BlockSpec((SWA_BLOCK, 2 * SWA_KV_WIDTH),
                               lambda b, n: (b * nb + jnp.maximum(n * nsub - 1, 0), kv_blk)),
                  const(bias_g.shape), const(sink_col.shape),
                  const((1, SWA_WIDTH)), const((1, SWA_HEAD_DIM)), const((1, SWA_WIDTH)),
                  const(e.shape), const(e.T.shape)],
        out_specs=pl.BlockSpec((step_rows, SWA_WIDTH), lambda b, n: (b * ns + n, 0)),
        out_shape=jax.ShapeDtypeStruct((batch * seq, SWA_WIDTH), BF16),
        scratch_shapes=[pltpu.VMEM((nsub, SWA_BLOCK, SWA_WIDTH), F32)],
        compiler_params=_params(("parallel", "parallel"), nbytes),
        name="swa_mixer",
    )(qkv, qkv, qkv, bias_g, sink_col, jnp.tile(q_gain, SWA_HEADS).reshape(1, -1), k_gain.reshape(1, -1),
      out_gain.reshape(1, -1), e, e.T)


def _t5_bucket(dist):
    n = jnp.maximum(dist, 0)
    max_exact = REL_BUCKETS // 2
    nf = jnp.maximum(n, 1).astype(F32)
    large = max_exact + (jnp.log(nf / max_exact) / math.log(REL_MAX_DIST / max_exact)
                         * (REL_BUCKETS - max_exact)).astype(jnp.int32)
    large = jnp.minimum(large, REL_BUCKETS - 1)
    return jnp.where(n < max_exact, n, large)


def swa_band_bias(rel_bias):
    i = jnp.arange(SWA_BLOCK, dtype=jnp.int32)[:, None]
    j = jnp.arange(2 * SWA_BLOCK, dtype=jnp.int32)[None, :]
    dist = i + SWA_BLOCK - j
    onehot = _t5_bucket(dist)[None] == jnp.arange(REL_BUCKETS, dtype=jnp.int32)[:, None, None]
    bias = jnp.sum(jnp.where(onehot[:, None], rel_bias[:, :, None, None], 0.0), axis=0)
    in_window = (dist >= 0) & (dist < SWA_WINDOW)
    return jnp.where(in_window[None], bias, -jnp.inf)


def _gla_body(x_ref, gl_ref, wup_ref, bup_ref, gain_ref, wq_ref, wk_ref, wv_ref, wo_ref,
              o_ref, wq16_ref, wkv16_ref, wo16_ref, state_ref):
    dk, dv, c = GLA_KEY_DIM, GLA_VAL_DIM, GLA_CHUNK

    wq16_ref[...] = wq_ref[...].astype(BF16)
    wkv16_ref[:, :X_WIDTH] = wk_ref[...].astype(BF16)
    wkv16_ref[:, X_WIDTH:] = wv_ref[...].astype(BF16)
    wo16_ref[...] = wo_ref[...].astype(BF16)

    @pl.when(pl.program_id(1) == 0)
    def _():
        state_ref[...] = jnp.zeros_like(state_ref)

    w_hi, w_lo = _split_bf16(wup_ref[...], 2)
    tb = min(GLA_TIME_BLOCK, x_ref.shape[0])
    row = lax.broadcasted_iota(jnp.int32, (tb, tb), 0)
    colm = lax.broadcasted_iota(jnp.int32, (tb, tb), 1)
    same_chunk_lower = (((row // c) == (colm // c)) & (colm <= row)).astype(BF16)
    ri = lax.broadcasted_iota(jnp.int32, (c, c), 0)
    ci = lax.broadcasted_iota(jnp.int32, (c, c), 1)
    causal = ci <= ri
    for r0 in range(0, x_ref.shape[0], c):
        if r0 % tb == 0:
            gl_hi, gl_lo = _split_bf16(gl_ref[r0:r0 + tb, :], 2)
            pre = _dot(gl_hi, w_hi) + _dot(gl_hi, w_lo) + _dot(gl_lo, w_hi) + bup_ref[...]
            g = (jnp.minimum(pre, 0.0) - jnp.log1p(jnp.exp(-jnp.abs(pre)))) * (1.0 / GLA_GATE_NORMALIZER)
            bcum_all = _dot_exact_lhs(same_chunk_lower, g, 3)
        rb = r0 % tb
        for h in range(GLA_HEADS):
            bcum = bcum_all[rb:rb + c, dk * h:dk * (h + 1)]
            blast = bcum[c - 1:c, :]
            q = x_ref[r0:r0 + c, dk * h:dk * (h + 1)].astype(F32)
            k = x_ref[r0:r0 + c, GLA_KEY_WIDTH + dk * h:GLA_KEY_WIDTH + dk * (h + 1)].astype(F32)
            v = x_ref[r0:r0 + c, 2 * GLA_KEY_WIDTH + dv * h:2 * GLA_KEY_WIDTH + dv * (h + 1)]
            r = x_ref[r0:r0 + c, 2 * GLA_KEY_WIDTH + GLA_WIDTH + dv * h:
                      2 * GLA_KEY_WIDTH + GLA_WIDTH + dv * (h + 1)].astype(F32)
            qd = (q * (dk ** -0.5) * jnp.exp(bcum)).astype(BF16)
            kd = (k * jnp.exp(-bcum)).astype(BF16)
            kl = (k * jnp.exp(blast - bcum)).astype(BF16)
            att = jnp.where(causal, _dot_nt(qd, kd), 0.0)
            st = state_ref[h]
            o = _dot(att.astype(BF16), v) + _dot_nt(qd, st.astype(BF16))
            state_ref[h] = st * jnp.exp(blast) + _dot_tn(v, kl)
            o = _rms(o, gain_ref[...]) * _silu(r)
            o_ref[r0:r0 + c, dv * h:dv * (h + 1)] = o.astype(o_ref.dtype)


def gla_mixer(x, small, w_up_pad, b_up, norm_gain, batch, seq, x_w_q, x_w_k, x_w_v, x_w_o, layer):
    tb = min(GLA_BLOCKS_PER_STEP * GLA_TIME_BLOCK, seq)
    nt = seq // tb
    steps = batch * nt
    width = x.shape[1]
    d, xw = x_w_q.shape[1:]
    assert d % (steps * V7X_SUBLANES) == 0 and xw % (steps * 2 * V7X_SUBLANES) == 0
    row_blk = lambda rows, cols: pl.BlockSpec((rows // steps, cols), lambda b, t: (b * nt + t, 0))
    layer_blk = lambda rows, cols: pl.BlockSpec((None, rows // steps, cols), lambda b, t: (layer, b * nt + t, 0))
    nbytes = (_nbytes((tb, width), BF16) + _nbytes((tb, V7X_LANES), F32) + _nbytes(w_up_pad.shape, F32)
              + _nbytes((tb, GLA_WIDTH), BF16) + _nbytes((GLA_HEADS, GLA_VAL_DIM, GLA_KEY_DIM), F32)
              + 8 * _nbytes((tb, GLA_KEY_WIDTH), F32) + 6 * _nbytes((d // steps, xw), F32))
    return pl.pallas_call(
        _gla_body,
        grid=(batch, nt),
        in_specs=[pl.BlockSpec((tb, width), lambda b, t: (b * nt + t, 0)),
                  pl.BlockSpec((tb, V7X_LANES), lambda b, t: (b * nt + t, 0)),
                  pl.BlockSpec(w_up_pad.shape, lambda b, t: (0, 0)),
                  pl.BlockSpec((1, GLA_KEY_WIDTH), lambda b, t: (0, 0)),
                  pl.BlockSpec((1, GLA_VAL_DIM), lambda b, t: (0, 0)),
                  layer_blk(d, xw), layer_blk(d, xw), layer_blk(d, xw), layer_blk(xw, d)],
        out_specs=[pl.BlockSpec((tb, GLA_WIDTH), lambda b, t: (b * nt + t, 0)),
                   row_blk(d, xw), row_blk(d, 2 * xw), row_blk(xw, d)],
        out_shape=[jax.ShapeDtypeStruct((batch * seq, GLA_WIDTH), BF16),
                   jax.ShapeDtypeStruct((d, xw), BF16), jax.ShapeDtypeStruct((d, 2 * xw), BF16),
                   jax.ShapeDtypeStruct((xw, d), BF16)],
        scratch_shapes=[pltpu.VMEM((GLA_HEADS, GLA_VAL_DIM, GLA_KEY_DIM), F32)],
        compiler_params=_params(("arbitrary", "arbitrary"), nbytes),
        name="gla_mixer",
    )(x, small, w_up_pad, b_up.reshape(1, -1), norm_gain.reshape(1, -1), x_w_q, x_w_k, x_w_v, x_w_o)


def _ssd_body(x_ref, dt_ref, cw_ref, cb_ref, dtb_ref, alog_ref, dexp_ref, gain_ref, expand_ref, shift_ref,
              wcast_in_ref, o_ref, wcast_out_ref, state_ref, tail_ref, y_ref, *, wcast_blocks):
    L, P, N = SSD_CHUNK, SSD_HEAD_DIM, SSD_STATE
    gw = SSD_GROUP_WIDTH

    @pl.when(pl.program_id(0) * pl.num_programs(1) + pl.program_id(1) < wcast_blocks)
    def _():
        wcast_out_ref[...] = wcast_in_ref[...].astype(BF16)

    @pl.when(pl.program_id(1) == 0)
    def _():
        state_ref[...] = jnp.zeros_like(state_ref)
        tail_ref[...] = jnp.zeros_like(tail_ref)

    xin = x_ref[:, SSD_WIDTH:]
    xin32 = xin.astype(F32)
    shifted = _dot(shift_ref[...], xin)
    conv = cb_ref[...] + cw_ref[SSD_CONV - 1:SSD_CONV, :] * xin32
    head = jnp.zeros((8, SSD_CONV_CH), F32)
    for j in range(1, SSD_CONV):
        wj = cw_ref[SSD_CONV - 1 - j:SSD_CONV - j, :]
        conv = conv + wj * shifted[L * (j - 1):L * j, :]
        head = head + wj * tail_ref[8 - j:16 - j, :]
    conv = jnp.concatenate([conv[0:8, :] + head, conv[8:, :]], axis=0)
    tail_ref[0:8, :] = xin32[L - 8:, :]
    xbc = _silu(conv)
    xs = xbc[:, :SSD_WIDTH]
    bm = xbc[:, SSD_WIDTH:SSD_WIDTH + SSD_BC_WIDTH].astype(BF16)
    cm = xbc[:, SSD_WIDTH + SSD_BC_WIDTH:].astype(BF16)

    dt = _softplus(dt_ref[...] + dtb_ref[...])
    dta = dt * (-jnp.exp(alog_ref[...]))
    ri = lax.broadcasted_iota(jnp.int32, (L, L), 0)
    ci = lax.broadcasted_iota(jnp.int32, (L, L), 1)
    causal = ci <= ri
    a_cum = _dot_exact_lhs(causal.astype(BF16), dta, 3)
    a_cum_t = a_cum.T
    expand = expand_ref[...]
    a_exp = _dot_exact_rhs(a_cum, expand, 3)
    dt_exp = _dot_exact_rhs(dt, expand, 2)
    a_last = a_exp[L - 1:L, :]
    xd = xs * dt_exp
    xdec = (xd * jnp.exp(a_last - a_exp)).astype(BF16)
    xd16 = xd.astype(BF16)
    out_scale = jnp.exp(a_exp)

    for g in range(SSD_GROUPS):
        bg = bm[:, N * g:N * (g + 1)]
        cg = cm[:, N * g:N * (g + 1)]
        cb = _dot_nt(cg, bg)
        for kk in range(SSD_HEADS_PER_GROUP):
            h = g * SSD_HEADS_PER_GROUP + kk
            hl = SSD_DT_LANE0 + h
            diff = a_cum[:, hl:hl + 1] - a_cum_t[hl:hl + 1, :]
            m = cb * jnp.exp(jnp.where(causal, diff, -jnp.inf))
            y_ref[:, P * h:P * (h + 1)] = _dot(m.astype(BF16), xd16[:, P * h:P * (h + 1)])
        sg = state_ref[:, gw * g:gw * (g + 1)]
        y_off = _dot(cg, sg.astype(BF16)) * out_scale[:, gw * g:gw * (g + 1)]
        state_ref[:, gw * g:gw * (g + 1)] = (sg * jnp.exp(a_last[:, gw * g:gw * (g + 1)])
                                             + _dot_tn(bg, xdec[:, gw * g:gw * (g + 1)]))
        yg = y_ref[:, gw * g:gw * (g + 1)] + y_off + xs[:, gw * g:gw * (g + 1)] * dexp_ref[:, gw * g:gw * (g + 1)]
        yg = yg * _silu(x_ref[:, gw * g:gw * (g + 1)].astype(F32))
        o_ref[:, gw * g:gw * (g + 1)] = _rms(yg, gain_ref[:, gw * g:gw * (g + 1)]).astype(o_ref.dtype)


def ssd_mixer(x, small, conv_w, conv_b, dt_bias, a_log, d_skip, norm_gain, batch, seq, wcast_all, layer):
    L = SSD_CHUNK
    nc = seq // L
    width = x.shape[1]
    _, wc_rows, wc_cols = wcast_all.shape
    wc_tile = V7X_MXU_WIDTH
    wc_blocks = wc_rows // wc_tile
    assert wc_rows % wc_tile == 0 and wc_blocks <= batch * nc
    wc_index = lambda b, c: jnp.minimum(b * nc + c, wc_blocks - 1)
    t = jnp.arange(L)
    shift = jnp.concatenate([(t[:, None] - j == t[None, :]) for j in range(1, SSD_CONV)], axis=0).astype(BF16)
    lane_pad = (SSD_DT_LANE0, V7X_LANES - SSD_DT_LANE0 - SSD_HEADS)
    dtb = jnp.pad(dt_bias, lane_pad).reshape(1, V7X_LANES)
    alog = jnp.pad(a_log, lane_pad).reshape(1, V7X_LANES)
    dexp = jnp.repeat(d_skip, SSD_HEAD_DIM).reshape(1, SSD_WIDTH)
    expand = (jnp.arange(V7X_LANES)[:, None] - SSD_DT_LANE0
              == (jnp.arange(SSD_WIDTH)[None, :] // SSD_HEAD_DIM)).astype(BF16)
    nbytes = (_nbytes((L, width), BF16) + _nbytes((L, V7X_LANES), F32) + _nbytes((L, SSD_WIDTH), BF16)
              + _nbytes(expand.shape, BF16) + 2 * _nbytes((SSD_STATE, SSD_WIDTH), F32)
              + 16 * _nbytes((L, SSD_CONV_CH), F32) + _nbytes((wc_tile, wc_cols), F32)
              + _nbytes((wc_tile, wc_cols), BF16))
    return pl.pallas_call(
        functools.partial(_ssd_body, wcast_blocks=wc_blocks),
        grid=(batch, nc),
        in_specs=[pl.BlockSpec((L, width), lambda b, c: (b * nc + c, 0)),
                  pl.BlockSpec((L, V7X_LANES), lambda b, c: (b * nc + c, 1)),
                  pl.BlockSpec((SSD_CONV, SSD_CONV_CH), lambda b, c: (0, 0)),
                  pl.BlockSpec((1, SSD_CONV_CH), lambda b, c: (0, 0)),
                  pl.BlockSpec((1, V7X_LANES), lambda b, c: (0, 0)),
                  pl.BlockSpec((1, V7X_LANES), lambda b, c: (0, 0)),
                  pl.BlockSpec((1, SSD_WIDTH), lambda b, c: (0, 0)),
                  pl.BlockSpec((1, SSD_WIDTH), lambda b, c: (0, 0)),
                  pl.BlockSpec(expand.shape, lambda b, c: (0, 0)),
                  pl.BlockSpec(shift.shape, lambda b, c: (0, 0)),
                  pl.BlockSpec((None, wc_tile, wc_cols), lambda b, c: (layer, wc_index(b, c), 0))],
        out_specs=[pl.BlockSpec((L, SSD_WIDTH), lambda b, c: (b * nc + c, 0)),
                   pl.BlockSpec((wc_tile, wc_cols), lambda b, c: (wc_index(b, c), 0))],
        out_shape=[jax.ShapeDtypeStruct((batch * seq, SSD_WIDTH), BF16),
                   jax.ShapeDtypeStruct((wc_rows, wc_cols), BF16)],
        scratch_shapes=[pltpu.VMEM((SSD_STATE, SSD_WIDTH), F32),
                        pltpu.VMEM((16, SSD_CONV_CH), F32),
                        pltpu.VMEM((L, SSD_WIDTH), F32)],
        compiler_params=_params(("arbitrary", "arbitrary"), nbytes),
        name="ssd_mixer",
    )(x, small, conv_w.reshape(SSD_CONV, SSD_CONV_CH), conv_b.reshape(1, -1), dtb, alog, dexp,
      norm_gain.reshape(1, -1), expand, shift, wcast_all)


def _xattn_body(h_ref, kv_ref, wq_ref, wo_ref, lnx_ref, qg_ref, kg_ref, lnf_ref, o_ref, hf_ref, att_ref):
    hd = X_HEAD_DIM
    mem_len = kv_ref.shape[0]
    half = mem_len // 2
    ones = jnp.ones((mem_len, V7X_LANES), BF16)
    h = h_ref[...]
    q = _dot(_rms(h, lnx_ref[...]).astype(BF16), wq_ref[...])
    for i in range(X_HEADS):
        qh = _rms(q[:, hd * i:hd * (i + 1)], qg_ref[...]).astype(BF16)
        kh = _rms(kv_ref[:, hd * i:hd * (i + 1)], kg_ref[...]).astype(BF16)
        vh = kv_ref[:, X_WIDTH + hd * i:X_WIDTH + hd * (i + 1)].astype(BF16)
        s = _dot_nt(qh, kh) * (hd ** -0.5)
        s_lo, s_hi = s[:, :half], s[:, half:]
        mx = jnp.broadcast_to(jnp.max(jnp.maximum(s_lo, s_hi), axis=-1, keepdims=True), s_lo.shape)
        p = jnp.concatenate([jnp.exp(s_lo - mx), jnp.exp(s_hi - mx)], axis=1).astype(BF16)
        att_ref[:, hd * i:hd * (i + 1)] = (_dot(p, vh) / _dot(p, ones)).astype(BF16)
    h_new = h + _dot(att_ref[...], wo_ref[...])
    o_ref[...] = h_new
    hf_ref[...] = _rms(h_new, lnf_ref[...]).astype(hf_ref.dtype)


def cross_attention_block(h, kv, w_q, w_o, ln_x, q_gain, k_gain, ln_ffn, batch, seq, mem_len, tq=256):
    m, d = h.shape
    tq = min(tq, seq)
    nq = seq // tq
    assert mem_len == 2 * V7X_LANES and X_HEAD_DIM == V7X_LANES
    const = lambda shape: pl.BlockSpec(shape, lambda b, t: (0,) * len(shape))
    row_blk = lambda width: pl.BlockSpec((tq, width), lambda b, t: (b * nq + t, 0))
    nbytes = (2 * _nbytes((tq, d), F32) + _nbytes((tq, d), BF16) + _nbytes((mem_len, 2 * X_WIDTH), F32)
              + 2 * _nbytes((d, X_WIDTH), BF16) + 2 * _nbytes((tq, d), F32))
    return pl.pallas_call(
        _xattn_body,
        grid=(batch, nq),
        in_specs=[row_blk(d),
                  pl.BlockSpec((mem_len, 2 * X_WIDTH), lambda b, t: (b, 0)),
                  const((d, X_WIDTH)), const((X_WIDTH, d)), const((1, d)),
                  const((1, X_HEAD_DIM)), const((1, X_HEAD_DIM)), const((1, d))],
        out_specs=[row_blk(d), row_blk(d)],
        out_shape=[jax.ShapeDtypeStruct((m, d), F32), jax.ShapeDtypeStruct((m, d), BF16)],
        scratch_shapes=[pltpu.VMEM((tq, X_WIDTH), BF16)],
        compiler_params=_params(("parallel", "parallel"), nbytes),
        name="xattn_block",
    )(h, kv, w_q, w_o, ln_x.reshape(1, -1), q_gain.reshape(1, -1), k_gain.reshape(1, -1), ln_ffn.reshape(1, -1))


_IN_OFF = np.cumsum([0, SWA_WIDTH, SWA_KV_WIDTH, SWA_KV_WIDTH, GLA_KEY_WIDTH, GLA_KEY_WIDTH, GLA_WIDTH,
                     GLA_WIDTH, GLA_GATE_RANK, SSD_WIDTH, SSD_CONV_CH, SSD_HEADS]).tolist()
SWA_COL0, GLA_COL0, GLOW_COL0, SSD_COL0, DT_COL0 = _IN_OFF[0], _IN_OFF[3], _IN_OFF[7], _IN_OFF[8], _IN_OFF[10]


IN_DIM = _IN_OFF[-1]
SMALL_ROW_STARTS = (GLOW_COL0, IN_DIM - V7X_LANES)
assert SMALL_ROW_STARTS[1] + SSD_DT_LANE0 == DT_COL0


def kernel(x, mem, rel_bias, ln_mix, w_in, swa_q_gain, swa_k_gain, swa_sinks, swa_out_gain, gla_w_gk_up, gla_b_gk_up, gla_norm_gain, ssd_conv_w, ssd_conv_b, ssd_dt_bias, ssd_a_log, ssd_d, ssd_norm_gain, w_mix_out, ln_x, ln_mem, x_w_q, x_w_k, x_w_v, x_w_o, x_q_gain, x_k_gain, ln_ffn, ffn_w_gate, ffn_w_up, ffn_w_down):
    batch, seq, d = x.shape
    mem_len = mem.shape[1]
    m = batch * seq
    band_bias = swa_band_bias(rel_bias)
    h = x.reshape(m, d)
    mem2 = mem.reshape(batch * mem_len, d)
    w_in_t = jnp.swapaxes(w_in, 1, 2)
    for l in range(DEPTH):
        w_up_pad = jnp.pad(gla_w_gk_up[l], ((0, V7X_LANES - GLA_GATE_RANK), (0, 0)))
        hn, p_small = rmsnorm_small(h, ln_mix[l], w_in_t, l, SMALL_ROW_STARTS)
        proj = functools.partial(slab_matmul, [hn], w_in_t, l, features_on_rows=True, out_dtype=BF16)
        p_swa = proj(SWA_COL0, 1, GLA_COL0 - SWA_COL0, tm=512, name="proj_swa")
        p_gla = proj(GLA_COL0, (GLOW_COL0 - GLA_COL0) // PROJ_SLAB, PROJ_SLAB, name="proj_gla")
        p_ssd = proj(SSD_COL0, (DT_COL0 - SSD_COL0) // PROJ_SLAB, PROJ_SLAB, name="proj_ssd")
        y_a = swa_mixer(p_swa, band_bias, swa_sinks[l], swa_q_gain[l], swa_k_gain[l], swa_out_gain[l], batch, seq)
        y_b, w_q16, w_kv16, w_o16 = gla_mixer(p_gla, p_small, w_up_pad, gla_b_gk_up[l], gla_norm_gain[l], batch, seq,
                                              x_w_q, x_w_k, x_w_v, x_w_o, l)
        y_c, w_down16 = ssd_mixer(p_ssd, p_small, ssd_conv_w[l], ssd_conv_b[l], ssd_dt_bias[l], ssd_a_log[l],
                                  ssd_d[l], ssd_norm_gain[l], batch, seq, ffn_w_down, l)
        h = slab_matmul([y_a, y_b, y_c], w_mix_out, l, 0, d // MIX_SLAB, MIX_SLAB, features_on_rows=False,
                        out_dtype=F32, res=h, tm=MIX_TOKENS, name="mix_out")
        memn = rmsnorm(mem2, ln_mem[l])
        kv = matmul(memn, w_kv16, out_dtype=F32, tm=1024, tn=512, name="xattn_kv")
        h, hf = cross_attention_block(h, kv, w_q16, w_o16, ln_x[l], x_q_gain[l], x_k_gain[l], ln_ffn[l],
                                      batch, seq, mem_len)
        hidden = ffn_gate_up(hf, ffn_w_gate, ffn_w_up, l)
        h = matmul(hidden, w_down16, out_dtype=F32, res=h, tm=512, tn=512, name="ffn_down")
    return h.reshape(batch, seq, d)
```

```python
import functools
import math

import numpy as np
import jax
import jax.numpy as jnp
from jax import lax
from jax.experimental import pallas as pl
from jax.experimental.pallas import tpu as pltpu

F32 = jnp.float32
BF16 = jnp.bfloat16

D_MODEL = 4096
DEPTH = 2
EPS = 1e-6
SWA_WIDTH = 1024
SWA_HEAD_DIM = 64
SWA_HEADS = 16
SWA_KV_HEADS = 2
SWA_GROUP = SWA_HEADS // SWA_KV_HEADS
SWA_KV_WIDTH = SWA_KV_HEADS * SWA_HEAD_DIM
SWA_WINDOW = 128
SWA_BLOCK = 128
REL_BUCKETS = 32
REL_MAX_DIST = 128
GLA_WIDTH = 1024
GLA_HEADS = 4
GLA_VAL_DIM = 256
GLA_KEY_DIM = 128
GLA_KEY_WIDTH = GLA_HEADS * GLA_KEY_DIM
GLA_GATE_RANK = 16
GLA_GATE_NORMALIZER = 16.0
GLA_CHUNK = 64
SSD_WIDTH = 2048
SSD_HEAD_DIM = 64
SSD_HEADS = 32
SSD_GROUPS = 8
SSD_HEADS_PER_GROUP = SSD_HEADS // SSD_GROUPS
SSD_STATE = 128
SSD_CONV = 4
SSD_CHUNK = 128
SSD_BC_WIDTH = SSD_GROUPS * SSD_STATE
SSD_CONV_CH = SSD_WIDTH + 2 * SSD_BC_WIDTH
SSD_GROUP_WIDTH = SSD_WIDTH // SSD_GROUPS
X_HEADS = 4
X_HEAD_DIM = 128
X_WIDTH = X_HEADS * X_HEAD_DIM
FFN_HIDDEN = 11008

V7X_LANES = 128
V7X_SUBLANES = 8
V7X_VMEM_BYTES = 64 * 1024 * 1024
VMEM_REQUEST_CAP = (V7X_VMEM_BYTES * 7) // 8
V7X_MXU_WIDTH = 256
FFN_SLAB = 2 * V7X_MXU_WIDTH
PROJ_SLAB = 4 * V7X_MXU_WIDTH
MIX_SLAB = 4 * V7X_MXU_WIDTH
MIX_TOKENS = 512
GLA_TIME_BLOCK = 256
GLA_BLOCKS_PER_STEP = 2
SWA_BLOCKS_PER_STEP = 4
SSD_DT_LANE0 = V7X_LANES - SSD_HEADS


VMEM_TEMP_ALLOWANCE = 16 * 1024 * 1024


def _params(semantics, block_bytes):
    limit = min(VMEM_REQUEST_CAP, 2 * block_bytes + VMEM_TEMP_ALLOWANCE)
    return pltpu.CompilerParams(dimension_semantics=semantics, vmem_limit_bytes=int(limit))


def _nbytes(shape, dtype):
    return int(np.prod(shape)) * jnp.dtype(dtype).itemsize


def _split_bf16(x, terms):
    parts = []
    r = x
    for t in range(terms):
        p = r.astype(BF16)
        parts.append(p)
        if t + 1 < terms:
            r = r - p.astype(F32)
    return parts


def _dot(a, b):
    return jnp.dot(a, b, preferred_element_type=F32)


def _dot_nt(a, b):
    return lax.dot_general(a, b, (((1,), (1,)), ((), ())), preferred_element_type=F32)


def _dot_tn(a, b):
    return lax.dot_general(a, b, (((0,), (0,)), ((), ())), preferred_element_type=F32)


def _dot_exact_lhs(a_bf16, x, terms):
    acc = None
    for p in _split_bf16(x, terms):
        d = _dot(a_bf16, p)
        acc = d if acc is None else acc + d
    return acc


def _dot_exact_rhs(x, b_bf16, terms):
    acc = None
    for p in _split_bf16(x, terms):
        d = _dot(p, b_bf16)
        acc = d if acc is None else acc + d
    return acc


def _silu(x):
    return x / (1.0 + jnp.exp(-x))


def _softplus(x):
    return jnp.maximum(x, 0.0) + jnp.log1p(jnp.exp(-jnp.abs(x)))


def _rms(x, gain):
    return x * lax.rsqrt(jnp.mean(x * x, axis=-1, keepdims=True) + EPS) * gain


def _rmsnorm_body(x_ref, g_ref, o_ref):
    o_ref[...] = _rms(x_ref[...], g_ref[...]).astype(o_ref.dtype)


def rmsnorm(x, gain, tm=256):
    m, d = x.shape
    tm = min(tm, m)
    return pl.pallas_call(
        _rmsnorm_body,
        grid=(m // tm,),
        in_specs=[pl.BlockSpec((tm, d), lambda i: (i, 0)),
                  pl.BlockSpec((1, d), lambda i: (0, 0))],
        out_specs=pl.BlockSpec((tm, d), lambda i: (i, 0)),
        out_shape=jax.ShapeDtypeStruct((m, d), BF16),
        compiler_params=_params(("parallel",), _nbytes((tm, d), F32) + _nbytes((tm, d), BF16)),
        name="rmsnorm",
    )(x, gain.reshape(1, d))


def _rmsnorm_small_body(x_ref, g_ref, wa_ref, wb_ref, o_ref, small_ref, wt_ref):
    @pl.when(pl.program_id(0) == 0)
    def _():
        for t, w_ref in enumerate((wa_ref, wb_ref)):
            for k0 in range(0, w_ref.shape[2], _XPOSE_COLS):
                wt_ref[k0:k0 + _XPOSE_COLS, V7X_LANES * t:V7X_LANES * (t + 1)] = (
                    w_ref[0, :, k0:k0 + _XPOSE_COLS].T.astype(BF16))

    hn = _rms(x_ref[...], g_ref[...]).astype(BF16)
    o_ref[...] = hn
    small_ref[...] = _dot(hn, wt_ref[...])


def rmsnorm_small(x, gain, wt_all, layer, row_starts, tm=512):
    m, d = x.shape
    tm = min(tm, m)
    w_tile = lambda r0: pl.BlockSpec((pl.Element(1), pl.Element(V7X_LANES), pl.Element(d)),
                                     lambda i: (layer, r0, 0))
    nbytes = (_nbytes((tm, d), F32) + _nbytes((tm, d), BF16) + 2 * _nbytes((V7X_LANES, d), F32)
              + _nbytes((d, 2 * V7X_LANES), BF16) + _nbytes((tm, 2 * V7X_LANES), F32))
    return pl.pallas_call(
        _rmsnorm_small_body,
        grid=(m // tm,),
        in_specs=[pl.BlockSpec((tm, d), lambda i: (i, 0)),
                  pl.BlockSpec((1, d), lambda i: (0, 0)),
                  w_tile(row_starts[0]), w_tile(row_starts[1])],
        out_specs=[pl.BlockSpec((tm, d), lambda i: (i, 0)),
                   pl.BlockSpec((tm, 2 * V7X_LANES), lambda i: (i, 0))],
        out_shape=[jax.ShapeDtypeStruct((m, d), BF16), jax.ShapeDtypeStruct((m, 2 * V7X_LANES), F32)],
        scratch_shapes=[pltpu.VMEM((d, 2 * V7X_LANES), BF16)],
        compiler_params=_params(("arbitrary",), nbytes),
        name="rmsnorm_small",
    )(x, gain.reshape(1, d), wt_all, wt_all)


def _mm_body(*refs, has_res):
    a_ref, w_ref = refs[0], refs[1]
    o_ref = refs[2 + has_res]
    part = _dot(a_ref[...], w_ref[...])
    if has_res:
        part = part + refs[2][...]
    o_ref[...] = part.astype(o_ref.dtype)


def matmul(a, w, *, out_dtype, res=None, tm, tn, name="matmul"):
    m, kdim = a.shape
    n = w.shape[1]
    tm, tn = min(tm, m), min(tn, n)
    assert m % tm == 0 and n % tn == 0
    has_res = res is not None
    in_specs = [pl.BlockSpec((tm, kdim), lambda i, j: (i, 0)), pl.BlockSpec((kdim, tn), lambda i, j: (0, j))]
    args = [a, w]
    nbytes = _nbytes((tm, kdim), BF16) + _nbytes((kdim, tn), BF16) + _nbytes((tm, tn), out_dtype) + _nbytes((tm, tn), F32)
    if has_res:
        in_specs.append(pl.BlockSpec((tm, tn), lambda i, j: (i, j)))
        args.append(res)
        nbytes += _nbytes((tm, tn), F32)
    return pl.pallas_call(
        functools.partial(_mm_body, has_res=has_res),
        grid=(m // tm, n // tn),
        in_specs=in_specs,
        out_specs=pl.BlockSpec((tm, tn), lambda i, j: (i, j)),
        out_shape=jax.ShapeDtypeStruct((m, n), out_dtype),
        compiler_params=_params(("parallel", "parallel"), nbytes),
        name=name,
    )(*args)


_XPOSE_ROWS, _XPOSE_COLS = 256, 512


def _slab_copy(w_hbm, stage_ref, sem, layer, slab, start0, tn, features_on_rows):
    if features_on_rows:
        src = w_hbm.at[layer, pl.ds(pl.multiple_of(start0 + slab * tn, V7X_SUBLANES), tn), :]
    else:
        src = w_hbm.at[layer, :, pl.ds(pl.multiple_of(start0 + slab * tn, V7X_LANES), tn)]
    return pltpu.make_async_copy(src, stage_ref, sem.at[0])


def _slab_body(*refs, n_a, has_res, layer, start0, features_on_rows):
    a_refs, w_hbm = refs[:n_a], refs[n_a]
    r_ref = refs[n_a + 1] if has_res else None
    o_ref, stage_ref, wb_ref, sem = refs[n_a + 1 + has_res:]
    kdim, tn = wb_ref.shape
    slab, i = pl.program_id(0), pl.program_id(1)
    copy = lambda s: _slab_copy(w_hbm, stage_ref, sem, layer, s, start0, tn, features_on_rows)

    @pl.when(i == 0)
    def _():
        @pl.when(slab == 0)
        def _():
            copy(slab).start()

        copy(slab).wait()
        if features_on_rows:
            for r0 in range(0, tn, _XPOSE_ROWS):
                r1 = min(r0 + _XPOSE_ROWS, tn)
                for k0 in range(0, kdim, _XPOSE_COLS):
                    wb_ref[k0:k0 + _XPOSE_COLS, r0:r1] = stage_ref[r0:r1, k0:k0 + _XPOSE_COLS].T.astype(BF16)
        else:
            wb_ref[...] = stage_ref[...].astype(BF16)

    @pl.when((i == 1) & (slab + 1 < pl.num_programs(0)))
    def _():
        copy(slab + 1).start()

    acc = r_ref[...] if has_res else None
    k0 = 0
    for a_ref in a_refs:
        k1 = k0 + a_ref.shape[1]
        part = _dot(a_ref[...], wb_ref[k0:k1, :])
        acc = part if acc is None else acc + part
        k0 = k1
    o_ref[...] = acc.astype(o_ref.dtype)


def slab_matmul(a_list, w_all, layer, start0, n_slabs, tn, *, features_on_rows, out_dtype, res=None,
                tm=1024, name="slab_matmul"):
    m = a_list[0].shape[0]
    kdim = sum(a.shape[1] for a in a_list)
    tm = min(tm, m)
    assert m // tm >= 2
    has_res = res is not None
    in_specs = [pl.BlockSpec((tm, a.shape[1]), lambda s, i: (i, 0)) for a in a_list]
    in_specs.append(pl.BlockSpec(memory_space=pl.ANY))
    args = list(a_list) + [w_all]
    nbytes = (2 * _nbytes((tm, kdim), BF16) + _nbytes((kdim, tn), F32) + _nbytes((kdim, tn), BF16)
              + 2 * _nbytes((tm, tn), out_dtype) + 4 * _nbytes((tm, tn), F32))
    if has_res:
        in_specs.append(pl.BlockSpec((tm, tn), lambda s, i: (i, s)))
        args.append(res)
        nbytes += 2 * _nbytes((tm, tn), F32)
    return pl.pallas_call(
        functools.partial(_slab_body, n_a=len(a_list), has_res=has_res, layer=layer, start0=start0,
                          features_on_rows=features_on_rows),
        grid=(n_slabs, m // tm),
        in_specs=in_specs,
        out_specs=pl.BlockSpec((tm, tn), lambda s, i: (i, s)),
        out_shape=jax.ShapeDtypeStruct((m, n_slabs * tn), out_dtype),
        scratch_shapes=[pltpu.VMEM((tn, kdim) if features_on_rows else (kdim, tn), F32),
                        pltpu.VMEM((kdim, tn), BF16),
                        pltpu.SemaphoreType.DMA((1,))],
        compiler_params=pltpu.CompilerParams(dimension_semantics=("arbitrary", "arbitrary"),
                                             vmem_limit_bytes=min(VMEM_REQUEST_CAP, nbytes)),
        name=name,
    )(*args)


def _gateup_copies(wg_hbm, wu_hbm, stage_ref, sem, layer, slab, width):
    c0 = pl.multiple_of(slab * FFN_SLAB, FFN_SLAB)
    return [pltpu.make_async_copy(w_hbm.at[layer, :, pl.ds(c0, width)], stage_ref.at[k, :, pl.ds(0, width)],
                                  sem.at[k])
            for k, w_hbm in enumerate((wg_hbm, wu_hbm))]


def _gateup_body(a_ref, wg_hbm, wu_hbm, o_ref, stage_ref, wgb_ref, wub_ref, sem, *, layer, last_width):
    slab, i = pl.program_id(0), pl.program_id(1)
    last = pl.num_programs(0) - 1

    def for_slab(s, action):
        @pl.when(s < last)
        def _():
            for c in _gateup_copies(wg_hbm, wu_hbm, stage_ref, sem, layer, s, FFN_SLAB):
                action(c)

        @pl.when(s == last)
        def _():
            for c in _gateup_copies(wg_hbm, wu_hbm, stage_ref, sem, layer, s, last_width):
                action(c)

    @pl.when(i == 0)
    def _():
        @pl.when(slab == 0)
        def _():
            for_slab(slab, lambda c: c.start())

        for_slab(slab, lambda c: c.wait())
        wgb_ref[...] = stage_ref[0].astype(BF16)
        wub_ref[...] = stage_ref[1].astype(BF16)

    @pl.when((i == 1) & (slab < last))
    def _():
        for_slab(slab + 1, lambda c: c.start())

    a = a_ref[...]
    g = _dot(a, wgb_ref[...])
    u = _dot(a, wub_ref[...])
    o_ref[...] = (_silu(g) * u).astype(o_ref.dtype)


def ffn_gate_up(a, wg_all, wu_all, layer, tm=1024):
    m, kdim = a.shape
    n = wg_all.shape[2]
    tm = min(tm, m)
    n_slabs = pl.cdiv(n, FFN_SLAB)
    last_width = n - (n_slabs - 1) * FFN_SLAB
    assert m // tm >= 2 and last_width % V7X_LANES == 0
    nbytes = (2 * _nbytes((tm, kdim), BF16) + 2 * _nbytes((kdim, FFN_SLAB), F32)
              + 2 * _nbytes((kdim, FFN_SLAB), BF16) + 2 * _nbytes((tm, FFN_SLAB), BF16)
              + 6 * _nbytes((tm, FFN_SLAB), F32))
    return pl.pallas_call(
        functools.partial(_gateup_body, layer=layer, last_width=last_width),
        grid=(n_slabs, m // tm),
        in_specs=[pl.BlockSpec((tm, kdim), lambda s, i: (i, 0)),
                  pl.BlockSpec(memory_space=pl.ANY),
                  pl.BlockSpec(memory_space=pl.ANY)],
        out_specs=pl.BlockSpec((tm, FFN_SLAB), lambda s, i: (i, s)),
        out_shape=jax.ShapeDtypeStruct((m, n), BF16),
        scratch_shapes=[pltpu.VMEM((2, kdim, FFN_SLAB), F32),
                        pltpu.VMEM((kdim, FFN_SLAB), BF16), pltpu.VMEM((kdim, FFN_SLAB), BF16),
                        pltpu.SemaphoreType.DMA((2,))],
        compiler_params=pltpu.CompilerParams(dimension_semantics=("arbitrary", "arbitrary"),
                                             vmem_limit_bytes=min(VMEM_REQUEST_CAP, nbytes)),
        name="ffn_gate_up",
    )(a, wg_all, wu_all)


def _swa_body(q_ref, kvc_ref, kvp_ref, bias_ref, sink_ref, qg_ref, kg_ref, og_ref, e_ref, et_ref, o_ref, acc_ref):
    hd, blk, grp = SWA_HEAD_DIM, SWA_BLOCK, SWA_GROUP
    ones = jnp.ones((2 * blk, V7X_LANES), BF16)
    first_variant = jnp.minimum(pl.program_id(1), 1)

    for sub in range(SWA_BLOCKS_PER_STEP):
        rows = slice(blk * sub, blk * (sub + 1))
        kv_prev = kvp_ref if sub == 0 else kvc_ref.at[blk * (sub - 1):blk * sub]
        kband = jnp.concatenate([kv_prev[:, :SWA_KV_WIDTH], kvc_ref[rows, :SWA_KV_WIDTH]], axis=0).astype(F32)
        vband = jnp.concatenate([kv_prev[:, SWA_KV_WIDTH:], kvc_ref[rows, SWA_KV_WIDTH:]], axis=0)
        variant = first_variant if sub == 0 else 1

        q = q_ref[rows, :].astype(F32)
        ssq = _dot_exact_rhs(q * q, e_ref[...], 2)
        inv = lax.rsqrt(ssq * (1.0 / hd) + EPS)
        qn = (q * _dot_exact_rhs(inv, et_ref[...], 3) * qg_ref[...]).astype(BF16)

        for j in range(SWA_KV_HEADS):
            kn = _rms(kband[:, hd * j:hd * (j + 1)], kg_ref[...]).astype(BF16)
            qs = jnp.concatenate([qn[:, hd * (grp * j + g):hd * (grp * j + g + 1)] for g in range(grp)], axis=0)
            s = _dot_nt(qs, kn) * (hd ** -0.5) + bias_ref[variant, j]
            sink = sink_ref[j]
            s_prev, s_cur = s[:, :blk], s[:, blk:]
            row_max = jnp.max(jnp.maximum(s_prev, s_cur), axis=-1, keepdims=True)
            mx = jnp.maximum(jnp.broadcast_to(row_max, sink.shape), sink)
            p = jnp.concatenate([jnp.exp(s_prev - mx), jnp.exp(s_cur - mx)], axis=1).astype(BF16)
            total = _dot(p, ones) + jnp.exp(sink - mx)
            o = _dot(p, vband[:, hd * j:hd * (j + 1)]) / total[:, :hd]
            for g in range(grp):
                h = grp * j + g
                acc_ref[sub, :, hd * h:hd * (h + 1)] = o[blk * g:blk * (g + 1), :]
        o_ref[rows, :] = _rms(acc_ref[sub], og_ref[...]).astype(o_ref.dtype)


def swa_mixer(qkv, bias, sinks, q_gain, k_gain, out_gain, batch, seq):
    nb = seq // SWA_BLOCK
    kv_blk = SWA_WIDTH // (2 * SWA_KV_WIDTH)
    rows = SWA_GROUP * SWA_BLOCK
    no_prev = jnp.arange(2 * SWA_BLOCK) < SWA_BLOCK
    bias_g = jnp.stack([jnp.where(no_prev, -jnp.inf, bias), bias]).reshape(2, SWA_KV_HEADS, rows, 2 * SWA_BLOCK)
    sink_col = jnp.broadcast_to(jnp.repeat(sinks, SWA_BLOCK)[:, None],
                                (SWA_HEADS * SWA_BLOCK, V7X_LANES)).reshape(SWA_KV_HEADS, rows, V7X_LANES)
    head_of_col = jnp.arange(SWA_WIDTH) // SWA_HEAD_DIM
    e = (head_of_col[:, None] == jnp.arange(V7X_LANES)[None, :]).astype(BF16)
    const = lambda shape: pl.BlockSpec(shape, lambda b, n: (0,) * len(shape))
    nsub = SWA_BLOCKS_PER_STEP
    step_rows = nsub * SWA_BLOCK
    ns = seq // step_rows
    assert seq % step_rows == 0
    nbytes = (_nbytes((step_rows, SWA_WIDTH), BF16) * 2 + 2 * _nbytes((step_rows, 2 * SWA_KV_WIDTH), BF16)
              + _nbytes(bias_g.shape, F32) + _nbytes((SWA_KV_HEADS, rows, V7X_LANES), F32)
              + 2 * _nbytes(e.shape, BF16) + _nbytes((step_rows, SWA_WIDTH), F32)
              + 4 * _nbytes((rows, 2 * SWA_BLOCK), F32))
    return pl.pallas_call(
        _swa_body,
        grid=(batch, ns),
        in_specs=[pl.BlockSpec((step_rows, SWA_WIDTH), lambda b, n: (b * ns + n, 0)),
                  pl.BlockSpec((step_rows, 2 * SWA_KV_WIDTH), lambda b, n: (b * ns + n, kv_blk)),
                  pl.BlockSpec((SWA_BLOCK, 2 * SWA_KV_WIDTH),
                               lambda b, n: (b * nb + jnp.maximum(n * nsub - 1, 0), kv_blk)),
                  const(bias_g.shape), const(sink_col.shape),
                  const((1, SWA_WIDTH)), const((1, SWA_HEAD_DIM)), const((1, SWA_WIDTH)),
                  const(e.shape), const(e.T.shape)],
        out_specs=pl.BlockSpec((step_rows, SWA_WIDTH), lambda b, n: (b * ns + n, 0)),
        out_shape=jax.ShapeDtypeStruct((batch * seq, SWA_WIDTH), BF16),
        scratch_shapes=[pltpu.VMEM((nsub, SWA_BLOCK, SWA_WIDTH), F32)],
        compiler_params=_params(("parallel", "parallel"), nbytes),
        name="swa_mixer",
    )(qkv, qkv, qkv, bias_g, sink_col, jnp.tile(q_gain, SWA_HEADS).reshape(1, -1), k_gain.reshape(1, -1),
      out_gain.reshape(1, -1), e, e.T)


def _t5_bucket(dist):
    n = jnp.maximum(dist, 0)
    max_exact = REL_BUCKETS // 2
    nf = jnp.maximum(n, 1).astype(F32)
    large = max_exact + (jnp.log(nf / max_exact) / math.log(REL_MAX_DIST / max_exact)
                         * (REL_BUCKETS - max_exact)).astype(jnp.int32)
    large = jnp.minimum(large, REL_BUCKETS - 1)
    return jnp.where(n < max_exact, n, large)


def swa_band_bias(rel_bias):
    i = jnp.arange(SWA_BLOCK, dtype=jnp.int32)[:, None]
    j = jnp.arange(2 * SWA_BLOCK, dtype=jnp.int32)[None, :]
    dist = i + SWA_BLOCK - j
    onehot = _t5_bucket(dist)[None] == jnp.arange(REL_BUCKETS, dtype=jnp.int32)[:, None, None]
    bias = jnp.sum(jnp.where(onehot[:, None], rel_bias[:, :, None, None], 0.0), axis=0)
    in_window = (dist >= 0) & (dist < SWA_WINDOW)
    return jnp.where(in_window[None], bias, -jnp.inf)


def _gla_body(x_ref, gl_ref, wup_ref, bup_ref, gain_ref, wq_ref, wk_ref, wv_ref, wo_ref,
              o_ref, wq16_ref, wkv16_ref, wo16_ref, state_ref):
    dk, dv, c = GLA_KEY_DIM, GLA_VAL_DIM, GLA_CHUNK

    wq16_ref[...] = wq_ref[...].astype(BF16)
    wkv16_ref[:, :X_WIDTH] = wk_ref[...].astype(BF16)
    wkv16_ref[:, X_WIDTH:] = wv_ref[...].astype(BF16)
    wo16_ref[...] = wo_ref[...].astype(BF16)

    @pl.when(pl.program_id(1) == 0)
    def _():
        state_ref[...] = jnp.zeros_like(state_ref)

    w_hi, w_lo = _split_bf16(wup_ref[...], 2)
    tb = min(GLA_TIME_BLOCK, x_ref.shape[0])
    row = lax.broadcasted_iota(jnp.int32, (tb, tb), 0)
    colm = lax.broadcasted_iota(jnp.int32, (tb, tb), 1)
    same_chunk_lower = (((row // c) == (colm // c)) & (colm <= row)).astype(BF16)
    ri = lax.broadcasted_iota(jnp.int32, (c, c), 0)
    ci = lax.broadcasted_iota(jnp.int32, (c, c), 1)
    causal = ci <= ri
    for r0 in range(0, x_ref.shape[0], c):
        if r0 % tb == 0:
            gl_hi, gl_lo = _split_bf16(gl_ref[r0:r0 + tb, :], 2)
            pre = _dot(gl_hi, w_hi) + _dot(gl_hi, w_lo) + _dot(gl_lo, w_hi) + bup_ref[...]
            g = (jnp.minimum(pre, 0.0) - jnp.log1p(jnp.exp(-jnp.abs(pre)))) * (1.0 / GLA_GATE_NORMALIZER)
            bcum_all = _dot_exact_lhs(same_chunk_lower, g, 3)
        rb = r0 % tb
        for h in range(GLA_HEADS):
            bcum = bcum_all[rb:rb + c, dk * h:dk * (h + 1)]
            blast = bcum[c - 1:c, :]
            q = x_ref[r0:r0 + c, dk * h:dk * (h + 1)].astype(F32)
            k = x_ref[r0:r0 + c, GLA_KEY_WIDTH + dk * h:GLA_KEY_WIDTH + dk * (h + 1)].astype(F32)
            v = x_ref[r0:r0 + c, 2 * GLA_KEY_WIDTH + dv * h:2 * GLA_KEY_WIDTH + dv * (h + 1)]
            r = x_ref[r0:r0 + c, 2 * GLA_KEY_WIDTH + GLA_WIDTH + dv * h:
                      2 * GLA_KEY_WIDTH + GLA_WIDTH + dv * (h + 1)].astype(F32)
            qd = (q * (dk ** -0.5) * jnp.exp(bcum)).astype(BF16)
            kd = (k * jnp.exp(-bcum)).astype(BF16)
            kl = (k * jnp.exp(blast - bcum)).astype(BF16)
            att = jnp.where(causal, _dot_nt(qd, kd), 0.0)
            st = state_ref[h]
            o = _dot(att.astype(BF16), v) + _dot_nt(qd, st.astype(BF16))
            state_ref[h] = st * jnp.exp(blast) + _dot_tn(v, kl)
            o = _rms(o, gain_ref[...]) * _silu(r)
            o_ref[r0:r0 + c, dv * h:dv * (h + 1)] = o.astype(o_ref.dtype)


def gla_mixer(x, small, w_up_pad, b_up, norm_gain, batch, seq, x_w_q, x_w_k, x_w_v, x_w_o, layer):
    tb = min(GLA_BLOCKS_PER_STEP * GLA_TIME_BLOCK, seq)
    nt = seq // tb
    steps = batch * nt
    width = x.shape[1]
    d, xw = x_w_q.shape[1:]
    assert d % (steps * V7X_SUBLANES) == 0 and xw % (steps * 2 * V7X_SUBLANES) == 0
    row_blk = lambda rows, cols: pl.BlockSpec((rows // steps, cols), lambda b, t: (b * nt + t, 0))
    layer_blk = lambda rows, cols: pl.BlockSpec((None, rows // steps, cols), lambda b, t: (layer, b * nt + t, 0))
    nbytes = (_nbytes((tb, width), BF16) + _nbytes((tb, V7X_LANES), F32) + _nbytes(w_up_pad.shape, F32)
              + _nbytes((tb, GLA_WIDTH), BF16) + _nbytes((GLA_HEADS, GLA_VAL_DIM, GLA_KEY_DIM), F32)
              + 8 * _nbytes((tb, GLA_KEY_WIDTH), F32) + 6 * _nbytes((d // steps, xw), F32))
    return pl.pallas_call(
        _gla_body,
        grid=(batch, nt),
        in_specs=[pl.BlockSpec((tb, width), lambda b, t: (b * nt + t, 0)),
                  pl.BlockSpec((tb, V7X_LANES), lambda b, t: (b * nt + t, 0)),
                  pl.BlockSpec(w_up_pad.shape, lambda b, t: (0, 0)),
                  pl.BlockSpec((1, GLA_KEY_WIDTH), lambda b, t: (0, 0)),
                  pl.BlockSpec((1, GLA_VAL_DIM), lambda b, t: (0, 0)),
                  layer_blk(d, xw), layer_blk(d, xw), layer_blk(d, xw), layer_blk(xw, d)],
        out_specs=[pl.BlockSpec((tb, GLA_WIDTH), lambda b, t: (b * nt + t, 0)),
                   row_blk(d, xw), row_blk(d, 2 * xw), row_blk(xw, d)],
        out_shape=[jax.ShapeDtypeStruct((batch * seq, GLA_WIDTH), BF16),
                   jax.ShapeDtypeStruct((d, xw), BF16), jax.ShapeDtypeStruct((d, 2 * xw), BF16),
                   jax.ShapeDtypeStruct((xw, d), BF16)],
        scratch_shapes=[pltpu.VMEM((GLA_HEADS, GLA_VAL_DIM, GLA_KEY_DIM), F32)],
        compiler_params=_params(("arbitrary", "arbitrary"), nbytes),
        name="gla_mixer",
    )(x, small, w_up_pad, b_up.reshape(1, -1), norm_gain.reshape(1, -1), x_w_q, x_w_k, x_w_v, x_w_o)


def _ssd_body(x_ref, dt_ref, cw_ref, cb_ref, dtb_ref, alog_ref, dexp_ref, gain_ref, expand_ref, shift_ref,
              wcast_in_ref, o_ref, wcast_out_ref, state_ref, tail_ref, y_ref, *, wcast_blocks):
    L, P, N = SSD_CHUNK, SSD_HEAD_DIM, SSD_STATE
    gw = SSD_GROUP_WIDTH

    @pl.when(pl.program_id(0) * pl.num_programs(1) + pl.program_id(1) < wcast_blocks)
    def _():
        wcast_out_ref[...] = wcast_in_ref[...].astype(BF16)

    @pl.when(pl.program_id(1) == 0)
    def _():
        state_ref[...] = jnp.zeros_like(state_ref)
        tail_ref[...] = jnp.zeros_like(tail_ref)

    xin = x_ref[:, SSD_WIDTH:]
    xin32 = xin.astype(F32)
    shifted = _dot(shift_ref[...], xin)
    conv = cb_ref[...] + cw_ref[SSD_CONV - 1:SSD_CONV, :] * xin32
    head = jnp.zeros((8, SSD_CONV_CH), F32)
    for j in range(1, SSD_CONV):
        wj = cw_ref[SSD_CONV - 1 - j:SSD_CONV - j, :]
        conv = conv + wj * shifted[L * (j - 1):L * j, :]
        head = head + wj * tail_ref[8 - j:16 - j, :]
    conv = jnp.concatenate([conv[0:8, :] + head, conv[8:, :]], axis=0)
    tail_ref[0:8, :] = xin32[L - 8:, :]
    xbc = _silu(conv)
    xs = xbc[:, :SSD_WIDTH]
    bm = xbc[:, SSD_WIDTH:SSD_WIDTH + SSD_BC_WIDTH].astype(BF16)
    cm = xbc[:, SSD_WIDTH + SSD_BC_WIDTH:].astype(BF16)

    dt = _softplus(dt_ref[...] + dtb_ref[...])
    dta = dt * (-jnp.exp(alog_ref[...]))
    ri = lax.broadcasted_iota(jnp.int32, (L, L), 0)
    ci = lax.broadcasted_iota(jnp.int32, (L, L), 1)
    causal = ci <= ri
    a_cum = _dot_exact_lhs(causal.astype(BF16), dta, 3)
    a_cum_t = a_cum.T
    expand = expand_ref[...]
    a_exp = _dot_exact_rhs(a_cum, expand, 3)
    dt_exp = _dot_exact_rhs(dt, expand, 2)
    a_last = a_exp[L - 1:L, :]
    xd = xs * dt_exp
    xdec = (xd * jnp.exp(a_last - a_exp)).astype(BF16)
    xd16 = xd.astype(BF16)
    out_scale = jnp.exp(a_exp)

    for g in range(SSD_GROUPS):
        bg = bm[:, N * g:N * (g + 1)]
        cg = cm[:, N * g:N * (g + 1)]
        cb = _dot_nt(cg, bg)
        for kk in range(SSD_HEADS_PER_GROUP):
            h = g * SSD_HEADS_PER_GROUP + kk
            hl = SSD_DT_LANE0 + h
            diff = a_cum[:, hl:hl + 1] - a_cum_t[hl:hl + 1, :]
            m = cb * jnp.exp(jnp.where(causal, diff, -jnp.inf))
            y_ref[:, P * h:P * (h + 1)] = _dot(m.astype(BF16), xd16[:, P * h:P * (h + 1)])
        sg = state_ref[:, gw * g:gw * (g + 1)]
        y_off = _dot(cg, sg.astype(BF16)) * out_scale[:, gw * g:gw * (g + 1)]
        state_ref[:, gw * g:gw * (g + 1)] = (sg * jnp.exp(a_last[:, gw * g:gw * (g + 1)])
                                             + _dot_tn(bg, xdec[:, gw * g:gw * (g + 1)]))
        yg = y_ref[:, gw * g:gw * (g + 1)] + y_off + xs[:, gw * g:gw * (g + 1)] * dexp_ref[:, gw * g:gw * (g + 1)]
        yg = yg * _silu(x_ref[:, gw * g:gw * (g + 1)].astype(F32))
        o_ref[:, gw * g:gw * (g + 1)] = _rms(yg, gain_ref[:, gw * g:gw * (g + 1)]).astype(o_ref.dtype)


def ssd_mixer(x, small, conv_w, conv_b, dt_bias, a_log, d_skip, norm_gain, batch, seq, wcast_all, layer):
    L = SSD_CHUNK
    nc = seq // L
    width = x.shape[1]
    _, wc_rows, wc_cols = wcast_all.shape
    wc_tile = V7X_MXU_WIDTH
    wc_blocks = wc_rows // wc_tile
    assert wc_rows % wc_tile == 0 and wc_blocks <= batch * nc
    wc_index = lambda b, c: jnp.minimum(b * nc + c, wc_blocks - 1)
    t = jnp.arange(L)
    shift = jnp.concatenate([(t[:, None] - j == t[None, :]) for j in range(1, SSD_CONV)], axis=0).astype(BF16)
    lane_pad = (SSD_DT_LANE0, V7X_LANES - SSD_DT_LANE0 - SSD_HEADS)
    dtb = jnp.pad(dt_bias, lane_pad).reshape(1, V7X_LANES)
    alog = jnp.pad(a_log, lane_pad).reshape(1, V7X_LANES)
    dexp = jnp.repeat(d_skip, SSD_HEAD_DIM).reshape(1, SSD_WIDTH)
    expand = (jnp.arange(V7X_LANES)[:, None] - SSD_DT_LANE0
              == (jnp.arange(SSD_WIDTH)[None, :] // SSD_HEAD_DIM)).astype(BF16)
    nbytes = (_nbytes((L, width), BF16) + _nbytes((L, V7X_LANES), F32) + _nbytes((L, SSD_WIDTH), BF16)
              + _nbytes(expand.shape, BF16) + 2 * _nbytes((SSD_STATE, SSD_WIDTH), F32)
              + 16 * _nbytes((L, SSD_CONV_CH), F32) + _nbytes((wc_tile, wc_cols), F32)
              + _nbytes((wc_tile, wc_cols), BF16))
    return pl.pallas_call(
        functools.partial(_ssd_body, wcast_blocks=wc_blocks),
        grid=(batch, nc),
        in_specs=[pl.BlockSpec((L, width), lambda b, c: (b * nc + c, 0)),
                  pl.BlockSpec((L, V7X_LANES), lambda b, c: (b * nc + c, 1)),
                  pl.BlockSpec((SSD_CONV, SSD_CONV_CH), lambda b, c: (0, 0)),
                  pl.BlockSpec((1, SSD_CONV_CH), lambda b, c: (0, 0)),
                  pl.BlockSpec((1, V7X_LANES), lambda b, c: (0, 0)),
                  pl.BlockSpec((1, V7X_LANES), lambda b, c: (0, 0)),
                  pl.BlockSpec((1, SSD_WIDTH), lambda b, c: (0, 0)),
                  pl.BlockSpec((1, SSD_WIDTH), lambda b, c: (0, 0)),
                  pl.BlockSpec(expand.shape, lambda b, c: (0, 0)),
                  pl.BlockSpec(shift.shape, lambda b, c: (0, 0)),
                  pl.BlockSpec((None, wc_tile, wc_cols), lambda b, c: (layer, wc_index(b, c), 0))],
        out_specs=[pl.BlockSpec((L, SSD_WIDTH), lambda b, c: (b * nc + c, 0)),
                   pl.BlockSpec((wc_tile, wc_cols), lambda b, c: (wc_index(b, c), 0))],
        out_shape=[jax.ShapeDtypeStruct((batch * seq, SSD_WIDTH), BF16),
                   jax.ShapeDtypeStruct((wc_rows, wc_cols), BF16)],
        scratch_shapes=[pltpu.VMEM((SSD_STATE, SSD_WIDTH), F32),
                        pltpu.VMEM((16, SSD_CONV_CH), F32),
                        pltpu.VMEM((L, SSD_WIDTH), F32)],
        compiler_params=_params(("arbitrary", "arbitrary"), nbytes),
        name="ssd_mixer",
    )(x, small, conv_w.reshape(SSD_CONV, SSD_CONV_CH), conv_b.reshape(1, -1), dtb, alog, dexp,
      norm_gain.reshape(1, -1), expand, shift, wcast_all)


def _xattn_body(h_ref, kv_ref, wq_ref, wo_ref, lnx_ref, qg_ref, kg_ref, lnf_ref, o_ref, hf_ref, att_ref):
    hd = X_HEAD_DIM
    mem_len = kv_ref.shape[0]
    half = mem_len // 2
    ones = jnp.ones((mem_len, V7X_LANES), BF16)
    h = h_ref[...]
    q = _dot(_rms(h, lnx_ref[...]).astype(BF16), wq_ref[...])
    for i in range(X_HEADS):
        qh = _rms(q[:, hd * i:hd * (i + 1)], qg_ref[...]).astype(BF16)
        kh = _rms(kv_ref[:, hd * i:hd * (i + 1)], kg_ref[...]).astype(BF16)
        vh = kv_ref[:, X_WIDTH + hd * i:X_WIDTH + hd * (i + 1)].astype(BF16)
        s = _dot_nt(qh, kh) * (hd ** -0.5)
        s_lo, s_hi = s[:, :half], s[:, half:]
        mx = jnp.broadcast_to(jnp.max(jnp.maximum(s_lo, s_hi), axis=-1, keepdims=True), s_lo.shape)
        p = jnp.concatenate([jnp.exp(s_lo - mx), jnp.exp(s_hi - mx)], axis=1).astype(BF16)
        att_ref[:, hd * i:hd * (i + 1)] = (_dot(p, vh) / _dot(p, ones)).astype(BF16)
    h_new = h + _dot(att_ref[...], wo_ref[...])
    o_ref[...] = h_new
    hf_ref[...] = _rms(h_new, lnf_ref[...]).astype(hf_ref.dtype)


def cross_attention_block(h, kv, w_q, w_o, ln_x, q_gain, k_gain, ln_ffn, batch, seq, mem_len, tq=512):
    m, d = h.shape
    tq = min(tq, seq)
    nq = seq // tq
    assert mem_len == 2 * V7X_LANES and X_HEAD_DIM == V7X_LANES
    const = lambda shape: pl.BlockSpec(shape, lambda b, t: (0,) * len(shape), pipeline_mode=pl.Buffered(1))
    row_blk = lambda width: pl.BlockSpec((tq, width), lambda b, t: (b * nq + t, 0))
    nbytes = (2 * _nbytes((tq, d), F32) + _nbytes((tq, d), BF16) + _nbytes((mem_len, 2 * X_WIDTH), F32)
              + 2 * _nbytes((d, X_WIDTH), BF16) + 2 * _nbytes((tq, d), F32))
    return pl.pallas_call(
        _xattn_body,
        grid=(batch, nq),
        in_specs=[row_blk(d),
                  pl.BlockSpec((mem_len, 2 * X_WIDTH), lambda b, t: (b, 0)),
                  const((d, X_WIDTH)), const((X_WIDTH, d)), const((1, d)),
                  const((1, X_HEAD_DIM)), const((1, X_HEAD_DIM)), const((1, d))],
        out_specs=[row_blk(d), row_blk(d)],
        out_shape=[jax.ShapeDtypeStruct((m, d), F32), jax.ShapeDtypeStruct((m, d), BF16)],
        scratch_shapes=[pltpu.VMEM((tq, X_WIDTH), BF16)],
        compiler_params=_params(("parallel", "parallel"), nbytes),
        name="xattn_block",
    )(h, kv, w_q, w_o, ln_x.reshape(1, -1), q_gain.reshape(1, -1), k_gain.reshape(1, -1), ln_ffn.reshape(1, -1))


_IN_OFF = np.cumsum([0, SWA_WIDTH, SWA_KV_WIDTH, SWA_KV_WIDTH, GLA_KEY_WIDTH, GLA_KEY_WIDTH, GLA_WIDTH,
                     GLA_WIDTH, GLA_GATE_RANK, SSD_WIDTH, SSD_CONV_CH, SSD_HEADS]).tolist()
SWA_COL0, GLA_COL0, GLOW_COL0, SSD_COL0, DT_COL0 = _IN_OFF[0], _IN_OFF[3], _IN_OFF[7], _IN_OFF[8], _IN_OFF[10]


IN_DIM = _IN_OFF[-1]
SMALL_ROW_STARTS = (GLOW_COL0, IN_DIM - V7X_LANES)
assert SMALL_ROW_STARTS[1] + SSD_DT_LANE0 == DT_COL0


def kernel(x, mem, rel_bias, ln_mix, w_in, swa_q_gain, swa_k_gain, swa_sinks, swa_out_gain, gla_w_gk_up, gla_b_gk_up, gla_norm_gain, ssd_conv_w, ssd_conv_b, ssd_dt_bias, ssd_a_log, ssd_d, ssd_norm_gain, w_mix_out, ln_x, ln_mem, x_w_q, x_w_k, x_w_v, x_w_o, x_q_gain, x_k_gain, ln_ffn, ffn_w_gate, ffn_w_up, ffn_w_down):
    batch, seq, d = x.shape
    mem_len = mem.shape[1]
    m = batch * seq
    band_bias = swa_band_bias(rel_bias)
    h = x.reshape(m, d)
    mem2 = mem.reshape(batch * mem_len, d)
    w_in_t = jnp.swapaxes(w_in, 1, 2)
    for l in range(DEPTH):
        w_up_pad = jnp.pad(gla_w_gk_up[l], ((0, V7X_LANES - GLA_GATE_RANK), (0, 0)))
        hn, p_small = rmsnorm_small(h, ln_mix[l], w_in_t, l, SMALL_ROW_STARTS)
        proj = functools.partial(slab_matmul, [hn], w_in_t, l, features_on_rows=True, out_dtype=BF16)
        p_swa = proj(SWA_COL0, 1, GLA_COL0 - SWA_COL0, tm=512, name="proj_swa")
        p_gla = proj(GLA_COL0, (GLOW_COL0 - GLA_COL0) // PROJ_SLAB, PROJ_SLAB, name="proj_gla")
        p_ssd = proj(SSD_COL0, (DT_COL0 - SSD_COL0) // PROJ_SLAB, PROJ_SLAB, name="proj_ssd")
        y_a = swa_mixer(p_swa, band_bias, swa_sinks[l], swa_q_gain[l], swa_k_gain[l], swa_out_gain[l], batch, seq)
        y_b, w_q16, w_kv16, w_o16 = gla_mixer(p_gla, p_small, w_up_pad, gla_b_gk_up[l], gla_norm_gain[l], batch, seq,
                                              x_w_q, x_w_k, x_w_v, x_w_o, l)
        y_c, w_down16 = ssd_mixer(p_ssd, p_small, ssd_conv_w[l], ssd_conv_b[l], ssd_dt_bias[l], ssd_a_log[l],
                                  ssd_d[l], ssd_norm_gain[l], batch, seq, ffn_w_down, l)
        h = slab_matmul([y_a, y_b, y_c], w_mix_out, l, 0, d // MIX_SLAB, MIX_SLAB, features_on_rows=False,
                        out_dtype=F32, res=h, tm=MIX_TOKENS, name="mix_out")
        memn = rmsnorm(mem2, ln_mem[l])
        kv = matmul(memn, w_kv16, out_dtype=F32, tm=1024, tn=512, name="xattn_kv")
        h, hf = cross_attention_block(h, kv, w_q16, w_o16, ln_x[l], x_q_gain[l], x_k_gain[l], ln_ffn[l],
                                      batch, seq, mem_len)
        hidden = ffn_gate_up(hf, ffn_w_gate, ffn_w_up, l)
        h = matmul(hidden, w_down16, out_dtype=F32, res=h, tm=512, tn=512, name="ffn_down")
    return h.reshape(batch, seq, d)
```

```python
import functools
import math

import numpy as np
import jax
import jax.numpy as jnp
from jax import lax
from jax.experimental import pallas as pl
from jax.experimental.pallas import tpu as pltpu

F32 = jnp.float32
BF16 = jnp.bfloat16

D_MODEL = 4096
DEPTH = 2
EPS = 1e-6
SWA_WIDTH = 1024
SWA_HEAD_DIM = 64
SWA_HEADS = 16
SWA_KV_HEADS = 2
SWA_GROUP = SWA_HEADS // SWA_KV_HEADS
SWA_KV_WIDTH = SWA_KV_HEADS * SWA_HEAD_DIM
SWA_WINDOW = 128
SWA_BLOCK = 128
REL_BUCKETS = 32
REL_MAX_DIST = 128
GLA_WIDTH = 1024
GLA_HEADS = 4
GLA_VAL_DIM = 256
GLA_KEY_DIM = 128
GLA_KEY_WIDTH = GLA_HEADS * GLA_KEY_DIM
GLA_GATE_RANK = 16
GLA_GATE_NORMALIZER = 16.0
GLA_CHUNK = 64
SSD_WIDTH = 2048
SSD_HEAD_DIM = 64
SSD_HEADS = 32
SSD_GROUPS = 8
SSD_HEADS_PER_GROUP = SSD_HEADS // SSD_GROUPS
SSD_STATE = 128
SSD_CONV = 4
SSD_CHUNK = 128
SSD_BC_WIDTH = SSD_GROUPS * SSD_STATE
SSD_CONV_CH = SSD_WIDTH + 2 * SSD_BC_WIDTH
SSD_GROUP_WIDTH = SSD_WIDTH // SSD_GROUPS
X_HEADS = 4
X_HEAD_DIM = 128
X_WIDTH = X_HEADS * X_HEAD_DIM
FFN_HIDDEN = 11008

V7X_LANES = 128
V7X_SUBLANES = 8
V7X_VMEM_BYTES = 64 * 1024 * 1024
VMEM_REQUEST_CAP = (V7X_VMEM_BYTES * 7) // 8
V7X_MXU_WIDTH = 256
FFN_SLAB = 2 * V7X_MXU_WIDTH
PROJ_SLAB = 4 * V7X_MXU_WIDTH
MIX_SLAB = 4 * V7X_MXU_WIDTH
MIX_TOKENS = 512
GLA_TIME_BLOCK = 256
GLA_BLOCKS_PER_STEP = 4
SWA_BLOCKS_PER_STEP = 8
SSD_DT_LANE0 = V7X_LANES - SSD_HEADS


VMEM_TEMP_ALLOWANCE = 16 * 1024 * 1024


def _params(semantics, block_bytes):
    limit = min(VMEM_REQUEST_CAP, 2 * block_bytes + VMEM_TEMP_ALLOWANCE)
    return pltpu.CompilerParams(dimension_semantics=semantics, vmem_limit_bytes=int(limit))


def _nbytes(shape, dtype):
    return int(np.prod(shape)) * jnp.dtype(dtype).itemsize


def _split_bf16(x, terms):
    parts = []
    r = x
    for t in range(terms):
        p = r.astype(BF16)
        parts.append(p)
        if t + 1 < terms:
            r = r - p.astype(F32)
    return parts


def _dot(a, b):
    return jnp.dot(a, b, preferred_element_type=F32)


def _dot_nt(a, b):
    return lax.dot_general(a, b, (((1,), (1,)), ((), ())), preferred_element_type=F32)


def _dot_tn(a, b):
    return lax.dot_general(a, b, (((0,), (0,)), ((), ())), preferred_element_type=F32)


def _dot_exact_lhs(a_bf16, x, terms):
    acc = None
    for p in _split_bf16(x, terms):
        d = _dot(a_bf16, p)
        acc = d if acc is None else acc + d
    return acc


def _dot_exact_rhs(x, b_bf16, terms):
    acc = None
    for p in _split_bf16(x, terms):
        d = _dot(p, b_bf16)
        acc = d if acc is None else acc + d
    return acc


def _silu(x):
    return x / (1.0 + jnp.exp(-x))


def _softplus(x):
    return jnp.maximum(x, 0.0) + jnp.log1p(jnp.exp(-jnp.abs(x)))


def _rms(x, gain):
    return x * lax.rsqrt(jnp.mean(x * x, axis=-1, keepdims=True) + EPS) * gain


def _rmsnorm_body(x_ref, g_ref, o_ref):
    o_ref[...] = _rms(x_ref[...], g_ref[...]).astype(o_ref.dtype)


def rmsnorm(x, gain, tm=256):
    m, d = x.shape
    tm = min(tm, m)
    return pl.pallas_call(
        _rmsnorm_body,
        grid=(m // tm,),
        in_specs=[pl.BlockSpec((tm, d), lambda i: (i, 0)),
                  pl.BlockSpec((1, d), lambda i: (0, 0))],
        out_specs=pl.BlockSpec((tm, d), lambda i: (i, 0)),
        out_shape=jax.ShapeDtypeStruct((m, d), BF16),
        compiler_params=_params(("parallel",), _nbytes((tm, d), F32) + _nbytes((tm, d), BF16)),
        name="rmsnorm",
    )(x, gain.reshape(1, d))


def _rmsnorm_small_body(x_ref, g_ref, wa_ref, wb_ref, o_ref, small_ref, wt_ref):
    @pl.when(pl.program_id(0) == 0)
    def _():
        for t, w_ref in enumerate((wa_ref, wb_ref)):
            for k0 in range(0, w_ref.shape[2], _XPOSE_COLS):
                wt_ref[k0:k0 + _XPOSE_COLS, V7X_LANES * t:V7X_LANES * (t + 1)] = (
                    w_ref[0, :, k0:k0 + _XPOSE_COLS].T.astype(BF16))

    hn = _rms(x_ref[...], g_ref[...]).astype(BF16)
    o_ref[...] = hn
    small_ref[...] = _dot(hn, wt_ref[...])


def rmsnorm_small(x, gain, wt_all, layer, row_starts, tm=512):
    m, d = x.shape
    tm = min(tm, m)
    w_tile = lambda r0: pl.BlockSpec((pl.Element(1), pl.Element(V7X_LANES), pl.Element(d)),
                                     lambda i: (layer, r0, 0))
    nbytes = (_nbytes((tm, d), F32) + _nbytes((tm, d), BF16) + 2 * _nbytes((V7X_LANES, d), F32)
              + _nbytes((d, 2 * V7X_LANES), BF16) + _nbytes((tm, 2 * V7X_LANES), F32))
    return pl.pallas_call(
        _rmsnorm_small_body,
        grid=(m // tm,),
        in_specs=[pl.BlockSpec((tm, d), lambda i: (i, 0)),
                  pl.BlockSpec((1, d), lambda i: (0, 0)),
                  w_tile(row_starts[0]), w_tile(row_starts[1])],
        out_specs=[pl.BlockSpec((tm, d), lambda i: (i, 0)),
                   pl.BlockSpec((tm, 2 * V7X_LANES), lambda i: (i, 0))],
        out_shape=[jax.ShapeDtypeStruct((m, d), BF16), jax.ShapeDtypeStruct((m, 2 * V7X_LANES), F32)],
        scratch_shapes=[pltpu.VMEM((d, 2 * V7X_LANES), BF16)],
        compiler_params=_params(("arbitrary",), nbytes),
        name="rmsnorm_small",
    )(x, gain.reshape(1, d), wt_all, wt_all)


def _mm_body(*refs, has_res):
    a_ref, w_ref = refs[0], refs[1]
    o_ref = refs[2 + has_res]
    part = _dot(a_ref[...], w_ref[...])
    if has_res:
        part = part + refs[2][...]
    o_ref[...] = part.astype(o_ref.dtype)


def matmul(a, w, *, out_dtype, res=None, tm, tn, name="matmul"):
    m, kdim = a.shape
    n = w.shape[1]
    tm, tn = min(tm, m), min(tn, n)
    assert m % tm == 0 and n % tn == 0
    has_res = res is not None
    in_specs = [pl.BlockSpec((tm, kdim), lambda i, j: (i, 0)), pl.BlockSpec((kdim, tn), lambda i, j: (0, j))]
    args = [a, w]
    nbytes = _nbytes((tm, kdim), BF16) + _nbytes((kdim, tn), BF16) + _nbytes((tm, tn), out_dtype) + _nbytes((tm, tn), F32)
    if has_res:
        in_specs.append(pl.BlockSpec((tm, tn), lambda i, j: (i, j)))
        args.append(res)
        nbytes += _nbytes((tm, tn), F32)
    return pl.pallas_call(
        functools.partial(_mm_body, has_res=has_res),
        grid=(m // tm, n // tn),
        in_specs=in_specs,
        out_specs=pl.BlockSpec((tm, tn), lambda i, j: (i, j)),
        out_shape=jax.ShapeDtypeStruct((m, n), out_dtype),
        compiler_params=_params(("parallel", "parallel"), nbytes),
        name=name,
    )(*args)


_XPOSE_ROWS, _XPOSE_COLS = 256, 512


def _slab_copy(w_hbm, stage_ref, sem, layer, slab, start0, tn, features_on_rows):
    if features_on_rows:
        src = w_hbm.at[layer, pl.ds(pl.multiple_of(start0 + slab * tn, V7X_SUBLANES), tn), :]
    else:
        src = w_hbm.at[layer, :, pl.ds(pl.multiple_of(start0 + slab * tn, V7X_LANES), tn)]
    return pltpu.make_async_copy(src, stage_ref, sem.at[0])


def _slab_body(*refs, n_a, has_res, layer, start0, features_on_rows):
    a_refs, w_hbm = refs[:n_a], refs[n_a]
    r_ref = refs[n_a + 1] if has_res else None
    o_ref, stage_ref, wb_ref, sem = refs[n_a + 1 + has_res:]
    kdim, tn = wb_ref.shape
    slab, i = pl.program_id(0), pl.program_id(1)
    copy = lambda s: _slab_copy(w_hbm, stage_ref, sem, layer, s, start0, tn, features_on_rows)

    @pl.when(i == 0)
    def _():
        @pl.when(slab == 0)
        def _():
            copy(slab).start()

        copy(slab).wait()
        if features_on_rows:
            for r0 in range(0, tn, _XPOSE_ROWS):
                r1 = min(r0 + _XPOSE_ROWS, tn)
                for k0 in range(0, kdim, _XPOSE_COLS):
                    wb_ref[k0:k0 + _XPOSE_COLS, r0:r1] = stage_ref[r0:r1, k0:k0 + _XPOSE_COLS].T.astype(BF16)
        else:
            wb_ref[...] = stage_ref[...].astype(BF16)

    @pl.when((i == 1) & (slab + 1 < pl.num_programs(0)))
    def _():
        copy(slab + 1).start()

    acc = r_ref[...] if has_res else None
    k0 = 0
    for a_ref in a_refs:
        k1 = k0 + a_ref.shape[1]
        part = _dot(a_ref[...], wb_ref[k0:k1, :])
        acc = part if acc is None else acc + part
        k0 = k1
    o_ref[...] = acc.astype(o_ref.dtype)


def slab_matmul(a_list, w_all, layer, start0, n_slabs, tn, *, features_on_rows, out_dtype, res=None,
                tm=1024, name="slab_matmul"):
    m = a_list[0].shape[0]
    kdim = sum(a.shape[1] for a in a_list)
    tm = min(tm, m)
    assert m // tm >= 2
    has_res = res is not None
    in_specs = [pl.BlockSpec((tm, a.shape[1]), lambda s, i: (i, 0)) for a in a_list]
    in_specs.append(pl.BlockSpec(memory_space=pl.ANY))
    args = list(a_list) + [w_all]
    nbytes = (2 * _nbytes((tm, kdim), BF16) + _nbytes((kdim, tn), F32) + _nbytes((kdim, tn), BF16)
              + 2 * _nbytes((tm, tn), out_dtype) + 4 * _nbytes((tm, tn), F32))
    if has_res:
        in_specs.append(pl.BlockSpec((tm, tn), lambda s, i: (i, s)))
        args.append(res)
        nbytes += 2 * _nbytes((tm, tn), F32)
    return pl.pallas_call(
        functools.partial(_slab_body, n_a=len(a_list), has_res=has_res, layer=layer, start0=start0,
                          features_on_rows=features_on_rows),
        grid=(n_slabs, m // tm),
        in_specs=in_specs,
        out_specs=pl.BlockSpec((tm, tn), lambda s, i: (i, s)),
        out_shape=jax.ShapeDtypeStruct((m, n_slabs * tn), out_dtype),
        scratch_shapes=[pltpu.VMEM((tn, kdim) if features_on_rows else (kdim, tn), F32),
                        pltpu.VMEM((kdim, tn), BF16),
                        pltpu.SemaphoreType.DMA((1,))],
        compiler_params=pltpu.CompilerParams(dimension_semantics=("arbitrary", "arbitrary"),
                                             vmem_limit_bytes=min(VMEM_REQUEST_CAP, nbytes)),
        name=name,
    )(*args)


def _gateup_copies(wg_hbm, wu_hbm, stage_ref, sem, layer, slab, width):
    c0 = pl.multiple_of(slab * FFN_SLAB, FFN_SLAB)
    return [pltpu.make_async_copy(w_hbm.at[layer, :, pl.ds(c0, width)], stage_ref.at[k, :, pl.ds(0, width)],
                                  sem.at[k])
            for k, w_hbm in enumerate((wg_hbm, wu_hbm))]


def _gateup_body(a_ref, wg_hbm, wu_hbm, o_ref, stage_ref, wgb_ref, wub_ref, sem, *, layer, last_width):
    slab, i = pl.program_id(0), pl.program_id(1)
    last = pl.num_programs(0) - 1

    def for_slab(s, action):
        @pl.when(s < last)
        def _():
            for c in _gateup_copies(wg_hbm, wu_hbm, stage_ref, sem, layer, s, FFN_SLAB):
                action(c)

        @pl.when(s == last)
        def _():
            for c in _gateup_copies(wg_hbm, wu_hbm, stage_ref, sem, layer, s, last_width):
                action(c)

    @pl.when(i == 0)
    def _():
        @pl.when(slab == 0)
        def _():
            for_slab(slab, lambda c: c.start())

        for_slab(slab, lambda c: c.wait())
        wgb_ref[...] = stage_ref[0].astype(BF16)
        wub_ref[...] = stage_ref[1].astype(BF16)

    @pl.when((i == 1) & (slab < last))
    def _():
        for_slab(slab + 1, lambda c: c.start())

    a = a_ref[...]
    g = _dot(a, wgb_ref[...])
    u = _dot(a, wub_ref[...])
    o_ref[...] = (_silu(g) * u).astype(o_ref.dtype)


def ffn_gate_up(a, wg_all, wu_all, layer, tm=1024):
    m, kdim = a.shape
    n = wg_all.shape[2]
    tm = min(tm, m)
    n_slabs = pl.cdiv(n, FFN_SLAB)
    last_width = n - (n_slabs - 1) * FFN_SLAB
    assert m // tm >= 2 and last_width % V7X_LANES == 0
    nbytes = (2 * _nbytes((tm, kdim), BF16) + 2 * _nbytes((kdim, FFN_SLAB), F32)
              + 2 * _nbytes((kdim, FFN_SLAB), BF16) + 2 * _nbytes((tm, FFN_SLAB), BF16)
              + 6 * _nbytes((tm, FFN_SLAB), F32))
    return pl.pallas_call(
        functools.partial(_gateup_body, layer=layer, last_width=last_width),
        grid=(n_slabs, m // tm),
        in_specs=[pl.BlockSpec((tm, kdim), lambda s, i: (i, 0)),
                  pl.BlockSpec(memory_space=pl.ANY),
                  pl.BlockSpec(memory_space=pl.ANY)],
        out_specs=pl.BlockSpec((tm, FFN_SLAB), lambda s, i: (i, s)),
        out_shape=jax.ShapeDtypeStruct((m, n), BF16),
        scratch_shapes=[pltpu.VMEM((2, kdim, FFN_SLAB), F32),
                        pltpu.VMEM((kdim, FFN_SLAB), BF16), pltpu.VMEM((kdim, FFN_SLAB), BF16),
                        pltpu.SemaphoreType.DMA((2,))],
        compiler_params=pltpu.CompilerParams(dimension_semantics=("arbitrary", "arbitrary"),
                                             vmem_limit_bytes=min(VMEM_REQUEST_CAP, nbytes)),
        name="ffn_gate_up",
    )(a, wg_all, wu_all)


def _swa_body(q_ref, kvc_ref, kvp_ref, bias_ref, sink_ref, qg_ref, kg_ref, og_ref, e_ref, et_ref, o_ref, acc_ref):
    hd, blk, grp = SWA_HEAD_DIM, SWA_BLOCK, SWA_GROUP
    ones = jnp.ones((2 * blk, V7X_LANES), BF16)
    first_variant = jnp.minimum(pl.program_id(1), 1)

    for sub in range(SWA_BLOCKS_PER_STEP):
        rows = slice(blk * sub, blk * (sub + 1))
        kv_prev = kvp_ref if sub == 0 else kvc_ref.at[blk * (sub - 1):blk * sub]
        kband = jnp.concatenate([kv_prev[:, :SWA_KV_WIDTH], kvc_ref[rows, :SWA_KV_WIDTH]], axis=0).astype(F32)
        vband = jnp.concatenate([kv_prev[:, SWA_KV_WIDTH:], kvc_ref[rows, SWA_KV_WIDTH:]], axis=0)
        variant = first_variant if sub == 0 else 1

        q = q_ref[rows, :].astype(F32)
        ssq = _dot_exact_rhs(q * q, e_ref[...], 2)
        inv = lax.rsqrt(ssq * (1.0 / hd) + EPS)
        qn = (q * _dot_exact_rhs(inv, et_ref[...], 3) * qg_ref[...]).astype(BF16)

        for j in range(SWA_KV_HEADS):
            kn = _rms(kband[:, hd * j:hd * (j + 1)], kg_ref[...]).astype(BF16)
            qs = jnp.concatenate([qn[:, hd * (grp * j + g):hd * (grp * j + g + 1)] for g in range(grp)], axis=0)
            s = _dot_nt(qs, kn) * (hd ** -0.5) + bias_ref[variant, j]
            sink = sink_ref[j]
            s_prev, s_cur = s[:, :blk], s[:, blk:]
            row_max = jnp.max(jnp.maximum(s_prev, s_cur), axis=-1, keepdims=True)
            mx = jnp.maximum(jnp.broadcast_to(row_max, sink.shape), sink)
            p = jnp.concatenate([jnp.exp(s_prev - mx), jnp.exp(s_cur - mx)], axis=1).astype(BF16)
            total = _dot(p, ones) + jnp.exp(sink - mx)
            o = _dot(p, vband[:, hd * j:hd * (j + 1)]) / total[:, :hd]
            for g in range(grp):
                h = grp * j + g
                acc_ref[sub, :, hd * h:hd * (h + 1)] = o[blk * g:blk * (g + 1), :]
        o_ref[rows, :] = _rms(acc_ref[sub], og_ref[...]).astype(o_ref.dtype)


def swa_mixer(qkv, bias, sinks, q_gain, k_gain, out_gain, batch, seq):
    nb = seq // SWA_BLOCK
    kv_blk = SWA_WIDTH // (2 * SWA_KV_WIDTH)
    rows = SWA_GROUP * SWA_BLOCK
    no_prev = jnp.arange(2 * SWA_BLOCK) < SWA_BLOCK
    bias_g = jnp.stack([jnp.where(no_prev, -jnp.inf, bias), bias]).reshape(2, SWA_KV_HEADS, rows, 2 * SWA_BLOCK)
    sink_col = jnp.broadcast_to(jnp.repeat(sinks, SWA_BLOCK)[:, None],
                                (SWA_HEADS * SWA_BLOCK, V7X_LANES)).reshape(SWA_KV_HEADS, rows, V7X_LANES)
    head_of_col = jnp.arange(SWA_WIDTH) // SWA_HEAD_DIM
    e = (head_of_col[:, None] == jnp.arange(V7X_LANES)[None, :]).astype(BF16)
    const = lambda shape: pl.BlockSpec(shape, lambda b, n: (0,) * len(shape))
    nsub = SWA_BLOCKS_PER_STEP
    step_rows = nsub * SWA_BLOCK
    ns = seq // step_rows
    assert seq % step_rows == 0
    nbytes = (_nbytes((step_rows, SWA_WIDTH), BF16) * 2 + 2 * _nbytes((step_rows, 2 * SWA_KV_WIDTH), BF16)
              + _nbytes(bias_g.shape, F32) + _nbytes((SWA_KV_HEADS, rows, V7X_LANES), F32)
              + 2 * _nbytes(e.shape, BF16) + _nbytes((step_rows, SWA_WIDTH), F32)
              + 4 * _nbytes((rows, 2 * SWA_BLOCK), F32))
    return pl.pallas_call(
        _swa_body,
        grid=(batch, ns),
        in_specs=[pl.BlockSpec((step_rows, SWA_WIDTH), lambda b, n: (b * ns + n, 0)),
                  pl.BlockSpec((step_rows, 2 * SWA_KV_WIDTH), lambda b, n: (b * ns + n, kv_blk)),
                  pl.BlockSpec((SWA_BLOCK, 2 * SWA_KV_WIDTH),
                               lambda b, n: (b * nb + jnp.maximum(n * nsub - 1, 0), kv_blk)),
                  const(bias_g.shape), const(sink_col.shape),
                  const((1, SWA_WIDTH)), const((1, SWA_HEAD_DIM)), const((1, SWA_WIDTH)),
                  const(e.shape), const(e.T.shape)],
        out_specs=pl.BlockSpec((step_rows, SWA_WIDTH), lambda b, n: (b * ns + n, 0)),
        out_shape=jax.ShapeDtypeStruct((batch * seq, SWA_WIDTH), BF16),
        scratch_shapes=[pltpu.VMEM((nsub, SWA_BLOCK, SWA_WIDTH), F32)],
        compiler_params=_params(("parallel", "parallel"), nbytes),
        name="swa_mixer",
    )(qkv, qkv, qkv, bias_g, sink_col, jnp.tile(q_gain, SWA_HEADS).reshape(1, -1), k_gain.reshape(1, -1),
      out_gain.reshape(1, -1), e, e.T)


def _t5_bucket(dist):
    n = jnp.maximum(dist, 0)
    max_exact = REL_BUCKETS // 2
    nf = jnp.maximum(n, 1).astype(F32)
    large = max_exact + (jnp.log(nf / max_exact) / math.log(REL_MAX_DIST / max_exact)
                         * (REL_BUCKETS - max_exact)).astype(jnp.int32)
    large = jnp.minimum(large, REL_BUCKETS - 1)
    return jnp.where(n < max_exact, n, large)


def swa_band_bias(rel_bias):
    i = jnp.arange(SWA_BLOCK, dtype=jnp.int32)[:, None]
    j = jnp.arange(2 * SWA_BLOCK, dtype=jnp.int32)[None, :]
    dist = i + SWA_BLOCK - j
    onehot = _t5_bucket(dist)[None] == jnp.arange(REL_BUCKETS, dtype=jnp.int32)[:, None, None]
    bias = jnp.sum(jnp.where(onehot[:, None], rel_bias[:, :, None, None], 0.0), axis=0)
    in_window = (dist >= 0) & (dist < SWA_WINDOW)
    return jnp.where(in_window[None], bias, -jnp.inf)


def _gla_body(x_ref, gl_ref, wup_ref, bup_ref, gain_ref, wq_ref, wk_ref, wv_ref, wo_ref,
              o_ref, wq16_ref, wkv16_ref, wo16_ref, state_ref):
    dk, dv, c = GLA_KEY_DIM, GLA_VAL_DIM, GLA_CHUNK

    wq16_ref[...] = wq_ref[...].astype(BF16)
    wkv16_ref[:, :X_WIDTH] = wk_ref[...].astype(BF16)
    wkv16_ref[:, X_WIDTH:] = wv_ref[...].astype(BF16)
    wo16_ref[...] = wo_ref[...].astype(BF16)

    @pl.when(pl.program_id(1) == 0)
    def _():
        state_ref[...] = jnp.zeros_like(state_ref)

    w_hi, w_lo = _split_bf16(wup_ref[...], 2)
    tb = min(GLA_TIME_BLOCK, x_ref.shape[0])
    row = lax.broadcasted_iota(jnp.int32, (tb, tb), 0)
    colm = lax.broadcasted_iota(jnp.int32, (tb, tb), 1)
    same_chunk_lower = (((row // c) == (colm // c)) & (colm <= row)).astype(BF16)
    ri = lax.broadcasted_iota(jnp.int32, (c, c), 0)
    ci = lax.broadcasted_iota(jnp.int32, (c, c), 1)
    causal = ci <= ri
    for r0 in range(0, x_ref.shape[0], c):
        if r0 % tb == 0:
            gl_hi, gl_lo = _split_bf16(gl_ref[r0:r0 + tb, :], 2)
            pre = _dot(gl_hi, w_hi) + _dot(gl_hi, w_lo) + _dot(gl_lo, w_hi) + bup_ref[...]
            g = (jnp.minimum(pre, 0.0) - jnp.log1p(jnp.exp(-jnp.abs(pre)))) * (1.0 / GLA_GATE_NORMALIZER)
            bcum_all = _dot_exact_lhs(same_chunk_lower, g, 3)
        rb = r0 % tb
        for h in range(GLA_HEADS):
            bcum = bcum_all[rb:rb + c, dk * h:dk * (h + 1)]
            blast = bcum[c - 1:c, :]
            q = x_ref[r0:r0 + c, dk * h:dk * (h + 1)].astype(F32)
            k = x_ref[r0:r0 + c, GLA_KEY_WIDTH + dk * h:GLA_KEY_WIDTH + dk * (h + 1)].astype(F32)
            v = x_ref[r0:r0 + c, 2 * GLA_KEY_WIDTH + dv * h:2 * GLA_KEY_WIDTH + dv * (h + 1)]
            r = x_ref[r0:r0 + c, 2 * GLA_KEY_WIDTH + GLA_WIDTH + dv * h:
                      2 * GLA_KEY_WIDTH + GLA_WIDTH + dv * (h + 1)].astype(F32)
            qd = (q * (dk ** -0.5) * jnp.exp(bcum)).astype(BF16)
            kd = (k * jnp.exp(-bcum)).astype(BF16)
            kl = (k * jnp.exp(blast - bcum)).astype(BF16)
            att = jnp.where(causal, _dot_nt(qd, kd), 0.0)
            st = state_ref[h]
            o = _dot(att.astype(BF16), v) + _dot_nt(qd, st.astype(BF16))
            state_ref[h] = st * jnp.exp(blast) + _dot_tn(v, kl)
            o = _rms(o, gain_ref[...]) * _silu(r)
            o_ref[r0:r0 + c, dv * h:dv * (h + 1)] = o.astype(o_ref.dtype)


def gla_mixer(x, small, w_up_pad, b_up, norm_gain, batch, seq, x_w_q, x_w_k, x_w_v, x_w_o, layer):
    tb = min(GLA_BLOCKS_PER_STEP * GLA_TIME_BLOCK, seq)
    nt = seq // tb
    steps = batch * nt
    width = x.shape[1]
    d, xw = x_w_q.shape[1:]
    assert d % (steps * V7X_SUBLANES) == 0 and xw % (steps * 2 * V7X_SUBLANES) == 0
    row_blk = lambda rows, cols: pl.BlockSpec((rows // steps, cols), lambda b, t: (b * nt + t, 0))
    layer_blk = lambda rows, cols: pl.BlockSpec((None, rows // steps, cols), lambda b, t: (layer, b * nt + t, 0))
    nbytes = (_nbytes((tb, width), BF16) + _nbytes((tb, V7X_LANES), F32) + _nbytes(w_up_pad.shape, F32)
              + _nbytes((tb, GLA_WIDTH), BF16) + _nbytes((GLA_HEADS, GLA_VAL_DIM, GLA_KEY_DIM), F32)
              + 8 * _nbytes((tb, GLA_KEY_WIDTH), F32) + 6 * _nbytes((d // steps, xw), F32))
    return pl.pallas_call(
        _gla_body,
        grid=(batch, nt),
        in_specs=[pl.BlockSpec((tb, width), lambda b, t: (b * nt + t, 0)),
                  pl.BlockSpec((tb, V7X_LANES), lambda b, t: (b * nt + t, 0)),
                  pl.BlockSpec(w_up_pad.shape, lambda b, t: (0, 0)),
                  pl.BlockSpec((1, GLA_KEY_WIDTH), lambda b, t: (0, 0)),
                  pl.BlockSpec((1, GLA_VAL_DIM), lambda b, t: (0, 0)),
                  layer_blk(d, xw), layer_blk(d, xw), layer_blk(d, xw), layer_blk(xw, d)],
        out_specs=[pl.BlockSpec((tb, GLA_WIDTH), lambda b, t: (b * nt + t, 0)),
                   row_blk(d, xw), row_blk(d, 2 * xw), row_blk(xw, d)],
        out_shape=[jax.ShapeDtypeStruct((batch * seq, GLA_WIDTH), BF16),
                   jax.ShapeDtypeStruct((d, xw), BF16), jax.ShapeDtypeStruct((d, 2 * xw), BF16),
                   jax.ShapeDtypeStruct((xw, d), BF16)],
        scratch_shapes=[pltpu.VMEM((GLA_HEADS, GLA_VAL_DIM, GLA_KEY_DIM), F32)],
        compiler_params=_params(("arbitrary", "arbitrary"), nbytes),
        name="gla_mixer",
    )(x, small, w_up_pad, b_up.reshape(1, -1), norm_gain.reshape(1, -1), x_w_q, x_w_k, x_w_v, x_w_o)


def _ssd_body(x_ref, dt_ref, cw_ref, cb_ref, dtb_ref, alog_ref, dexp_ref, gain_ref, expand_ref, shift_ref,
              wcast_in_ref, o_ref, wcast_out_ref, state_ref, tail_ref, y_ref, *, wcast_blocks):
    L, P, N = SSD_CHUNK, SSD_HEAD_DIM, SSD_STATE
    gw = SSD_GROUP_WIDTH

    @pl.when(pl.program_id(0) * pl.num_programs(1) + pl.program_id(1) < wcast_blocks)
    def _():
        wcast_out_ref[...] = wcast_in_ref[...].astype(BF16)

    @pl.when(pl.program_id(1) == 0)
    def _():
        state_ref[...] = jnp.zeros_like(state_ref)
        tail_ref[...] = jnp.zeros_like(tail_ref)

    xin = x_ref[:, SSD_WIDTH:]
    xin32 = xin.astype(F32)
    shifted = _dot(shift_ref[...], xin)
    conv = cb_ref[...] + cw_ref[SSD_CONV - 1:SSD_CONV, :] * xin32
    head = jnp.zeros((8, SSD_CONV_CH), F32)
    for j in range(1, SSD_CONV):
        wj = cw_ref[SSD_CONV - 1 - j:SSD_CONV - j, :]
        conv = conv + wj * shifted[L * (j - 1):L * j, :]
        head = head + wj * tail_ref[8 - j:16 - j, :]
    conv = jnp.concatenate([conv[0:8, :] + head, conv[8:, :]], axis=0)
    tail_ref[0:8, :] = xin32[L - 8:, :]
    xbc = _silu(conv)
    xs = xbc[:, :SSD_WIDTH]
    bm = xbc[:, SSD_WIDTH:SSD_WIDTH + SSD_BC_WIDTH].astype(BF16)
    cm = xbc[:, SSD_WIDTH + SSD_BC_WIDTH:].astype(BF16)

    dt = _softplus(dt_ref[...] + dtb_ref[...])
    dta = dt * (-jnp.exp(alog_ref[...]))
    ri = lax.broadcasted_iota(jnp.int32, (L, L), 0)
    ci = lax.broadcasted_iota(jnp.int32, (L, L), 1)
    causal = ci <= ri
    a_cum = _dot_exact_lhs(causal.astype(BF16), dta, 3)
    a_cum_t = a_cum.T
    expand = expand_ref[...]
    a_exp = _dot_exact_rhs(a_cum, expand, 3)
    dt_exp = _dot_exact_rhs(dt, expand, 2)
    a_last = a_exp[L - 1:L, :]
    xd = xs * dt_exp
    xdec = (xd * jnp.exp(a_last - a_exp)).astype(BF16)
    xd16 = xd.astype(BF16)
    out_scale = jnp.exp(a_exp)

    for g in range(SSD_GROUPS):
        bg = bm[:, N * g:N * (g + 1)]
        cg = cm[:, N * g:N * (g + 1)]
        cb = _dot_nt(cg, bg)
        for kk in range(SSD_HEADS_PER_GROUP):
            h = g * SSD_HEADS_PER_GROUP + kk
            hl = SSD_DT_LANE0 + h
            diff = a_cum[:, hl:hl + 1] - a_cum_t[hl:hl + 1, :]
            m = cb * jnp.exp(jnp.where(causal, diff, -jnp.inf))
            y_ref[:, P * h:P * (h + 1)] = _dot(m.astype(BF16), xd16[:, P * h:P * (h + 1)])
        sg = state_ref[:, gw * g:gw * (g + 1)]
        y_off = _dot(cg, sg.astype(BF16)) * out_scale[:, gw * g:gw * (g + 1)]
        state_ref[:, gw * g:gw * (g + 1)] = (sg * jnp.exp(a_last[:, gw * g:gw * (g + 1)])
                                             + _dot_tn(bg, xdec[:, gw * g:gw * (g + 1)]))
        yg = y_ref[:, gw * g:gw * (g + 1)] + y_off + xs[:, gw * g:gw * (g + 1)] * dexp_ref[:, gw * g:gw * (g + 1)]
        yg = yg * _silu(x_ref[:, gw * g:gw * (g + 1)].astype(F32))
        o_ref[:, gw * g:gw * (g + 1)] = _rms(yg, gain_ref[:, gw * g:gw * (g + 1)]).astype(o_ref.dtype)


def ssd_mixer(x, small, conv_w, conv_b, dt_bias, a_log, d_skip, norm_gain, batch, seq, wcast_all, layer):
    L = SSD_CHUNK
    nc = seq // L
    width = x.shape[1]
    _, wc_rows, wc_cols = wcast_all.shape
    wc_tile = V7X_MXU_WIDTH
    wc_blocks = wc_rows // wc_tile
    assert wc_rows % wc_tile == 0 and wc_blocks <= batch * nc
    wc_index = lambda b, c: jnp.minimum(b * nc + c, wc_blocks - 1)
    t = jnp.arange(L)
    shift = jnp.concatenate([(t[:, None] - j == t[None, :]) for j in range(1, SSD_CONV)], axis=0).astype(BF16)
    lane_pad = (SSD_DT_LANE0, V7X_LANES - SSD_DT_LANE0 - SSD_HEADS)
    dtb = jnp.pad(dt_bias, lane_pad).reshape(1, V7X_LANES)
    alog = jnp.pad(a_log, lane_pad).reshape(1, V7X_LANES)
    dexp = jnp.repeat(d_skip, SSD_HEAD_DIM).reshape(1, SSD_WIDTH)
    expand = (jnp.arange(V7X_LANES)[:, None] - SSD_DT_LANE0
              == (jnp.arange(SSD_WIDTH)[None, :] // SSD_HEAD_DIM)).astype(BF16)
    nbytes = (_nbytes((L, width), BF16) + _nbytes((L, V7X_LANES), F32) + _nbytes((L, SSD_WIDTH), BF16)
              + _nbytes(expand.shape, BF16) + 2 * _nbytes((SSD_STATE, SSD_WIDTH), F32)
              + 16 * _nbytes((L, SSD_CONV_CH), F32) + _nbytes((wc_tile, wc_cols), F32)
              + _nbytes((wc_tile, wc_cols), BF16))
    return pl.pallas_call(
        functools.partial(_ssd_body, wcast_blocks=wc_blocks),
        grid=(batch, nc),
        in_specs=[pl.BlockSpec((L, width), lambda b, c: (b * nc + c, 0)),
                  pl.BlockSpec((L, V7X_LANES), lambda b, c: (b * nc + c, 1)),
                  pl.BlockSpec((SSD_CONV, SSD_CONV_CH), lambda b, c: (0, 0)),
                  pl.BlockSpec((1, SSD_CONV_CH), lambda b, c: (0, 0)),
                  pl.BlockSpec((1, V7X_LANES), lambda b, c: (0, 0)),
                  pl.BlockSpec((1, V7X_LANES), lambda b, c: (0, 0)),
                  pl.BlockSpec((1, SSD_WIDTH), lambda b, c: (0, 0)),
                  pl.BlockSpec((1, SSD_WIDTH), lambda b, c: (0, 0)),
                  pl.BlockSpec(expand.shape, lambda b, c: (0, 0)),
                  pl.BlockSpec(shift.shape, lambda b, c: (0, 0)),
                  pl.BlockSpec((None, wc_tile, wc_cols), lambda b, c: (layer, wc_index(b, c), 0))],
        out_specs=[pl.BlockSpec((L, SSD_WIDTH), lambda b, c: (b * nc + c, 0)),
                   pl.BlockSpec((wc_tile, wc_cols), lambda b, c: (wc_index(b, c), 0))],
        out_shape=[jax.ShapeDtypeStruct((batch * seq, SSD_WIDTH), BF16),
                   jax.ShapeDtypeStruct((wc_rows, wc_cols), BF16)],
        scratch_shapes=[pltpu.VMEM((SSD_STATE, SSD_WIDTH), F32),
                        pltpu.VMEM((16, SSD_CONV_CH), F32),
                        pltpu.VMEM((L, SSD_WIDTH), F32)],
        compiler_params=_params(("arbitrary", "arbitrary"), nbytes),
        name="ssd_mixer",
    )(x, small, conv_w.reshape(SSD_CONV, SSD_CONV_CH), conv_b.reshape(1, -1), dtb, alog, dexp,
      norm_gain.reshape(1, -1), expand, shift, wcast_all)


def _xattn_body(h_ref, kv_ref, wq_ref, wo_ref, lnx_ref, qg_ref, kg_ref, lnf_ref, o_ref, hf_ref, att_ref):
    hd = X_HEAD_DIM
    mem_len = kv_ref.shape[0]
    half = mem_len // 2
    ones = jnp.ones((mem_len, V7X_LANES), BF16)
    h = h_ref[...]
    q = _dot(_rms(h, lnx_ref[...]).astype(BF16), wq_ref[...])
    for i in range(X_HEADS):
        qh = _rms(q[:, hd * i:hd * (i + 1)], qg_ref[...]).astype(BF16)
        kh = _rms(kv_ref[:, hd * i:hd * (i + 1)], kg_ref[...]).astype(BF16)
        vh = kv_ref[:, X_WIDTH + hd * i:X_WIDTH + hd * (i + 1)].astype(BF16)
        s = _dot_nt(qh, kh) * (hd ** -0.5)
        s_lo, s_hi = s[:, :half], s[:, half:]
        mx = jnp.broadcast_to(jnp.max(jnp.maximum(s_lo, s_hi), axis=-1, keepdims=True), s_lo.shape)
        p = jnp.concatenate([jnp.exp(s_lo - mx), jnp.exp(s_hi - mx)], axis=1).astype(BF16)
        att_ref[:, hd * i:hd * (i + 1)] = (_dot(p, vh) / _dot(p, ones)).astype(BF16)
    h_new = h + _dot(att_ref[...], wo_ref[...])
    o_ref[...] = h_new
    hf_ref[...] = _rms(h_new, lnf_ref[...]).astype(hf_ref.dtype)


def cross_attention_block(h, kv, w_q, w_o, ln_x, q_gain, k_gain, ln_ffn, batch, seq, mem_len, tq=512):
    m, d = h.shape
    tq = min(tq, seq)
    nq = seq // tq
    assert mem_len == 2 * V7X_LANES and X_HEAD_DIM == V7X_LANES
    const = lambda shape: pl.BlockSpec(shape, lambda b, t: (0,) * len(shape), pipeline_mode=pl.Buffered(1))
    row_blk = lambda width: pl.BlockSpec((tq, width), lambda b, t: (b * nq + t, 0))
    nbytes = (2 * _nbytes((tq, d), F32) + _nbytes((tq, d), BF16) + _nbytes((mem_len, 2 * X_WIDTH), F32)
              + 2 * _nbytes((d, X_WIDTH), BF16) + 2 * _nbytes((tq, d), F32))
    return pl.pallas_call(
        _xattn_body,
        grid=(batch, nq),
        in_specs=[row_blk(d),
                  pl.BlockSpec((mem_len, 2 * X_WIDTH), lambda b, t: (b, 0)),
                  const((d, X_WIDTH)), const((X_WIDTH, d)), const((1, d)),
                  const((1, X_HEAD_DIM)), const((1, X_HEAD_DIM)), const((1, d))],
        out_specs=[row_blk(d), row_blk(d)],
        out_shape=[jax.ShapeDtypeStruct((m, d), F32), jax.ShapeDtypeStruct((m, d), BF16)],
        scratch_shapes=[pltpu.VMEM((tq, X_WIDTH), BF16)],
        compiler_params=_params(("parallel", "parallel"), nbytes),
        name="xattn_block",
    )(h, kv, w_q, w_o, ln_x.reshape(1, -1), q_gain.reshape(1, -1), k_gain.reshape(1, -1), ln_ffn.reshape(1, -1))


_IN_OFF = np.cumsum([0, SWA_WIDTH, SWA_KV_WIDTH, SWA_KV_WIDTH, GLA_KEY_WIDTH, GLA_KEY_WIDTH, GLA_WIDTH,
                     GLA_WIDTH, GLA_GATE_RANK, SSD_WIDTH, SSD_CONV_CH, SSD_HEADS]).tolist()
SWA_COL0, GLA_COL0, GLOW_COL0, SSD_COL0, DT_COL0 = _IN_OFF[0], _IN_OFF[3], _IN_OFF[7], _IN_OFF[8], _IN_OFF[10]


IN_DIM = _IN_OFF[-1]
SMALL_ROW_STARTS = (GLOW_COL0, IN_DIM - V7X_LANES)
assert SMALL_ROW_STARTS[1] + SSD_DT_LANE0 == DT_COL0


def kernel(x, mem, rel_bias, ln_mix, w_in, swa_q_gain, swa_k_gain, swa_sinks, swa_out_gain, gla_w_gk_up, gla_b_gk_up, gla_norm_gain, ssd_conv_w, ssd_conv_b, ssd_dt_bias, ssd_a_log, ssd_d, ssd_norm_gain, w_mix_out, ln_x, ln_mem, x_w_q, x_w_k, x_w_v, x_w_o, x_q_gain, x_k_gain, ln_ffn, ffn_w_gate, ffn_w_up, ffn_w_down):
    batch, seq, d = x.shape
    mem_len = mem.shape[1]
    m = batch * seq
    band_bias = swa_band_bias(rel_bias)
    h = x.reshape(m, d)
    mem2 = mem.reshape(batch * mem_len, d)
    w_in_t = jnp.swapaxes(w_in, 1, 2)
    for l in range(DEPTH):
        w_up_pad = jnp.pad(gla_w_gk_up[l], ((0, V7X_LANES - GLA_GATE_RANK), (0, 0)))
        hn, p_small = rmsnorm_small(h, ln_mix[l], w_in_t, l, SMALL_ROW_STARTS)
        proj = functools.partial(slab_matmul, [hn], w_in_t, l, features_on_rows=True, out_dtype=BF16)
        p_swa = proj(SWA_COL0, 1, GLA_COL0 - SWA_COL0, tm=512, name="proj_swa")
        p_gla = proj(GLA_COL0, (GLOW_COL0 - GLA_COL0) // PROJ_SLAB, PROJ_SLAB, name="proj_gla")
        p_ssd = proj(SSD_COL0, (DT_COL0 - SSD_COL0) // PROJ_SLAB, PROJ_SLAB, name="proj_ssd")
        y_a = swa_mixer(p_swa, band_bias, swa_sinks[l], swa_q_gain[l], swa_k_gain[l], swa_out_gain[l], batch, seq)
        y_b, w_q16, w_kv16, w_o16 = gla_mixer(p_gla, p_small, w_up_pad, gla_b_gk_up[l], gla_norm_gain[l], batch, seq,
                                              x_w_q, x_w_k, x_w_v, x_w_o, l)
        y_c, w_down16 = ssd_mixer(p_ssd, p_small, ssd_conv_w[l], ssd_conv_b[l], ssd_dt_bias[l], ssd_a_log[l],
                                  ssd_d[l], ssd_norm_gain[l], batch, seq, ffn_w_down, l)
        h = slab_matmul([y_a, y_b, y_c], w_mix_out, l, 0, d // MIX_SLAB, MIX_SLAB, features_on_rows=False,
                        out_dtype=F32, res=h, tm=MIX_TOKENS, name="mix_out")
        memn = rmsnorm(mem2, ln_mem[l])
        kv = matmul(memn, w_kv16, out_dtype=F32, tm=1024, tn=512, name="xattn_kv")
        h, hf = cross_attention_block(h, kv, w_q16, w_o16, ln_x[l], x_q_gain[l], x_k_gain[l], ln_ffn[l],
                                      batch, seq, mem_len)
        hidden = ffn_gate_up(hf, ffn_w_gate, ffn_w_up, l)
        h = matmul(hidden, w_down16, out_dtype=F32, res=h, tm=512, tn=512, name="ffn_down")
    return h.reshape(batch, seq, d)
```
